```python
import jax, jax.numpy as jnp
from jax import lax
import numpy as np

D_MODEL = 1024
BATCH = 16
SEQ = 256
DEPTH = 2
DEC_BATCH = 2
DEC_SEQ = 2048
PAST_LEN = 512

GRID_W = 64
N_EVEN = (DEPTH + 1) // 2
N_ODD = DEPTH // 2
N_DIR = 2
CONV_W = D_MODEL
CONV_TAPS = 3
WKV_W = D_MODEL
WKV_HEAD_DIM = 64
WKV_HEADS = WKV_W // WKV_HEAD_DIM
DECAY_RANK = 64
ICLR_RANK = 64
GN_EPS = 64e-5
GLA_HEADS = 4
GLA_DK_TOTAL = D_MODEL // 2
GLA_DV_TOTAL = D_MODEL
GLA_DK = GLA_DK_TOTAL // GLA_HEADS
GLA_DV = GLA_DV_TOTAL // GLA_HEADS
GLA_GATE_RANK = 16
GLA_GATE_NORM = 16.0
GLA_CHUNK = 64
EVEN_IN = 4 * CONV_W + 4 * WKV_W
ODD_IN = 2 * GLA_DK_TOTAL + 2 * GLA_DV_TOTAL
EVEN_SPLITS = [CONV_W, 2 * CONV_W, 3 * CONV_W, 4 * CONV_W, 4 * CONV_W + WKV_W, 4 * CONV_W + 2 * WKV_W, 4 * CONV_W + 3 * WKV_W]
ODD_SPLITS = [GLA_DK_TOTAL, 2 * GLA_DK_TOTAL, 2 * GLA_DK_TOTAL + GLA_DV_TOTAL]
NORM_EPS = 1e-6

kernel_name = 'bidir_conv_rwkv7_gla_diffusion_step'


def _rmsnorm(x, g):
    xf = x.astype(jnp.float32)
    y = xf * lax.rsqrt(jnp.mean(xf * xf, axis=-1, keepdims=True) + NORM_EPS)
    return (y * g.astype(jnp.float32)).astype(x.dtype)


def _short_conv(u, w, row_len):
    b, t, ch = u.shape
    ug = u.reshape(b, t // row_len, row_len, ch)
    up = jnp.pad(ug, ((0, 0), (0, 0), (1, 1), (0, 0)))
    y = w[0] * up[:, :, :-2] + w[1] * up[:, :, 1:-1] + w[2] * up[:, :, 2:]
    return y.reshape(b, t, ch)


def _orient(xf, xb):
    return jnp.stack([xf, jnp.flip(xb, axis=1)], axis=0)


def _unorient(o):
    return o[0] + jnp.flip(o[1], axis=1)


def _wkv7_scan(r, w, k, kk, a, v, s0):
    def step(s, inp):
        r_t, w_t, k_t, kk_t, a_t, v_t = inp
        s_kk = jnp.einsum('zbhvk,zbhk->zbhv', s, kk_t)
        s = (s * w_t[..., None, :] - s_kk[..., :, None] * (kk_t * a_t)[..., None, :]
             + v_t[..., :, None] * k_t[..., None, :])
        return s, jnp.einsum('zbhvk,zbhk->zbhv', s, r_t)
    xs = tuple(jnp.moveaxis(z, 2, 0) for z in (r, w, k, kk, a, v))
    s_fin, o = lax.scan(step, s0, xs)
    return jnp.moveaxis(o, 0, 2), s_fin


def _gla_chunk_scan(q, k, v, g, s0):
    z, b, t, h, _ = q.shape
    n = t // GLA_CHUNK

    def chunks(x):
        return x.reshape(z, b, n, GLA_CHUNK, h, x.shape[-1]).transpose(2, 0, 1, 4, 3, 5)

    mask = jnp.tril(jnp.ones((GLA_CHUNK, GLA_CHUNK), dtype=bool))

    def step(s, inp):
        q_c, k_c, v_c, g_c = inp
        bc = jnp.cumsum(g_c, axis=-2)
        b_last = bc[..., -1:, :]
        qe = q_c * jnp.exp(bc)
        ke = k_c * jnp.exp(-bc)
        att = jnp.where(mask, jnp.einsum('zbhid,zbhjd->zbhij', qe, ke), 0.0)
        o = jnp.einsum('zbhij,zbhje->zbhie', att, v_c) + jnp.einsum('zbhid,zbhde->zbhie', qe, s)
        s = (jnp.exp(b_last)[..., 0, :, None] * s
             + jnp.einsum('zbhjd,zbhje->zbhde', k_c * jnp.exp(b_last - bc), v_c))
        return s, o

    s_fin, o = lax.scan(step, s0, (chunks(q), chunks(k), chunks(v), chunks(g)))
    o = o.transpose(1, 2, 0, 4, 3, 5).reshape(z, b, t, h, v.shape[-1])
    return o, s_fin


def _mixer_conv_wkv(h, s0, row_len, w_in, w_out, conv_w, w0, w1, w2, a0, a1, a2,
                    k_k, k_a, r_k, ln_w, ln_b):
    f32 = jnp.float32
    b, t, _ = h.shape
    heads = lambda z: z.reshape(z.shape[:-1] + (WKV_HEADS, WKV_HEAD_DIM))
    u, gb, gc, zc, r, k, v, zw = jnp.split(h @ w_in, EVEN_SPLITS, axis=-1)
    o_conv = jax.nn.silu(zc) * gb * _short_conv(gc * u, conv_w, row_len)
    hf = h.astype(f32)
    w_raw = w0[:, None, None, :] + jnp.einsum('zbtr,zrc->zbtc', jnp.tanh(jnp.einsum('btd,zdr->zbtr', hf, w1)), w2)
    decay = jnp.exp(-jnp.exp(-jax.nn.softplus(-w_raw) - 0.5))
    iclr = jax.nn.sigmoid(a0[:, None, None, :] + jnp.einsum('zbtr,zrc->zbtc', jnp.einsum('btd,zdr->zbtr', hf, a1), a2))
    rf, kf, vf = r.astype(f32), k.astype(f32), v.astype(f32)
    kk = heads(kf * k_k)
    kk = kk / jnp.maximum(jnp.sqrt(jnp.sum(kk * kk, axis=-1, keepdims=True)), 1e-12)
    k_mod = heads(kf[None] * (1.0 + (iclr - 1.0) * k_a))
    decay_h, iclr_h = heads(decay), heads(iclr)
    rh, vh = heads(rf), heads(vf)
    o, s_fin = _wkv7_scan(_orient(rh, rh), _orient(decay_h[0], decay_h[1]), _orient(k_mod[0], k_mod[1]),
                          _orient(kk, kk), _orient(iclr_h[0], iclr_h[1]), _orient(vh, vh), s0.astype(f32))
    o = _unorient(o)
    mu = jnp.mean(o, axis=-1, keepdims=True)
    var = jnp.mean(jnp.square(o - mu), axis=-1, keepdims=True)
    gn = ((o - mu) * lax.rsqrt(var + GN_EPS)).reshape(b, t, WKV_W) * ln_w + ln_b
    bonus = jnp.sum(rh[None] * k_mod * r_k, axis=(0, -1))[..., None] * vh
    o_wkv = (gn + bonus.reshape(b, t, WKV_W)) * jax.nn.silu(zw.astype(f32))
    y = jnp.concatenate([o_conv, o_wkv.astype(h.dtype)], axis=-1) @ w_out
    return y, s_fin


def _mixer_gla(h, s0, w_in, w_out, gk1, gk2, gk_b, g_norm):
    f32 = jnp.float32
    b, t, _ = h.shape
    kh = lambda z: z.reshape(z.shape[:-1] + (GLA_HEADS, GLA_DK))
    vhd = lambda z: z.reshape(z.shape[:-1] + (GLA_HEADS, GLA_DV))
    q, k, v, zg = jnp.split(h @ w_in, ODD_SPLITS, axis=-1)
    hf = h.astype(f32)
    g = jax.nn.log_sigmoid(jnp.einsum('zbtr,zrc->zbtc', jnp.einsum('btd,zdr->zbtr', hf, gk1), gk2)
                           + gk_b[:, None, None, :]) / GLA_GATE_NORM
    g = kh(g)
    qh = kh(q.astype(f32) * GLA_DK ** -0.5)
    khh = kh(k.astype(f32))
    vh = vhd(v.astype(f32))
    o, s_fin = _gla_chunk_scan(_orient(qh, qh), _orient(khh, khh), _orient(vh, vh),
                               _orient(g[0], g[1]), s0.astype(f32))
    o = _unorient(o)
    o = o * lax.rsqrt(jnp.mean(o * o, axis=-1, keepdims=True) + NORM_EPS) * g_norm
    o = o.reshape(b, t, GLA_DV_TOTAL) * jax.nn.silu(zg.astype(f32))
    return o.astype(h.dtype) @ w_out, s_fin


def _trunk(x, cvec, s_wkv, s_gla, row_len, norm_g, ada_w, ada_b, final_g, even, odd):
    f32 = jnp.float32
    new_wkv, new_gla = [], []
    cf = jax.nn.silu(cvec.astype(f32))
    for l in range(DEPTH):
        mod = cf @ ada_w[l].astype(f32) + ada_b[l].astype(f32)
        shift, scale, gate = jnp.split(mod[:, None, :], 3, axis=-1)
        h = (_rmsnorm(x, norm_g[l]).astype(f32) * (1.0 + scale) + shift).astype(x.dtype)
        i = l // 2
        if l % 2 == 0:
            y, s = _mixer_conv_wkv(h, jnp.swapaxes(s_wkv[:, i], 0, 1), row_len, *[p[i] for p in even])
            new_wkv.append(jnp.swapaxes(s, 0, 1))
        else:
            y, s = _mixer_gla(h, jnp.swapaxes(s_gla[:, i], 0, 1), *[p[i] for p in odd])
            new_gla.append(jnp.swapaxes(s, 0, 1))
        x = (x.astype(f32) + gate * y.astype(f32)).astype(x.dtype)
    return _rmsnorm(x, final_g), jnp.stack(new_wkv, axis=1), jnp.stack(new_gla, axis=1)


def setup_inputs(seed: int = 0) -> dict:
    key = jax.random.key(seed)
    ks = iter(jax.random.split(key, 40))
    f32 = jnp.float32
    nrm = lambda shape, s: jax.random.normal(next(ks), shape, f32) * s
    D = D_MODEL
    return {
        'x_prompt': nrm((BATCH, SEQ, D), 1.0),
        'x_sample': nrm((DEC_BATCH, DEC_SEQ, D), 1.0),
        'state_wkv': nrm((DEC_BATCH, N_EVEN, N_DIR, WKV_HEADS, WKV_HEAD_DIM, WKV_HEAD_DIM), 0.3),
        'state_gla': nrm((DEC_BATCH, N_ODD, N_DIR, GLA_HEADS, GLA_DK, GLA_DV), 0.3),
        'c': nrm((DEC_BATCH, D), 1.0),
        'c_ctx': nrm((D,), 1.0),
        'norm_g': 1.0 + nrm((DEPTH, D), 0.02),
        'ada_w': nrm((DEPTH, D, 3 * D), 0.5 * D ** -0.5),
        'ada_b': nrm((DEPTH, 3 * D), 0.01),
        'final_g': 1.0 + nrm((D,), 0.02),
        'e_w_in': nrm((N_EVEN, D, EVEN_IN), D ** -0.5),
        'e_w_out': nrm((N_EVEN, CONV_W + WKV_W, D), (CONV_W + WKV_W) ** -0.5),
        'conv_w': nrm((N_EVEN, CONV_TAPS, CONV_W), 0.5),
        'wkv_w0': jax.random.uniform(next(ks), (N_EVEN, N_DIR, WKV_W), f32, -6.0, 1.0),
        'wkv_w1': nrm((N_EVEN, N_DIR, D, DECAY_RANK), D ** -0.5),
        'wkv_w2': nrm((N_EVEN, N_DIR, DECAY_RANK, WKV_W), 0.1 * DECAY_RANK ** -0.5),
        'wkv_a0': nrm((N_EVEN, N_DIR, WKV_W), 0.5),
        'wkv_a1': nrm((N_EVEN, N_DIR, D, ICLR_RANK), D ** -0.5),
        'wkv_a2': nrm((N_EVEN, N_DIR, ICLR_RANK, WKV_W), 0.1 * ICLR_RANK ** -0.5),
        'wkv_k_k': 0.85 + nrm((N_EVEN, WKV_W), 0.1),
        'wkv_k_a': 1.0 + nrm((N_EVEN, WKV_W), 0.1),
        'wkv_r_k': nrm((N_EVEN, WKV_HEADS, WKV_HEAD_DIM), 0.1),
        'wkv_ln_w': 1.0 + nrm((N_EVEN, WKV_W), 0.02),
        'wkv_ln_b': nrm((N_EVEN, WKV_W), 0.01),
        'o_w_in': nrm((N_ODD, D, ODD_IN), D ** -0.5),
        'o_w_out': nrm((N_ODD, GLA_DV_TOTAL, D), GLA_DV_TOTAL ** -0.5),
        'gla_gk1': nrm((N_ODD, N_DIR, D, GLA_GATE_RANK), D ** -0.5),
        'gla_gk2': nrm((N_ODD, N_DIR, GLA_GATE_RANK, GLA_DK_TOTAL), GLA_GATE_RANK ** -0.5),
        'gla_gk_b': nrm((N_ODD, N_DIR, GLA_DK_TOTAL), 0.5),
        'gla_g_norm': 1.0 + nrm((N_ODD, GLA_DV), 0.02),
    }


def reference(x_prompt, x_sample, state_wkv, state_gla, c, c_ctx, norm_g, ada_w, ada_b, final_g,
              e_w_in, e_w_out, conv_w, wkv_w0, wkv_w1, wkv_w2, wkv_a0, wkv_a1, wkv_a2,
              wkv_k_k, wkv_k_a, wkv_r_k, wkv_ln_w, wkv_ln_b,
              o_w_in, o_w_out, gla_gk1, gla_gk2, gla_gk_b, gla_g_norm):
    even = (e_w_in, e_w_out, conv_w, wkv_w0, wkv_w1, wkv_w2, wkv_a0, wkv_a1, wkv_a2,
            wkv_k_k, wkv_k_a, wkv_r_k, wkv_ln_w, wkv_ln_b)
    odd = (o_w_in, o_w_out, gla_gk1, gla_gk2, gla_gk_b, gla_g_norm)
    n_ctx_req, ctx_len = x_prompt.shape[0], x_prompt.shape[1]
    zero_wkv = jnp.zeros((n_ctx_req,) + state_wkv.shape[1:], jnp.float32)
    zero_gla = jnp.zeros((n_ctx_req,) + state_gla.shape[1:], jnp.float32)
    y_prompt, new_state_wkv, new_state_gla = _trunk(x_prompt, c_ctx[None, :], zero_wkv, zero_gla, ctx_len,
                                                    norm_g, ada_w, ada_b, final_g, even, odd)
    y_sample, _, _ = _trunk(x_sample, c, state_wkv, state_gla, GRID_W,
                            norm_g, ada_w, ada_b, final_g, even, odd)
    return (y_prompt, y_sample, new_state_wkv, new_state_gla)
```

```python
import functools
import math

import jax
import jax.numpy as jnp
from jax import lax
from jax.experimental import pallas as pl
from jax.experimental.pallas import tpu as pltpu

F32 = jnp.float32
BF16 = jnp.bfloat16
HI = lax.Precision.HIGHEST

D = 1024
N_CTX_TOK = 16 * 256
N_TOK = 2 * N_CTX_TOK
CTX_B, CTX_T = 16, 256
SMP_B, SMP_T = 2, 2048
GRID_W = 64
WKV_H, WKV_N = 16, 64
GLA_H, GLA_DK, GLA_DV = 4, 128, 256
GLA_GATE_NORM = 16.0
GN_EPS = 64e-5
NORM_EPS = 1e-6
CHUNK = 64
LANES = 128
TM_PROJ = 1024
TN_PROJ = 1024
TM_OUT = 256
VMEM_LIMIT = 56 * 1024 * 1024


def _silu(x):
    return x * jax.nn.sigmoid(x)


def _dot(a, b, precision=None):
    return jnp.dot(a, b, preferred_element_type=F32, precision=precision)


def _dot_nt(a, b, precision=None):
    return lax.dot_general(a, b, (((1,), (1,)), ((), ())), preferred_element_type=F32, precision=precision)


def _dot_tn(a, b, precision=None):
    return lax.dot_general(a, b, (((0,), (0,)), ((), ())), preferred_element_type=F32, precision=precision)


def _tile_row(i, tile):
    ctx_tiles = N_CTX_TOK // tile
    per_req = SMP_T // tile
    return jnp.where(i < ctx_tiles, 0, 1 + (i - ctx_tiles) // per_req)


def _mod_kernel(c_ref, w_ref, b_ref, o_ref):
    cf = _silu(c_ref[...])
    o_ref[0] = _dot(cf, w_ref[0], HI) + b_ref[0]


def _modulation(cvec, ada_w, ada_b):
    depth = ada_w.shape[0]
    return pl.pallas_call(
        _mod_kernel,
        grid=(depth, 3),
        in_specs=[
            pl.BlockSpec((8, D), lambda l, j: (0, 0)),
            pl.BlockSpec((1, D, D), lambda l, j: (l, 0, j)),
            pl.BlockSpec((1, 1, D), lambda l, j: (l, 0, j)),
        ],
        out_specs=pl.BlockSpec((1, 8, D), lambda l, j: (l, 0, j)),
        out_shape=jax.ShapeDtypeStruct((depth, 8, 3 * D), F32),
        compiler_params=pltpu.CompilerParams(
            dimension_semantics=("arbitrary", "arbitrary"), vmem_limit_bytes=VMEM_LIMIT),
        name="adaln_mod",
    )(cvec, ada_w, ada_b.reshape(depth, 1, 3 * D))


def _inproj_kernel(x_ref, mod_ref, g_ref, w_ref, wlr_ref, p_ref, lr_ref, h_scr, *, n_tanh):
    j = pl.program_id(1)

    @pl.when(j == 0)
    def _():
        x = x_ref[...]
        y = x * lax.rsqrt(jnp.mean(x * x, axis=-1, keepdims=True) + NORM_EPS) * g_ref[...]
        shift = mod_ref[0, 0:1, :]
        scale = mod_ref[0, 1:2, :]
        h = y * (1.0 + scale) + shift
        hb = h.astype(BF16)
        h_scr[...] = hb
        lr = _dot(hb, wlr_ref[...])
        if n_tanh:
            lane = lax.broadcasted_iota(jnp.int32, lr.shape, 1)
            lr = jnp.where(lane < n_tanh, jnp.tanh(lr), lr)
        lr_ref[...] = lr

    p_ref[...] = _dot(h_scr[...], w_ref[...])


def _inproj(x, mod3, g, w_bf, wlr_bf, n_tanh):
    n_out = w_bf.shape[1]
    n_lr = wlr_bf.shape[1]
    return pl.pallas_call(
        functools.partial(_inproj_kernel, n_tanh=n_tanh),
        grid=(N_TOK // TM_PROJ, n_out // TN_PROJ),
        in_specs=[
            pl.BlockSpec((TM_PROJ, D), lambda i, j: (i, 0)),
            pl.BlockSpec((1, 3, D), lambda i, j: (_tile_row(i, TM_PROJ), 0, 0)),
            pl.BlockSpec((1, D), lambda i, j: (0, 0)),
            pl.BlockSpec((D, TN_PROJ), lambda i, j: (0, j)),
            pl.BlockSpec((D, n_lr), lambda i, j: (0, 0)),
        ],
        out_specs=[
            pl.BlockSpec((TM_PROJ, TN_PROJ), lambda i, j: (i, j)),
            pl.BlockSpec((TM_PROJ, n_lr), lambda i, j: (i, 0)),
        ],
        out_shape=[
            jax.ShapeDtypeStruct((N_TOK, n_out), F32),
            jax.ShapeDtypeStruct((N_TOK, n_lr), F32),
        ],
        scratch_shapes=[pltpu.VMEM((TM_PROJ, D), BF16)],
        compiler_params=pltpu.CompilerParams(
            dimension_semantics=("arbitrary", "arbitrary"), vmem_limit_bytes=VMEM_LIMIT),
        name="inproj",
    )(x, mod3, g, w_bf, wlr_bf)


def _chunk_masks():
    idx = jnp.arange(CHUNK)
    lower = (idx[:, None] >= idx[None, :]).astype(F32)
    return jnp.stack([lower, lower.T], axis=0)


def _inv_unit_tri(lmat):
    row = lax.broadcasted_iota(jnp.int32, (CHUNK, CHUNK), 0)
    col = lax.broadcasted_iota(jnp.int32, (CHUNK, CHUNK), 1)
    eye = (row == col).astype(F32)

    def same_block(n):
        sh = n.bit_length() - 1
        return lax.shift_right_logical(row, sh) == lax.shift_right_logical(col, sh)

    l16 = jnp.where(same_block(16), lmat, 0.0)
    x = eye - l16
    p = _dot(l16, l16, HI)
    x = _dot(x, eye + p, HI)
    p = _dot(p, p, HI)
    x = _dot(x, eye + p, HI)
    p = _dot(p, p, HI)
    x = _dot(x, eye + p, HI)
    n = 16
    while n < CHUNK:
        loff = jnp.where(same_block(2 * n) & jnp.logical_not(same_block(n)), lmat, 0.0)
        x = x - _dot(_dot(x, loff, HI), x, HI)
        n *= 2
    return x


def _wkv_kernel(r_ref, k_ref, v_ref, lrw_ref, lra_ref, w2_ref, a2_ref, w0_ref, a0_ref,
                kk_ref, ka_ref, s0_ref, m_ref, o_ref, sout_ref, *, seq_len):
    z = pl.program_id(0)
    n_chunks = seq_len // CHUNK
    incl = m_ref[0]
    row = lax.broadcasted_iota(jnp.int32, (CHUNK, CHUNK), 0)
    col = lax.broadcasted_iota(jnp.int32, (CHUNK, CHUNK), 1)
    incl_b = incl > 0.5
    strict_b = incl_b & (row != col)
    w2 = w2_ref[0]
    a2 = a2_ref[0]
    w0 = w0_ref[0]
    a0 = a0_ref[0]
    k_k = kk_ref[...]
    k_a = ka_ref[...]

    def body(ci, carry):
        c = ci + z * (n_chunks - 1 - 2 * ci)
        t0 = pl.multiple_of(c * CHUNK, CHUNK)
        rows = pl.ds(t0, CHUNK)
        r = r_ref[rows, :]
        k = k_ref[rows, :]
        v = v_ref[rows, :]
        w_raw = w0 + _dot(lrw_ref[rows, :], w2, HI)
        lw = -math.exp(-0.5) * jax.nn.sigmoid(w_raw)
        a = jax.nn.sigmoid(a0 + _dot(lra_ref[rows, :], a2, HI))
        kkr = k * k_k
        kmod = k * (1.0 + (a - 1.0) * k_a)
        g = _dot(incl, lw, HI)
        gtot = jnp.sum(lw, axis=0, keepdims=True)
        e_g = jnp.exp(g)
        e_gx = jnp.exp(g - lw)
        e_ng = jnp.exp(-g)
        e_gc = jnp.exp(gtot - g)
        d_tot = jnp.exp(gtot)
        outs = []
        new_s = []
        for h in range(2):
            sl = slice(h * WKV_N, (h + 1) * WKV_N)
            kk_h = kkr[:, sl]
            nrm = jnp.sqrt(jnp.sum(kk_h * kk_h, axis=-1, keepdims=True))
            kk_h = kk_h / jnp.maximum(nrm, 1e-12)
            b_h = kk_h * a[:, sl]
            kt = kk_h * e_gx[:, sl]
            rt = r[:, sl] * e_g[:, sl]
            kh = kmod[:, sl] * e_ng[:, sl]
            bh = b_h * e_ng[:, sl]
            kp = kmod[:, sl] * e_gc[:, sl]
            bp = b_h * e_gc[:, sl]
            v_h = v[:, sl]
            lmat = jnp.where(strict_b, _dot_nt(kt, bh, HI), 0.0)
            mkk = jnp.where(strict_b, _dot_nt(kt, kh, HI), 0.0)
            mrk = jnp.where(incl_b, _dot_nt(rt, kh, HI), 0.0)
            mrb = jnp.where(incl_b, _dot_nt(rt, bh, HI), 0.0)
            tinv = _inv_unit_tri(lmat)
            mv = _dot(mkk, v_h, HI)
            p = _dot(tinv, kt, HI)
            q = _dot(tinv, mv, HI)
            rp = rt - _dot(mrb, p, HI)
            op = _dot(mrk, v_h, HI) - _dot(mrb, q, HI)
            wmat = _dot_tn(p, bp, HI)
            hmat = _dot_tn(v_h, kp, HI) - _dot_tn(q, bp, HI)
            s = carry[h]
            outs.append(_dot_nt(rp, s, HI) + op)
            new_s.append(s * d_tot[:, sl] - _dot(s, wmat, HI) + hmat)
        o_ref[0, rows, :] = jnp.concatenate(outs, axis=-1)
        return tuple(new_s)

    s_fin = lax.fori_loop(0, n_chunks, body, (s0_ref[0, 0, 0], s0_ref[0, 0, 1]))
    sout_ref[0, 0, 0] = s_fin[0]
    sout_ref[0, 0, 1] = s_fin[1]


def _wkv_scan(p_all, lr, w2pad, a2pad, w0, a0, k_k, k_a, s0, masks, *, n_seq, seq_len, tok_blk0):
    def tok(b):
        return tok_blk0 + b

    col = lambda base: (lambda z, b, hp: (tok(b), base + hp))
    vec = lambda z, b, hp: (z, 0, hp)
    return pl.pallas_call(
        functools.partial(_wkv_kernel, seq_len=seq_len),
        grid=(2, n_seq, WKV_H // 2),
        in_specs=[
            pl.BlockSpec((seq_len, LANES), col(32)),
            pl.BlockSpec((seq_len, LANES), col(40)),
            pl.BlockSpec((seq_len, LANES), col(48)),
            pl.BlockSpec((seq_len, LANES), lambda z, b, hp: (tok(b), 0)),
            pl.BlockSpec((seq_len, LANES), lambda z, b, hp: (tok(b), 1)),
            pl.BlockSpec((1, LANES, LANES), vec),
            pl.BlockSpec((1, LANES, LANES), vec),
            pl.BlockSpec((1, 1, LANES), vec),
            pl.BlockSpec((1, 1, LANES), vec),
            pl.BlockSpec((1, LANES), lambda z, b, hp: (0, hp)),
            pl.BlockSpec((1, LANES), lambda z, b, hp: (0, hp)),
            pl.BlockSpec((1, 1, 2, WKV_N, WKV_N), lambda z, b, hp: (b, z, hp, 0, 0)),
            pl.BlockSpec((1, CHUNK, CHUNK), lambda z, b, hp: (z, 0, 0)),
        ],
        out_specs=[
            pl.BlockSpec((1, seq_len, LANES), lambda z, b, hp: (z, b, hp)),
            pl.BlockSpec((1, 1, 2, WKV_N, WKV_N), lambda z, b, hp: (b, z, hp, 0, 0)),
        ],
        out_shape=[
            jax.ShapeDtypeStruct((2, n_seq * seq_len, D), F32),
            jax.ShapeDtypeStruct((n_seq, 2, WKV_H, WKV_N, WKV_N), F32),
        ],
        compiler_params=pltpu.CompilerParams(
            dimension_semantics=("arbitrary", "arbitrary", "arbitrary"), vmem_limit_bytes=VMEM_LIMIT),
        name="wkv_scan_%d" % seq_len,
    )(p_all, p_all, p_all, lr, lr, w2pad, a2pad, w0, a0, k_k, k_a, s0, masks)


def _group_stat(x, ind_ref, bind_ref):
    return _dot(_dot(x, ind_ref[...], HI), bind_ref[...], HI)


N_CTX_TILES = N_CTX_TOK // TM_OUT


def _sum_dirs(i, oc_ref, os_ref):
    is_ctx = i < N_CTX_TILES
    return jnp.where(is_ctx, oc_ref[0], os_ref[0]) + jnp.where(is_ctx, oc_ref[1], os_ref[1])


def _ctx_tile(i):
    return (0, jnp.minimum(i, N_CTX_TILES - 1), 0)


def _smp_tile(i):
    return (0, jnp.maximum(i - N_CTX_TILES, 0), 0)


def _even_out_kernel(x_ref, mod_ref, pc_ref, pw_ref, oc_ref, os_ref, lra_ref, a2_ref, a0_ref, ka_ref, rk_ref,
                     cw_ref, lnw_ref, lnb_ref, ind_ref, bind_ref, wo_ref, out_ref):
    i = pl.program_id(0)
    u = pc_ref[:, 0:D]
    gb = pc_ref[:, D:2 * D]
    gc = pc_ref[:, 2 * D:3 * D]
    zc = pc_ref[:, 3 * D:4 * D]
    xg = gc * u
    row_len = jnp.where(i < N_CTX_TILES, CTX_T, GRID_W)
    pos = lax.broadcasted_iota(jnp.int32, (TM_OUT, 1), 0) & (row_len - 1)
    prev = jnp.where(pos == 0, 0.0, pltpu.roll(xg, 1, 0))
    nxt = jnp.where(pos == row_len - 1, 0.0, pltpu.roll(xg, TM_OUT - 1, 0))
    conv = cw_ref[0:1, :] * prev + cw_ref[1:2, :] * xg + cw_ref[2:3, :] * nxt
    o_conv = _silu(zc) * gb * conv
    r = pw_ref[:, 0:D]
    k = pw_ref[:, D:2 * D]
    v = pw_ref[:, 2 * D:3 * D]
    zw = pw_ref[:, 3 * D:4 * D]
    o = _sum_dirs(i, oc_ref, os_ref)
    mu = _group_stat(o, ind_ref, bind_ref)
    dlt = o - mu
    var = _group_stat(dlt * dlt, ind_ref, bind_ref)
    gn = dlt * lax.rsqrt(var + GN_EPS) * lnw_ref[...] + lnb_ref[...]
    lra = lra_ref[...]
    ic0 = jax.nn.sigmoid(a0_ref[0] + _dot(lra, a2_ref[0], HI))
    ic1 = jax.nn.sigmoid(a0_ref[1] + _dot(lra, a2_ref[1], HI))
    ksum = k * (2.0 + (ic0 + ic1 - 2.0) * ka_ref[...])
    bonus = _group_stat(r * ksum * rk_ref[...], ind_ref, bind_ref) * float(WKV_N)
    o_wkv = (gn + bonus * v) * _silu(zw)
    y = _dot(o_conv.astype(BF16), wo_ref[0:D, :]) + _dot(o_wkv.astype(BF16), wo_ref[D:2 * D, :])
    gate = mod_ref[0, 2:3, :]
    out_ref[...] = x_ref[...] + gate * y


def _even_out(x, mod3, p_all, o_ctx, o_smp, lr, a2pad, a0, k_a, r_k, conv_w, ln_w, ln_b, ind, bind, wo_bf):
    row = lambda i: (i, 0)
    const2 = lambda i: (0, 0)
    const3 = lambda i: (0, 0, 0)
    return pl.pallas_call(
        _even_out_kernel,
        grid=(N_TOK // TM_OUT,),
        in_specs=[
            pl.BlockSpec((TM_OUT, D), row),
            pl.BlockSpec((1, 3, D), lambda i: (_tile_row(i, TM_OUT), 0, 0)),
            pl.BlockSpec((TM_OUT, 4 * D), lambda i: (i, 0)),
            pl.BlockSpec((TM_OUT, 4 * D), lambda i: (i, 1)),
            pl.BlockSpec((2, TM_OUT, D), _ctx_tile),
            pl.BlockSpec((2, TM_OUT, D), _smp_tile),
            pl.BlockSpec((TM_OUT, LANES), lambda i: (i, 1)),
            pl.BlockSpec((2, LANES, D), const3),
            pl.BlockSpec((2, 1, D), const3),
            pl.BlockSpec((1, D), const2),
            pl.BlockSpec((1, D), const2),
            pl.BlockSpec((3, D), const2),
            pl.BlockSpec((1, D), const2),
            pl.BlockSpec((1, D), const2),
            pl.BlockSpec((D, LANES), const2),
            pl.BlockSpec((LANES, D), const2),
            pl.BlockSpec((2 * D, D), const2),
        ],
        out_specs=pl.BlockSpec((TM_OUT, D), row),
        out_shape=jax.ShapeDtypeStruct((N_TOK, D), F32),
        compiler_params=pltpu.CompilerParams(
            dimension_semantics=("arbitrary",), vmem_limit_bytes=VMEM_LIMIT),
        name="even_out",
    )(x, mod3, p_all, p_all, o_ctx, o_smp, lr, a2pad, a0, k_a, r_k, conv_w, ln_w, ln_b, ind, bind, wo_bf)


def _gla_kernel(q_ref, k_ref, v_ref, lrg_ref, gk2_ref, gkb_ref, s0_ref, m_ref, o_ref, sout_ref, *, seq_len):
    z = pl.program_id(0)
    n_chunks = seq_len // CHUNK
    incl = m_ref[0]
    incl_b = incl > 0.5
    gk2 = gk2_ref[0]
    gkb = gkb_ref[0]
    scale = GLA_DK ** -0.5

    def body(ci, st):
        c = ci + z * (n_chunks - 1 - 2 * ci)
        t0 = pl.multiple_of(c * CHUNK, CHUNK)
        rows = pl.ds(t0, CHUNK)
        q = q_ref[rows, :] * scale
        k = k_ref[rows, :]
        v = v_ref[rows, :]
        logit = _dot(lrg_ref[rows, :], gk2, HI) + gkb
        g = jax.nn.log_sigmoid(logit) / GLA_GATE_NORM
        bc = _dot(incl, g, HI)
        b_last = jnp.sum(g, axis=0, keepdims=True)
        qe = q * jnp.exp(bc)
        ke = k * jnp.exp(-bc)
        kd = k * jnp.exp(b_last - bc)
        att = jnp.where(incl_b, _dot_nt(qe, ke, HI), 0.0)
        o = _dot(att, v, HI) + _dot_nt(qe, st, HI)
        o_ref[0, rows, :] = o
        return st * jnp.exp(b_last) + _dot_tn(v, kd, HI)

    st_fin = lax.fori_loop(0, n_chunks, body, s0_ref[0, 0, 0].T)
    sout_ref[0, 0, 0] = st_fin.T


def _gla_scan(p_all, lrg, gk2pad, gkb, s0, masks, *, n_seq, seq_len, tok_blk0):
    def tok(b):
        return tok_blk0 + b

    return pl.pallas_call(
        functools.partial(_gla_kernel, seq_len=seq_len),
        grid=(2, n_seq, GLA_H),
        in_specs=[
            pl.BlockSpec((seq_len, GLA_DK), lambda z, b, h: (tok(b), h)),
            pl.BlockSpec((seq_len, GLA_DK), lambda z, b, h: (tok(b), GLA_H + h)),
            pl.BlockSpec((seq_len, GLA_DV), lambda z, b, h: (tok(b), GLA_H + h)),
            pl.BlockSpec((seq_len, LANES), lambda z, b, h: (tok(b), 0)),
            pl.BlockSpec((1, LANES, GLA_DK), lambda z, b, h: (z, 0, h)),
            pl.BlockSpec((1, 1, GLA_DK), lambda z, b, h: (z, 0, h)),
            pl.BlockSpec((1, 1, 1, GLA_DK, GLA_DV), lambda z, b, h: (b, z, h, 0, 0)),
            pl.BlockSpec((1, CHUNK, CHUNK), lambda z, b, h: (z, 0, 0)),
        ],
        out_specs=[
            pl.BlockSpec((1, seq_len, GLA_DV), lambda z, b, h: (z, b, h)),
            pl.BlockSpec((1, 1, 1, GLA_DK, GLA_DV), lambda z, b, h: (b, z, h, 0, 0)),
        ],
        out_shape=[
            jax.ShapeDtypeStruct((2, n_seq * seq_len, D), F32),
            jax.ShapeDtypeStruct((n_seq, 2, GLA_H, GLA_DK, GLA_DV), F32),
        ],
        compiler_params=pltpu.CompilerParams(
            dimension_semantics=("arbitrary", "arbitrary", "arbitrary"), vmem_limit_bytes=VMEM_LIMIT),
        name="gla_scan_%d" % seq_len,
    )(p_all, p_all, p_all, lrg, gk2pad, gkb, s0, masks)


def _odd_out_kernel(x_ref, mod_ref, zg_ref, oc_ref, os_ref, gn_ref, ind_ref, bind_ref, wo_ref, fg_ref, out_ref):
    o = _sum_dirs(pl.program_id(0), oc_ref, os_ref)
    ms = _group_stat(o * o, ind_ref, bind_ref)
    on = o * lax.rsqrt(ms + NORM_EPS) * gn_ref[...]
    og = on * _silu(zg_ref[...])
    y = _dot(og.astype(BF16), wo_ref[...])
    gate = mod_ref[0, 2:3, :]
    x = x_ref[...] + gate * y
    out_ref[...] = x * lax.rsqrt(jnp.mean(x * x, axis=-1, keepdims=True) + NORM_EPS) * fg_ref[...]


def _odd_out(x, mod3, p_all, o_ctx, o_smp, g_norm, ind, bind, wo_bf, final_g):
    row = lambda i: (i, 0)
    const2 = lambda i: (0, 0)
    return pl.pallas_call(
        _odd_out_kernel,
        grid=(N_TOK // TM_OUT,),
        in_specs=[
            pl.BlockSpec((TM_OUT, D), row),
            pl.BlockSpec((1, 3, D), lambda i: (_tile_row(i, TM_OUT), 0, 0)),
            pl.BlockSpec((TM_OUT, D), lambda i: (i, 2)),
            pl.BlockSpec((2, TM_OUT, D), _ctx_tile),
            pl.BlockSpec((2, TM_OUT, D), _smp_tile),
            pl.BlockSpec((1, D), const2),
            pl.BlockSpec((D, LANES), const2),
            pl.BlockSpec((LANES, D), const2),
            pl.BlockSpec((D, D), const2),
            pl.BlockSpec((1, D), const2),
        ],
        out_specs=pl.BlockSpec((TM_OUT, D), row),
        out_shape=jax.ShapeDtypeStruct((N_TOK, D), F32),
        compiler_params=pltpu.CompilerParams(
            dimension_semantics=("arbitrary",), vmem_limit_bytes=VMEM_LIMIT),
        name="odd_out",
    )(x, mod3, p_all, o_ctx, o_smp, g_norm, ind, bind, wo_bf, final_g)


def _group_indicators(group):
    ch = jnp.arange(D) // group
    lane = jnp.arange(LANES)
    hit = (ch[:, None] == lane[None, :]).astype(F32)
    return hit / float(group), hit.T


def _pad_dirs(w):
    r = w.shape[1]
    out = jnp.zeros((2, LANES, w.shape[2]), F32)
    out = out.at[0, 0:r].set(w[0])
    out = out.at[1, r:2 * r].set(w[1])
    return out


def kernel(x_prompt, x_sample, state_wkv, state_gla, c, c_ctx, norm_g, ada_w, ada_b, final_g, e_w_in, e_w_out, conv_w, wkv_w0, wkv_w1, wkv_w2, wkv_a0, wkv_a1, wkv_a2, wkv_k_k, wkv_k_a, wkv_r_k, wkv_ln_w, wkv_ln_b, o_w_in, o_w_out, gla_gk1, gla_gk2, gla_gk_b, gla_g_norm):
    x = jnp.concatenate([x_prompt.reshape(N_CTX_TOK, D), x_sample.reshape(N_CTX_TOK, D)], axis=0)
    cvec = jnp.zeros((8, D), F32).at[0].set(c_ctx).at[1:1 + SMP_B].set(c)
    mod = _modulation(cvec, ada_w, ada_b)
    masks = _chunk_masks()

    mod3 = mod[0].reshape(8, 3, D)
    wlr = jnp.concatenate([wkv_w1[0, 0], wkv_w1[0, 1], wkv_a1[0, 0], wkv_a1[0, 1]], axis=1)
    p_e, lr_e = _inproj(x, mod3, norm_g[0:1], e_w_in[0].astype(BF16), wlr.astype(BF16), n_tanh=2 * 64)
    w2pad = _pad_dirs(wkv_w2[0])
    a2pad = _pad_dirs(wkv_a2[0])
    w0 = wkv_w0[0].reshape(2, 1, D)
    a0 = wkv_a0[0].reshape(2, 1, D)
    k_k = wkv_k_k[0].reshape(1, D)
    k_a = wkv_k_a[0].reshape(1, D)
    r_k = wkv_r_k[0].reshape(1, D)
    zero_wkv = jnp.zeros((CTX_B, 2, WKV_H, WKV_N, WKV_N), F32)
    o_ctx, new_wkv = _wkv_scan(p_e, lr_e, w2pad, a2pad, w0, a0, k_k, k_a, zero_wkv, masks,
                               n_seq=CTX_B, seq_len=CTX_T, tok_blk0=0)
    o_smp, _ = _wkv_scan(p_e, lr_e, w2pad, a2pad, w0, a0, k_k, k_a, state_wkv[:, 0], masks,
                         n_seq=SMP_B, seq_len=SMP_T, tok_blk0=N_CTX_TOK // SMP_T)
    ind64, bind64 = _group_indicators(WKV_N)
    x = _even_out(x, mod3, p_e, o_ctx, o_smp, lr_e, a2pad, a0, k_a, r_k, conv_w[0],
                  wkv_ln_w[0].reshape(1, D), wkv_ln_b[0].reshape(1, D), ind64, bind64,
                  e_w_out[0].astype(BF16))

    mod3 = mod[1].reshape(8, 3, D)
    rank = gla_gk1.shape[-1]
    wlr = jnp.zeros((D, LANES), F32).at[:, 0:rank].set(gla_gk1[0, 0]).at[:, rank:2 * rank].set(gla_gk1[0, 1])
    p_o, lr_o = _inproj(x, mod3, norm_g[1:2], o_w_in[0].astype(BF16), wlr.astype(BF16), n_tanh=0)
    gk2pad = _pad_dirs(gla_gk2[0])
    gkb = gla_gk_b[0].reshape(2, 1, GLA_H * GLA_DK)
    zero_gla = jnp.zeros((CTX_B, 2, GLA_H, GLA_DK, GLA_DV), F32)
    g_ctx, new_gla = _gla_scan(p_o, lr_o, gk2pad, gkb, zero_gla, masks,
                               n_seq=CTX_B, seq_len=CTX_T, tok_blk0=0)
    g_smp, _ = _gla_scan(p_o, lr_o, gk2pad, gkb, state_gla[:, 0], masks,
                         n_seq=SMP_B, seq_len=SMP_T, tok_blk0=N_CTX_TOK // SMP_T)
    ind256, bind256 = _group_indicators(GLA_DV)
    y = _odd_out(x, mod3, p_o, g_ctx, g_smp,jnp.tile(gla_g_norm[0], GLA_H).reshape(1, D), ind256, bind256,
                 o_w_out[0].astype(BF16), final_g.reshape(1, D))

    y_prompt = y[:N_CTX_TOK].reshape(CTX_B, CTX_T, D)
    y_sample = y[N_CTX_TOK:].reshape(SMP_B, SMP_T, D)
    return (y_prompt, y_sample, new_wkv[:, None], new_gla[:, None])
```

```python
import functools
import math

import jax
import jax.numpy as jnp
from jax import lax
from jax.experimental import pallas as pl
from jax.experimental.pallas import tpu as pltpu

F32 = jnp.float32
BF16 = jnp.bfloat16
HI = lax.Precision.HIGHEST

D = 1024
N_CTX_TOK = 16 * 256
N_TOK = 2 * N_CTX_TOK
CTX_B, CTX_T = 16, 256
SMP_B, SMP_T = 2, 2048
GRID_W = 64
WKV_H, WKV_N = 16, 64
GLA_H, GLA_DK, GLA_DV = 4, 128, 256
GLA_GATE_NORM = 16.0
GN_EPS = 64e-5
NORM_EPS = 1e-6
CHUNK = 64
WKV_GROUP = 2
LANES = 128
TM_PROJ = 1024
TN_PROJ = 1024
TM_OUT = 256
VMEM_LIMIT = 56 * 1024 * 1024


def _silu(x):
    return x * jax.nn.sigmoid(x)


def _dot(a, b, precision=None):
    return jnp.dot(a, b, preferred_element_type=F32, precision=precision)


def _dot_nt(a, b, precision=None):
    return lax.dot_general(a, b, (((1,), (1,)), ((), ())), preferred_element_type=F32, precision=precision)


def _dot_tn(a, b, precision=None):
    return lax.dot_general(a, b, (((0,), (0,)), ((), ())), preferred_element_type=F32, precision=precision)


def _bdot(a, b):
    return _dot(a.astype(BF16), b.astype(BF16))


def _bdot_nt(a, b):
    return _dot_nt(a.astype(BF16), b.astype(BF16))


def _bdot_tn(a, b):
    return _dot_tn(a.astype(BF16), b.astype(BF16))


def _split2(x):
    hi = x.astype(BF16)
    lo = (x - hi.astype(F32)).astype(BF16)
    return hi, lo


def _dot3(a, b):
    ah, al = _split2(a)
    bh, bl = _split2(b)
    return _dot(ah, bh) + _dot(al, bh) + _dot(ah, bl)


def _dot3_nt(a, b):
    ah, al = _split2(a)
    bh, bl = _split2(b)
    return _dot_nt(ah, bh) + _dot_nt(al, bh) + _dot_nt(ah, bl)


def _cumsum_rows(tri_bf, x):
    h1 = x.astype(BF16)
    r1 = x - h1.astype(F32)
    h2 = r1.astype(BF16)
    h3 = (r1 - h2.astype(F32)).astype(BF16)
    return _dot(tri_bf, h1) + _dot(tri_bf, h2) + _dot(tri_bf, h3)


def _tile_row(i, tile):
    ctx_tiles = N_CTX_TOK // tile
    per_req = SMP_T // tile
    return jnp.where(i < ctx_tiles, 0, 1 + (i - ctx_tiles) // per_req)


def _mod_kernel(c_ref, w_ref, b_ref, o_ref):
    cf = _silu(c_ref[...])
    o_ref[0] = _dot(cf, w_ref[0], HI) + b_ref[0]


def _modulation(cvec, ada_w, ada_b):
    depth = ada_w.shape[0]
    return pl.pallas_call(
        _mod_kernel,
        grid=(depth, 3),
        in_specs=[
            pl.BlockSpec((8, D), lambda l, j: (0, 0)),
            pl.BlockSpec((1, D, D), lambda l, j: (l, 0, j)),
            pl.BlockSpec((1, 1, D), lambda l, j: (l, 0, j)),
        ],
        out_specs=pl.BlockSpec((1, 8, D), lambda l, j: (l, 0, j)),
        out_shape=jax.ShapeDtypeStruct((depth, 8, 3 * D), F32),
        compiler_params=pltpu.CompilerParams(
            dimension_semantics=("arbitrary", "arbitrary"), vmem_limit_bytes=VMEM_LIMIT),
        name="adaln_mod",
    )(cvec, ada_w, ada_b.reshape(depth, 1, 3 * D))


def _inproj_kernel(x_ref, mod_ref, g_ref, w_ref, wlr_ref, p_ref, lr_ref, h_scr, *, n_tanh):
    j = pl.program_id(1)

    @pl.when(j == 0)
    def _():
        x = x_ref[...]
        y = x * lax.rsqrt(jnp.mean(x * x, axis=-1, keepdims=True) + NORM_EPS) * g_ref[...]
        shift = mod_ref[0, 0:1, :]
        scale = mod_ref[0, 1:2, :]
        h = y * (1.0 + scale) + shift
        hb = h.astype(BF16)
        h_scr[...] = hb
        lr = _dot(hb, wlr_ref[...])
        if n_tanh:
            lane = lax.broadcasted_iota(jnp.int32, lr.shape, 1)
            lr = jnp.where(lane < n_tanh, jnp.tanh(lr), lr)
        lr_ref[...] = lr

    p_ref[...] = _dot(h_scr[...], w_ref[...])


def _inproj(x, mod3, g, w_bf, wlr_bf, n_tanh):
    n_out = w_bf.shape[1]
    n_lr = wlr_bf.shape[1]
    return pl.pallas_call(
        functools.partial(_inproj_kernel, n_tanh=n_tanh),
        grid=(N_TOK // TM_PROJ, n_out // TN_PROJ),
        in_specs=[
            pl.BlockSpec((TM_PROJ, D), lambda i, j: (i, 0)),
            pl.BlockSpec((1, 3, D), lambda i, j: (_tile_row(i, TM_PROJ), 0, 0)),
            pl.BlockSpec((1, D), lambda i, j: (0, 0)),
            pl.BlockSpec((D, TN_PROJ), lambda i, j: (0, j)),
            pl.BlockSpec((D, n_lr), lambda i, j: (0, 0)),
        ],
        out_specs=[
            pl.BlockSpec((TM_PROJ, TN_PROJ), lambda i, j: (i, j)),
            pl.BlockSpec((TM_PROJ, n_lr), lambda i, j: (i, 0)),
        ],
        out_shape=[
            jax.ShapeDtypeStruct((N_TOK, n_out), F32),
            jax.ShapeDtypeStruct((N_TOK, n_lr), F32),
        ],
        scratch_shapes=[pltpu.VMEM((TM_PROJ, D), BF16)],
        compiler_params=pltpu.CompilerParams(
            dimension_semantics=("arbitrary", "arbitrary"), vmem_limit_bytes=VMEM_LIMIT),
        name="inproj",
    )(x, mod3, g, w_bf, wlr_bf)


def _chunk_masks():
    idx = jnp.arange(CHUNK)
    lower = (idx[:, None] >= idx[None, :]).astype(F32)
    return jnp.stack([lower, lower.T], axis=0)


def _inv_unit_tri(lmats):
    row = lax.broadcasted_iota(jnp.int32, (CHUNK, CHUNK), 0)
    col = lax.broadcasted_iota(jnp.int32, (CHUNK, CHUNK), 1)
    eye = (row == col).astype(F32)

    def same_block(n):
        sh = n.bit_length() - 1
        return lax.shift_right_logical(row, sh) == lax.shift_right_logical(col, sh)

    blk16 = same_block(16)
    l16 = [jnp.where(blk16, m, 0.0) for m in lmats]
    x = [eye - m for m in l16]
    p = [_bdot(m, m) for m in l16]
    for step in range(3):
        x = [xi + _bdot(xi, pi) for xi, pi in zip(x, p)]
        if step < 2:
            p = [_bdot(pi, pi) for pi in p]
    n = 16
    while n < CHUNK:
        off = same_block(2 * n) & jnp.logical_not(same_block(n))
        xl = [_bdot(xi, jnp.where(off, m, 0.0)) for xi, m in zip(x, lmats)]
        x = [xi - _bdot(yi, xi) for xi, yi in zip(x, xl)]
        n *= 2
    return x


def _wkv_kernel(r_ref, k_ref, v_ref, lrw_ref, lra_ref, w2_ref, a2_ref, w0_ref, a0_ref,
                kk_ref, ka_ref, s0_ref, m_ref, o_ref, sout_ref,
                rp_scr, op_scr, w_scr, h_scr, d_scr, *, seq_len):
    z = pl.program_id(0)
    n_chunks = seq_len // CHUNK
    incl = m_ref[0]
    incl_bf = incl.astype(BF16)
    row = lax.broadcasted_iota(jnp.int32, (CHUNK, CHUNK), 0)
    col = lax.broadcasted_iota(jnp.int32, (CHUNK, CHUNK), 1)
    incl_b = incl > 0.5
    strict_b = incl_b & (row != col)
    w2 = w2_ref[0].astype(BF16)
    a2 = a2_ref[0].astype(BF16)
    w0 = w0_ref[0]
    a0 = a0_ref[0]
    k_k = kk_ref[...]
    k_a = ka_ref[...]

    def phase1(it, carry):
        cs = [it * WKV_GROUP + j for j in range(WKV_GROUP)]
        rows = [pl.ds(pl.multiple_of(c * CHUNK, CHUNK), CHUNK) for c in cs]
        r = [r_ref[rw, :] for rw in rows]
        k = [k_ref[rw, :] for rw in rows]
        v = [v_ref[rw, :] for rw in rows]
        w_raw = [w0 + _bdot(lrw_ref[rw, :], w2) for rw in rows]
        a = [jax.nn.sigmoid(a0 + _bdot(lra_ref[rw, :], a2)) for rw in rows]
        lw = [-math.exp(-0.5) * jax.nn.sigmoid(x) for x in w_raw]
        g = [_cumsum_rows(incl_bf, x) for x in lw]
        gtot = [jnp.sum(x, axis=0, keepdims=True) for x in lw]
        kt, rt, kh, bh, kp, bp, vh = [], [], [], [], [], [], []
        for j in range(WKV_GROUP):
            kkr = k[j] * k_k
            kmod = k[j] * (1.0 + (a[j] - 1.0) * k_a)
            e_g = jnp.exp(g[j])
            e_gx = jnp.exp(g[j] - lw[j])
            e_ng = jnp.exp(-g[j])
            e_gc = jnp.exp(gtot[j] - g[j])
            for h in range(2):
                sl = slice(h * WKV_N, (h + 1) * WKV_N)
                kk_h = kkr[:, sl]
                nrm = jnp.sqrt(jnp.sum(kk_h * kk_h, axis=-1, keepdims=True))
                kk_h = kk_h / jnp.maximum(nrm, 1e-12)
                b_h = kk_h * a[j][:, sl]
                kt.append(kk_h * e_gx[:, sl])
                rt.append(r[j][:, sl] * e_g[:, sl])
                kh.append(kmod[:, sl] * e_ng[:, sl])
                bh.append(b_h * e_ng[:, sl])
                kp.append(kmod[:, sl] * e_gc[:, sl])
                bp.append(b_h * e_gc[:, sl])
                vh.append(v[j][:, sl])
        units = range(2 * WKV_GROUP)
        lmat = [jnp.where(strict_b, _bdot_nt(kt[u], bh[u]), 0.0) for u in units]
        mkk = [jnp.where(strict_b, _bdot_nt(kt[u], kh[u]), 0.0) for u in units]
        mrk = [jnp.where(incl_b, _bdot_nt(rt[u], kh[u]), 0.0) for u in units]
        mrb = [jnp.where(incl_b, _bdot_nt(rt[u], bh[u]), 0.0) for u in units]
        tinv = _inv_unit_tri(lmat)
        mv = [_bdot(mkk[u], vh[u]) for u in units]
        mrkv = [_bdot(mrk[u], vh[u]) for u in units]
        vtk = [_bdot_tn(vh[u], kp[u]) for u in units]
        p = [_bdot(tinv[u], kt[u]) for u in units]
        q = [_bdot(tinv[u], mv[u]) for u in units]
        rp = [rt[u] - _bdot(mrb[u], p[u]) for u in units]
        op = [mrkv[u] - _bdot(mrb[u], q[u]) for u in units]
        wm = [_bdot_tn(p[u], bp[u]) for u in units]
        hm = [vtk[u] - _bdot_tn(q[u], bp[u]) for u in units]
        for j in range(WKV_GROUP):
            d_scr[cs[j]] = jnp.broadcast_to(jnp.exp(gtot[j]), (8, LANES))
            for h in range(2):
                u = 2 * j + h
                rp_scr[cs[j], h] = rp[u]
                op_scr[cs[j], h] = op[u]
                w_scr[cs[j], h] = wm[u]
                h_scr[cs[j], h] = hm[u]
        return carry

    lax.fori_loop(0, n_chunks // WKV_GROUP, phase1, 0)

    def phase2(ci, carry):
        c = ci + z * (n_chunks - 1 - 2 * ci)
        rows = pl.ds(pl.multiple_of(c * CHUNK, CHUNK), CHUNK)
        d_tot = d_scr[c][0:1, :]
        outs = []
        new_s = []
        for h in range(2):
            sl = slice(h * WKV_N, (h + 1) * WKV_N)
            s = carry[h]
            outs.append(_dot3_nt(rp_scr[c, h], s) + op_scr[c, h])
            new_s.append(s * d_tot[:, sl] - _dot3(s, w_scr[c, h]) + h_scr[c, h])
        o_ref[0, rows, :] = jnp.concatenate(outs, axis=-1)
        return tuple(new_s)

    s_fin = lax.fori_loop(0, n_chunks, phase2, (s0_ref[0, 0, 0], s0_ref[0, 0, 1]))
    sout_ref[0, 0, 0] = s_fin[0]
    sout_ref[0, 0, 1] = s_fin[1]


def _wkv_scan(p_all, lr, w2pad, a2pad, w0, a0, k_k, k_a, s0, masks, *, n_seq, seq_len, tok_blk0):
    def tok(b):
        return tok_blk0 + b

    col = lambda base: (lambda z, b, hp: (tok(b), base + hp))
    vec = lambda z, b, hp: (z, 0, hp)
    n_chunks = seq_len // CHUNK
    return pl.pallas_call(
        functools.partial(_wkv_kernel, seq_len=seq_len),
        grid=(2, n_seq, WKV_H // 2),
        in_specs=[
            pl.BlockSpec((seq_len, LANES), col(32)),
            pl.BlockSpec((seq_len, LANES), col(40)),
            pl.BlockSpec((seq_len, LANES), col(48)),
            pl.BlockSpec((seq_len, LANES), lambda z, b, hp: (tok(b), 0)),
            pl.BlockSpec((seq_len, LANES), lambda z, b, hp: (tok(b), 1)),
            pl.BlockSpec((1, LANES, LANES), vec),
            pl.BlockSpec((1, LANES, LANES), vec),
            pl.BlockSpec((1, 1, LANES), vec),
            pl.BlockSpec((1, 1, LANES), vec),
            pl.BlockSpec((1, LANES), lambda z, b, hp: (0, hp)),
            pl.BlockSpec((1, LANES), lambda z, b, hp: (0, hp)),
            pl.BlockSpec((1, 1, 2, WKV_N, WKV_N), lambda z, b, hp: (b, z, hp, 0, 0)),
            pl.BlockSpec((1, CHUNK, CHUNK), lambda z, b, hp: (z, 0, 0)),
        ],
        out_specs=[
            pl.BlockSpec((1, seq_len, LANES), lambda z, b, hp: (z, b, hp)),
            pl.BlockSpec((1, 1, 2, WKV_N, WKV_N), lambda z, b, hp: (b, z, hp, 0, 0)),
        ],
        out_shape=[
            jax.ShapeDtypeStruct((2, n_seq * seq_len, D), F32),
            jax.ShapeDtypeStruct((n_seq, 2, WKV_H, WKV_N, WKV_N), F32),
        ],
        scratch_shapes=[pltpu.VMEM((n_chunks, 2, CHUNK, WKV_N), F32)] * 4
        + [pltpu.VMEM((n_chunks, 8, LANES), F32)],
        compiler_params=pltpu.CompilerParams(
            dimension_semantics=("arbitrary", "arbitrary", "arbitrary"), vmem_limit_bytes=VMEM_LIMIT),
        name="wkv_scan_%d" % seq_len,
    )(p_all, p_all, p_all, lr, lr, w2pad, a2pad, w0, a0, k_k, k_a, s0, masks)


def _dot2(x, w_bf):
    hi, lo = _split2(x)
    return _dot(hi, w_bf) + _dot(lo, w_bf)


def _group_stat(x, ind_ref, bind_ref):
    return _dot2(_dot2(x, ind_ref[...]), bind_ref[...])


N_CTX_TILES = N_CTX_TOK // TM_OUT


def _sum_dirs(i, oc_ref, os_ref):
    is_ctx = i < N_CTX_TILES
    return jnp.where(is_ctx, oc_ref[0], os_ref[0]) + jnp.where(is_ctx, oc_ref[1], os_ref[1])


def _ctx_tile(i):
    return (0, jnp.minimum(i, N_CTX_TILES - 1), 0)


def _smp_tile(i):
    return (0, jnp.maximum(i - N_CTX_TILES, 0), 0)


def _even_out_kernel(x_ref, mod_ref, pc_ref, pw_ref, oc_ref, os_ref, lra_ref, a2_ref, a0_ref, ka_ref, rk_ref,
                     cw_ref, lnw_ref, lnb_ref, ind_ref, bind_ref, wo_ref, out_ref):
    i = pl.program_id(0)
    u = pc_ref[:, 0:D]
    gb = pc_ref[:, D:2 * D]
    gc = pc_ref[:, 2 * D:3 * D]
    zc = pc_ref[:, 3 * D:4 * D]
    xg = gc * u
    row_len = jnp.where(i < N_CTX_TILES, CTX_T, GRID_W)
    pos = lax.broadcasted_iota(jnp.int32, (TM_OUT, 1), 0) & (row_len - 1)
    prev = jnp.where(pos == 0, 0.0, pltpu.roll(xg, 1, 0))
    nxt = jnp.where(pos == row_len - 1, 0.0, pltpu.roll(xg, TM_OUT - 1, 0))
    conv = cw_ref[0:1, :] * prev + cw_ref[1:2, :] * xg + cw_ref[2:3, :] * nxt
    o_conv = _silu(zc) * gb * conv
    r = pw_ref[:, 0:D]
    k = pw_ref[:, D:2 * D]
    v = pw_ref[:, 2 * D:3 * D]
    zw = pw_ref[:, 3 * D:4 * D]
    o = _sum_dirs(i, oc_ref, os_ref)
    mu = _group_stat(o, ind_ref, bind_ref)
    dlt = o - mu
    var = _group_stat(dlt * dlt, ind_ref, bind_ref)
    gn = dlt * lax.rsqrt(var + GN_EPS) * lnw_ref[...] + lnb_ref[...]
    lra = lra_ref[...]
    ic0 = jax.nn.sigmoid(a0_ref[0] + _bdot(lra, a2_ref[0]))
    ic1 = jax.nn.sigmoid(a0_ref[1] + _bdot(lra, a2_ref[1]))
    ksum = k * (2.0 + (ic0 + ic1 - 2.0) * ka_ref[...])
    bonus = _group_stat(r * ksum * rk_ref[...], ind_ref, bind_ref) * float(WKV_N)
    o_wkv = (gn + bonus * v) * _silu(zw)
    y = _dot(o_conv.astype(BF16), wo_ref[0:D, :]) + _dot(o_wkv.astype(BF16), wo_ref[D:2 * D, :])
    gate = mod_ref[0, 2:3, :]
    out_ref[...] = x_ref[...] + gate * y


def _even_out(x, mod3, p_all, o_ctx, o_smp, lr, a2pad, a0, k_a, r_k, conv_w, ln_w, ln_b, ind, bind, wo_bf):
    row = lambda i: (i, 0)
    const2 = lambda i: (0, 0)
    const3 = lambda i: (0, 0, 0)
    return pl.pallas_call(
        _even_out_kernel,
        grid=(N_TOK // TM_OUT,),
        in_specs=[
            pl.BlockSpec((TM_OUT, D), row),
            pl.BlockSpec((1, 3, D), lambda i: (_tile_row(i, TM_OUT), 0, 0)),
            pl.BlockSpec((TM_OUT, 4 * D), lambda i: (i, 0)),
            pl.BlockSpec((TM_OUT, 4 * D), lambda i: (i, 1)),
            pl.BlockSpec((2, TM_OUT, D), _ctx_tile),
            pl.BlockSpec((2, TM_OUT, D), _smp_tile),
            pl.BlockSpec((TM_OUT, LANES), lambda i: (i, 1)),
            pl.BlockSpec((2, LANES, D), const3),
            pl.BlockSpec((2, 1, D), const3),
            pl.BlockSpec((1, D), const2),
            pl.BlockSpec((1, D), const2),
            pl.BlockSpec((3, D), const2),
            pl.BlockSpec((1, D), const2),
            pl.BlockSpec((1, D), const2),
            pl.BlockSpec((D, LANES), const2),
            pl.BlockSpec((LANES, D), const2),
            pl.BlockSpec((2 * D, D), const2),
        ],
        out_specs=pl.BlockSpec((TM_OUT, D), row),
        out_shape=jax.ShapeDtypeStruct((N_TOK, D), F32),
        compiler_params=pltpu.CompilerParams(
            dimension_semantics=("arbitrary",), vmem_limit_bytes=VMEM_LIMIT),
        name="even_out",
    )(x, mod3, p_all, p_all, o_ctx, o_smp, lr, a2pad, a0, k_a, r_k, conv_w, ln_w, ln_b, ind, bind, wo_bf)


def _gla_kernel(q_ref, k_ref, v_ref, lrg_ref, gk2_ref, gkb_ref, s0_ref, m_ref, o_ref, sout_ref, *, seq_len):
    z = pl.program_id(0)
    n_chunks = seq_len // CHUNK
    incl = m_ref[0]
    incl_bf = incl.astype(BF16)
    incl_b = incl > 0.5
    gk2 = gk2_ref[0].astype(BF16)
    gkb = gkb_ref[0]
    scale = GLA_DK ** -0.5

    def body(ci, st):
        c = ci + z * (n_chunks - 1 - 2 * ci)
        t0 = pl.multiple_of(c * CHUNK, CHUNK)
        rows = pl.ds(t0, CHUNK)
        q = q_ref[rows, :] * scale
        k = k_ref[rows, :]
        v = v_ref[rows, :]
        logit = _bdot(lrg_ref[rows, :], gk2) + gkb
        g = jax.nn.log_sigmoid(logit) / GLA_GATE_NORM
        bc = _cumsum_rows(incl_bf, g)
        b_last = jnp.sum(g, axis=0, keepdims=True)
        qe = q * jnp.exp(bc)
        ke = k * jnp.exp(-bc)
        kd = k * jnp.exp(b_last - bc)
        att = jnp.where(incl_b, _bdot_nt(qe, ke), 0.0)
        o = _bdot(att, v) + _bdot_nt(qe, st)
        o_ref[0, rows, :] = o
        return st * jnp.exp(b_last) + _bdot_tn(v, kd)

    st_fin = lax.fori_loop(0, n_chunks, body, s0_ref[0, 0, 0].T)
    sout_ref[0, 0, 0] = st_fin.T


def _gla_scan(p_all, lrg, gk2pad, gkb, s0, masks, *, n_seq, seq_len, tok_blk0):
    def tok(b):
        return tok_blk0 + b

    return pl.pallas_call(
        functools.partial(_gla_kernel, seq_len=seq_len),
        grid=(2, n_seq, GLA_H),
        in_specs=[
            pl.BlockSpec((seq_len, GLA_DK), lambda z, b, h: (tok(b), h)),
            pl.BlockSpec((seq_len, GLA_DK), lambda z, b, h: (tok(b), GLA_H + h)),
            pl.BlockSpec((seq_len, GLA_DV), lambda z, b, h: (tok(b), GLA_H + h)),
            pl.BlockSpec((seq_len, LANES), lambda z, b, h: (tok(b), 0)),
            pl.BlockSpec((1, LANES, GLA_DK), lambda z, b, h: (z, 0, h)),
            pl.BlockSpec((1, 1, GLA_DK), lambda z, b, h: (z, 0, h)),
            pl.BlockSpec((1, 1, 1, GLA_DK, GLA_DV), lambda z, b, h: (b, z, h, 0, 0)),
            pl.BlockSpec((1, CHUNK, CHUNK), lambda z, b, h: (z, 0, 0)),
        ],
        out_specs=[
            pl.BlockSpec((1, seq_len, GLA_DV), lambda z, b, h: (z, b, h)),
            pl.BlockSpec((1, 1, 1, GLA_DK, GLA_DV), lambda z, b, h: (b, z, h, 0, 0)),
        ],
        out_shape=[
            jax.ShapeDtypeStruct((2, n_seq * seq_len, D), F32),
            jax.ShapeDtypeStruct((n_seq, 2, GLA_H, GLA_DK, GLA_DV), F32),
        ],
        compiler_params=pltpu.CompilerParams(
            dimension_semantics=("arbitrary", "arbitrary", "arbitrary"), vmem_limit_bytes=VMEM_LIMIT),
        name="gla_scan_%d" % seq_len,
    )(p_all, p_all, p_all, lrg, gk2pad, gkb, s0, masks)


def _odd_out_kernel(x_ref, mod_ref, zg_ref, oc_ref, os_ref, gn_ref, ind_ref, bind_ref, wo_ref, fg_ref, out_ref):
    o = _sum_dirs(pl.program_id(0), oc_ref, os_ref)
    ms = _group_stat(o * o, ind_ref, bind_ref)
    on = o * lax.rsqrt(ms + NORM_EPS) * gn_ref[...]
    og = on * _silu(zg_ref[...])
    y = _dot(og.astype(BF16), wo_ref[...])
    gate = mod_ref[0, 2:3, :]
    x = x_ref[...] + gate * y
    out_ref[...] = x * lax.rsqrt(jnp.mean(x * x, axis=-1, keepdims=True) + NORM_EPS) * fg_ref[...]


def _odd_out(x, mod3, p_all, o_ctx, o_smp, g_norm, ind, bind, wo_bf, final_g):
    row = lambda i: (i, 0)
    const2 = lambda i: (0, 0)
    return pl.pallas_call(
        _odd_out_kernel,
        grid=(N_TOK // TM_OUT,),
        in_specs=[
            pl.BlockSpec((TM_OUT, D), row),
            pl.BlockSpec((1, 3, D), lambda i: (_tile_row(i, TM_OUT), 0, 0)),
            pl.BlockSpec((TM_OUT, D), lambda i: (i, 2)),
            pl.BlockSpec((2, TM_OUT, D), _ctx_tile),
            pl.BlockSpec((2, TM_OUT, D), _smp_tile),
            pl.BlockSpec((1, D), const2),
            pl.BlockSpec((D, LANES), const2),
            pl.BlockSpec((LANES, D), const2),
            pl.BlockSpec((D, D), const2),
            pl.BlockSpec((1, D), const2),
        ],
        out_specs=pl.BlockSpec((TM_OUT, D), row),
        out_shape=jax.ShapeDtypeStruct((N_TOK, D), F32),
        compiler_params=pltpu.CompilerParams(
            dimension_semantics=("arbitrary",), vmem_limit_bytes=VMEM_LIMIT),
        name="odd_out",
    )(x, mod3, p_all, o_ctx, o_smp, g_norm, ind, bind, wo_bf, final_g)


def _group_indicators(group):
    ch = jnp.arange(D) // group
    lane = jnp.arange(LANES)
    hit = (ch[:, None] == lane[None, :]).astype(F32)
    return (hit / float(group)).astype(BF16), hit.T.astype(BF16)


def _pad_dirs(w):
    r = w.shape[1]
    out = jnp.zeros((2, LANES, w.shape[2]), F32)
    out = out.at[0, 0:r].set(w[0])
    out = out.at[1, r:2 * r].set(w[1])
    return out


def kernel(x_prompt, x_sample, state_wkv, state_gla, c, c_ctx, norm_g, ada_w, ada_b, final_g, e_w_in, e_w_out, conv_w, wkv_w0, wkv_w1, wkv_w2, wkv_a0, wkv_a1, wkv_a2, wkv_k_k, wkv_k_a, wkv_r_k, wkv_ln_w, wkv_ln_b, o_w_in, o_w_out, gla_gk1, gla_gk2, gla_gk_b, gla_g_norm):
    x = jnp.concatenate([x_prompt.reshape(N_CTX_TOK, D), x_sample.reshape(N_CTX_TOK, D)], axis=0)
    cvec = jnp.zeros((8, D), F32).at[0].set(c_ctx).at[1:1 + SMP_B].set(c)
    mod = _modulation(cvec, ada_w, ada_b)
    masks = _chunk_masks()

    mod3 = mod[0].reshape(8, 3, D)
    wlr = jnp.concatenate([wkv_w1[0, 0], wkv_w1[0, 1], wkv_a1[0, 0], wkv_a1[0, 1]], axis=1)
    p_e, lr_e = _inproj(x, mod3, norm_g[0:1], e_w_in[0].astype(BF16), wlr.astype(BF16), n_tanh=2 * 64)
    w2pad = _pad_dirs(wkv_w2[0])
    a2pad = _pad_dirs(wkv_a2[0])
    w0 = wkv_w0[0].reshape(2, 1, D)
    a0 = wkv_a0[0].reshape(2, 1, D)
    k_k = wkv_k_k[0].reshape(1, D)
    k_a = wkv_k_a[0].reshape(1, D)
    r_k = wkv_r_k[0].reshape(1, D)
    zero_wkv = jnp.zeros((CTX_B, 2, WKV_H, WKV_N, WKV_N), F32)
    o_ctx, new_wkv = _wkv_scan(p_e, lr_e, w2pad, a2pad, w0, a0, k_k, k_a, zero_wkv, masks,
                               n_seq=CTX_B, seq_len=CTX_T, tok_blk0=0)
    o_smp, _ = _wkv_scan(p_e, lr_e, w2pad, a2pad, w0, a0, k_k, k_a, state_wkv[:, 0], masks,
                         n_seq=SMP_B, seq_len=SMP_T, tok_blk0=N_CTX_TOK // SMP_T)
    ind64, bind64 = _group_indicators(WKV_N)
    x = _even_out(x, mod3, p_e, o_ctx, o_smp, lr_e, a2pad, a0, k_a, r_k, conv_w[0],
                  wkv_ln_w[0].reshape(1, D), wkv_ln_b[0].reshape(1, D), ind64, bind64,
                  e_w_out[0].astype(BF16))

    mod3 = mod[1].reshape(8, 3, D)
    rank = gla_gk1.shape[-1]
    wlr = jnp.zeros((D, LANES), F32).at[:, 0:rank].set(gla_gk1[0, 0]).at[:, rank:2 * rank].set(gla_gk1[0, 1])
    p_o, lr_o = _inproj(x, mod3, norm_g[1:2], o_w_in[0].astype(BF16), wlr.astype(BF16), n_tanh=0)
    gk2pad = _pad_dirs(gla_gk2[0])
    gkb = gla_gk_b[0].reshape(2, 1, GLA_H * GLA_DK)
    zero_gla = jnp.zeros((CTX_B, 2, GLA_H, GLA_DK, GLA_DV), F32)
    g_ctx, new_gla = _gla_scan(p_o, lr_o, gk2pad, gkb, zero_gla, masks,
                               n_seq=CTX_B, seq_len=CTX_T, tok_blk0=0)
    g_smp, _ = _gla_scan(p_o, lr_o, gk2pad, gkb, state_gla[:, 0], masks,
                         n_seq=SMP_B, seq_len=SMP_T, tok_blk0=N_CTX_TOK // SMP_T)
    ind256, bind256 = _group_indicators(GLA_DV)
    y = _odd_out(x, mod3, p_o, g_ctx, g_smp,jnp.tile(gla_g_norm[0], GLA_H).reshape(1, D), ind256, bind256,
                 o_w_out[0].astype(BF16), final_g.reshape(1, D))

    y_prompt = y[:N_CTX_TOK].reshape(CTX_B, CTX_T, D)
    y_sample = y[N_CTX_TOK:].reshape(SMP_B, SMP_T, D)
    return (y_prompt, y_sample, new_wkv[:, None], new_gla[:, None])
```

```python
import functools
import math

import jax
import jax.numpy as jnp
from jax import lax
from jax.experimental import pallas as pl
from jax.experimental.pallas import tpu as pltpu

F32 = jnp.float32
BF16 = jnp.bfloat16
HI = lax.Precision.HIGHEST

D = 1024
N_CTX_TOK = 16 * 256
N_TOK = 2 * N_CTX_TOK
CTX_B, CTX_T = 16, 256
SMP_B, SMP_T = 2, 2048
GRID_W = 64
WKV_H, WKV_N = 16, 64
GLA_H, GLA_DK, GLA_DV = 4, 128, 256
GLA_GATE_NORM = 16.0
GN_EPS = 64e-5
NORM_EPS = 1e-6
CHUNK = 64
WKV_GROUP = 2
WKV_HEADS_STEP = 4
GLA_GROUP = 4
LANES = 128
TM_PROJ = 1024
TN_PROJ = 1024
TM_OUT = 256
VMEM_LIMIT = 56 * 1024 * 1024


def _silu(x):
    return x * jax.nn.sigmoid(x)


def _dot(a, b, precision=None):
    return jnp.dot(a, b, preferred_element_type=F32, precision=precision)


def _dot_nt(a, b, precision=None):
    return lax.dot_general(a, b, (((1,), (1,)), ((), ())), preferred_element_type=F32, precision=precision)


def _dot_tn(a, b, precision=None):
    return lax.dot_general(a, b, (((0,), (0,)), ((), ())), preferred_element_type=F32, precision=precision)


def _bdot(a, b):
    return _dot(a.astype(BF16), b.astype(BF16))


def _bdot_nt(a, b):
    return _dot_nt(a.astype(BF16), b.astype(BF16))


def _bdot_tn(a, b):
    return _dot_tn(a.astype(BF16), b.astype(BF16))


def _split2(x):
    hi = x.astype(BF16)
    lo = (x - hi.astype(F32)).astype(BF16)
    return hi, lo


def _dot3(a, b):
    ah, al = _split2(a)
    bh, bl = _split2(b)
    return _dot(ah, bh) + _dot(al, bh) + _dot(ah, bl)


def _dot3_nt(a, b):
    ah, al = _split2(a)
    bh, bl = _split2(b)
    return _dot_nt(ah, bh) + _dot_nt(al, bh) + _dot_nt(ah, bl)


def _cumsum_rows(tri_bf, x):
    h1 = x.astype(BF16)
    r1 = x - h1.astype(F32)
    h2 = r1.astype(BF16)
    h3 = (r1 - h2.astype(F32)).astype(BF16)
    return _dot(tri_bf, h1) + _dot(tri_bf, h2) + _dot(tri_bf, h3)


def _tile_row(i, tile):
    ctx_tiles = N_CTX_TOK // tile
    per_req = SMP_T // tile
    return jnp.where(i < ctx_tiles, 0, 1 + (i - ctx_tiles) // per_req)


def _mod_kernel(c_ref, w_ref, b_ref, o_ref):
    cf = _silu(c_ref[...])
    o_ref[0] = _dot(cf, w_ref[0], HI) + b_ref[0]


def _modulation(cvec, ada_w, ada_b):
    depth = ada_w.shape[0]
    return pl.pallas_call(
        _mod_kernel,
        grid=(depth, 3),
        in_specs=[
            pl.BlockSpec((8, D), lambda l, j: (0, 0)),
            pl.BlockSpec((1, D, D), lambda l, j: (l, 0, j)),
            pl.BlockSpec((1, 1, D), lambda l, j: (l, 0, j)),
        ],
        out_specs=pl.BlockSpec((1, 8, D), lambda l, j: (l, 0, j)),
        out_shape=jax.ShapeDtypeStruct((depth, 8, 3 * D), F32),
        compiler_params=pltpu.CompilerParams(
            dimension_semantics=("arbitrary", "arbitrary"), vmem_limit_bytes=VMEM_LIMIT),
        name="adaln_mod",
    )(cvec, ada_w, ada_b.reshape(depth, 1, 3 * D))


def _inproj_kernel(x_ref, mod_ref, g_ref, w_ref, wlr_ref, p_ref, lr_ref, h_scr, *, n_tanh):
    j = pl.program_id(1)

    @pl.when(j == 0)
    def _():
        x = x_ref[...]
        y = x * lax.rsqrt(jnp.mean(x * x, axis=-1, keepdims=True) + NORM_EPS) * g_ref[...]
        shift = mod_ref[0, 0:1, :]
        scale = mod_ref[0, 1:2, :]
        h = y * (1.0 + scale) + shift
        hb = h.astype(BF16)
        h_scr[...] = hb
        lr = _dot(hb, wlr_ref[...])
        if n_tanh:
            lane = lax.broadcasted_iota(jnp.int32, lr.shape, 1)
            lr = jnp.where(lane < n_tanh, jnp.tanh(lr), lr)
        lr_ref[...] = lr

    p_ref[...] = _dot(h_scr[...], w_ref[...])


def _inproj(x, mod3, g, w_bf, wlr_bf, n_tanh):
    n_out = w_bf.shape[1]
    n_lr = wlr_bf.shape[1]
    return pl.pallas_call(
        functools.partial(_inproj_kernel, n_tanh=n_tanh),
        grid=(N_TOK // TM_PROJ, n_out // TN_PROJ),
        in_specs=[
            pl.BlockSpec((TM_PROJ, D), lambda i, j: (i, 0)),
            pl.BlockSpec((1, 3, D), lambda i, j: (_tile_row(i, TM_PROJ), 0, 0)),
            pl.BlockSpec((1, D), lambda i, j: (0, 0)),
            pl.BlockSpec((D, TN_PROJ), lambda i, j: (0, j)),
            pl.BlockSpec((D, n_lr), lambda i, j: (0, 0)),
        ],
        out_specs=[
            pl.BlockSpec((TM_PROJ, TN_PROJ), lambda i, j: (i, j)),
            pl.BlockSpec((TM_PROJ, n_lr), lambda i, j: (i, 0)),
        ],
        out_shape=[
            jax.ShapeDtypeStruct((N_TOK, n_out), F32),
            jax.ShapeDtypeStruct((N_TOK, n_lr), F32),
        ],
        scratch_shapes=[pltpu.VMEM((TM_PROJ, D), BF16)],
        compiler_params=pltpu.CompilerParams(
            dimension_semantics=("arbitrary", "arbitrary"), vmem_limit_bytes=VMEM_LIMIT),
        name="inproj",
    )(x, mod3, g, w_bf, wlr_bf)


def _chunk_masks():
    idx = jnp.arange(CHUNK)
    lower = (idx[:, None] >= idx[None, :]).astype(F32)
    return jnp.stack([lower, lower.T], axis=0)


def _chunk_rows(c):
    return pl.ds(pl.multiple_of(c * CHUNK, CHUNK), CHUNK)


def _inv_unit_tri(lmats):
    row = lax.broadcasted_iota(jnp.int32, (CHUNK, CHUNK), 0)
    col = lax.broadcasted_iota(jnp.int32, (CHUNK, CHUNK), 1)
    eye = (row == col).astype(F32)

    def same_block(n):
        sh = n.bit_length() - 1
        return lax.shift_right_logical(row, sh) == lax.shift_right_logical(col, sh)

    blk16 = same_block(16)
    l16 = [jnp.where(blk16, m, 0.0) for m in lmats]
    x = [eye - m for m in l16]
    p = [_bdot(m, m) for m in l16]
    for step in range(3):
        x = [xi + _bdot(xi, pi) for xi, pi in zip(x, p)]
        if step < 2:
            p = [_bdot(pi, pi) for pi in p]
    n = 16
    while n < CHUNK:
        off = same_block(2 * n) & jnp.logical_not(same_block(n))
        xl = [_bdot(xi, jnp.where(off, m, 0.0)) for xi, m in zip(x, lmats)]
        x = [xi - _bdot(yi, xi) for xi, yi in zip(x, xl)]
        n *= 2
    return x


def _wkv_kernel(r_ref, k_ref, v_ref, lrw_ref, lra_ref, w2_ref, a2_ref, w0_ref, a0_ref,
                kk_ref, ka_ref, s0_ref, m_ref, o_ref, sout_ref,
                rp_scr, w_scr, h_scr, d_scr, *, seq_len):
    z = pl.program_id(0)
    n_chunks = seq_len // CHUNK
    n_groups = n_chunks // WKV_GROUP
    heads = range(WKV_HEADS_STEP)
    incl = m_ref[0]
    incl_bf = incl.astype(BF16)
    row = lax.broadcasted_iota(jnp.int32, (CHUNK, CHUNK), 0)
    col = lax.broadcasted_iota(jnp.int32, (CHUNK, CHUNK), 1)
    incl_b = incl > 0.5
    strict_b = incl_b & (row != col)
    w2 = w2_ref[0].astype(BF16)
    a2 = a2_ref[0].astype(BF16)
    w0 = w0_ref[0]
    a0 = a0_ref[0]
    k_k = kk_ref[...]
    k_a = ka_ref[...]

    def head_sl(h):
        return slice(h * WKV_N, (h + 1) * WKV_N)

    def phase1(it, carry):
        cs = [it * WKV_GROUP + j for j in range(WKV_GROUP)]
        rows = [_chunk_rows(c) for c in cs]
        r = [r_ref[rw, :] for rw in rows]
        k = [k_ref[rw, :] for rw in rows]
        v = [v_ref[rw, :] for rw in rows]
        w_raw = [w0 + _bdot(lrw_ref[rw, :], w2) for rw in rows]
        a = [jax.nn.sigmoid(a0 + _bdot(lra_ref[rw, :], a2)) for rw in rows]
        lw = [-math.exp(-0.5) * jax.nn.sigmoid(x) for x in w_raw]
        g = [_cumsum_rows(incl_bf, x) for x in lw]
        gtot = [jnp.sum(x, axis=0, keepdims=True) for x in lw]
        kt, rt, kh, bh, kp, bp, vh = [], [], [], [], [], [], []
        for j in range(WKV_GROUP):
            kkr = k[j] * k_k
            kmod = k[j] * (1.0 + (a[j] - 1.0) * k_a)
            e_g = jnp.exp(g[j])
            e_gx = jnp.exp(g[j] - lw[j])
            e_ng = jnp.exp(-g[j])
            e_gc = jnp.exp(gtot[j] - g[j])
            for h in heads:
                sl = head_sl(h)
                kk_h = kkr[:, sl]
                nrm = jnp.sqrt(jnp.sum(kk_h * kk_h, axis=-1, keepdims=True))
                kk_h = kk_h / jnp.maximum(nrm, 1e-12)
                b_h = kk_h * a[j][:, sl]
                kt.append(kk_h * e_gx[:, sl])
                rt.append(r[j][:, sl] * e_g[:, sl])
                kh.append(kmod[:, sl] * e_ng[:, sl])
                bh.append(b_h * e_ng[:, sl])
                kp.append(kmod[:, sl] * e_gc[:, sl])
                bp.append(b_h * e_gc[:, sl])
                vh.append(v[j][:, sl])
        units = range(WKV_HEADS_STEP * WKV_GROUP)
        lmat = [jnp.where(strict_b, _bdot_nt(kt[u], bh[u]), 0.0) for u in units]
        mkk = [jnp.where(strict_b, _bdot_nt(kt[u], kh[u]), 0.0) for u in units]
        mrk = [jnp.where(incl_b, _bdot_nt(rt[u], kh[u]), 0.0) for u in units]
        mrb = [jnp.where(incl_b, _bdot_nt(rt[u], bh[u]), 0.0) for u in units]
        tinv = _inv_unit_tri(lmat)
        mv = [_bdot(mkk[u], vh[u]) for u in units]
        mrkv = [_bdot(mrk[u], vh[u]) for u in units]
        vtk = [_bdot_tn(vh[u], kp[u]) for u in units]
        p = [_bdot(tinv[u], kt[u]) for u in units]
        q = [_bdot(tinv[u], mv[u]) for u in units]
        rp = [rt[u] - _bdot(mrb[u], p[u]) for u in units]
        op = [mrkv[u] - _bdot(mrb[u], q[u]) for u in units]
        wm = [_bdot_tn(p[u], bp[u]) for u in units]
        hm = [vtk[u] - _bdot_tn(q[u], bp[u]) for u in units]
        for j in range(WKV_GROUP):
            d_scr[cs[j]] = jnp.broadcast_to(jnp.exp(gtot[j]), (8, WKV_HEADS_STEP * WKV_N))
            o_ref[0, rows[j], :] = jnp.concatenate([op[WKV_HEADS_STEP * j + h] for h in heads], axis=-1)
            for h in heads:
                u = WKV_HEADS_STEP * j + h
                rp_scr[cs[j], h] = rp[u]
                w_scr[cs[j], h] = wm[u]
                h_scr[cs[j], h] = hm[u]
        return carry

    lax.fori_loop(0, n_groups, phase1, 0)

    def phase2(ci, carry):
        c = ci + z * (n_chunks - 1 - 2 * ci)
        d_tot = d_scr[c][0:1, :]
        sw = [_dot3(carry[h], w_scr[c, h]) for h in heads]
        new_s = [carry[h] * d_tot[:, head_sl(h)] - sw[h] + h_scr[c, h] for h in heads]
        for h in heads:
            h_scr[c, h] = carry[h]
        return tuple(new_s)

    s_fin = lax.fori_loop(0, n_chunks, phase2, tuple(s0_ref[0, 0, h] for h in heads))
    for h in heads:
        sout_ref[0, 0, h] = s_fin[h]

    def phase3(it, carry):
        cs = [it * WKV_GROUP + j for j in range(WKV_GROUP)]
        inter = [[_dot3_nt(rp_scr[c, h], h_scr[c, h]) for h in heads] for c in cs]
        for j in range(WKV_GROUP):
            rows = _chunk_rows(cs[j])
            o_ref[0, rows, :] = o_ref[0, rows, :] + jnp.concatenate(inter[j], axis=-1)
        return carry

    lax.fori_loop(0, n_groups, phase3, 0)


def _wkv_scan(p_all, lr, w2pad, a2pad, w0, a0, k_k, k_a, s0, masks, *, n_seq, seq_len, tok_blk0):
    def tok(b):
        return tok_blk0 + b

    width = WKV_HEADS_STEP * WKV_N
    col = lambda base: (lambda z, b, hq: (tok(b), base // width + hq))
    vec = lambda z, b, hq: (z, 0, hq)
    n_chunks = seq_len // CHUNK
    return pl.pallas_call(
        functools.partial(_wkv_kernel, seq_len=seq_len),
        grid=(2, n_seq, WKV_H // WKV_HEADS_STEP),
        in_specs=[
            pl.BlockSpec((seq_len, width), col(4 * D)),
            pl.BlockSpec((seq_len, width), col(5 * D)),
            pl.BlockSpec((seq_len, width), col(6 * D)),
            pl.BlockSpec((seq_len, LANES), lambda z, b, hq: (tok(b), 0)),
            pl.BlockSpec((seq_len, LANES), lambda z, b, hq: (tok(b), 1)),
            pl.BlockSpec((1, LANES, width), vec),
            pl.BlockSpec((1, LANES, width), vec),
            pl.BlockSpec((1, 1, width), vec),
            pl.BlockSpec((1, 1, width), vec),
            pl.BlockSpec((1, width), lambda z, b, hq: (0, hq)),
            pl.BlockSpec((1, width), lambda z, b, hq: (0, hq)),
            pl.BlockSpec((1, 1, WKV_HEADS_STEP, WKV_N, WKV_N), lambda z, b, hq: (b, z, hq, 0, 0)),
            pl.BlockSpec((1, CHUNK, CHUNK), lambda z, b, hq: (z, 0, 0)),
        ],
        out_specs=[
            pl.BlockSpec((1, seq_len, width), lambda z, b, hq: (z, b, hq)),
            pl.BlockSpec((1, 1, WKV_HEADS_STEP, WKV_N, WKV_N), lambda z, b, hq: (b, z, hq, 0, 0)),
        ],
        out_shape=[
            jax.ShapeDtypeStruct((2, n_seq * seq_len, D), F32),
            jax.ShapeDtypeStruct((n_seq, 2, WKV_H, WKV_N, WKV_N), F32),
        ],
        scratch_shapes=[pltpu.VMEM((n_chunks, WKV_HEADS_STEP, CHUNK, WKV_N), F32)] * 3
        + [pltpu.VMEM((n_chunks, 8, width), F32)],
        compiler_params=pltpu.CompilerParams(
            dimension_semantics=("arbitrary", "arbitrary", "arbitrary"), vmem_limit_bytes=VMEM_LIMIT),
        name="wkv_scan_%d" % seq_len,
    )(p_all, p_all, p_all, lr, lr, w2pad, a2pad, w0, a0, k_k, k_a, s0, masks)


def _dot2(x, w_bf):
    hi, lo = _split2(x)
    return _dot(hi, w_bf) + _dot(lo, w_bf)


def _group_stat(x, ind_ref, bind_ref):
    return _dot2(_dot2(x, ind_ref[...]), bind_ref[...])


N_CTX_TILES = N_CTX_TOK // TM_OUT


def _sum_dirs(i, oc_ref, os_ref):
    is_ctx = i < N_CTX_TILES
    return jnp.where(is_ctx, oc_ref[0], os_ref[0]) + jnp.where(is_ctx, oc_ref[1], os_ref[1])


def _ctx_tile(i):
    return (0, jnp.minimum(i, N_CTX_TILES - 1), 0)


def _smp_tile(i):
    return (0, jnp.maximum(i - N_CTX_TILES, 0), 0)


def _even_out_kernel(x_ref, mod_ref, pc_ref, pw_ref, oc_ref, os_ref, lra_ref, a2_ref, a0_ref, ka_ref, rk_ref,
                     cw_ref, lnw_ref, lnb_ref, ind_ref, bind_ref, wo_ref, out_ref):
    i = pl.program_id(0)
    u = pc_ref[:, 0:D]
    gb = pc_ref[:, D:2 * D]
    gc = pc_ref[:, 2 * D:3 * D]
    zc = pc_ref[:, 3 * D:4 * D]
    xg = gc * u
    row_len = jnp.where(i < N_CTX_TILES, CTX_T, GRID_W)
    pos = lax.broadcasted_iota(jnp.int32, (TM_OUT, 1), 0) & (row_len - 1)
    prev = jnp.where(pos == 0, 0.0, pltpu.roll(xg, 1, 0))
    nxt = jnp.where(pos == row_len - 1, 0.0, pltpu.roll(xg, TM_OUT - 1, 0))
    conv = cw_ref[0:1, :] * prev + cw_ref[1:2, :] * xg + cw_ref[2:3, :] * nxt
    o_conv = _silu(zc) * gb * conv
    r = pw_ref[:, 0:D]
    k = pw_ref[:, D:2 * D]
    v = pw_ref[:, 2 * D:3 * D]
    zw = pw_ref[:, 3 * D:4 * D]
    o = _sum_dirs(i, oc_ref, os_ref)
    mu = _group_stat(o, ind_ref, bind_ref)
    dlt = o - mu
    var = _group_stat(dlt * dlt, ind_ref, bind_ref)
    gn = dlt * lax.rsqrt(var + GN_EPS) * lnw_ref[...] + lnb_ref[...]
    lra = lra_ref[...]
    ic0 = jax.nn.sigmoid(a0_ref[0] + _bdot(lra, a2_ref[0]))
    ic1 = jax.nn.sigmoid(a0_ref[1] + _bdot(lra, a2_ref[1]))
    ksum = k * (2.0 + (ic0 + ic1 - 2.0) * ka_ref[...])
    bonus = _group_stat(r * ksum * rk_ref[...], ind_ref, bind_ref) * float(WKV_N)
    o_wkv = (gn + bonus * v) * _silu(zw)
    y = _dot(o_conv.astype(BF16), wo_ref[0:D, :]) + _dot(o_wkv.astype(BF16), wo_ref[D:2 * D, :])
    gate = mod_ref[0, 2:3, :]
    out_ref[...] = x_ref[...] + gate * y


def _even_out(x, mod3, p_all, o_ctx, o_smp, lr, a2pad, a0, k_a, r_k, conv_w, ln_w, ln_b, ind, bind, wo_bf):
    row = lambda i: (i, 0)
    const2 = lambda i: (0, 0)
    const3 = lambda i: (0, 0, 0)
    return pl.pallas_call(
        _even_out_kernel,
        grid=(N_TOK // TM_OUT,),
        in_specs=[
            pl.BlockSpec((TM_OUT, D), row),
            pl.BlockSpec((1, 3, D), lambda i: (_tile_row(i, TM_OUT), 0, 0)),
            pl.BlockSpec((TM_OUT, 4 * D), lambda i: (i, 0)),
            pl.BlockSpec((TM_OUT, 4 * D), lambda i: (i, 1)),
            pl.BlockSpec((2, TM_OUT, D), _ctx_tile),
            pl.BlockSpec((2, TM_OUT, D), _smp_tile),
            pl.BlockSpec((TM_OUT, LANES), lambda i: (i, 1)),
            pl.BlockSpec((2, LANES, D), const3),
            pl.BlockSpec((2, 1, D), const3),
            pl.BlockSpec((1, D), const2),
            pl.BlockSpec((1, D), const2),
            pl.BlockSpec((3, D), const2),
            pl.BlockSpec((1, D), const2),
            pl.BlockSpec((1, D), const2),
            pl.BlockSpec((D, LANES), const2),
            pl.BlockSpec((LANES, D), const2),
            pl.BlockSpec((2 * D, D), const2),
        ],
        out_specs=pl.BlockSpec((TM_OUT, D), row),
        out_shape=jax.ShapeDtypeStruct((N_TOK, D), F32),
        compiler_params=pltpu.CompilerParams(
            dimension_semantics=("arbitrary",), vmem_limit_bytes=VMEM_LIMIT),
        name="even_out",
    )(x, mod3, p_all, p_all, o_ctx, o_smp, lr, a2pad, a0, k_a, r_k, conv_w, ln_w, ln_b, ind, bind, wo_bf)


def _gla_kernel(q_ref, k_ref, v_ref, lrg_ref, gk2_ref, gkb_ref, s0_ref, m_ref, o_ref, sout_ref,
                qe_scr, st_scr, e_scr, *, seq_len):
    z = pl.program_id(0)
    n_chunks = seq_len // CHUNK
    n_groups = n_chunks // GLA_GROUP
    incl = m_ref[0]
    incl_bf = incl.astype(BF16)
    incl_b = incl > 0.5
    gk2 = gk2_ref[0].astype(BF16)
    gkb = gkb_ref[0]
    scale = GLA_DK ** -0.5

    def phase1(it, carry):
        cs = [it * GLA_GROUP + j for j in range(GLA_GROUP)]
        rows = [_chunk_rows(c) for c in cs]
        logit = [_bdot(lrg_ref[rw, :], gk2) + gkb for rw in rows]
        g = [jax.nn.log_sigmoid(x) / GLA_GATE_NORM for x in logit]
        bc = [_cumsum_rows(incl_bf, x) for x in g]
        b_last = [jnp.sum(x, axis=0, keepdims=True) for x in g]
        qe = [q_ref[rw, :] * scale * jnp.exp(x) for rw, x in zip(rows, bc)]
        ke = [k_ref[rw, :] * jnp.exp(-x) for rw, x in zip(rows, bc)]
        kd = [k_ref[rw, :] * jnp.exp(bl - x) for rw, x, bl in zip(rows, bc, b_last)]
        v = [v_ref[rw, :] for rw in rows]
        att = [jnp.where(incl_b, _bdot_nt(a, b), 0.0) for a, b in zip(qe, ke)]
        o_in = [_bdot(a, b) for a, b in zip(att, v)]
        vtk = [_bdot_tn(a, b) for a, b in zip(v, kd)]
        for j in range(GLA_GROUP):
            o_ref[0, rows[j], :] = o_in[j]
            qe_scr[cs[j]] = qe[j]
            st_scr[cs[j]] = vtk[j]
            e_scr[cs[j]] = jnp.broadcast_to(jnp.exp(b_last[j]), (8, GLA_DK))
        return carry

    lax.fori_loop(0, n_groups, phase1, 0)

    def phase2(ci, st):
        c = ci + z * (n_chunks - 1 - 2 * ci)
        new_st = st * e_scr[c][0:1, :] + st_scr[c]
        st_scr[c] = st
        return new_st

    st_fin = lax.fori_loop(0, n_chunks, phase2, s0_ref[0, 0, 0].T)
    sout_ref[0, 0, 0] = st_fin.T

    def phase3(it, carry):
        cs = [it * GLA_GROUP + j for j in range(GLA_GROUP)]
        inter = [_bdot_nt(qe_scr[c], st_scr[c]) for c in cs]
        for j in range(GLA_GROUP):
            rows = _chunk_rows(cs[j])
            o_ref[0, rows, :] = o_ref[0, rows, :] + inter[j]
        return carry

    lax.fori_loop(0, n_groups, phase3, 0)


def _gla_scan(p_all, lrg, gk2pad, gkb, s0, masks, *, n_seq, seq_len, tok_blk0):
    def tok(b):
        return tok_blk0 + b

    n_chunks = seq_len // CHUNK
    return pl.pallas_call(
        functools.partial(_gla_kernel, seq_len=seq_len),
        grid=(2, n_seq, GLA_H),
        in_specs=[
            pl.BlockSpec((seq_len, GLA_DK), lambda z, b, h: (tok(b), h)),
            pl.BlockSpec((seq_len, GLA_DK), lambda z, b, h: (tok(b), GLA_H + h)),
            pl.BlockSpec((seq_len, GLA_DV), lambda z, b, h: (tok(b), GLA_H + h)),
            pl.BlockSpec((seq_len, LANES), lambda z, b, h: (tok(b), 0)),
            pl.BlockSpec((1, LANES, GLA_DK), lambda z, b, h: (z, 0, h)),
            pl.BlockSpec((1, 1, GLA_DK), lambda z, b, h: (z, 0, h)),
            pl.BlockSpec((1, 1, 1, GLA_DK, GLA_DV), lambda z, b, h: (b, z, h, 0, 0)),
            pl.BlockSpec((1, CHUNK, CHUNK), lambda z, b, h: (z, 0, 0)),
        ],
        out_specs=[
            pl.BlockSpec((1, seq_len, GLA_DV), lambda z, b, h: (z, b, h)),
            pl.BlockSpec((1, 1, 1, GLA_DK, GLA_DV), lambda z, b, h: (b, z, h, 0, 0)),
        ],
        out_shape=[
            jax.ShapeDtypeStruct((2, n_seq * seq_len, D), F32),
            jax.ShapeDtypeStruct((n_seq, 2, GLA_H, GLA_DK, GLA_DV), F32),
        ],
        scratch_shapes=[
            pltpu.VMEM((n_chunks, CHUNK, GLA_DK), F32),
            pltpu.VMEM((n_chunks, GLA_DV, GLA_DK), F32),
            pltpu.VMEM((n_chunks, 8, GLA_DK), F32),
        ],
        compiler_params=pltpu.CompilerParams(
            dimension_semantics=("arbitrary", "arbitrary", "arbitrary"), vmem_limit_bytes=VMEM_LIMIT),
        name="gla_scan_%d" % seq_len,
    )(p_all, p_all, p_all, lrg, gk2pad, gkb, s0, masks)


def _odd_out_kernel(x_ref, mod_ref, zg_ref, oc_ref, os_ref, gn_ref, ind_ref, bind_ref, wo_ref, fg_ref, out_ref):
    o = _sum_dirs(pl.program_id(0), oc_ref, os_ref)
    ms = _group_stat(o * o, ind_ref, bind_ref)
    on = o * lax.rsqrt(ms + NORM_EPS) * gn_ref[...]
    og = on * _silu(zg_ref[...])
    y = _dot(og.astype(BF16), wo_ref[...])
    gate = mod_ref[0, 2:3, :]
    x = x_ref[...] + gate * y
    out_ref[...] = x * lax.rsqrt(jnp.mean(x * x, axis=-1, keepdims=True) + NORM_EPS) * fg_ref[...]


def _odd_out(x, mod3, p_all, o_ctx, o_smp, g_norm, ind, bind, wo_bf, final_g):
    row = lambda i: (i, 0)
    const2 = lambda i: (0, 0)
    return pl.pallas_call(
        _odd_out_kernel,
        grid=(N_TOK // TM_OUT,),
        in_specs=[
            pl.BlockSpec((TM_OUT, D), row),
            pl.BlockSpec((1, 3, D), lambda i: (_tile_row(i, TM_OUT), 0, 0)),
            pl.BlockSpec((TM_OUT, D), lambda i: (i, 2)),
            pl.BlockSpec((2, TM_OUT, D), _ctx_tile),
            pl.BlockSpec((2, TM_OUT, D), _smp_tile),
            pl.BlockSpec((1, D), const2),
            pl.BlockSpec((D, LANES), const2),
            pl.BlockSpec((LANES, D), const2),
            pl.BlockSpec((D, D), const2),
            pl.BlockSpec((1, D), const2),
        ],
        out_specs=pl.BlockSpec((TM_OUT, D), row),
        out_shape=jax.ShapeDtypeStruct((N_TOK, D), F32),
        compiler_params=pltpu.CompilerParams(
            dimension_semantics=("arbitrary",), vmem_limit_bytes=VMEM_LIMIT),
        name="odd_out",
    )(x, mod3, p_all, o_ctx, o_smp, g_norm, ind, bind, wo_bf, final_g)


def _group_indicators(group):
    ch = jnp.arange(D) // group
    lane = jnp.arange(LANES)
    hit = (ch[:, None] == lane[None, :]).astype(F32)
    return (hit / float(group)).astype(BF16), hit.T.astype(BF16)


def _pad_dirs(w):
    r = w.shape[1]
    out = jnp.zeros((2, LANES, w.shape[2]), F32)
    out = out.at[0, 0:r].set(w[0])
    out = out.at[1, r:2 * r].set(w[1])
    return out


def kernel(x_prompt, x_sample, state_wkv, state_gla, c, c_ctx, norm_g, ada_w, ada_b, final_g, e_w_in, e_w_out, conv_w, wkv_w0, wkv_w1, wkv_w2, wkv_a0, wkv_a1, wkv_a2, wkv_k_k, wkv_k_a, wkv_r_k, wkv_ln_w, wkv_ln_b, o_w_in, o_w_out, gla_gk1, gla_gk2, gla_gk_b, gla_g_norm):
    x = jnp.concatenate([x_prompt.reshape(N_CTX_TOK, D), x_sample.reshape(N_CTX_TOK, D)], axis=0)
    cvec = jnp.zeros((8, D), F32).at[0].set(c_ctx).at[1:1 + SMP_B].set(c)
    mod = _modulation(cvec, ada_w, ada_b)
    masks = _chunk_masks()

    mod3 = mod[0].reshape(8, 3, D)
    wlr = jnp.concatenate([wkv_w1[0, 0], wkv_w1[0, 1], wkv_a1[0, 0], wkv_a1[0, 1]], axis=1)
    p_e, lr_e = _inproj(x, mod3, norm_g[0:1], e_w_in[0].astype(BF16), wlr.astype(BF16), n_tanh=2 * 64)
    w2pad = _pad_dirs(wkv_w2[0])
    a2pad = _pad_dirs(wkv_a2[0])
    w0 = wkv_w0[0].reshape(2, 1, D)
    a0 = wkv_a0[0].reshape(2, 1, D)
    k_k = wkv_k_k[0].reshape(1, D)
    k_a = wkv_k_a[0].reshape(1, D)
    r_k = wkv_r_k[0].reshape(1, D)
    zero_wkv = jnp.zeros((CTX_B, 2, WKV_H, WKV_N, WKV_N), F32)
    o_ctx, new_wkv = _wkv_scan(p_e, lr_e, w2pad, a2pad, w0, a0, k_k, k_a, zero_wkv, masks,
                               n_seq=CTX_B, seq_len=CTX_T, tok_blk0=0)
    o_smp, _ = _wkv_scan(p_e, lr_e, w2pad, a2pad, w0, a0, k_k, k_a, state_wkv[:, 0], masks,
                         n_seq=SMP_B, seq_len=SMP_T, tok_blk0=N_CTX_TOK // SMP_T)
    ind64, bind64 = _group_indicators(WKV_N)
    x = _even_out(x, mod3, p_e, o_ctx, o_smp, lr_e, a2pad, a0, k_a, r_k, conv_w[0],
                  wkv_ln_w[0].reshape(1, D), wkv_ln_b[0].reshape(1, D), ind64, bind64,
                  e_w_out[0].astype(BF16))

    mod3 = mod[1].reshape(8, 3, D)
    rank = gla_gk1.shape[-1]
    wlr = jnp.zeros((D, LANES), F32).at[:, 0:rank].set(gla_gk1[0, 0]).at[:, rank:2 * rank].set(gla_gk1[0, 1])
    p_o, lr_o = _inproj(x, mod3, norm_g[1:2], o_w_in[0].astype(BF16), wlr.astype(BF16), n_tanh=0)
    gk2pad = _pad_dirs(gla_gk2[0])
    gkb = gla_gk_b[0].reshape(2, 1, GLA_H * GLA_DK)
    zero_gla = jnp.zeros((CTX_B, 2, GLA_H, GLA_DK, GLA_DV), F32)
    g_ctx, new_gla = _gla_scan(p_o, lr_o, gk2pad, gkb, zero_gla, masks,
                               n_seq=CTX_B, seq_len=CTX_T, tok_blk0=0)
    g_smp, _ = _gla_scan(p_o, lr_o, gk2pad, gkb, state_gla[:, 0], masks,
                         n_seq=SMP_B, seq_len=SMP_T, tok_blk0=N_CTX_TOK // SMP_T)
    ind256, bind256 = _group_indicators(GLA_DV)
    y = _odd_out(x, mod3, p_o, g_ctx, g_smp, jnp.tile(gla_g_norm[0], GLA_H).reshape(1, D), ind256, bind256,
                 o_w_out[0].astype(BF16), final_g.reshape(1, D))

    y_prompt = y[:N_CTX_TOK].reshape(CTX_B, CTX_T, D)
    y_sample = y[N_CTX_TOK:].reshape(SMP_B, SMP_T, D)
    return (y_prompt, y_sample, new_wkv[:, None], new_gla[:, None])
```

```python
import functools
import math

import jax
import jax.numpy as jnp
from jax import lax
from jax.experimental import pallas as pl
from jax.experimental.pallas import tpu as pltpu

F32 = jnp.float32
BF16 = jnp.bfloat16
HI = lax.Precision.HIGHEST

D = 1024
N_CTX_TOK = 16 * 256
N_TOK = 2 * N_CTX_TOK
CTX_B, CTX_T = 16, 256
SMP_B, SMP_T = 2, 2048
GRID_W = 64
WKV_H, WKV_N = 16, 64
GLA_H, GLA_DK, GLA_DV = 4, 128, 256
GLA_GATE_NORM = 16.0
GN_EPS = 64e-5
NORM_EPS = 1e-6
CHUNK = 64
WKV_GROUP = 4
WKV_HEADS_STEP = 4
GLA_GROUP = 4
LANES = 128
TM_PROJ = 1024
TN_PROJ = 1024
TM_OUT = 256
VMEM_LIMIT = 56 * 1024 * 1024


def _silu(x):
    return x * jax.nn.sigmoid(x)


def _dot(a, b, precision=None):
    return jnp.dot(a, b, preferred_element_type=F32, precision=precision)


def _dot_nt(a, b, precision=None):
    return lax.dot_general(a, b, (((1,), (1,)), ((), ())), preferred_element_type=F32, precision=precision)


def _dot_tn(a, b, precision=None):
    return lax.dot_general(a, b, (((0,), (0,)), ((), ())), preferred_element_type=F32, precision=precision)


def _bdot(a, b):
    return _dot(a.astype(BF16), b.astype(BF16))


def _bdot_nt(a, b):
    return _dot_nt(a.astype(BF16), b.astype(BF16))


def _bdot_tn(a, b):
    return _dot_tn(a.astype(BF16), b.astype(BF16))


def _split2(x):
    hi = x.astype(BF16)
    lo = (x - hi.astype(F32)).astype(BF16)
    return hi, lo


def _dot3(a, b):
    ah, al = _split2(a)
    bh, bl = _split2(b)
    return _dot(ah, bh) + _dot(al, bh) + _dot(ah, bl)


def _dot3_nt(a, b):
    ah, al = _split2(a)
    bh, bl = _split2(b)
    return _dot_nt(ah, bh) + _dot_nt(al, bh) + _dot_nt(ah, bl)


def _scan_cumsum(x, z):
    n = x.shape[0]
    row = lax.broadcasted_iota(jnp.int32, (n, 1), 0)
    pre = x
    s = 1
    while s < n:
        pre = pre + jnp.where(row >= s, pltpu.roll(pre, s, 0), 0.0)
        s *= 2
    tot = pre[n - 1:n, :]
    return jnp.where(z == 0, pre, tot - pre + x), tot


def _tile_row(i, tile):
    ctx_tiles = N_CTX_TOK // tile
    per_req = SMP_T // tile
    return jnp.where(i < ctx_tiles, 0, 1 + (i - ctx_tiles) // per_req)


def _mod_kernel(c_ref, w_ref, b_ref, o_ref):
    cf = _silu(c_ref[...])
    o_ref[0] = _dot(cf, w_ref[0], HI) + b_ref[0]


def _modulation(cvec, ada_w, ada_b):
    depth = ada_w.shape[0]
    return pl.pallas_call(
        _mod_kernel,
        grid=(depth, 3),
        in_specs=[
            pl.BlockSpec((8, D), lambda l, j: (0, 0)),
            pl.BlockSpec((1, D, D), lambda l, j: (l, 0, j)),
            pl.BlockSpec((1, 1, D), lambda l, j: (l, 0, j)),
        ],
        out_specs=pl.BlockSpec((1, 8, D), lambda l, j: (l, 0, j)),
        out_shape=jax.ShapeDtypeStruct((depth, 8, 3 * D), F32),
        compiler_params=pltpu.CompilerParams(
            dimension_semantics=("arbitrary", "arbitrary"), vmem_limit_bytes=VMEM_LIMIT),
        name="adaln_mod",
    )(cvec, ada_w, ada_b.reshape(depth, 1, 3 * D))


def _inproj_kernel(x_ref, mod_ref, g_ref, w_ref, wlr_ref, p_ref, lr_ref, h_scr, *, n_tanh):
    j = pl.program_id(1)

    @pl.when(j == 0)
    def _():
        x = x_ref[...]
        y = x * lax.rsqrt(jnp.mean(x * x, axis=-1, keepdims=True) + NORM_EPS) * g_ref[...]
        shift = mod_ref[0, 0:1, :]
        scale = mod_ref[0, 1:2, :]
        h = y * (1.0 + scale) + shift
        hb = h.astype(BF16)
        h_scr[...] = hb
        lr = _dot(hb, wlr_ref[...])
        if n_tanh:
            lane = lax.broadcasted_iota(jnp.int32, lr.shape, 1)
            lr = jnp.where(lane < n_tanh, jnp.tanh(lr), lr)
        lr_ref[...] = lr

    p_ref[...] = _dot(h_scr[...], w_ref[...])


def _inproj(x, mod3, g, w_bf, wlr_bf, n_tanh):
    n_out = w_bf.shape[1]
    n_lr = wlr_bf.shape[1]
    return pl.pallas_call(
        functools.partial(_inproj_kernel, n_tanh=n_tanh),
        grid=(N_TOK // TM_PROJ, n_out // TN_PROJ),
        in_specs=[
            pl.BlockSpec((TM_PROJ, D), lambda i, j: (i, 0)),
            pl.BlockSpec((1, 3, D), lambda i, j: (_tile_row(i, TM_PROJ), 0, 0)),
            pl.BlockSpec((1, D), lambda i, j: (0, 0)),
            pl.BlockSpec((D, TN_PROJ), lambda i, j: (0, j)),
            pl.BlockSpec((D, n_lr), lambda i, j: (0, 0)),
        ],
        out_specs=[
            pl.BlockSpec((TM_PROJ, TN_PROJ), lambda i, j: (i, j)),
            pl.BlockSpec((TM_PROJ, n_lr), lambda i, j: (i, 0)),
        ],
        out_shape=[
            jax.ShapeDtypeStruct((N_TOK, n_out), F32),
            jax.ShapeDtypeStruct((N_TOK, n_lr), F32),
        ],
        scratch_shapes=[pltpu.VMEM((TM_PROJ, D), BF16)],
        compiler_params=pltpu.CompilerParams(
            dimension_semantics=("arbitrary", "arbitrary"), vmem_limit_bytes=VMEM_LIMIT),
        name="inproj",
    )(x, mod3, g, w_bf, wlr_bf)


def _chunk_masks():
    idx = jnp.arange(CHUNK)
    lower = (idx[:, None] >= idx[None, :]).astype(F32)
    return jnp.stack([lower, lower.T], axis=0)


def _chunk_rows(c):
    return pl.ds(pl.multiple_of(c * CHUNK, CHUNK), CHUNK)


def _inv_unit_tri(lmats):
    row = lax.broadcasted_iota(jnp.int32, (CHUNK, CHUNK), 0)
    col = lax.broadcasted_iota(jnp.int32, (CHUNK, CHUNK), 1)
    eye = (row == col).astype(F32)

    def same_block(n):
        sh = n.bit_length() - 1
        return lax.shift_right_logical(row, sh) == lax.shift_right_logical(col, sh)

    blk16 = same_block(16)
    l16 = [jnp.where(blk16, m, 0.0) for m in lmats]
    x = [eye - m for m in l16]
    p = [_bdot(m, m) for m in l16]
    for step in range(3):
        x = [xi + _bdot(xi, pi) for xi, pi in zip(x, p)]
        if step < 2:
            p = [_bdot(pi, pi) for pi in p]
    n = 16
    while n < CHUNK:
        off = same_block(2 * n) & jnp.logical_not(same_block(n))
        xl = [_bdot(xi, jnp.where(off, m, 0.0)) for xi, m in zip(x, lmats)]
        x = [xi - _bdot(yi, xi) for xi, yi in zip(x, xl)]
        n *= 2
    return x


def _wkv_kernel(r_ref, k_ref, v_ref, lrw_ref, lra_ref, w2_ref, a2_ref, w0_ref, a0_ref,
                kk_ref, ka_ref, s0_ref, m_ref, o_ref, sout_ref,
                rp_scr, w_scr, h_scr, d_scr, *, seq_len):
    z = pl.program_id(0)
    n_chunks = seq_len // CHUNK
    n_groups = n_chunks // WKV_GROUP
    heads = range(WKV_HEADS_STEP)
    row = lax.broadcasted_iota(jnp.int32, (CHUNK, CHUNK), 0)
    col = lax.broadcasted_iota(jnp.int32, (CHUNK, CHUNK), 1)
    incl_b = m_ref[0] > 0.5
    strict_b = incl_b & (row != col)
    w2 = w2_ref[0].astype(BF16)
    a2 = a2_ref[0].astype(BF16)
    w0 = w0_ref[0]
    a0 = a0_ref[0]
    k_k = kk_ref[...]
    k_a = ka_ref[...]

    def head_sl(h):
        return slice(h * WKV_N, (h + 1) * WKV_N)

    def phase1(it, carry):
        cs = [it * WKV_GROUP + j for j in range(WKV_GROUP)]
        rows = [_chunk_rows(c) for c in cs]
        r = [r_ref[rw, :] for rw in rows]
        k = [k_ref[rw, :] for rw in rows]
        v = [v_ref[rw, :] for rw in rows]
        w_raw = [w0 + _bdot(lrw_ref[rw, :], w2) for rw in rows]
        a = [jax.nn.sigmoid(a0 + _bdot(lra_ref[rw, :], a2)) for rw in rows]
        lw = [-math.exp(-0.5) * jax.nn.sigmoid(x) for x in w_raw]
        g, gtot = zip(*[_scan_cumsum(x, z) for x in lw])
        kt, rt, kh, bh, kp, bp, vh = [], [], [], [], [], [], []
        for j in range(WKV_GROUP):
            kkr = k[j] * k_k
            kmod = k[j] * (1.0 + (a[j] - 1.0) * k_a)
            e_g = jnp.exp(g[j])
            e_gx = jnp.exp(g[j] - lw[j])
            e_ng = jnp.exp(-g[j])
            e_gc = jnp.exp(gtot[j] - g[j])
            for h in heads:
                sl = head_sl(h)
                kk_h = kkr[:, sl]
                nrm = jnp.sqrt(jnp.sum(kk_h * kk_h, axis=-1, keepdims=True))
                kk_h = kk_h / jnp.maximum(nrm, 1e-12)
                b_h = kk_h * a[j][:, sl]
                kt.append(kk_h * e_gx[:, sl])
                rt.append(r[j][:, sl] * e_g[:, sl])
                kh.append(kmod[:, sl] * e_ng[:, sl])
                bh.append(b_h * e_ng[:, sl])
                kp.append(kmod[:, sl] * e_gc[:, sl])
                bp.append(b_h * e_gc[:, sl])
                vh.append(v[j][:, sl])
        units = range(WKV_HEADS_STEP * WKV_GROUP)
        gram = [_bdot_nt(jnp.concatenate([kt[u], rt[u]], axis=0), jnp.concatenate([kh[u], bh[u]], axis=0))
                for u in units]
        mkk = [jnp.where(strict_b, gm[:CHUNK, :CHUNK], 0.0) for gm in gram]
        lmat = [jnp.where(strict_b, gm[:CHUNK, CHUNK:], 0.0) for gm in gram]
        mrk = [jnp.where(incl_b, gm[CHUNK:, :CHUNK], 0.0) for gm in gram]
        mrb = [jnp.where(incl_b, gm[CHUNK:, CHUNK:], 0.0) for gm in gram]
        tinv = _inv_unit_tri(lmat)
        mv = [_bdot(mkk[u], vh[u]) for u in units]
        mrkv = [_bdot(mrk[u], vh[u]) for u in units]
        vtk = [_bdot_tn(vh[u], kp[u]) for u in units]
        pq = [_bdot(tinv[u], jnp.concatenate([kt[u], mv[u]], axis=-1)) for u in units]
        corr = [_bdot(mrb[u], pq[u]) for u in units]
        rp = [rt[u] - corr[u][:, :WKV_N] for u in units]
        op = [mrkv[u] - corr[u][:, WKV_N:] for u in units]
        wq = [_bdot_tn(pq[u], bp[u]) for u in units]
        wm = [x[:WKV_N] for x in wq]
        hm = [vtk[u] - wq[u][WKV_N:] for u in units]
        for j in range(WKV_GROUP):
            d_scr[cs[j]] = jnp.broadcast_to(jnp.exp(gtot[j]), (8, WKV_HEADS_STEP * WKV_N))
            o_ref[0, rows[j], :] = jnp.concatenate([op[WKV_HEADS_STEP * j + h] for h in heads], axis=-1)
            for h in heads:
                u = WKV_HEADS_STEP * j + h
                rp_scr[cs[j], h] = rp[u]
                w_scr[cs[j], h] = wm[u]
                h_scr[cs[j], h] = hm[u]
        return carry

    lax.fori_loop(0, n_groups, phase1, 0)

    def phase2(ci, carry):
        c = ci + z * (n_chunks - 1 - 2 * ci)
        d_tot = d_scr[c][0:1, :]
        sw = [_dot3(carry[h], w_scr[c, h]) for h in heads]
        new_s = [carry[h] * d_tot[:, head_sl(h)] - sw[h] + h_scr[c, h] for h in heads]
        for h in heads:
            h_scr[c, h] = carry[h]
        return tuple(new_s)

    s_fin = lax.fori_loop(0, n_chunks, phase2, tuple(s0_ref[0, 0, h] for h in heads))
    for h in heads:
        sout_ref[0, 0, h] = s_fin[h]

    def phase3(it, carry):
        cs = [it * WKV_GROUP + j for j in range(WKV_GROUP)]
        inter = [[_bdot_nt(rp_scr[c, h], h_scr[c, h]) for h in heads] for c in cs]
        for j in range(WKV_GROUP):
            rows = _chunk_rows(cs[j])
            o_ref[0, rows, :] = o_ref[0, rows, :] + jnp.concatenate(inter[j], axis=-1)
        return carry

    lax.fori_loop(0, n_groups, phase3, 0)


def _wkv_scan(p_all, lr, w2pad, a2pad, w0, a0, k_k, k_a, s0, masks, *, n_seq, seq_len, tok_blk0):
    def tok(b):
        return tok_blk0 + b

    width = WKV_HEADS_STEP * WKV_N
    col = lambda base: (lambda z, b, hq: (tok(b), base // width + hq))
    vec = lambda z, b, hq: (z, 0, hq)
    n_chunks = seq_len // CHUNK
    return pl.pallas_call(
        functools.partial(_wkv_kernel, seq_len=seq_len),
        grid=(2, n_seq, WKV_H // WKV_HEADS_STEP),
        in_specs=[
            pl.BlockSpec((seq_len, width), col(4 * D)),
            pl.BlockSpec((seq_len, width), col(5 * D)),
            pl.BlockSpec((seq_len, width), col(6 * D)),
            pl.BlockSpec((seq_len, LANES), lambda z, b, hq: (tok(b), 0)),
            pl.BlockSpec((seq_len, LANES), lambda z, b, hq: (tok(b), 1)),
            pl.BlockSpec((1, LANES, width), vec),
            pl.BlockSpec((1, LANES, width), vec),
            pl.BlockSpec((1, 1, width), vec),
            pl.BlockSpec((1, 1, width), vec),
            pl.BlockSpec((1, width), lambda z, b, hq: (0, hq)),
            pl.BlockSpec((1, width), lambda z, b, hq: (0, hq)),
            pl.BlockSpec((1, 1, WKV_HEADS_STEP, WKV_N, WKV_N), lambda z, b, hq: (b, z, hq, 0, 0)),
            pl.BlockSpec((1, CHUNK, CHUNK), lambda z, b, hq: (z, 0, 0)),
        ],
        out_specs=[
            pl.BlockSpec((1, seq_len, width), lambda z, b, hq: (z, b, hq)),
            pl.BlockSpec((1, 1, WKV_HEADS_STEP, WKV_N, WKV_N), lambda z, b, hq: (b, z, hq, 0, 0)),
        ],
        out_shape=[
            jax.ShapeDtypeStruct((2, n_seq * seq_len, D), F32),
            jax.ShapeDtypeStruct((n_seq, 2, WKV_H, WKV_N, WKV_N), F32),
        ],
        scratch_shapes=[pltpu.VMEM((n_chunks, WKV_HEADS_STEP, CHUNK, WKV_N), F32)] * 3
        + [pltpu.VMEM((n_chunks, 8, width), F32)],
        compiler_params=pltpu.CompilerParams(
            dimension_semantics=("arbitrary", "arbitrary", "arbitrary"), vmem_limit_bytes=VMEM_LIMIT),
        name="wkv_scan_%d" % seq_len,
    )(p_all, p_all, p_all, lr, lr, w2pad, a2pad, w0, a0, k_k, k_a, s0, masks)


def _dot2(x, w_bf):
    hi, lo = _split2(x)
    return _dot(hi, w_bf) + _dot(lo, w_bf)


def _group_stat(x, ind_ref, bind_ref):
    return _dot2(_dot2(x, ind_ref[...]), bind_ref[...])


N_CTX_TILES = N_CTX_TOK // TM_OUT


def _sum_dirs(i, oc_ref, os_ref):
    is_ctx = i < N_CTX_TILES
    return jnp.where(is_ctx, oc_ref[0], os_ref[0]) + jnp.where(is_ctx, oc_ref[1], os_ref[1])


def _ctx_tile(i):
    return (0, jnp.minimum(i, N_CTX_TILES - 1), 0)


def _smp_tile(i):
    return (0, jnp.maximum(i - N_CTX_TILES, 0), 0)


def _even_out_kernel(x_ref, mod_ref, pc_ref, pw_ref, oc_ref, os_ref, lra_ref, a2_ref, a0_ref, ka_ref, rk_ref,
                     cw_ref, lnw_ref, lnb_ref, ind_ref, bind_ref, wo_ref, out_ref):
    i = pl.program_id(0)
    u = pc_ref[:, 0:D]
    gb = pc_ref[:, D:2 * D]
    gc = pc_ref[:, 2 * D:3 * D]
    zc = pc_ref[:, 3 * D:4 * D]
    xg = gc * u
    row_len = jnp.where(i < N_CTX_TILES, CTX_T, GRID_W)
    pos = lax.broadcasted_iota(jnp.int32, (TM_OUT, 1), 0) & (row_len - 1)
    prev = jnp.where(pos == 0, 0.0, pltpu.roll(xg, 1, 0))
    nxt = jnp.where(pos == row_len - 1, 0.0, pltpu.roll(xg, TM_OUT - 1, 0))
    conv = cw_ref[0:1, :] * prev + cw_ref[1:2, :] * xg + cw_ref[2:3, :] * nxt
    o_conv = _silu(zc) * gb * conv
    r = pw_ref[:, 0:D]
    k = pw_ref[:, D:2 * D]
    v = pw_ref[:, 2 * D:3 * D]
    zw = pw_ref[:, 3 * D:4 * D]
    o = _sum_dirs(i, oc_ref, os_ref)
    mu = _group_stat(o, ind_ref, bind_ref)
    dlt = o - mu
    var = _group_stat(dlt * dlt, ind_ref, bind_ref)
    gn = dlt * lax.rsqrt(var + GN_EPS) * lnw_ref[...] + lnb_ref[...]
    lra = lra_ref[...]
    ic0 = jax.nn.sigmoid(a0_ref[0] + _bdot(lra, a2_ref[0]))
    ic1 = jax.nn.sigmoid(a0_ref[1] + _bdot(lra, a2_ref[1]))
    ksum = k * (2.0 + (ic0 + ic1 - 2.0) * ka_ref[...])
    bonus = _group_stat(r * ksum * rk_ref[...], ind_ref, bind_ref) * float(WKV_N)
    o_wkv = (gn + bonus * v) * _silu(zw)
    y = _dot(o_conv.astype(BF16), wo_ref[0:D, :]) + _dot(o_wkv.astype(BF16), wo_ref[D:2 * D, :])
    gate = mod_ref[0, 2:3, :]
    out_ref[...] = x_ref[...] + gate * y


def _even_out(x, mod3, p_all, o_ctx, o_smp, lr, a2pad, a0, k_a, r_k, conv_w, ln_w, ln_b, ind, bind, wo_bf):
    row = lambda i: (i, 0)
    const2 = lambda i: (0, 0)
    const3 = lambda i: (0, 0, 0)
    return pl.pallas_call(
        _even_out_kernel,
        grid=(N_TOK // TM_OUT,),
        in_specs=[
            pl.BlockSpec((TM_OUT, D), row),
            pl.BlockSpec((1, 3, D), lambda i: (_tile_row(i, TM_OUT), 0, 0)),
            pl.BlockSpec((TM_OUT, 4 * D), lambda i: (i, 0)),
            pl.BlockSpec((TM_OUT, 4 * D), lambda i: (i, 1)),
            pl.BlockSpec((2, TM_OUT, D), _ctx_tile),
            pl.BlockSpec((2, TM_OUT, D), _smp_tile),
            pl.BlockSpec((TM_OUT, LANES), lambda i: (i, 1)),
            pl.BlockSpec((2, LANES, D), const3),
            pl.BlockSpec((2, 1, D), const3),
            pl.BlockSpec((1, D), const2),
            pl.BlockSpec((1, D), const2),
            pl.BlockSpec((3, D), const2),
            pl.BlockSpec((1, D), const2),
            pl.BlockSpec((1, D), const2),
            pl.BlockSpec((D, LANES), const2),
            pl.BlockSpec((LANES, D), const2),
            pl.BlockSpec((2 * D, D), const2),
        ],
        out_specs=pl.BlockSpec((TM_OUT, D), row),
        out_shape=jax.ShapeDtypeStruct((N_TOK, D), F32),
        compiler_params=pltpu.CompilerParams(
            dimension_semantics=("arbitrary",), vmem_limit_bytes=VMEM_LIMIT),
        name="even_out",
    )(x, mod3, p_all, p_all, o_ctx, o_smp, lr, a2pad, a0, k_a, r_k, conv_w, ln_w, ln_b, ind, bind, wo_bf)


def _gla_kernel(q_ref, k_ref, v_ref, lrg_ref, gk2_ref, gkb_ref, s0_ref, m_ref, o_ref, sout_ref,
                qe_scr, st_scr, e_scr, *, seq_len):
    z = pl.program_id(0)
    n_chunks = seq_len // CHUNK
    n_groups = n_chunks // GLA_GROUP
    incl_b = m_ref[0] > 0.5
    gk2 = gk2_ref[0].astype(BF16)
    gkb = gkb_ref[0]
    scale = GLA_DK ** -0.5

    def phase1(it, carry):
        cs = [it * GLA_GROUP + j for j in range(GLA_GROUP)]
        rows = [_chunk_rows(c) for c in cs]
        logit = [_bdot(lrg_ref[rw, :], gk2) + gkb for rw in rows]
        g = [jax.nn.log_sigmoid(x) / GLA_GATE_NORM for x in logit]
        bc, b_last = zip(*[_scan_cumsum(x, z) for x in g])
        qe = [q_ref[rw, :] * scale * jnp.exp(x) for rw, x in zip(rows, bc)]
        ke = [k_ref[rw, :] * jnp.exp(-x) for rw, x in zip(rows, bc)]
        kd = [k_ref[rw, :] * jnp.exp(bl - x) for rw, x, bl in zip(rows, bc, b_last)]
        v = [v_ref[rw, :] for rw in rows]
        att = [jnp.where(incl_b, _bdot_nt(a, b), 0.0) for a, b in zip(qe, ke)]
        o_in = [_bdot(a, b) for a, b in zip(att, v)]
        vtk = [_bdot_tn(a, b) for a, b in zip(v, kd)]
        for j in range(GLA_GROUP):
            o_ref[0, rows[j], :] = o_in[j]
            qe_scr[cs[j]] = qe[j]
            st_scr[cs[j]] = vtk[j]
            e_scr[cs[j]] = jnp.broadcast_to(jnp.exp(b_last[j]), (8, GLA_DK))
        return carry

    lax.fori_loop(0, n_groups, phase1, 0)

    def phase2(ci, st):
        c = ci + z * (n_chunks - 1 - 2 * ci)
        new_st = st * e_scr[c][0:1, :] + st_scr[c]
        st_scr[c] = st
        return new_st

    st_fin = lax.fori_loop(0, n_chunks, phase2, s0_ref[0, 0, 0].T)
    sout_ref[0, 0, 0] = st_fin.T

    def phase3(it, carry):
        cs = [it * GLA_GROUP + j for j in range(GLA_GROUP)]
        inter = [_bdot_nt(qe_scr[c], st_scr[c]) for c in cs]
        for j in range(GLA_GROUP):
            rows = _chunk_rows(cs[j])
            o_ref[0, rows, :] = o_ref[0, rows, :] + inter[j]
        return carry

    lax.fori_loop(0, n_groups, phase3, 0)


def _gla_scan(p_all, lrg, gk2pad, gkb, s0, masks, *, n_seq, seq_len, tok_blk0):
    def tok(b):
        return tok_blk0 + b

    n_chunks = seq_len // CHUNK
    return pl.pallas_call(
        functools.partial(_gla_kernel, seq_len=seq_len),
        grid=(2, n_seq, GLA_H),
        in_specs=[
            pl.BlockSpec((seq_len, GLA_DK), lambda z, b, h: (tok(b), h)),
            pl.BlockSpec((seq_len, GLA_DK), lambda z, b, h: (tok(b), GLA_H + h)),
            pl.BlockSpec((seq_len, GLA_DV), lambda z, b, h: (tok(b), GLA_H + h)),
            pl.BlockSpec((seq_len, LANES), lambda z, b, h: (tok(b), 0)),
            pl.BlockSpec((1, LANES, GLA_DK), lambda z, b, h: (z, 0, h)),
            pl.BlockSpec((1, 1, GLA_DK), lambda z, b, h: (z, 0, h)),
            pl.BlockSpec((1, 1, 1, GLA_DK, GLA_DV), lambda z, b, h: (b, z, h, 0, 0)),
            pl.BlockSpec((1, CHUNK, CHUNK), lambda z, b, h: (z, 0, 0)),
        ],
        out_specs=[
            pl.BlockSpec((1, seq_len, GLA_DV), lambda z, b, h: (z, b, h)),
            pl.BlockSpec((1, 1, 1, GLA_DK, GLA_DV), lambda z, b, h: (b, z, h, 0, 0)),
        ],
        out_shape=[
            jax.ShapeDtypeStruct((2, n_seq * seq_len, D), F32),
            jax.ShapeDtypeStruct((n_seq, 2, GLA_H, GLA_DK, GLA_DV), F32),
        ],
        scratch_shapes=[
            pltpu.VMEM((n_chunks, CHUNK, GLA_DK), F32),
            pltpu.VMEM((n_chunks, GLA_DV, GLA_DK), F32),
            pltpu.VMEM((n_chunks, 8, GLA_DK), F32),
        ],
        compiler_params=pltpu.CompilerParams(
            dimension_semantics=("arbitrary", "arbitrary", "arbitrary"), vmem_limit_bytes=VMEM_LIMIT),
        name="gla_scan_%d" % seq_len,
    )(p_all, p_all, p_all, lrg, gk2pad, gkb, s0, masks)


def _odd_out_kernel(x_ref, mod_ref, zg_ref, oc_ref, os_ref, gn_ref, ind_ref, bind_ref, wo_ref, fg_ref, out_ref):
    o = _sum_dirs(pl.program_id(0), oc_ref, os_ref)
    ms = _group_stat(o * o, ind_ref, bind_ref)
    on = o * lax.rsqrt(ms + NORM_EPS) * gn_ref[...]
    og = on * _silu(zg_ref[...])
    y = _dot(og.astype(BF16), wo_ref[...])
    gate = mod_ref[0, 2:3, :]
    x = x_ref[...] + gate * y
    out_ref[...] = x * lax.rsqrt(jnp.mean(x * x, axis=-1, keepdims=True) + NORM_EPS) * fg_ref[...]


def _odd_out(x, mod3, p_all, o_ctx, o_smp, g_norm, ind, bind, wo_bf, final_g):
    row = lambda i: (i, 0)
    const2 = lambda i: (0, 0)
    return pl.pallas_call(
        _odd_out_kernel,
        grid=(N_TOK // TM_OUT,),
        in_specs=[
            pl.BlockSpec((TM_OUT, D), row),
            pl.BlockSpec((1, 3, D), lambda i: (_tile_row(i, TM_OUT), 0, 0)),
            pl.BlockSpec((TM_OUT, D), lambda i: (i, 2)),
            pl.BlockSpec((2, TM_OUT, D), _ctx_tile),
            pl.BlockSpec((2, TM_OUT, D), _smp_tile),
            pl.BlockSpec((1, D), const2),
            pl.BlockSpec((D, LANES), const2),
            pl.BlockSpec((LANES, D), const2),
            pl.BlockSpec((D, D), const2),
            pl.BlockSpec((1, D), const2),
        ],
        out_specs=pl.BlockSpec((TM_OUT, D), row),
        out_shape=jax.ShapeDtypeStruct((N_TOK, D), F32),
        compiler_params=pltpu.CompilerParams(
            dimension_semantics=("arbitrary",), vmem_limit_bytes=VMEM_LIMIT),
        name="odd_out",
    )(x, mod3, p_all, o_ctx, o_smp, g_norm, ind, bind, wo_bf, final_g)


def _group_indicators(group):
    ch = jnp.arange(D) // group
    lane = jnp.arange(LANES)
    hit = (ch[:, None] == lane[None, :]).astype(F32)
    return (hit / float(group)).astype(BF16), hit.T.astype(BF16)


def _pad_dirs(w):
    r = w.shape[1]
    out = jnp.zeros((2, LANES, w.shape[2]), F32)
    out = out.at[0, 0:r].set(w[0])
    out = out.at[1, r:2 * r].set(w[1])
    return out


def kernel(x_prompt, x_sample, state_wkv, state_gla, c, c_ctx, norm_g, ada_w, ada_b, final_g, e_w_in, e_w_out, conv_w, wkv_w0, wkv_w1, wkv_w2, wkv_a0, wkv_a1, wkv_a2, wkv_k_k, wkv_k_a, wkv_r_k, wkv_ln_w, wkv_ln_b, o_w_in, o_w_out, gla_gk1, gla_gk2, gla_gk_b, gla_g_norm):
    x = jnp.concatenate([x_prompt.reshape(N_CTX_TOK, D), x_sample.reshape(N_CTX_TOK, D)], axis=0)
    cvec = jnp.zeros((8, D), F32).at[0].set(c_ctx).at[1:1 + SMP_B].set(c)
    mod = _modulation(cvec, ada_w, ada_b)
    masks = _chunk_masks()

    mod3 = mod[0].reshape(8, 3, D)
    wlr = jnp.concatenate([wkv_w1[0, 0], wkv_w1[0, 1], wkv_a1[0, 0], wkv_a1[0, 1]], axis=1)
    p_e, lr_e = _inproj(x, mod3, norm_g[0:1], e_w_in[0].astype(BF16), wlr.astype(BF16), n_tanh=2 * 64)
    w2pad = _pad_dirs(wkv_w2[0])
    a2pad = _pad_dirs(wkv_a2[0])
    w0 = wkv_w0[0].reshape(2, 1, D)
    a0 = wkv_a0[0].reshape(2, 1, D)
    k_k = wkv_k_k[0].reshape(1, D)
    k_a = wkv_k_a[0].reshape(1, D)
    r_k = wkv_r_k[0].reshape(1, D)
    zero_wkv = jnp.zeros((CTX_B, 2, WKV_H, WKV_N, WKV_N), F32)
    o_ctx, new_wkv = _wkv_scan(p_e, lr_e, w2pad, a2pad, w0, a0, k_k, k_a, zero_wkv, masks,
                               n_seq=CTX_B, seq_len=CTX_T, tok_blk0=0)
    o_smp, _ = _wkv_scan(p_e, lr_e, w2pad, a2pad, w0, a0, k_k, k_a, state_wkv[:, 0], masks,
                         n_seq=SMP_B, seq_len=SMP_T, tok_blk0=N_CTX_TOK // SMP_T)
    ind64, bind64 = _group_indicators(WKV_N)
    x = _even_out(x, mod3, p_e, o_ctx, o_smp, lr_e, a2pad, a0, k_a, r_k, conv_w[0],
                  wkv_ln_w[0].reshape(1, D), wkv_ln_b[0].reshape(1, D), ind64, bind64,
                  e_w_out[0].astype(BF16))

    mod3 = mod[1].reshape(8, 3, D)
    rank = gla_gk1.shape[-1]
    wlr = jnp.zeros((D, LANES), F32).at[:, 0:rank].set(gla_gk1[0, 0]).at[:, rank:2 * rank].set(gla_gk1[0, 1])
    p_o, lr_o = _inproj(x, mod3, norm_g[1:2], o_w_in[0].astype(BF16), wlr.astype(BF16), n_tanh=0)
    gk2pad = _pad_dirs(gla_gk2[0])
    gkb = gla_gk_b[0].reshape(2, 1, GLA_H * GLA_DK)
    zero_gla = jnp.zeros((CTX_B, 2, GLA_H, GLA_DK, GLA_DV), F32)
    g_ctx, new_gla = _gla_scan(p_o, lr_o, gk2pad, gkb, zero_gla, masks,
                               n_seq=CTX_B, seq_len=CTX_T, tok_blk0=0)
    g_smp, _ = _gla_scan(p_o, lr_o, gk2pad, gkb, state_gla[:, 0], masks,
                         n_seq=SMP_B, seq_len=SMP_T, tok_blk0=N_CTX_TOK // SMP_T)
    ind256, bind256 = _group_indicators(GLA_DV)
    y = _odd_out(x, mod3, p_o, g_ctx, g_smp, jnp.tile(gla_g_norm[0], GLA_H).reshape(1, D), ind256, bind256,
                 o_w_out[0].astype(BF16), final_g.reshape(1, D))

    y_prompt = y[:N_CTX_TOK].reshape(CTX_B, CTX_T, D)
    y_sample = y[N_CTX_TOK:].reshape(SMP_B, SMP_T, D)
    return (y_prompt, y_sample, new_wkv[:, None], new_gla[:, None])
```

```python
import functools
import math

import jax
import jax.numpy as jnp
from jax import lax
from jax.experimental import pallas as pl
from jax.experimental.pallas import tpu as pltpu

F32 = jnp.float32
BF16 = jnp.bfloat16
HI = lax.Precision.HIGHEST

D = 1024
N_CTX_TOK = 16 * 256
N_TOK = 2 * N_CTX_TOK
CTX_B, CTX_T = 16, 256
SMP_B, SMP_T = 2, 2048
GRID_W = 64
WKV_H, WKV_N = 16, 64
GLA_H, GLA_DK, GLA_DV = 4, 128, 256
GLA_GATE_NORM = 16.0
GN_EPS = 64e-5
NORM_EPS = 1e-6
CHUNK = 64
WKV_GROUP = 4
WKV_HEADS_STEP = 4
GLA_GROUP = 4
LANES = 128
TM_PROJ = 1024
TN_PROJ = 1024
TM_OUT = 256
VMEM_LIMIT = 56 * 1024 * 1024


def _silu(x):
    return x * jax.nn.sigmoid(x)


def _dot(a, b, precision=None):
    return jnp.dot(a, b, preferred_element_type=F32, precision=precision)


def _dot_nt(a, b, precision=None):
    return lax.dot_general(a, b, (((1,), (1,)), ((), ())), preferred_element_type=F32, precision=precision)


def _dot_tn(a, b, precision=None):
    return lax.dot_general(a, b, (((0,), (0,)), ((), ())), preferred_element_type=F32, precision=precision)


def _bdot(a, b):
    return _dot(a.astype(BF16), b.astype(BF16))


def _bdot_nt(a, b):
    return _dot_nt(a.astype(BF16), b.astype(BF16))


def _bdot_tn(a, b):
    return _dot_tn(a.astype(BF16), b.astype(BF16))


def _split2(x):
    hi = x.astype(BF16)
    lo = (x - hi.astype(F32)).astype(BF16)
    return hi, lo


def _dot3(a, b):
    ah, al = _split2(a)
    bh, bl = _split2(b)
    return _dot(ah, bh) + _dot(al, bh) + _dot(ah, bl)


def _dot3_nt(a, b):
    ah, al = _split2(a)
    bh, bl = _split2(b)
    return _dot_nt(ah, bh) + _dot_nt(al, bh) + _dot_nt(ah, bl)


def _scan_cumsum(x, z):
    n = x.shape[0]
    row = lax.broadcasted_iota(jnp.int32, (n, 1), 0)
    pre = x
    s = 1
    while s < n:
        pre = pre + jnp.where(row >= s, pltpu.roll(pre, s, 0), 0.0)
        s *= 2
    tot = pre[n - 1:n, :]
    return jnp.where(z == 0, pre, tot - pre + x), tot


def _tile_row(i, tile):
    ctx_tiles = N_CTX_TOK // tile
    per_req = SMP_T // tile
    return jnp.where(i < ctx_tiles, 0, 1 + (i - ctx_tiles) // per_req)


def _mod_kernel(c_ref, w_ref, b_ref, o_ref):
    cf = _silu(c_ref[...])
    o_ref[0] = _dot(cf, w_ref[0], HI) + b_ref[0]


def _modulation(cvec, ada_w, ada_b):
    depth = ada_w.shape[0]
    return pl.pallas_call(
        _mod_kernel,
        grid=(depth, 3),
        in_specs=[
            pl.BlockSpec((8, D), lambda l, j: (0, 0)),
            pl.BlockSpec((1, D, D), lambda l, j: (l, 0, j)),
            pl.BlockSpec((1, 1, D), lambda l, j: (l, 0, j)),
        ],
        out_specs=pl.BlockSpec((1, 8, D), lambda l, j: (l, 0, j)),
        out_shape=jax.ShapeDtypeStruct((depth, 8, 3 * D), F32),
        compiler_params=pltpu.CompilerParams(
            dimension_semantics=("arbitrary", "arbitrary"), vmem_limit_bytes=VMEM_LIMIT),
        name="adaln_mod",
    )(cvec, ada_w, ada_b.reshape(depth, 1, 3 * D))


def _inproj_kernel(xa_ref, xb_ref, mod_ref, g_ref, w_ref, wlr_ref, p_ref, lr_ref, h_scr, *, n_tanh):
    i = pl.program_id(0)
    j = pl.program_id(1)

    @pl.when(j == 0)
    def _():
        x = jnp.where(i < N_CTX_TOK // TM_PROJ, xa_ref[...], xb_ref[...])
        y = x * lax.rsqrt(jnp.mean(x * x, axis=-1, keepdims=True) + NORM_EPS) * g_ref[...]
        shift = mod_ref[0, 0:1, :]
        scale = mod_ref[0, 1:2, :]
        h = y * (1.0 + scale) + shift
        hb = h.astype(BF16)
        h_scr[...] = hb
        lr = _dot(hb, wlr_ref[...].astype(BF16))
        if n_tanh:
            lane = lax.broadcasted_iota(jnp.int32, lr.shape, 1)
            lr = jnp.where(lane < n_tanh, jnp.tanh(lr), lr)
        lr_ref[...] = lr

    p_ref[...] = _dot(h_scr[...], w_ref[...].astype(BF16))


def _inproj(x_ctx, x_smp, smp_tile0, mod3, g, w, wlr, n_tanh):
    n_out = w.shape[1]
    n_lr = wlr.shape[1]
    n_ctx = N_CTX_TOK // TM_PROJ
    return pl.pallas_call(
        functools.partial(_inproj_kernel, n_tanh=n_tanh),
        grid=(N_TOK // TM_PROJ, n_out // TN_PROJ),
        in_specs=[
            pl.BlockSpec((TM_PROJ, D), lambda i, j: (jnp.minimum(i, n_ctx - 1), 0)),
            pl.BlockSpec((TM_PROJ, D), lambda i, j: (smp_tile0 + jnp.maximum(i - n_ctx, 0), 0)),
            pl.BlockSpec((1, 3, D), lambda i, j: (_tile_row(i, TM_PROJ), 0, 0)),
            pl.BlockSpec((1, D), lambda i, j: (0, 0)),
            pl.BlockSpec((D, TN_PROJ), lambda i, j: (0, j)),
            pl.BlockSpec((D, n_lr), lambda i, j: (0, 0)),
        ],
        out_specs=[
            pl.BlockSpec((TM_PROJ, TN_PROJ), lambda i, j: (i, j)),
            pl.BlockSpec((TM_PROJ, n_lr), lambda i, j: (i, 0)),
        ],
        out_shape=[
            jax.ShapeDtypeStruct((N_TOK, n_out), F32),
            jax.ShapeDtypeStruct((N_TOK, n_lr), F32),
        ],
        scratch_shapes=[pltpu.VMEM((TM_PROJ, D), BF16)],
        compiler_params=pltpu.CompilerParams(
            dimension_semantics=("arbitrary", "arbitrary"), vmem_limit_bytes=VMEM_LIMIT),
        name="inproj",
    )(x_ctx, x_smp, mod3, g, w, wlr)


def _chunk_masks():
    idx = jnp.arange(CHUNK)
    lower = (idx[:, None] >= idx[None, :]).astype(F32)
    return jnp.stack([lower, lower.T], axis=0)


def _chunk_rows(c):
    return pl.ds(pl.multiple_of(c * CHUNK, CHUNK), CHUNK)


def _inv_unit_tri(lmats):
    row = lax.broadcasted_iota(jnp.int32, (CHUNK, CHUNK), 0)
    col = lax.broadcasted_iota(jnp.int32, (CHUNK, CHUNK), 1)
    eye = (row == col).astype(F32)

    def same_block(n):
        sh = n.bit_length() - 1
        return lax.shift_right_logical(row, sh) == lax.shift_right_logical(col, sh)

    blk16 = same_block(16)
    l16 = [jnp.where(blk16, m, 0.0) for m in lmats]
    x = [eye - m for m in l16]
    p = [_bdot(m, m) for m in l16]
    for step in range(3):
        x = [xi + _bdot(xi, pi) for xi, pi in zip(x, p)]
        if step < 2:
            p = [_bdot(pi, pi) for pi in p]
    n = 16
    while n < CHUNK:
        off = same_block(2 * n) & jnp.logical_not(same_block(n))
        xl = [_bdot(xi, jnp.where(off, m, 0.0)) for xi, m in zip(x, lmats)]
        x = [xi - _bdot(yi, xi) for xi, yi in zip(x, xl)]
        n *= 2
    return x


def _wkv_kernel(r_ref, k_ref, v_ref, lrw_ref, lra_ref, w2_ref, a2_ref, w0_ref, a0_ref,
                kk_ref, ka_ref, m_ref, *rest, seq_len, zero_init):
    s0_ref = None if zero_init else rest[0]
    o_ref, sout_ref, rp_scr, w_scr, h_scr, d_scr = rest[-6:]
    z = pl.program_id(0)
    n_chunks = seq_len // CHUNK
    n_groups = n_chunks // WKV_GROUP
    heads = range(WKV_HEADS_STEP)
    row = lax.broadcasted_iota(jnp.int32, (CHUNK, CHUNK), 0)
    col = lax.broadcasted_iota(jnp.int32, (CHUNK, CHUNK), 1)
    incl_b = m_ref[0] > 0.5
    strict_b = incl_b & (row != col)
    w2 = w2_ref[0].astype(BF16)
    a2 = a2_ref[0].astype(BF16)
    w0 = w0_ref[0]
    a0 = a0_ref[0]
    k_k = kk_ref[...]
    k_a = ka_ref[...]

    def head_sl(h):
        return slice(h * WKV_N, (h + 1) * WKV_N)

    def phase1(it, carry):
        cs = [it * WKV_GROUP + j for j in range(WKV_GROUP)]
        rows = [_chunk_rows(c) for c in cs]
        r = [r_ref[rw, :] for rw in rows]
        k = [k_ref[rw, :] for rw in rows]
        v = [v_ref[rw, :] for rw in rows]
        w_raw = [w0 + _bdot(lrw_ref[rw, :], w2) for rw in rows]
        a = [jax.nn.sigmoid(a0 + _bdot(lra_ref[rw, :], a2)) for rw in rows]
        lw = [-math.exp(-0.5) * jax.nn.sigmoid(x) for x in w_raw]
        g, gtot = zip(*[_scan_cumsum(x, z) for x in lw])
        kt, rt, kh, bh, kp, bp, vh = [], [], [], [], [], [], []
        for j in range(WKV_GROUP):
            kkr = k[j] * k_k
            kmod = k[j] * (1.0 + (a[j] - 1.0) * k_a)
            e_g = jnp.exp(g[j])
            e_gx = jnp.exp(g[j] - lw[j])
            e_ng = jnp.exp(-g[j])
            e_gc = jnp.exp(gtot[j] - g[j])
            for h in heads:
                sl = head_sl(h)
                kk_h = kkr[:, sl]
                nrm = jnp.sqrt(jnp.sum(kk_h * kk_h, axis=-1, keepdims=True))
                kk_h = kk_h / jnp.maximum(nrm, 1e-12)
                b_h = kk_h * a[j][:, sl]
                kt.append(kk_h * e_gx[:, sl])
                rt.append(r[j][:, sl] * e_g[:, sl])
                kh.append(kmod[:, sl] * e_ng[:, sl])
                bh.append(b_h * e_ng[:, sl])
                kp.append(kmod[:, sl] * e_gc[:, sl])
                bp.append(b_h * e_gc[:, sl])
                vh.append(v[j][:, sl])
        units = range(WKV_HEADS_STEP * WKV_GROUP)
        gram = [_bdot_nt(jnp.concatenate([kt[u], rt[u]], axis=0), jnp.concatenate([kh[u], bh[u]], axis=0))
                for u in units]
        mkk = [jnp.where(strict_b, gm[:CHUNK, :CHUNK], 0.0) for gm in gram]
        lmat = [jnp.where(strict_b, gm[:CHUNK, CHUNK:], 0.0) for gm in gram]
        mrk = [jnp.where(incl_b, gm[CHUNK:, :CHUNK], 0.0) for gm in gram]
        mrb = [jnp.where(incl_b, gm[CHUNK:, CHUNK:], 0.0) for gm in gram]
        tinv = _inv_unit_tri(lmat)
        mv = [_bdot(mkk[u], vh[u]) for u in units]
        mrkv = [_bdot(mrk[u], vh[u]) for u in units]
        vtk = [_bdot_tn(vh[u], kp[u]) for u in units]
        pq = [_bdot(tinv[u], jnp.concatenate([kt[u], mv[u]], axis=-1)) for u in units]
        corr = [_bdot(mrb[u], pq[u]) for u in units]
        rp = [rt[u] - corr[u][:, :WKV_N] for u in units]
        op = [mrkv[u] - corr[u][:, WKV_N:] for u in units]
        wq = [_bdot_tn(pq[u], bp[u]) for u in units]
        wm = [x[:WKV_N] for x in wq]
        hm = [vtk[u] - wq[u][WKV_N:] for u in units]
        for j in range(WKV_GROUP):
            d_scr[cs[j]] = jnp.broadcast_to(jnp.exp(gtot[j]), (8, WKV_HEADS_STEP * WKV_N))
            o_ref[0, rows[j], :] = jnp.concatenate([op[WKV_HEADS_STEP * j + h] for h in heads], axis=-1)
            for h in heads:
                u = WKV_HEADS_STEP * j + h
                rp_scr[cs[j], h] = rp[u]
                w_scr[cs[j], h] = wm[u]
                h_scr[cs[j], h] = hm[u]
        return carry

    lax.fori_loop(0, n_groups, phase1, 0)

    def phase2(ci, carry):
        c = ci + z * (n_chunks - 1 - 2 * ci)
        d_tot = d_scr[c][0:1, :]
        sw = [_dot3(carry[h], w_scr[c, h]) for h in heads]
        new_s = [carry[h] * d_tot[:, head_sl(h)] - sw[h] + h_scr[c, h] for h in heads]
        for h in heads:
            h_scr[c, h] = carry[h]
        return tuple(new_s)

    if zero_init:
        s_init = tuple(jnp.zeros((WKV_N, WKV_N), F32) for h in heads)
    else:
        s_init = tuple(s0_ref[0, 0, h] for h in heads)
    s_fin = lax.fori_loop(0, n_chunks, phase2, s_init)
    for h in heads:
        sout_ref[0, 0, h] = s_fin[h]

    def phase3(it, carry):
        cs = [it * WKV_GROUP + j for j in range(WKV_GROUP)]
        inter = [[_bdot_nt(rp_scr[c, h], h_scr[c, h]) for h in heads] for c in cs]
        for j in range(WKV_GROUP):
            rows = _chunk_rows(cs[j])
            o_ref[0, rows, :] = o_ref[0, rows, :] + jnp.concatenate(inter[j], axis=-1)
        return carry

    lax.fori_loop(0, n_groups, phase3, 0)


def _wkv_scan(p_all, lr, w2pad, a2pad, w0, a0, k_k, k_a, s0, masks, *, n_seq, seq_len, tok_blk0):
    def tok(b):
        return tok_blk0 + b

    width = WKV_HEADS_STEP * WKV_N
    col = lambda base: (lambda z, b, hq: (tok(b), base // width + hq))
    vec = lambda z, b, hq: (z, 0, hq)
    n_chunks = seq_len // CHUNK
    state_spec = pl.BlockSpec((1, 1, WKV_HEADS_STEP, WKV_N, WKV_N), lambda z, b, hq: (b, z, hq, 0, 0))
    init = () if s0 is None else (s0,)
    return pl.pallas_call(
        functools.partial(_wkv_kernel, seq_len=seq_len, zero_init=s0 is None),
        grid=(2, n_seq, WKV_H // WKV_HEADS_STEP),
        in_specs=[
            pl.BlockSpec((seq_len, width), col(4 * D)),
            pl.BlockSpec((seq_len, width), col(5 * D)),
            pl.BlockSpec((seq_len, width), col(6 * D)),
            pl.BlockSpec((seq_len, LANES), lambda z, b, hq: (tok(b), 0)),
            pl.BlockSpec((seq_len, LANES), lambda z, b, hq: (tok(b), 1)),
            pl.BlockSpec((1, LANES, width), vec),
            pl.BlockSpec((1, LANES, width), vec),
            pl.BlockSpec((1, 1, width), vec),
            pl.BlockSpec((1, 1, width), vec),
            pl.BlockSpec((1, width), lambda z, b, hq: (0, hq)),
            pl.BlockSpec((1, width), lambda z, b, hq: (0, hq)),
            pl.BlockSpec((1, CHUNK, CHUNK), lambda z, b, hq: (z, 0, 0)),
        ] + [state_spec] * len(init),
        out_specs=[
            pl.BlockSpec((1, seq_len, width), lambda z, b, hq: (z, b, hq)),
            state_spec,
        ],
        out_shape=[
            jax.ShapeDtypeStruct((2, n_seq * seq_len, D), F32),
            jax.ShapeDtypeStruct((n_seq, 2, WKV_H, WKV_N, WKV_N), F32),
        ],
        scratch_shapes=[pltpu.VMEM((n_chunks, WKV_HEADS_STEP, CHUNK, WKV_N), F32)] * 3
        + [pltpu.VMEM((n_chunks, 8, width), F32)],
        compiler_params=pltpu.CompilerParams(
            dimension_semantics=("arbitrary", "arbitrary", "arbitrary"), vmem_limit_bytes=VMEM_LIMIT),
        name="wkv_scan_%d" % seq_len,
    )(p_all, p_all, p_all, lr, lr, w2pad, a2pad, w0, a0, k_k, k_a, masks, *init)


def _dot2(x, w_bf):
    hi, lo = _split2(x)
    return _dot(hi, w_bf) + _dot(lo, w_bf)


def _group_stat(x, ind_ref, bind_ref):
    return _dot2(_dot2(x, ind_ref[...]), bind_ref[...])


N_CTX_TILES = N_CTX_TOK // TM_OUT


def _sum_dirs(i, oc_ref, os_ref):
    is_ctx = i < N_CTX_TILES
    return jnp.where(is_ctx, oc_ref[0], os_ref[0]) + jnp.where(is_ctx, oc_ref[1], os_ref[1])


def _ctx_tile(i):
    return (0, jnp.minimum(i, N_CTX_TILES - 1), 0)


def _smp_tile(i):
    return (0, jnp.maximum(i - N_CTX_TILES, 0), 0)


def _even_out_kernel(xc_ref, xs_ref, mod_ref, pc_ref, pw_ref, oc_ref, os_ref, lra_ref, a2_ref, a0_ref, ka_ref,
                     rk_ref, cw_ref, lnw_ref, lnb_ref, ind_ref, bind_ref, wo_ref, out_ref):
    i = pl.program_id(0)
    u = pc_ref[:, 0:D]
    gb = pc_ref[:, D:2 * D]
    gc = pc_ref[:, 2 * D:3 * D]
    zc = pc_ref[:, 3 * D:4 * D]
    xg = gc * u
    row_len = jnp.where(i < N_CTX_TILES, CTX_T, GRID_W)
    pos = lax.broadcasted_iota(jnp.int32, (TM_OUT, 1), 0) & (row_len - 1)
    prev = jnp.where(pos == 0, 0.0, pltpu.roll(xg, 1, 0))
    nxt = jnp.where(pos == row_len - 1, 0.0, pltpu.roll(xg, TM_OUT - 1, 0))
    conv = cw_ref[0:1, :] * prev + cw_ref[1:2, :] * xg + cw_ref[2:3, :] * nxt
    o_conv = _silu(zc) * gb * conv
    r = pw_ref[:, 0:D]
    k = pw_ref[:, D:2 * D]
    v = pw_ref[:, 2 * D:3 * D]
    zw = pw_ref[:, 3 * D:4 * D]
    o = _sum_dirs(i, oc_ref, os_ref)
    mu = _group_stat(o, ind_ref, bind_ref)
    dlt = o - mu
    var = _group_stat(dlt * dlt, ind_ref, bind_ref)
    gn = dlt * lax.rsqrt(var + GN_EPS) * lnw_ref[...] + lnb_ref[...]
    lra = lra_ref[...]
    ic0 = jax.nn.sigmoid(a0_ref[0] + _bdot(lra, a2_ref[0]))
    ic1 = jax.nn.sigmoid(a0_ref[1] + _bdot(lra, a2_ref[1]))
    ksum = k * (2.0 + (ic0 + ic1 - 2.0) * ka_ref[...])
    bonus = _group_stat(r * ksum * rk_ref[...], ind_ref, bind_ref) * float(WKV_N)
    o_wkv = (gn + bonus * v) * _silu(zw)
    y = _dot(o_conv.astype(BF16), wo_ref[0:D, :]) + _dot(o_wkv.astype(BF16), wo_ref[D:2 * D, :])
    gate = mod_ref[0, 2:3, :]
    out_ref[...] = jnp.where(i < N_CTX_TILES, xc_ref[...], xs_ref[...]) + gate * y


def _even_out(x_ctx, x_smp, mod3, p_all, o_ctx, o_smp, lr, a2pad, a0, k_a, r_k, conv_w, ln_w, ln_b, ind, bind,
              wo_bf):
    row = lambda i: (i, 0)
    const2 = lambda i: (0, 0)
    const3 = lambda i: (0, 0, 0)
    return pl.pallas_call(
        _even_out_kernel,
        grid=(N_TOK // TM_OUT,),
        in_specs=[
            pl.BlockSpec((TM_OUT, D), lambda i: _ctx_tile(i)[1:]),
            pl.BlockSpec((TM_OUT, D), lambda i: _smp_tile(i)[1:]),
            pl.BlockSpec((1, 3, D), lambda i: (_tile_row(i, TM_OUT), 0, 0)),
            pl.BlockSpec((TM_OUT, 4 * D), lambda i: (i, 0)),
            pl.BlockSpec((TM_OUT, 4 * D), lambda i: (i, 1)),
            pl.BlockSpec((2, TM_OUT, D), _ctx_tile),
            pl.BlockSpec((2, TM_OUT, D), _smp_tile),
            pl.BlockSpec((TM_OUT, LANES), lambda i: (i, 1)),
            pl.BlockSpec((2, LANES, D), const3),
            pl.BlockSpec((2, 1, D), const3),
            pl.BlockSpec((1, D), const2),
            pl.BlockSpec((1, D), const2),
            pl.BlockSpec((3, D), const2),
            pl.BlockSpec((1, D), const2),
            pl.BlockSpec((1, D), const2),
            pl.BlockSpec((D, LANES), const2),
            pl.BlockSpec((LANES, D), const2),
            pl.BlockSpec((2 * D, D), const2),
        ],
        out_specs=pl.BlockSpec((TM_OUT, D), row),
        out_shape=jax.ShapeDtypeStruct((N_TOK, D), F32),
        compiler_params=pltpu.CompilerParams(
            dimension_semantics=("arbitrary",), vmem_limit_bytes=VMEM_LIMIT),
        name="even_out",
    )(x_ctx, x_smp, mod3, p_all, p_all, o_ctx, o_smp, lr, a2pad, a0, k_a, r_k, conv_w, ln_w, ln_b, ind, bind, wo_bf)


def _gla_kernel(q_ref, k_ref, v_ref, lrg_ref, gk2_ref, gkb_ref, m_ref, *rest, seq_len, zero_init):
    s0_ref = None if zero_init else rest[0]
    o_ref, sout_ref, qe_scr, st_scr, e_scr = rest[-5:]
    z = pl.program_id(0)
    n_chunks = seq_len // CHUNK
    n_groups = n_chunks // GLA_GROUP
    incl_b = m_ref[0] > 0.5
    gk2 = gk2_ref[0].astype(BF16)
    gkb = gkb_ref[0]
    scale = GLA_DK ** -0.5

    def phase1(it, carry):
        cs = [it * GLA_GROUP + j for j in range(GLA_GROUP)]
        rows = [_chunk_rows(c) for c in cs]
        logit = [_bdot(lrg_ref[rw, :], gk2) + gkb for rw in rows]
        g = [jax.nn.log_sigmoid(x) / GLA_GATE_NORM for x in logit]
        bc, b_last = zip(*[_scan_cumsum(x, z) for x in g])
        qe = [q_ref[rw, :] * scale * jnp.exp(x) for rw, x in zip(rows, bc)]
        ke = [k_ref[rw, :] * jnp.exp(-x) for rw, x in zip(rows, bc)]
        kd = [k_ref[rw, :] * jnp.exp(bl - x) for rw, x, bl in zip(rows, bc, b_last)]
        v = [v_ref[rw, :] for rw in rows]
        att = [jnp.where(incl_b, _bdot_nt(a, b), 0.0) for a, b in zip(qe, ke)]
        o_in = [_bdot(a, b) for a, b in zip(att, v)]
        vtk = [_bdot_tn(a, b) for a, b in zip(v, kd)]
        for j in range(GLA_GROUP):
            o_ref[0, rows[j], :] = o_in[j]
            qe_scr[cs[j]] = qe[j]
            st_scr[cs[j]] = vtk[j]
            e_scr[cs[j]] = jnp.broadcast_to(jnp.exp(b_last[j]), (8, GLA_DK))
        return carry

    lax.fori_loop(0, n_groups, phase1, 0)

    def phase2(ci, st):
        c = ci + z * (n_chunks - 1 - 2 * ci)
        new_st = st * e_scr[c][0:1, :] + st_scr[c]
        st_scr[c] = st
        return new_st

    st_init = jnp.zeros((GLA_DV, GLA_DK), F32) if zero_init else s0_ref[0, 0, 0].T
    st_fin = lax.fori_loop(0, n_chunks, phase2, st_init)
    sout_ref[0, 0, 0] = st_fin.T

    def phase3(it, carry):
        cs = [it * GLA_GROUP + j for j in range(GLA_GROUP)]
        inter = [_bdot_nt(qe_scr[c], st_scr[c]) for c in cs]
        for j in range(GLA_GROUP):
            rows = _chunk_rows(cs[j])
            o_ref[0, rows, :] = o_ref[0, rows, :] + inter[j]
        return carry

    lax.fori_loop(0, n_groups, phase3, 0)


def _gla_scan(p_all, lrg, gk2pad, gkb, s0, masks, *, n_seq, seq_len, tok_blk0):
    def tok(b):
        return tok_blk0 + b

    n_chunks = seq_len // CHUNK
    init = () if s0 is None else (s0,)
    return pl.pallas_call(
        functools.partial(_gla_kernel, seq_len=seq_len, zero_init=s0 is None),
        grid=(2, n_seq, GLA_H),
        in_specs=[
            pl.BlockSpec((seq_len, GLA_DK), lambda z, b, h: (tok(b), h)),
            pl.BlockSpec((seq_len, GLA_DK), lambda z, b, h: (tok(b), GLA_H + h)),
            pl.BlockSpec((seq_len, GLA_DV), lambda z, b, h: (tok(b), GLA_H + h)),
            pl.BlockSpec((seq_len, LANES), lambda z, b, h: (tok(b), 0)),
            pl.BlockSpec((1, LANES, GLA_DK), lambda z, b, h: (z, 0, h)),
            pl.BlockSpec((1, 1, GLA_DK), lambda z, b, h: (z, 0, h)),
            pl.BlockSpec((1, CHUNK, CHUNK), lambda z, b, h: (z, 0, 0)),
        ] + [pl.BlockSpec((1, 1, 1, GLA_DK, GLA_DV), lambda z, b, h: (b, z, h, 0, 0))] * len(init),
        out_specs=[
            pl.BlockSpec((1, seq_len, GLA_DV), lambda z, b, h: (z, b, h)),
            pl.BlockSpec((1, 1, 1, GLA_DK, GLA_DV), lambda z, b, h: (b, z, h, 0, 0)),
        ],
        out_shape=[
            jax.ShapeDtypeStruct((2, n_seq * seq_len, D), F32),
            jax.ShapeDtypeStruct((n_seq, 2, GLA_H, GLA_DK, GLA_DV), F32),
        ],
        scratch_shapes=[
            pltpu.VMEM((n_chunks, CHUNK, GLA_DK), F32),
            pltpu.VMEM((n_chunks, GLA_DV, GLA_DK), F32),
            pltpu.VMEM((n_chunks, 8, GLA_DK), F32),
        ],
        compiler_params=pltpu.CompilerParams(
            dimension_semantics=("arbitrary", "arbitrary", "arbitrary"), vmem_limit_bytes=VMEM_LIMIT),
        name="gla_scan_%d" % seq_len,
    )(p_all, p_all, p_all, lrg, gk2pad, gkb, masks, *init)


def _odd_out_kernel(x_ref, mod_ref, zg_ref, oc_ref, os_ref, gn_ref, ind_ref, bind_ref, wo_ref, fg_ref,
                    yc_ref, ys_ref):
    i = pl.program_id(0)
    o = _sum_dirs(i, oc_ref, os_ref)
    ms = _group_stat(o * o, ind_ref, bind_ref)
    on = o * lax.rsqrt(ms + NORM_EPS) * gn_ref[...]
    og = on * _silu(zg_ref[...])
    y = _dot(og.astype(BF16), wo_ref[...])
    gate = mod_ref[0, 2:3, :]
    x = x_ref[...] + gate * y
    out = x * lax.rsqrt(jnp.mean(x * x, axis=-1, keepdims=True) + NORM_EPS) * fg_ref[...]

    @pl.when(i < N_CTX_TILES)
    def _():
        yc_ref[...] = out

    @pl.when(i >= N_CTX_TILES)
    def _():
        ys_ref[...] = out


def _odd_out(x, mod3, p_all, o_ctx, o_smp, g_norm, ind, bind, wo_bf, final_g):
    row = lambda i: (i, 0)
    const2 = lambda i: (0, 0)
    return pl.pallas_call(
        _odd_out_kernel,
        grid=(N_TOK // TM_OUT,),
        in_specs=[
            pl.BlockSpec((TM_OUT, D), row),
            pl.BlockSpec((1, 3, D), lambda i: (_tile_row(i, TM_OUT), 0, 0)),
            pl.BlockSpec((TM_OUT, D), lambda i: (i, 2)),
            pl.BlockSpec((2, TM_OUT, D), _ctx_tile),
            pl.BlockSpec((2, TM_OUT, D), _smp_tile),
            pl.BlockSpec((1, D), const2),
            pl.BlockSpec((D, LANES), const2),
            pl.BlockSpec((LANES, D), const2),
            pl.BlockSpec((D, D), const2),
            pl.BlockSpec((1, D), const2),
        ],
        out_specs=[
            pl.BlockSpec((TM_OUT, D), lambda i: _ctx_tile(i)[1:]),
            pl.BlockSpec((TM_OUT, D), lambda i: _smp_tile(i)[1:]),
        ],
        out_shape=[
            jax.ShapeDtypeStruct((N_CTX_TOK, D), F32),
            jax.ShapeDtypeStruct((N_TOK - N_CTX_TOK, D), F32),
        ],
        compiler_params=pltpu.CompilerParams(
            dimension_semantics=("arbitrary",), vmem_limit_bytes=VMEM_LIMIT),
        name="odd_out",
    )(x, mod3, p_all, o_ctx, o_smp, g_norm, ind, bind, wo_bf, final_g)


def _group_indicators(group):
    ch = jnp.arange(D) // group
    lane = jnp.arange(LANES)
    hit = (ch[:, None] == lane[None, :]).astype(F32)
    return (hit / float(group)).astype(BF16), hit.T.astype(BF16)


def _pad_dirs(w):
    r = w.shape[1]
    out = jnp.zeros((2, LANES, w.shape[2]), F32)
    out = out.at[0, 0:r].set(w[0])
    out = out.at[1, r:2 * r].set(w[1])
    return out


def kernel(x_prompt, x_sample, state_wkv, state_gla, c, c_ctx, norm_g, ada_w, ada_b, final_g, e_w_in, e_w_out, conv_w, wkv_w0, wkv_w1, wkv_w2, wkv_a0, wkv_a1, wkv_a2, wkv_k_k, wkv_k_a, wkv_r_k, wkv_ln_w, wkv_ln_b, o_w_in, o_w_out, gla_gk1, gla_gk2, gla_gk_b, gla_g_norm):
    x_ctx = x_prompt.reshape(N_CTX_TOK, D)
    x_smp = x_sample.reshape(N_TOK - N_CTX_TOK, D)
    cvec = jnp.zeros((8, D), F32).at[0].set(c_ctx).at[1:1 + SMP_B].set(c)
    mod = _modulation(cvec, ada_w, ada_b)
    masks = _chunk_masks()

    mod3 = mod[0].reshape(8, 3, D)
    wlr = jnp.concatenate([wkv_w1[0, 0], wkv_w1[0, 1], wkv_a1[0, 0], wkv_a1[0, 1]], axis=1)
    p_e, lr_e = _inproj(x_ctx, x_smp, 0, mod3, norm_g[0:1], e_w_in[0], wlr, n_tanh=2 * 64)
    w2pad = _pad_dirs(wkv_w2[0])
    a2pad = _pad_dirs(wkv_a2[0])
    w0 = wkv_w0[0].reshape(2, 1, D)
    a0 = wkv_a0[0].reshape(2, 1, D)
    k_k = wkv_k_k[0].reshape(1, D)
    k_a = wkv_k_a[0].reshape(1, D)
    r_k = wkv_r_k[0].reshape(1, D)
    o_ctx, new_wkv = _wkv_scan(p_e, lr_e, w2pad, a2pad, w0, a0, k_k, k_a, None, masks,
                               n_seq=CTX_B, seq_len=CTX_T, tok_blk0=0)
    o_smp, _ = _wkv_scan(p_e, lr_e, w2pad, a2pad, w0, a0, k_k, k_a, state_wkv[:, 0], masks,
                         n_seq=SMP_B, seq_len=SMP_T, tok_blk0=N_CTX_TOK // SMP_T)
    ind64, bind64 = _group_indicators(WKV_N)
    x = _even_out(x_ctx, x_smp, mod3, p_e, o_ctx, o_smp, lr_e, a2pad, a0, k_a, r_k, conv_w[0],
                  wkv_ln_w[0].reshape(1, D), wkv_ln_b[0].reshape(1, D), ind64, bind64,
                  e_w_out[0].astype(BF16))

    mod3 = mod[1].reshape(8, 3, D)
    rank = gla_gk1.shape[-1]
    wlr = jnp.zeros((D, LANES), F32).at[:, 0:rank].set(gla_gk1[0, 0]).at[:, rank:2 * rank].set(gla_gk1[0, 1])
    p_o, lr_o = _inproj(x, x, N_CTX_TOK // TM_PROJ, mod3, norm_g[1:2], o_w_in[0], wlr, n_tanh=0)
    gk2pad = _pad_dirs(gla_gk2[0])
    gkb = gla_gk_b[0].reshape(2, 1, GLA_H * GLA_DK)
    g_ctx, new_gla = _gla_scan(p_o, lr_o, gk2pad, gkb, None, masks,
                               n_seq=CTX_B, seq_len=CTX_T, tok_blk0=0)
    g_smp, _ = _gla_scan(p_o, lr_o, gk2pad, gkb, state_gla[:, 0], masks,
                         n_seq=SMP_B, seq_len=SMP_T, tok_blk0=N_CTX_TOK // SMP_T)
    ind256, bind256 = _group_indicators(GLA_DV)
    y_ctx, y_smp = _odd_out(x, mod3, p_o, g_ctx, g_smp, jnp.tile(gla_g_norm[0], GLA_H).reshape(1, D),
                            ind256, bind256, o_w_out[0].astype(BF16), final_g.reshape(1, D))
    return (y_ctx.reshape(CTX_B, CTX_T, D), y_smp.reshape(SMP_B, SMP_T, D), new_wkv[:, None], new_gla[:, None])
```

```python
import functools
import math

import jax
import jax.numpy as jnp
from jax import lax
from jax.experimental import pallas as pl
from jax.experimental.pallas import tpu as pltpu

F32 = jnp.float32
BF16 = jnp.bfloat16
HI = lax.Precision.HIGHEST

D = 1024
N_CTX_TOK = 16 * 256
N_TOK = 2 * N_CTX_TOK
CTX_B, CTX_T = 16, 256
SMP_B, SMP_T = 2, 2048
GRID_W = 64
WKV_H, WKV_N = 16, 64
GLA_H, GLA_DK, GLA_DV = 4, 128, 256
GLA_GATE_NORM = 16.0
GN_EPS = 64e-5
NORM_EPS = 1e-6
CHUNK = 64
WKV_GROUP = 4
WKV_HEADS_STEP = 4
GLA_GROUP = 8
LANES = 128
TM_PROJ = 1024
TN_PROJ = 1024
TM_OUT = 256
VMEM_LIMIT = 56 * 1024 * 1024


def _silu(x):
    return x * jax.nn.sigmoid(x)


def _dot(a, b, precision=None):
    return jnp.dot(a, b, preferred_element_type=F32, precision=precision)


def _dot_nt(a, b, precision=None):
    return lax.dot_general(a, b, (((1,), (1,)), ((), ())), preferred_element_type=F32, precision=precision)


def _dot_tn(a, b, precision=None):
    return lax.dot_general(a, b, (((0,), (0,)), ((), ())), preferred_element_type=F32, precision=precision)


def _bdot(a, b):
    return _dot(a.astype(BF16), b.astype(BF16))


def _bdot_nt(a, b):
    return _dot_nt(a.astype(BF16), b.astype(BF16))


def _bdot_tn(a, b):
    return _dot_tn(a.astype(BF16), b.astype(BF16))


def _split2(x):
    hi = x.astype(BF16)
    lo = (x - hi.astype(F32)).astype(BF16)
    return hi, lo


def _dot3(a, b):
    ah, al = _split2(a)
    bh, bl = _split2(b)
    return _dot(ah, bh) + _dot(al, bh) + _dot(ah, bl)


def _dot3_nt(a, b):
    ah, al = _split2(a)
    bh, bl = _split2(b)
    return _dot_nt(ah, bh) + _dot_nt(al, bh) + _dot_nt(ah, bl)


def _scan_cumsum(x, z):
    n = x.shape[0]
    row = lax.broadcasted_iota(jnp.int32, (n, 1), 0)
    pre = x
    s = 1
    while s < n:
        pre = pre + jnp.where(row >= s, pltpu.roll(pre, s, 0), 0.0)
        s *= 2
    tot = pre[n - 1:n, :]
    return jnp.where(z == 0, pre, tot - pre + x), tot


def _tile_row(i, tile):
    ctx_tiles = N_CTX_TOK // tile
    per_req = SMP_T // tile
    return jnp.where(i < ctx_tiles, 0, 1 + (i - ctx_tiles) // per_req)


def _mod_kernel(c_ref, w_ref, b_ref, o_ref):
    cf = _silu(c_ref[...])
    o_ref[0] = _dot(cf, w_ref[0], HI) + b_ref[0]


def _modulation(cvec, ada_w, ada_b):
    depth = ada_w.shape[0]
    return pl.pallas_call(
        _mod_kernel,
        grid=(depth, 3),
        in_specs=[
            pl.BlockSpec((8, D), lambda l, j: (0, 0)),
            pl.BlockSpec((1, D, D), lambda l, j: (l, 0, j)),
            pl.BlockSpec((1, 1, D), lambda l, j: (l, 0, j)),
        ],
        out_specs=pl.BlockSpec((1, 8, D), lambda l, j: (l, 0, j)),
        out_shape=jax.ShapeDtypeStruct((depth, 8, 3 * D), F32),
        compiler_params=pltpu.CompilerParams(
            dimension_semantics=("arbitrary", "arbitrary"), vmem_limit_bytes=VMEM_LIMIT),
        name="adaln_mod",
    )(cvec, ada_w, ada_b.reshape(depth, 1, 3 * D))


def _inproj_kernel(xa_ref, xb_ref, mod_ref, g_ref, w_ref, wlr_ref, p_ref, lr_ref, h_scr, *, n_tanh):
    i = pl.program_id(0)
    j = pl.program_id(1)

    @pl.when(j == 0)
    def _():
        x = jnp.where(i < N_CTX_TOK // TM_PROJ, xa_ref[...], xb_ref[...])
        y = x * lax.rsqrt(jnp.mean(x * x, axis=-1, keepdims=True) + NORM_EPS) * g_ref[...]
        shift = mod_ref[0, 0:1, :]
        scale = mod_ref[0, 1:2, :]
        h = y * (1.0 + scale) + shift
        hb = h.astype(BF16)
        h_scr[...] = hb
        lr = _dot(hb, wlr_ref[...])
        if n_tanh:
            lane = lax.broadcasted_iota(jnp.int32, lr.shape, 1)
            lr = jnp.where(lane < n_tanh, jnp.tanh(lr), lr)
        lr_ref[...] = lr

    p_ref[...] = _dot(h_scr[...], w_ref[...]).astype(p_ref.dtype)


def _inproj(x_ctx, x_smp, smp_tile0, mod3, g, w_bf, wlr_bf, n_tanh):
    n_out = w_bf.shape[1]
    n_lr = wlr_bf.shape[1]
    n_ctx = N_CTX_TOK // TM_PROJ
    return pl.pallas_call(
        functools.partial(_inproj_kernel, n_tanh=n_tanh),
        grid=(N_TOK // TM_PROJ, n_out // TN_PROJ),
        in_specs=[
            pl.BlockSpec((TM_PROJ, D), lambda i, j: (jnp.minimum(i, n_ctx - 1), 0)),
            pl.BlockSpec((TM_PROJ, D), lambda i, j: (smp_tile0 + jnp.maximum(i - n_ctx, 0), 0)),
            pl.BlockSpec((1, 3, D), lambda i, j: (_tile_row(i, TM_PROJ), 0, 0)),
            pl.BlockSpec((1, D), lambda i, j: (0, 0)),
            pl.BlockSpec((D, TN_PROJ), lambda i, j: (0, j)),
            pl.BlockSpec((D, n_lr), lambda i, j: (0, 0)),
        ],
        out_specs=[
            pl.BlockSpec((TM_PROJ, TN_PROJ), lambda i, j: (i, j)),
            pl.BlockSpec((TM_PROJ, n_lr), lambda i, j: (i, 0)),
        ],
        out_shape=[
            jax.ShapeDtypeStruct((N_TOK, n_out), BF16),
            jax.ShapeDtypeStruct((N_TOK, n_lr), F32),
        ],
        scratch_shapes=[pltpu.VMEM((TM_PROJ, D), BF16)],
        compiler_params=pltpu.CompilerParams(
            dimension_semantics=("arbitrary", "arbitrary"), vmem_limit_bytes=VMEM_LIMIT),
        name="inproj",
    )(x_ctx, x_smp, mod3, g, w_bf, wlr_bf)


def _chunk_masks():
    idx = jnp.arange(CHUNK)
    lower = (idx[:, None] >= idx[None, :]).astype(F32)
    return jnp.stack([lower, lower.T], axis=0)


def _chunk_rows(c):
    return pl.ds(pl.multiple_of(c * CHUNK, CHUNK), CHUNK)


def _inv_unit_tri(lmats):
    row = lax.broadcasted_iota(jnp.int32, (CHUNK, CHUNK), 0)
    col = lax.broadcasted_iota(jnp.int32, (CHUNK, CHUNK), 1)
    eye = (row == col).astype(F32)

    def same_block(n):
        sh = n.bit_length() - 1
        return lax.shift_right_logical(row, sh) == lax.shift_right_logical(col, sh)

    blk16 = same_block(16)
    l16 = [jnp.where(blk16, m, 0.0) for m in lmats]
    x = [eye - m for m in l16]
    p = [_bdot(m, m) for m in l16]
    for step in range(3):
        x = [xi + _bdot(xi, pi) for xi, pi in zip(x, p)]
        if step < 2:
            p = [_bdot(pi, pi) for pi in p]
    n = 16
    while n < CHUNK:
        off = same_block(2 * n) & jnp.logical_not(same_block(n))
        xl = [_bdot(xi, jnp.where(off, m, 0.0)) for xi, m in zip(x, lmats)]
        x = [xi - _bdot(yi, xi) for xi, yi in zip(x, xl)]
        n *= 2
    return x


def _wkv_kernel(r_ref, k_ref, v_ref, lrw_ref, lra_ref, w2_ref, a2_ref, w0_ref, a0_ref,
                kk_ref, ka_ref, m_ref, *rest, seq_len, zero_init):
    s0_ref = None if zero_init else rest[0]
    o_ref, sout_ref, rp_scr, w_scr, h_scr, d_scr = rest[-6:]
    z = pl.program_id(0)
    n_chunks = seq_len // CHUNK
    n_groups = n_chunks // WKV_GROUP
    heads = range(WKV_HEADS_STEP)
    row = lax.broadcasted_iota(jnp.int32, (CHUNK, CHUNK), 0)
    col = lax.broadcasted_iota(jnp.int32, (CHUNK, CHUNK), 1)
    incl_b = m_ref[0] > 0.5
    strict_b = incl_b & (row != col)
    w2 = w2_ref[0].astype(BF16)
    a2 = a2_ref[0].astype(BF16)
    w0 = w0_ref[0]
    a0 = a0_ref[0]
    k_k = kk_ref[...]
    k_a = ka_ref[...]

    def head_sl(h):
        return slice(h * WKV_N, (h + 1) * WKV_N)

    def phase1(it, carry):
        cs = [it * WKV_GROUP + j for j in range(WKV_GROUP)]
        rows = [_chunk_rows(c) for c in cs]
        r = [r_ref[rw, :].astype(F32) for rw in rows]
        k = [k_ref[rw, :].astype(F32) for rw in rows]
        v = [v_ref[rw, :] for rw in rows]
        w_raw = [w0 + _bdot(lrw_ref[rw, :], w2) for rw in rows]
        a = [jax.nn.sigmoid(a0 + _bdot(lra_ref[rw, :], a2)) for rw in rows]
        lw = [-math.exp(-0.5) * jax.nn.sigmoid(x) for x in w_raw]
        g, gtot = zip(*[_scan_cumsum(x, z) for x in lw])
        kt, rt, kh, bh, kp, bp, vh = [], [], [], [], [], [], []
        for j in range(WKV_GROUP):
            kkr = k[j] * k_k
            kmod = k[j] * (1.0 + (a[j] - 1.0) * k_a)
            e_g = jnp.exp(g[j])
            e_gx = jnp.exp(g[j] - lw[j])
            e_ng = jnp.exp(-g[j])
            e_gc = jnp.exp(gtot[j] - g[j])
            for h in heads:
                sl = head_sl(h)
                kk_h = kkr[:, sl]
                nrm = jnp.sqrt(jnp.sum(kk_h * kk_h, axis=-1, keepdims=True))
                kk_h = kk_h / jnp.maximum(nrm, 1e-12)
                b_h = kk_h * a[j][:, sl]
                kt.append(kk_h * e_gx[:, sl])
                rt.append(r[j][:, sl] * e_g[:, sl])
                kh.append(kmod[:, sl] * e_ng[:, sl])
                bh.append(b_h * e_ng[:, sl])
                kp.append(kmod[:, sl] * e_gc[:, sl])
                bp.append(b_h * e_gc[:, sl])
                vh.append(v[j][:, sl])
        units = range(WKV_HEADS_STEP * WKV_GROUP)
        gram = [_bdot_nt(jnp.concatenate([kt[u], rt[u]], axis=0), jnp.concatenate([kh[u], bh[u]], axis=0))
                for u in units]
        mkk = [jnp.where(strict_b, gm[:CHUNK, :CHUNK], 0.0) for gm in gram]
        lmat = [jnp.where(strict_b, gm[:CHUNK, CHUNK:], 0.0) for gm in gram]
        mrk = [jnp.where(incl_b, gm[CHUNK:, :CHUNK], 0.0) for gm in gram]
        mrb = [jnp.where(incl_b, gm[CHUNK:, CHUNK:], 0.0) for gm in gram]
        tinv = _inv_unit_tri(lmat)
        mv = [_bdot(mkk[u], vh[u]) for u in units]
        mrkv = [_bdot(mrk[u], vh[u]) for u in units]
        vtk = [_bdot_tn(vh[u], kp[u]) for u in units]
        pq = [_bdot(tinv[u], jnp.concatenate([kt[u], mv[u]], axis=-1)) for u in units]
        corr = [_bdot(mrb[u], pq[u]) for u in units]
        rp = [rt[u] - corr[u][:, :WKV_N] for u in units]
        op = [mrkv[u] - corr[u][:, WKV_N:] for u in units]
        wq = [_bdot_tn(pq[u], bp[u]) for u in units]
        wm = [x[:WKV_N] for x in wq]
        hm = [vtk[u] - wq[u][WKV_N:] for u in units]
        for j in range(WKV_GROUP):
            d_scr[cs[j]] = jnp.broadcast_to(jnp.exp(gtot[j]), (8, WKV_HEADS_STEP * WKV_N))
            o_ref[0, rows[j], :] = jnp.concatenate([op[WKV_HEADS_STEP * j + h] for h in heads], axis=-1)
            for h in heads:
                u = WKV_HEADS_STEP * j + h
                rp_scr[cs[j], h] = rp[u]
                w_scr[cs[j], h] = wm[u]
                h_scr[cs[j], h] = hm[u]
        return carry

    lax.fori_loop(0, n_groups, phase1, 0)

    def phase2(ci, carry):
        c = ci + z * (n_chunks - 1 - 2 * ci)
        d_tot = d_scr[c][0:1, :]
        sw = [_dot3(carry[h], w_scr[c, h]) for h in heads]
        new_s = [carry[h] * d_tot[:, head_sl(h)] - sw[h] + h_scr[c, h] for h in heads]
        for h in heads:
            h_scr[c, h] = carry[h]
        return tuple(new_s)

    if zero_init:
        for h in heads:
            sout_ref[0, 0, h] = jnp.zeros((WKV_N, WKV_N), F32)
        s_init = tuple(sout_ref[0, 0, h] for h in heads)
    else:
        s_init = tuple(s0_ref[0, 0, h] for h in heads)
    s_fin = lax.fori_loop(0, n_chunks, phase2, s_init)
    for h in heads:
        sout_ref[0, 0, h] = s_fin[h]

    def phase3(it, carry):
        cs = [it * WKV_GROUP + j for j in range(WKV_GROUP)]
        inter = [[_bdot_nt(rp_scr[c, h], h_scr[c, h]) for h in heads] for c in cs]
        for j in range(WKV_GROUP):
            rows = _chunk_rows(cs[j])
            o_ref[0, rows, :] = o_ref[0, rows, :] + jnp.concatenate(inter[j], axis=-1)
        return carry

    lax.fori_loop(0, n_groups, phase3, 0)


def _wkv_scan(p_all, lr, w2pad, a2pad, w0, a0, k_k, k_a, s0, masks, *, n_seq, seq_len, tok_blk0):
    def tok(b):
        return tok_blk0 + b

    width = WKV_HEADS_STEP * WKV_N
    col = lambda base: (lambda z, b, hq: (tok(b), base // width + hq))
    vec = lambda z, b, hq: (z, 0, hq)
    n_chunks = seq_len // CHUNK
    state_spec = pl.BlockSpec((1, 1, WKV_HEADS_STEP, WKV_N, WKV_N), lambda z, b, hq: (b, z, hq, 0, 0))
    init = () if s0 is None else (s0,)
    return pl.pallas_call(
        functools.partial(_wkv_kernel, seq_len=seq_len, zero_init=s0 is None),
        grid=(2, n_seq, WKV_H // WKV_HEADS_STEP),
        in_specs=[
            pl.BlockSpec((seq_len, width), col(4 * D)),
            pl.BlockSpec((seq_len, width), col(5 * D)),
            pl.BlockSpec((seq_len, width), col(6 * D)),
            pl.BlockSpec((seq_len, LANES), lambda z, b, hq: (tok(b), 0)),
            pl.BlockSpec((seq_len, LANES), lambda z, b, hq: (tok(b), 1)),
            pl.BlockSpec((1, LANES, width), vec),
            pl.BlockSpec((1, LANES, width), vec),
            pl.BlockSpec((1, 1, width), vec),
            pl.BlockSpec((1, 1, width), vec),
            pl.BlockSpec((1, width), lambda z, b, hq: (0, hq)),
            pl.BlockSpec((1, width), lambda z, b, hq: (0, hq)),
            pl.BlockSpec((1, CHUNK, CHUNK), lambda z, b, hq: (z, 0, 0)),
        ] + [state_spec] * len(init),
        out_specs=[
            pl.BlockSpec((1, seq_len, width), lambda z, b, hq: (z, b, hq)),
            state_spec,
        ],
        out_shape=[
            jax.ShapeDtypeStruct((2, n_seq * seq_len, D), F32),
            jax.ShapeDtypeStruct((n_seq, 2, WKV_H, WKV_N, WKV_N), F32),
        ],
        scratch_shapes=[pltpu.VMEM((n_chunks, WKV_HEADS_STEP, CHUNK, WKV_N), F32)] * 3
        + [pltpu.VMEM((n_chunks, 8, width), F32)],
        compiler_params=pltpu.CompilerParams(
            dimension_semantics=("arbitrary", "arbitrary", "arbitrary"), vmem_limit_bytes=VMEM_LIMIT),
        name="wkv_scan_%d" % seq_len,
    )(p_all, p_all, p_all, lr, lr, w2pad, a2pad, w0, a0, k_k, k_a, masks, *init)


def _dot2(x, w_bf):
    hi, lo = _split2(x)
    return _dot(hi, w_bf) + _dot(lo, w_bf)


def _group_stat(x, ind_ref, bind_ref):
    return _dot2(_dot2(x, ind_ref[...]), bind_ref[...])


N_CTX_TILES = N_CTX_TOK // TM_OUT


def _sum_dirs(i, oc_ref, os_ref):
    is_ctx = i < N_CTX_TILES
    return jnp.where(is_ctx, oc_ref[0], os_ref[0]) + jnp.where(is_ctx, oc_ref[1], os_ref[1])


def _ctx_tile(i):
    return (0, jnp.minimum(i, N_CTX_TILES - 1), 0)


def _smp_tile(i):
    return (0, jnp.maximum(i - N_CTX_TILES, 0), 0)


def _even_out_kernel(xc_ref, xs_ref, mod_ref, pc_ref, pw_ref, oc_ref, os_ref, lra_ref, a2_ref, a0_ref, ka_ref,
                     rk_ref, cw_ref, lnw_ref, lnb_ref, ind_ref, bind_ref, wo_ref, out_ref):
    i = pl.program_id(0)
    u = pc_ref[:, 0:D].astype(F32)
    gb = pc_ref[:, D:2 * D].astype(F32)
    gc = pc_ref[:, 2 * D:3 * D].astype(F32)
    zc = pc_ref[:, 3 * D:4 * D].astype(F32)
    xg = gc * u
    row_len = jnp.where(i < N_CTX_TILES, CTX_T, GRID_W)
    pos = lax.broadcasted_iota(jnp.int32, (TM_OUT, 1), 0) & (row_len - 1)
    prev = jnp.where(pos == 0, 0.0, pltpu.roll(xg, 1, 0))
    nxt = jnp.where(pos == row_len - 1, 0.0, pltpu.roll(xg, TM_OUT - 1, 0))
    conv = cw_ref[0:1, :] * prev + cw_ref[1:2, :] * xg + cw_ref[2:3, :] * nxt
    o_conv = _silu(zc) * gb * conv
    r = pw_ref[:, 0:D].astype(F32)
    k = pw_ref[:, D:2 * D].astype(F32)
    v = pw_ref[:, 2 * D:3 * D].astype(F32)
    zw = pw_ref[:, 3 * D:4 * D].astype(F32)
    o = _sum_dirs(i, oc_ref, os_ref)
    mu = _group_stat(o, ind_ref, bind_ref)
    dlt = o - mu
    var = _group_stat(dlt * dlt, ind_ref, bind_ref)
    gn = dlt * lax.rsqrt(var + GN_EPS) * lnw_ref[...] + lnb_ref[...]
    lra = lra_ref[...]
    ic0 = jax.nn.sigmoid(a0_ref[0] + _bdot(lra, a2_ref[0]))
    ic1 = jax.nn.sigmoid(a0_ref[1] + _bdot(lra, a2_ref[1]))
    ksum = k * (2.0 + (ic0 + ic1 - 2.0) * ka_ref[...])
    bonus = _group_stat(r * ksum * rk_ref[...], ind_ref, bind_ref) * float(WKV_N)
    o_wkv = (gn + bonus * v) * _silu(zw)
    y = _dot(o_conv.astype(BF16), wo_ref[0:D, :]) + _dot(o_wkv.astype(BF16), wo_ref[D:2 * D, :])
    gate = mod_ref[0, 2:3, :]
    out_ref[...] = jnp.where(i < N_CTX_TILES, xc_ref[...], xs_ref[...]) + gate * y


def _even_out(x_ctx, x_smp, mod3, p_all, o_ctx, o_smp, lr, a2pad, a0, k_a, r_k, conv_w, ln_w, ln_b, ind, bind,
              wo_bf):
    row = lambda i: (i, 0)
    const2 = lambda i: (0, 0)
    const3 = lambda i: (0, 0, 0)
    return pl.pallas_call(
        _even_out_kernel,
        grid=(N_TOK // TM_OUT,),
        in_specs=[
            pl.BlockSpec((TM_OUT, D), lambda i: _ctx_tile(i)[1:]),
            pl.BlockSpec((TM_OUT, D), lambda i: _smp_tile(i)[1:]),
            pl.BlockSpec((1, 3, D), lambda i: (_tile_row(i, TM_OUT), 0, 0)),
            pl.BlockSpec((TM_OUT, 4 * D), lambda i: (i, 0)),
            pl.BlockSpec((TM_OUT, 4 * D), lambda i: (i, 1)),
            pl.BlockSpec((2, TM_OUT, D), _ctx_tile),
            pl.BlockSpec((2, TM_OUT, D), _smp_tile),
            pl.BlockSpec((TM_OUT, LANES), lambda i: (i, 1)),
            pl.BlockSpec((2, LANES, D), const3),
            pl.BlockSpec((2, 1, D), const3),
            pl.BlockSpec((1, D), const2),
            pl.BlockSpec((1, D), const2),
            pl.BlockSpec((3, D), const2),
            pl.BlockSpec((1, D), const2),
            pl.BlockSpec((1, D), const2),
            pl.BlockSpec((D, LANES), const2),
            pl.BlockSpec((LANES, D), const2),
            pl.BlockSpec((2 * D, D), const2),
        ],
        out_specs=pl.BlockSpec((TM_OUT, D), row),
        out_shape=jax.ShapeDtypeStruct((N_TOK, D), F32),
        compiler_params=pltpu.CompilerParams(
            dimension_semantics=("arbitrary",), vmem_limit_bytes=VMEM_LIMIT),
        name="even_out",
    )(x_ctx, x_smp, mod3, p_all, p_all, o_ctx, o_smp, lr, a2pad, a0, k_a, r_k, conv_w, ln_w, ln_b, ind, bind, wo_bf)


def _gla_kernel(q_ref, k_ref, v_ref, lrg_ref, gk2_ref, gkb_ref, m_ref, *rest, seq_len, zero_init):
    s0_ref = None if zero_init else rest[0]
    o_ref, sout_ref, qe_scr, st_scr, e_scr = rest[-5:]
    z = pl.program_id(0)
    n_chunks = seq_len // CHUNK
    group = min(GLA_GROUP, n_chunks)
    n_groups = n_chunks // group
    incl_b = m_ref[0] > 0.5
    gk2 = gk2_ref[0].astype(BF16)
    gkb = gkb_ref[0]
    scale = GLA_DK ** -0.5

    def phase1(it, carry):
        cs = [it * group + j for j in range(group)]
        rows = [_chunk_rows(c) for c in cs]
        logit = [_bdot(lrg_ref[rw, :], gk2) + gkb for rw in rows]
        g = [jax.nn.log_sigmoid(x) / GLA_GATE_NORM for x in logit]
        bc, b_last = zip(*[_scan_cumsum(x, z) for x in g])
        k = [k_ref[rw, :].astype(F32) for rw in rows]
        qe = [q_ref[rw, :].astype(F32) * scale * jnp.exp(x) for rw, x in zip(rows, bc)]
        ke = [kc * jnp.exp(-x) for kc, x in zip(k, bc)]
        kd = [kc * jnp.exp(bl - x) for kc, x, bl in zip(k, bc, b_last)]
        v = [v_ref[rw, :] for rw in rows]
        att = [jnp.where(incl_b, _bdot_nt(a, b), 0.0) for a, b in zip(qe, ke)]
        o_in = [_bdot(a, b) for a, b in zip(att, v)]
        vtk = [_bdot_tn(a, b) for a, b in zip(v, kd)]
        for j in range(group):
            o_ref[0, rows[j], :] = o_in[j]
            qe_scr[cs[j]] = qe[j]
            st_scr[cs[j]] = vtk[j]
            e_scr[cs[j]] = jnp.broadcast_to(jnp.exp(b_last[j]), (8, GLA_DK))
        return carry

    lax.fori_loop(0, n_groups, phase1, 0)

    def phase2(ci, st):
        c = ci + z * (n_chunks - 1 - 2 * ci)
        new_st = st * e_scr[c][0:1, :] + st_scr[c]
        st_scr[c] = st
        return new_st

    st_init = jnp.zeros((GLA_DV, GLA_DK), F32) if zero_init else s0_ref[0, 0, 0].T
    st_fin = lax.fori_loop(0, n_chunks, phase2, st_init)
    sout_ref[0, 0, 0] = st_fin.T

    def phase3(it, carry):
        cs = [it * group + j for j in range(group)]
        inter = [_bdot_nt(qe_scr[c], st_scr[c]) for c in cs]
        for j in range(group):
            rows = _chunk_rows(cs[j])
            o_ref[0, rows, :] = o_ref[0, rows, :] + inter[j]
        return carry

    lax.fori_loop(0, n_groups, phase3, 0)


def _gla_scan(p_all, lrg, gk2pad, gkb, s0, masks, *, n_seq, seq_len, tok_blk0):
    def tok(b):
        return tok_blk0 + b

    n_chunks = seq_len // CHUNK
    init = () if s0 is None else (s0,)
    return pl.pallas_call(
        functools.partial(_gla_kernel, seq_len=seq_len, zero_init=s0 is None),
        grid=(2, n_seq, GLA_H),
        in_specs=[
            pl.BlockSpec((seq_len, GLA_DK), lambda z, b, h: (tok(b), h)),
            pl.BlockSpec((seq_len, GLA_DK), lambda z, b, h: (tok(b), GLA_H + h)),
            pl.BlockSpec((seq_len, GLA_DV), lambda z, b, h: (tok(b), GLA_H + h)),
            pl.BlockSpec((seq_len, LANES), lambda z, b, h: (tok(b), 0)),
            pl.BlockSpec((1, LANES, GLA_DK), lambda z, b, h: (z, 0, h)),
            pl.BlockSpec((1, 1, GLA_DK), lambda z, b, h: (z, 0, h)),
            pl.BlockSpec((1, CHUNK, CHUNK), lambda z, b, h: (z, 0, 0)),
        ] + [pl.BlockSpec((1, 1, 1, GLA_DK, GLA_DV), lambda z, b, h: (b, z, h, 0, 0))] * len(init),
        out_specs=[
            pl.BlockSpec((1, seq_len, GLA_DV), lambda z, b, h: (z, b, h)),
            pl.BlockSpec((1, 1, 1, GLA_DK, GLA_DV), lambda z, b, h: (b, z, h, 0, 0)),
        ],
        out_shape=[
            jax.ShapeDtypeStruct((2, n_seq * seq_len, D), F32),
            jax.ShapeDtypeStruct((n_seq, 2, GLA_H, GLA_DK, GLA_DV), F32),
        ],
        scratch_shapes=[
            pltpu.VMEM((n_chunks, CHUNK, GLA_DK), F32),
            pltpu.VMEM((n_chunks, GLA_DV, GLA_DK), F32),
            pltpu.VMEM((n_chunks, 8, GLA_DK), F32),
        ],
        compiler_params=pltpu.CompilerParams(
            dimension_semantics=("arbitrary", "arbitrary", "arbitrary"), vmem_limit_bytes=VMEM_LIMIT),
        name="gla_scan_%d" % seq_len,
    )(p_all, p_all, p_all, lrg, gk2pad, gkb, masks, *init)


def _odd_out_kernel(x_ref, mod_ref, zg_ref, oc_ref, os_ref, gn_ref, ind_ref, bind_ref, wo_ref, fg_ref,
                    yc_ref, ys_ref):
    i = pl.program_id(0)
    o = _sum_dirs(i, oc_ref, os_ref)
    ms = _group_stat(o * o, ind_ref, bind_ref)
    on = o * lax.rsqrt(ms + NORM_EPS) * gn_ref[...]
    og = on * _silu(zg_ref[...].astype(F32))
    y = _dot(og.astype(BF16), wo_ref[...])
    gate = mod_ref[0, 2:3, :]
    x = x_ref[...] + gate * y
    out = x * lax.rsqrt(jnp.mean(x * x, axis=-1, keepdims=True) + NORM_EPS) * fg_ref[...]

    @pl.when(i < N_CTX_TILES)
    def _():
        yc_ref[...] = out

    @pl.when(i >= N_CTX_TILES)
    def _():
        ys_ref[...] = out


def _odd_out(x, mod3, p_all, o_ctx, o_smp, g_norm, ind, bind, wo_bf, final_g):
    row = lambda i: (i, 0)
    const2 = lambda i: (0, 0)
    return pl.pallas_call(
        _odd_out_kernel,
        grid=(N_TOK // TM_OUT,),
        in_specs=[
            pl.BlockSpec((TM_OUT, D), row),
            pl.BlockSpec((1, 3, D), lambda i: (_tile_row(i, TM_OUT), 0, 0)),
            pl.BlockSpec((TM_OUT, D), lambda i: (i, 2)),
            pl.BlockSpec((2, TM_OUT, D), _ctx_tile),
            pl.BlockSpec((2, TM_OUT, D), _smp_tile),
            pl.BlockSpec((1, D), const2),
            pl.BlockSpec((D, LANES), const2),
            pl.BlockSpec((LANES, D), const2),
            pl.BlockSpec((D, D), const2),
            pl.BlockSpec((1, D), const2),
        ],
        out_specs=[
            pl.BlockSpec((TM_OUT, D), lambda i: _ctx_tile(i)[1:]),
            pl.BlockSpec((TM_OUT, D), lambda i: _smp_tile(i)[1:]),
        ],
        out_shape=[
            jax.ShapeDtypeStruct((N_CTX_TOK, D), F32),
            jax.ShapeDtypeStruct((N_TOK - N_CTX_TOK, D), F32),
        ],
        compiler_params=pltpu.CompilerParams(
            dimension_semantics=("arbitrary",), vmem_limit_bytes=VMEM_LIMIT),
        name="odd_out",
    )(x, mod3, p_all, o_ctx, o_smp, g_norm, ind, bind, wo_bf, final_g)


def _group_indicators(group):
    ch = jnp.arange(D) // group
    lane = jnp.arange(LANES)
    hit = (ch[:, None] == lane[None, :]).astype(F32)
    return (hit / float(group)).astype(BF16), hit.T.astype(BF16)


def _pad_dirs(w):
    r = w.shape[1]
    out = jnp.zeros((2, LANES, w.shape[2]), F32)
    out = out.at[0, 0:r].set(w[0])
    out = out.at[1, r:2 * r].set(w[1])
    return out


def kernel(x_prompt, x_sample, state_wkv, state_gla, c, c_ctx, norm_g, ada_w, ada_b, final_g, e_w_in, e_w_out, conv_w, wkv_w0, wkv_w1, wkv_w2, wkv_a0, wkv_a1, wkv_a2, wkv_k_k, wkv_k_a, wkv_r_k, wkv_ln_w, wkv_ln_b, o_w_in, o_w_out, gla_gk1, gla_gk2, gla_gk_b, gla_g_norm):
    x_ctx = x_prompt.reshape(N_CTX_TOK, D)
    x_smp = x_sample.reshape(N_TOK - N_CTX_TOK, D)
    cvec = jnp.zeros((8, D), F32).at[0].set(c_ctx).at[1:1 + SMP_B].set(c)
    mod = _modulation(cvec, ada_w, ada_b)
    masks = _chunk_masks()

    mod3 = mod[0].reshape(8, 3, D)
    wlr = jnp.concatenate([wkv_w1[0, 0], wkv_w1[0, 1], wkv_a1[0, 0], wkv_a1[0, 1]], axis=1)
    p_e, lr_e = _inproj(x_ctx, x_smp, 0, mod3, norm_g[0:1], e_w_in[0].astype(BF16), wlr.astype(BF16),
                        n_tanh=2 * 64)
    w2pad = _pad_dirs(wkv_w2[0])
    a2pad = _pad_dirs(wkv_a2[0])
    w0 = wkv_w0[0].reshape(2, 1, D)
    a0 = wkv_a0[0].reshape(2, 1, D)
    k_k = wkv_k_k[0].reshape(1, D)
    k_a = wkv_k_a[0].reshape(1, D)
    r_k = wkv_r_k[0].reshape(1, D)
    o_ctx, new_wkv = _wkv_scan(p_e, lr_e, w2pad, a2pad, w0, a0, k_k, k_a, None, masks,
                               n_seq=CTX_B, seq_len=CTX_T, tok_blk0=0)
    o_smp, _ = _wkv_scan(p_e, lr_e, w2pad, a2pad, w0, a0, k_k, k_a, state_wkv[:, 0], masks,
                         n_seq=SMP_B, seq_len=SMP_T, tok_blk0=N_CTX_TOK // SMP_T)
    ind64, bind64 = _group_indicators(WKV_N)
    x = _even_out(x_ctx, x_smp, mod3, p_e, o_ctx, o_smp, lr_e, a2pad, a0, k_a, r_k, conv_w[0],
                  wkv_ln_w[0].reshape(1, D), wkv_ln_b[0].reshape(1, D), ind64, bind64,
                  e_w_out[0].astype(BF16))

    mod3 = mod[1].reshape(8, 3, D)
    rank = gla_gk1.shape[-1]
    wlr = jnp.zeros((D, LANES), F32).at[:, 0:rank].set(gla_gk1[0, 0]).at[:, rank:2 * rank].set(gla_gk1[0, 1])
    p_o, lr_o = _inproj(x, x, N_CTX_TOK // TM_PROJ, mod3, norm_g[1:2], o_w_in[0].astype(BF16),
                        wlr.astype(BF16), n_tanh=0)
    gk2pad = _pad_dirs(gla_gk2[0])
    gkb = gla_gk_b[0].reshape(2, 1, GLA_H * GLA_DK)
    g_ctx, new_gla = _gla_scan(p_o, lr_o, gk2pad, gkb, None, masks,
                               n_seq=CTX_B, seq_len=CTX_T, tok_blk0=0)
    g_smp, _ = _gla_scan(p_o, lr_o, gk2pad, gkb, state_gla[:, 0], masks,
                         n_seq=SMP_B, seq_len=SMP_T, tok_blk0=N_CTX_TOK // SMP_T)
    ind256, bind256 = _group_indicators(GLA_DV)
    y_ctx, y_smp = _odd_out(x, mod3, p_o, g_ctx, g_smp, jnp.tile(gla_g_norm[0], GLA_H).reshape(1, D),
                            ind256, bind256, o_w_out[0].astype(BF16), final_g.reshape(1, D))
    return (y_ctx.reshape(CTX_B, CTX_T, D), y_smp.reshape(SMP_B, SMP_T, D), new_wkv[:, None], new_gla[:, None])
```

```python
import functools
import math

import jax
import jax.numpy as jnp
from jax import lax
from jax.experimental import pallas as pl
from jax.experimental.pallas import tpu as pltpu

F32 = jnp.float32
BF16 = jnp.bfloat16
HI = lax.Precision.HIGHEST

D = 1024
N_CTX_TOK = 16 * 256
N_TOK = 2 * N_CTX_TOK
CTX_B, CTX_T = 16, 256
SMP_B, SMP_T = 2, 2048
GRID_W = 64
WKV_H, WKV_N = 16, 64
GLA_H, GLA_DK, GLA_DV = 4, 128, 256
GLA_GATE_NORM = 16.0
GN_EPS = 64e-5
NORM_EPS = 1e-6
CHUNK = 64
WKV_GROUP = 4
WKV_HEADS_STEP = 8
GLA_GROUP = 8
LANES = 128
TM_PROJ = 1024
TN_PROJ = 1024
TM_OUT = 256
VMEM_LIMIT = 56 * 1024 * 1024


def _silu(x):
    return x * jax.nn.sigmoid(x)


def _dot(a, b, precision=None):
    return jnp.dot(a, b, preferred_element_type=F32, precision=precision)


def _dot_nt(a, b, precision=None):
    return lax.dot_general(a, b, (((1,), (1,)), ((), ())), preferred_element_type=F32, precision=precision)


def _dot_tn(a, b, precision=None):
    return lax.dot_general(a, b, (((0,), (0,)), ((), ())), preferred_element_type=F32, precision=precision)


def _bdot(a, b):
    return _dot(a.astype(BF16), b.astype(BF16))


def _bdot_nt(a, b):
    return _dot_nt(a.astype(BF16), b.astype(BF16))


def _bdot_tn(a, b):
    return _dot_tn(a.astype(BF16), b.astype(BF16))


def _split2(x):
    hi = x.astype(BF16)
    lo = (x - hi.astype(F32)).astype(BF16)
    return hi, lo


def _scan_cumsum(x, z):
    n = x.shape[0]
    row = lax.broadcasted_iota(jnp.int32, (n, 1), 0)
    pre = x
    s = 1
    while s < n:
        pre = pre + jnp.where(row >= s, pltpu.roll(pre, s, 0), 0.0)
        s *= 2
    tot = pre[n - 1:n, :]
    return jnp.where(z == 0, pre, tot - pre + x), tot


def _tile_row(i, tile):
    ctx_tiles = N_CTX_TOK // tile
    per_req = SMP_T // tile
    return jnp.where(i < ctx_tiles, 0, 1 + (i - ctx_tiles) // per_req)


def _mod_kernel(c_ref, w_ref, b_ref, o_ref):
    cf = _silu(c_ref[...])
    o_ref[0] = _dot(cf, w_ref[0], HI) + b_ref[0]


def _modulation(cvec, ada_w, ada_b):
    depth = ada_w.shape[0]
    return pl.pallas_call(
        _mod_kernel,
        grid=(depth, 3),
        in_specs=[
            pl.BlockSpec((8, D), lambda l, j: (0, 0)),
            pl.BlockSpec((1, D, D), lambda l, j: (l, 0, j)),
            pl.BlockSpec((1, 1, D), lambda l, j: (l, 0, j)),
        ],
        out_specs=pl.BlockSpec((1, 8, D), lambda l, j: (l, 0, j)),
        out_shape=jax.ShapeDtypeStruct((depth, 8, 3 * D), F32),
        compiler_params=pltpu.CompilerParams(
            dimension_semantics=("arbitrary", "arbitrary"), vmem_limit_bytes=VMEM_LIMIT),
        name="adaln_mod",
    )(cvec, ada_w, ada_b.reshape(depth, 1, 3 * D))


def _inproj_kernel(xa_ref, xb_ref, mod_ref, g_ref, w_ref, wlr_ref, p_ref, lr_ref, h_scr, *, n_tanh):
    i = pl.program_id(0)
    j = pl.program_id(1)

    @pl.when(j == 0)
    def _():
        x = jnp.where(i < N_CTX_TOK // TM_PROJ, xa_ref[...], xb_ref[...])
        y = x * lax.rsqrt(jnp.mean(x * x, axis=-1, keepdims=True) + NORM_EPS) * g_ref[...]
        shift = mod_ref[0, 0:1, :]
        scale = mod_ref[0, 1:2, :]
        h = y * (1.0 + scale) + shift
        hb = h.astype(BF16)
        h_scr[...] = hb
        lr = _dot(hb, wlr_ref[...])
        if n_tanh:
            lane = lax.broadcasted_iota(jnp.int32, lr.shape, 1)
            lr = jnp.where(lane < n_tanh, jnp.tanh(lr), lr)
        lr_ref[...] = lr

    p_ref[...] = _dot(h_scr[...], w_ref[...]).astype(p_ref.dtype)


def _inproj(x_ctx, x_smp, smp_tile0, mod3, g, w_bf, wlr_bf, n_tanh):
    n_out = w_bf.shape[1]
    n_lr = wlr_bf.shape[1]
    n_ctx = N_CTX_TOK // TM_PROJ
    return pl.pallas_call(
        functools.partial(_inproj_kernel, n_tanh=n_tanh),
        grid=(N_TOK // TM_PROJ, n_out // TN_PROJ),
        in_specs=[
            pl.BlockSpec((TM_PROJ, D), lambda i, j: (jnp.minimum(i, n_ctx - 1), 0)),
            pl.BlockSpec((TM_PROJ, D), lambda i, j: (smp_tile0 + jnp.maximum(i - n_ctx, 0), 0)),
            pl.BlockSpec((1, 3, D), lambda i, j: (_tile_row(i, TM_PROJ), 0, 0)),
            pl.BlockSpec((1, D), lambda i, j: (0, 0)),
            pl.BlockSpec((D, TN_PROJ), lambda i, j: (0, j)),
            pl.BlockSpec((D, n_lr), lambda i, j: (0, 0)),
        ],
        out_specs=[
            pl.BlockSpec((TM_PROJ, TN_PROJ), lambda i, j: (i, j)),
            pl.BlockSpec((TM_PROJ, n_lr), lambda i, j: (i, 0)),
        ],
        out_shape=[
            jax.ShapeDtypeStruct((N_TOK, n_out), BF16),
            jax.ShapeDtypeStruct((N_TOK, n_lr), F32),
        ],
        scratch_shapes=[pltpu.VMEM((TM_PROJ, D), BF16)],
        compiler_params=pltpu.CompilerParams(
            dimension_semantics=("arbitrary", "arbitrary"), vmem_limit_bytes=VMEM_LIMIT),
        name="inproj",
    )(x_ctx, x_smp, mod3, g, w_bf, wlr_bf)


def _chunk_masks():
    idx = jnp.arange(CHUNK)
    lower = (idx[:, None] >= idx[None, :]).astype(F32)
    return jnp.stack([lower, lower.T], axis=0)


def _chunk_rows(c):
    return pl.ds(pl.multiple_of(c * CHUNK, CHUNK), CHUNK)


def _inv_unit_tri(lmats):
    row = lax.broadcasted_iota(jnp.int32, (CHUNK, CHUNK), 0)
    col = lax.broadcasted_iota(jnp.int32, (CHUNK, CHUNK), 1)
    eye = (row == col).astype(F32)

    def same_block(n):
        sh = n.bit_length() - 1
        return lax.shift_right_logical(row, sh) == lax.shift_right_logical(col, sh)

    blk16 = same_block(16)
    l16 = [jnp.where(blk16, m, 0.0) for m in lmats]
    x = [eye - m for m in l16]
    p = [_bdot(m, m) for m in l16]
    for step in range(3):
        x = [xi + _bdot(xi, pi) for xi, pi in zip(x, p)]
        if step < 2:
            p = [_bdot(pi, pi) for pi in p]
    n = 16
    while n < CHUNK:
        off = same_block(2 * n) & jnp.logical_not(same_block(n))
        xl = [_bdot(xi, jnp.where(off, m, 0.0)) for xi, m in zip(x, lmats)]
        x = [xi - _bdot(yi, xi) for xi, yi in zip(x, xl)]
        n *= 2
    return x


def _wkv_kernel(r_ref, k_ref, v_ref, lrw_ref, lra_ref, w2_ref, a2_ref, w0_ref, a0_ref,
                kk_ref, ka_ref, *rest, seq_len, zero_init):
    s0_ref = None if zero_init else rest[0]
    o_ref, sout_ref, rp_scr, w_scr, h_scr, d_scr = rest[-6:]
    z = pl.program_id(0)
    n_chunks = seq_len // CHUNK
    n_groups = n_chunks // WKV_GROUP
    pairs = range(WKV_HEADS_STEP // 2)
    row = lax.broadcasted_iota(jnp.int32, (CHUNK, LANES), 0)
    lane = lax.broadcasted_iota(jnp.int32, (CHUNK, LANES), 1)
    col = lane & (CHUNK - 1)
    left = lane < WKV_N
    incl2 = ((z == 0) & (row >= col)) | ((z == 1) & (row <= col))
    strict2 = incl2 & (row != col)
    row_b = lax.broadcasted_iota(jnp.int32, (LANES, LANES), 0) < WKV_N
    lane_b = lax.broadcasted_iota(jnp.int32, (LANES, LANES), 1) < WKV_N
    diag_blocks = row_b == lane_b
    zeros_bf = jnp.zeros((CHUNK, LANES), BF16)
    w2 = w2_ref[0].astype(BF16)
    a2 = a2_ref[0].astype(BF16)
    w0 = w0_ref[0]
    a0 = a0_ref[0]
    k_k = kk_ref[...]
    k_a = ka_ref[...]

    def pair_sl(p):
        return slice(p * LANES, (p + 1) * LANES)

    def stack(a, b):
        return jnp.concatenate([a, b], axis=0)

    def phase1(it, carry):
        cs = [it * WKV_GROUP + j for j in range(WKV_GROUP)]
        rows = [_chunk_rows(c) for c in cs]
        w_raw = [w0 + _bdot(lrw_ref[rw, :], w2) for rw in rows]
        a = [jax.nn.sigmoid(a0 + _bdot(lra_ref[rw, :], a2)) for rw in rows]
        lw = [-math.exp(-0.5) * jax.nn.sigmoid(x) for x in w_raw]
        g, gtot = zip(*[_scan_cumsum(x, z) for x in lw])
        cp = []
        for j in range(WKV_GROUP):
            r = r_ref[rows[j], :].astype(F32)
            k = k_ref[rows[j], :].astype(F32)
            v = v_ref[rows[j], :]
            kkr = k * k_k
            sq = kkr * kkr
            nrm = []
            for p in pairs:
                sq_p = sq[:, pair_sl(p)]
                n0 = jnp.sum(jnp.where(left, sq_p, 0.0), axis=-1, keepdims=True)
                n1 = jnp.sum(jnp.where(left, 0.0, sq_p), axis=-1, keepdims=True)
                nrm.append(jnp.where(left, jnp.sqrt(n0), jnp.sqrt(n1)))
            kk = kkr / jnp.maximum(jnp.concatenate(nrm, axis=-1), 1e-12)
            b = kk * a[j]
            kmod = k * (1.0 + (a[j] - 1.0) * k_a)
            e_ng = jnp.exp(-g[j])
            e_gc = jnp.exp(gtot[j] - g[j])
            rt = r * jnp.exp(g[j])
            kt_b = (kk * jnp.exp(g[j] - lw[j])).astype(BF16)
            rt_b = rt.astype(BF16)
            kh_b = (kmod * e_ng).astype(BF16)
            bh_b = (b * e_ng).astype(BF16)
            kp_b = (kmod * e_gc).astype(BF16)
            bp_b = (b * e_gc).astype(BF16)
            for p in pairs:
                sl = pair_sl(p)
                cp.append(dict(j=j, p=p, rt=rt[:, sl], kt=kt_b[:, sl], rtb=rt_b[:, sl], v=v[:, sl],
                               bk=stack(bh_b[:, sl], kh_b[:, sl]), kb=stack(kh_b[:, sl], bh_b[:, sl]),
                               kbp=stack(kp_b[:, sl], -bp_b[:, sl]), bp=bp_b[:, sl]))
        units = [(q, h) for q in range(len(cp)) for h in range(2)]
        pick = [left, jnp.logical_not(left)]
        lm = [jnp.where(strict2, _dot_nt(jnp.where(pick[h], cp[q]["kt"], zeros_bf), cp[q]["bk"]), 0.0)
              for q, h in units]
        rr = [jnp.where(incl2, _dot_nt(jnp.where(pick[h], cp[q]["rtb"], zeros_bf), cp[q]["kb"]), 0.0)
              for q, h in units]
        tinv = _inv_unit_tri([m[:, :CHUNK] for m in lm])
        lm_b = [m.astype(BF16) for m in lm]
        rr_b = [m.astype(BF16) for m in rr]
        mv = [_dot(lm_b[u], stack(zeros_bf, cp[q]["v"])) for u, (q, h) in enumerate(units)]
        mrkv = [_dot(rr_b[u], stack(cp[q]["v"], zeros_bf)) for u, (q, h) in enumerate(units)]
        pm = [_bdot(tinv[u], cp[q]["kt"]) for u, (q, h) in enumerate(units)]
        qm = [_bdot(tinv[u], mv[u]) for u in range(len(units))]
        corr_p = [_dot(rr_b[u], stack(zeros_bf, pm[u].astype(BF16))) for u in range(len(units))]
        corr_q = [_dot(rr_b[u], stack(zeros_bf, qm[u].astype(BF16))) for u in range(len(units))]

        def merge(xs, q):
            return jnp.where(left, xs[2 * q], xs[2 * q + 1])

        n_cp = range(len(cp))
        rp = [cp[q]["rt"] - merge(corr_p, q) for q in n_cp]
        op = [merge(mrkv, q) - merge(corr_q, q) for q in n_cp]
        p_pair = [merge(pm, q).astype(BF16) for q in n_cp]
        q_pair = [merge(qm, q).astype(BF16) for q in n_cp]
        wfull = [_dot_tn(p_pair[q], cp[q]["bp"]) for q in n_cp]
        hfull = [_dot_tn(stack(cp[q]["v"], q_pair[q]), cp[q]["kbp"]) for q in n_cp]
        for j in range(WKV_GROUP):
            d_scr[cs[j]] = jnp.broadcast_to(jnp.exp(gtot[j]), (8, WKV_HEADS_STEP * WKV_N))
        for q in n_cp:
            c, p = cs[cp[q]["j"]], cp[q]["p"]
            o_ref[0, rows[cp[q]["j"]], pair_sl(p)] = op[q]
            rp_scr[c, p] = rp[q].astype(BF16)
            w_hi, w_lo = _split2(jnp.where(diag_blocks, wfull[q], 0.0))
            w_scr[c, p, 0] = w_hi
            w_scr[c, p, 1] = w_lo
            h_scr[c, p] = jnp.where(left, hfull[q][:WKV_N], hfull[q][WKV_N:])
        return carry

    lax.fori_loop(0, n_groups, phase1, 0)

    def phase2(ci, carry):
        c = ci + z * (n_chunks - 1 - 2 * ci)
        d_tot = d_scr[c][0:1, :]
        split = [_split2(s) for s in carry]
        sw_hi = [_dot(stack(split[p][0], split[p][1]), w_scr[c, p, 0]) for p in pairs]
        sw_lo = [_dot(split[p][0], w_scr[c, p, 1]) for p in pairs]
        new_s = [carry[p] * d_tot[:, pair_sl(p)] - (sw_hi[p][:CHUNK] + sw_hi[p][CHUNK:] + sw_lo[p]) + h_scr[c, p]
                 for p in pairs]
        for p in pairs:
            h_scr[c, p] = carry[p]
        return tuple(new_s)

    if zero_init:
        s_init = tuple(jnp.zeros((WKV_N, LANES), F32) for p in pairs)
    else:
        s_init = tuple(jnp.concatenate([s0_ref[0, 0, 2 * p], s0_ref[0, 0, 2 * p + 1]], axis=-1) for p in pairs)
    s_fin = lax.fori_loop(0, n_chunks, phase2, s_init)
    for p in pairs:
        sout_ref[0, 0, 2 * p] = s_fin[p][:, :WKV_N]
        sout_ref[0, 0, 2 * p + 1] = s_fin[p][:, WKV_N:]

    def phase3(it, carry):
        cs = [it * WKV_GROUP + j for j in range(WKV_GROUP)]
        s_bd = [[stack(jnp.where(left, h_scr[c, p], 0.0), jnp.where(left, 0.0, h_scr[c, p])).astype(BF16)
                 for p in pairs] for c in cs]
        inter = [[_dot_nt(rp_scr[c, p], s_bd[j][p]) for p in pairs] for j, c in enumerate(cs)]
        for j, c in enumerate(cs):
            rows = _chunk_rows(c)
            o_ref[0, rows, :] = o_ref[0, rows, :] + jnp.concatenate(inter[j], axis=-1)
        return carry

    lax.fori_loop(0, n_groups, phase3, 0)


def _wkv_scan(p_all, lr, w2pad, a2pad, w0, a0, k_k, k_a, s0, *, n_seq, seq_len, tok_blk0):
    def tok(b):
        return tok_blk0 + b

    width = WKV_HEADS_STEP * WKV_N
    n_pairs = WKV_HEADS_STEP // 2
    col = lambda base: (lambda z, b, hq: (tok(b), base // width + hq))
    vec = lambda z, b, hq: (z, 0, hq)
    n_chunks = seq_len // CHUNK
    state_spec = pl.BlockSpec((1, 1, WKV_HEADS_STEP, WKV_N, WKV_N), lambda z, b, hq: (b, z, hq, 0, 0))
    init = () if s0 is None else (s0,)
    return pl.pallas_call(
        functools.partial(_wkv_kernel, seq_len=seq_len, zero_init=s0 is None),
        grid=(2, n_seq, WKV_H // WKV_HEADS_STEP),
        in_specs=[
            pl.BlockSpec((seq_len, width), col(4 * D)),
            pl.BlockSpec((seq_len, width), col(5 * D)),
            pl.BlockSpec((seq_len, width), col(6 * D)),
            pl.BlockSpec((seq_len, LANES), lambda z, b, hq: (tok(b), 0)),
            pl.BlockSpec((seq_len, LANES), lambda z, b, hq: (tok(b), 1)),
            pl.BlockSpec((1, LANES, width), vec),
            pl.BlockSpec((1, LANES, width), vec),
            pl.BlockSpec((1, 1, width), vec),
            pl.BlockSpec((1, 1, width), vec),
            pl.BlockSpec((1, width), lambda z, b, hq: (0, hq)),
            pl.BlockSpec((1, width), lambda z, b, hq: (0, hq)),
        ] + [state_spec] * len(init),
        out_specs=[
            pl.BlockSpec((1, seq_len, width), lambda z, b, hq: (z, b, hq)),
            state_spec,
        ],
        out_shape=[
            jax.ShapeDtypeStruct((2, n_seq * seq_len, D), F32),
            jax.ShapeDtypeStruct((n_seq, 2, WKV_H, WKV_N, WKV_N), F32),
        ],
        scratch_shapes=[
            pltpu.VMEM((n_chunks, n_pairs, CHUNK, LANES), BF16),
            pltpu.VMEM((n_chunks, n_pairs, 2, LANES, LANES), BF16),
            pltpu.VMEM((n_chunks, n_pairs, CHUNK, LANES), F32),
            pltpu.VMEM((n_chunks, 8, width), F32),
        ],
        compiler_params=pltpu.CompilerParams(
            dimension_semantics=("arbitrary", "arbitrary", "arbitrary"), vmem_limit_bytes=VMEM_LIMIT),
        name="wkv_scan_%d" % seq_len,
    )(p_all, p_all, p_all, lr, lr, w2pad, a2pad, w0, a0, k_k, k_a, *init)


def _dot2(x, w_bf):
    hi, lo = _split2(x)
    return _dot(hi, w_bf) + _dot(lo, w_bf)


def _group_stat(x, ind_ref, bind_ref):
    return _dot2(_dot2(x, ind_ref[...]), bind_ref[...])


N_CTX_TILES = N_CTX_TOK // TM_OUT


def _sum_dirs(i, oc_ref, os_ref):
    is_ctx = i < N_CTX_TILES
    return jnp.where(is_ctx, oc_ref[0], os_ref[0]) + jnp.where(is_ctx, oc_ref[1], os_ref[1])


def _ctx_tile(i):
    return (0, jnp.minimum(i, N_CTX_TILES - 1), 0)


def _smp_tile(i):
    return (0, jnp.maximum(i - N_CTX_TILES, 0), 0)


def _even_out_kernel(xc_ref, xs_ref, mod_ref, pc_ref, pw_ref, oc_ref, os_ref, lra_ref, a2_ref, a0_ref, ka_ref,
                     rk_ref, cw_ref, lnw_ref, lnb_ref, ind_ref, bind_ref, wo_ref, out_ref):
    i = pl.program_id(0)
    u = pc_ref[:, 0:D].astype(F32)
    gb = pc_ref[:, D:2 * D].astype(F32)
    gc = pc_ref[:, 2 * D:3 * D].astype(F32)
    zc = pc_ref[:, 3 * D:4 * D].astype(F32)
    xg = gc * u
    row_len = jnp.where(i < N_CTX_TILES, CTX_T, GRID_W)
    pos = lax.broadcasted_iota(jnp.int32, (TM_OUT, 1), 0) & (row_len - 1)
    prev = jnp.where(pos == 0, 0.0, pltpu.roll(xg, 1, 0))
    nxt = jnp.where(pos == row_len - 1, 0.0, pltpu.roll(xg, TM_OUT - 1, 0))
    conv = cw_ref[0:1, :] * prev + cw_ref[1:2, :] * xg + cw_ref[2:3, :] * nxt
    o_conv = _silu(zc) * gb * conv
    r = pw_ref[:, 0:D].astype(F32)
    k = pw_ref[:, D:2 * D].astype(F32)
    v = pw_ref[:, 2 * D:3 * D].astype(F32)
    zw = pw_ref[:, 3 * D:4 * D].astype(F32)
    o = _sum_dirs(i, oc_ref, os_ref)
    mu = _group_stat(o, ind_ref, bind_ref)
    dlt = o - mu
    var = _group_stat(dlt * dlt, ind_ref, bind_ref)
    gn = dlt * lax.rsqrt(var + GN_EPS) * lnw_ref[...] + lnb_ref[...]
    lra = lra_ref[...]
    ic0 = jax.nn.sigmoid(a0_ref[0] + _bdot(lra, a2_ref[0]))
    ic1 = jax.nn.sigmoid(a0_ref[1] + _bdot(lra, a2_ref[1]))
    ksum = k * (2.0 + (ic0 + ic1 - 2.0) * ka_ref[...])
    bonus = _group_stat(r * ksum * rk_ref[...], ind_ref, bind_ref) * float(WKV_N)
    o_wkv = (gn + bonus * v) * _silu(zw)
    y = _dot(o_conv.astype(BF16), wo_ref[0:D, :]) + _dot(o_wkv.astype(BF16), wo_ref[D:2 * D, :])
    gate = mod_ref[0, 2:3, :]
    out_ref[...] = jnp.where(i < N_CTX_TILES, xc_ref[...], xs_ref[...]) + gate * y


def _even_out(x_ctx, x_smp, mod3, p_all, o_ctx, o_smp, lr, a2pad, a0, k_a, r_k, conv_w, ln_w, ln_b, ind, bind,
              wo_bf):
    row = lambda i: (i, 0)
    const2 = lambda i: (0, 0)
    const3 = lambda i: (0, 0, 0)
    return pl.pallas_call(
        _even_out_kernel,
        grid=(N_TOK // TM_OUT,),
        in_specs=[
            pl.BlockSpec((TM_OUT, D), lambda i: _ctx_tile(i)[1:]),
            pl.BlockSpec((TM_OUT, D), lambda i: _smp_tile(i)[1:]),
            pl.BlockSpec((1, 3, D), lambda i: (_tile_row(i, TM_OUT), 0, 0)),
            pl.BlockSpec((TM_OUT, 4 * D), lambda i: (i, 0)),
            pl.BlockSpec((TM_OUT, 4 * D), lambda i: (i, 1)),
            pl.BlockSpec((2, TM_OUT, D), _ctx_tile),
            pl.BlockSpec((2, TM_OUT, D), _smp_tile),
            pl.BlockSpec((TM_OUT, LANES), lambda i: (i, 1)),
            pl.BlockSpec((2, LANES, D), const3),
            pl.BlockSpec((2, 1, D), const3),
            pl.BlockSpec((1, D), const2),
            pl.BlockSpec((1, D), const2),
            pl.BlockSpec((3, D), const2),
            pl.BlockSpec((1, D), const2),
            pl.BlockSpec((1, D), const2),
            pl.BlockSpec((D, LANES), const2),
            pl.BlockSpec((LANES, D), const2),
            pl.BlockSpec((2 * D, D), const2),
        ],
        out_specs=pl.BlockSpec((TM_OUT, D), row),
        out_shape=jax.ShapeDtypeStruct((N_TOK, D), F32),
        compiler_params=pltpu.CompilerParams(
            dimension_semantics=("arbitrary",), vmem_limit_bytes=VMEM_LIMIT),
        name="even_out",
    )(x_ctx, x_smp, mod3, p_all, p_all, o_ctx, o_smp, lr, a2pad, a0, k_a, r_k, conv_w, ln_w, ln_b, ind, bind, wo_bf)


def _gla_kernel(q_ref, k_ref, v_ref, lrg_ref, gk2_ref, gkb_ref, m_ref, *rest, seq_len, zero_init):
    s0_ref = None if zero_init else rest[0]
    o_ref, sout_ref, qe_scr, st_scr, e_scr = rest[-5:]
    z = pl.program_id(0)
    n_chunks = seq_len // CHUNK
    group = min(GLA_GROUP, n_chunks)
    n_groups = n_chunks // group
    incl_b = m_ref[0] > 0.5
    gk2 = gk2_ref[0].astype(BF16)
    gkb = gkb_ref[0]
    scale = GLA_DK ** -0.5

    def phase1(it, carry):
        cs = [it * group + j for j in range(group)]
        rows = [_chunk_rows(c) for c in cs]
        logit = [_bdot(lrg_ref[rw, :], gk2) + gkb for rw in rows]
        g = [jax.nn.log_sigmoid(x) / GLA_GATE_NORM for x in logit]
        bc, b_last = zip(*[_scan_cumsum(x, z) for x in g])
        k = [k_ref[rw, :].astype(F32) for rw in rows]
        qe = [q_ref[rw, :].astype(F32) * scale * jnp.exp(x) for rw, x in zip(rows, bc)]
        ke = [kc * jnp.exp(-x) for kc, x in zip(k, bc)]
        kd = [kc * jnp.exp(bl - x) for kc, x, bl in zip(k, bc, b_last)]
        v = [v_ref[rw, :] for rw in rows]
        att = [jnp.where(incl_b, _bdot_nt(a, b), 0.0) for a, b in zip(qe, ke)]
        o_in = [_bdot(a, b) for a, b in zip(att, v)]
        vtk = [_bdot_tn(a, b) for a, b in zip(v, kd)]
        for j in range(group):
            o_ref[0, rows[j], :] = o_in[j]
            qe_scr[cs[j]] = qe[j]
            st_scr[cs[j]] = vtk[j]
            e_scr[cs[j]] = jnp.broadcast_to(jnp.exp(b_last[j]), (8, GLA_DK))
        return carry

    lax.fori_loop(0, n_groups, phase1, 0)

    def phase2(ci, st):
        c = ci + z * (n_chunks - 1 - 2 * ci)
        new_st = st * e_scr[c][0:1, :] + st_scr[c]
        st_scr[c] = st
        return new_st

    st_init = jnp.zeros((GLA_DV, GLA_DK), F32) if zero_init else s0_ref[0, 0, 0].T
    st_fin = lax.fori_loop(0, n_chunks, phase2, st_init)
    sout_ref[0, 0, 0] = st_fin.T

    def phase3(it, carry):
        cs = [it * group + j for j in range(group)]
        inter = [_bdot_nt(qe_scr[c], st_scr[c]) for c in cs]
        for j in range(group):
            rows = _chunk_rows(cs[j])
            o_ref[0, rows, :] = o_ref[0, rows, :] + inter[j]
        return carry

    lax.fori_loop(0, n_groups, phase3, 0)


def _gla_scan(p_all, lrg, gk2pad, gkb, s0, masks, *, n_seq, seq_len, tok_blk0):
    def tok(b):
        return tok_blk0 + b

    n_chunks = seq_len // CHUNK
    init = () if s0 is None else (s0,)
    return pl.pallas_call(
        functools.partial(_gla_kernel, seq_len=seq_len, zero_init=s0 is None),
        grid=(2, n_seq, GLA_H),
        in_specs=[
            pl.BlockSpec((seq_len, GLA_DK), lambda z, b, h: (tok(b), h)),
            pl.BlockSpec((seq_len, GLA_DK), lambda z, b, h: (tok(b), GLA_H + h)),
            pl.BlockSpec((seq_len, GLA_DV), lambda z, b, h: (tok(b), GLA_H + h)),
            pl.BlockSpec((seq_len, LANES), lambda z, b, h: (tok(b), 0)),
            pl.BlockSpec((1, LANES, GLA_DK), lambda z, b, h: (z, 0, h)),
            pl.BlockSpec((1, 1, GLA_DK), lambda z, b, h: (z, 0, h)),
            pl.BlockSpec((1, CHUNK, CHUNK), lambda z, b, h: (z, 0, 0)),
        ] + [pl.BlockSpec((1, 1, 1, GLA_DK, GLA_DV), lambda z, b, h: (b, z, h, 0, 0))] * len(init),
        out_specs=[
            pl.BlockSpec((1, seq_len, GLA_DV), lambda z, b, h: (z, b, h)),
            pl.BlockSpec((1, 1, 1, GLA_DK, GLA_DV), lambda z, b, h: (b, z, h, 0, 0)),
        ],
        out_shape=[
            jax.ShapeDtypeStruct((2, n_seq * seq_len, D), F32),
            jax.ShapeDtypeStruct((n_seq, 2, GLA_H, GLA_DK, GLA_DV), F32),
        ],
        scratch_shapes=[
            pltpu.VMEM((n_chunks, CHUNK, GLA_DK), F32),
            pltpu.VMEM((n_chunks, GLA_DV, GLA_DK), F32),
            pltpu.VMEM((n_chunks, 8, GLA_DK), F32),
        ],
        compiler_params=pltpu.CompilerParams(
            dimension_semantics=("arbitrary", "arbitrary", "arbitrary"), vmem_limit_bytes=VMEM_LIMIT),
        name="gla_scan_%d" % seq_len,
    )(p_all, p_all, p_all, lrg, gk2pad, gkb, masks, *init)


def _odd_out_kernel(x_ref, mod_ref, zg_ref, oc_ref, os_ref, gn_ref, ind_ref, bind_ref, wo_ref, fg_ref,
                    yc_ref, ys_ref):
    i = pl.program_id(0)
    o = _sum_dirs(i, oc_ref, os_ref)
    ms = _group_stat(o * o, ind_ref, bind_ref)
    on = o * lax.rsqrt(ms + NORM_EPS) * gn_ref[...]
    og = on * _silu(zg_ref[...].astype(F32))
    y = _dot(og.astype(BF16), wo_ref[...])
    gate = mod_ref[0, 2:3, :]
    x = x_ref[...] + gate * y
    out = x * lax.rsqrt(jnp.mean(x * x, axis=-1, keepdims=True) + NORM_EPS) * fg_ref[...]

    @pl.when(i < N_CTX_TILES)
    def _():
        yc_ref[...] = out

    @pl.when(i >= N_CTX_TILES)
    def _():
        ys_ref[...] = out


def _odd_out(x, mod3, p_all, o_ctx, o_smp, g_norm, ind, bind, wo_bf, final_g):
    row = lambda i: (i, 0)
    const2 = lambda i: (0, 0)
    return pl.pallas_call(
        _odd_out_kernel,
        grid=(N_TOK // TM_OUT,),
        in_specs=[
            pl.BlockSpec((TM_OUT, D), row),
            pl.BlockSpec((1, 3, D), lambda i: (_tile_row(i, TM_OUT), 0, 0)),
            pl.BlockSpec((TM_OUT, D), lambda i: (i, 2)),
            pl.BlockSpec((2, TM_OUT, D), _ctx_tile),
            pl.BlockSpec((2, TM_OUT, D), _smp_tile),
            pl.BlockSpec((1, D), const2),
            pl.BlockSpec((D, LANES), const2),
            pl.BlockSpec((LANES, D), const2),
            pl.BlockSpec((D, D), const2),
            pl.BlockSpec((1, D), const2),
        ],
        out_specs=[
            pl.BlockSpec((TM_OUT, D), lambda i: _ctx_tile(i)[1:]),
            pl.BlockSpec((TM_OUT, D), lambda i: _smp_tile(i)[1:]),
        ],
        out_shape=[
            jax.ShapeDtypeStruct((N_CTX_TOK, D), F32),
            jax.ShapeDtypeStruct((N_TOK - N_CTX_TOK, D), F32),
        ],
        compiler_params=pltpu.CompilerParams(
            dimension_semantics=("arbitrary",), vmem_limit_bytes=VMEM_LIMIT),
        name="odd_out",
    )(x, mod3, p_all, o_ctx, o_smp, g_norm, ind, bind, wo_bf, final_g)


def _group_indicators(group):
    ch = jnp.arange(D) // group
    lane = jnp.arange(LANES)
    hit = (ch[:, None] == lane[None, :]).astype(F32)
    return (hit / float(group)).astype(BF16), hit.T.astype(BF16)


def _pad_dirs(w):
    r = w.shape[1]
    out = jnp.zeros((2, LANES, w.shape[2]), F32)
    out = out.at[0, 0:r].set(w[0])
    out = out.at[1, r:2 * r].set(w[1])
    return out


def kernel(x_prompt, x_sample, state_wkv, state_gla, c, c_ctx, norm_g, ada_w, ada_b, final_g, e_w_in, e_w_out, conv_w, wkv_w0, wkv_w1, wkv_w2, wkv_a0, wkv_a1, wkv_a2, wkv_k_k, wkv_k_a, wkv_r_k, wkv_ln_w, wkv_ln_b, o_w_in, o_w_out, gla_gk1, gla_gk2, gla_gk_b, gla_g_norm):
    x_ctx = x_prompt.reshape(N_CTX_TOK, D)
    x_smp = x_sample.reshape(N_TOK - N_CTX_TOK, D)
    cvec = jnp.zeros((8, D), F32).at[0].set(c_ctx).at[1:1 + SMP_B].set(c)
    mod = _modulation(cvec, ada_w, ada_b)
    masks = _chunk_masks()

    mod3 = mod[0].reshape(8, 3, D)
    wlr = jnp.concatenate([wkv_w1[0, 0], wkv_w1[0, 1], wkv_a1[0, 0], wkv_a1[0, 1]], axis=1)
    p_e, lr_e = _inproj(x_ctx, x_smp, 0, mod3, norm_g[0:1], e_w_in[0].astype(BF16), wlr.astype(BF16),
                        n_tanh=2 * 64)
    w2pad = _pad_dirs(wkv_w2[0])
    a2pad = _pad_dirs(wkv_a2[0])
    w0 = wkv_w0[0].reshape(2, 1, D)
    a0 = wkv_a0[0].reshape(2, 1, D)
    k_k = wkv_k_k[0].reshape(1, D)
    k_a = wkv_k_a[0].reshape(1, D)
    r_k = wkv_r_k[0].reshape(1, D)
    o_ctx, new_wkv = _wkv_scan(p_e, lr_e, w2pad, a2pad, w0, a0, k_k, k_a, None,
                               n_seq=CTX_B, seq_len=CTX_T, tok_blk0=0)
    o_smp, _ = _wkv_scan(p_e, lr_e, w2pad, a2pad, w0, a0, k_k, k_a, state_wkv[:, 0],
                         n_seq=SMP_B, seq_len=SMP_T, tok_blk0=N_CTX_TOK // SMP_T)
    ind64, bind64 = _group_indicators(WKV_N)
    x = _even_out(x_ctx, x_smp, mod3, p_e, o_ctx, o_smp, lr_e, a2pad, a0, k_a, r_k, conv_w[0],
                  wkv_ln_w[0].reshape(1, D), wkv_ln_b[0].reshape(1, D), ind64, bind64,
                  e_w_out[0].astype(BF16))

    mod3 = mod[1].reshape(8, 3, D)
    rank = gla_gk1.shape[-1]
    wlr = jnp.zeros((D, LANES), F32).at[:, 0:rank].set(gla_gk1[0, 0]).at[:, rank:2 * rank].set(gla_gk1[0, 1])
    p_o, lr_o = _inproj(x, x, N_CTX_TOK // TM_PROJ, mod3, norm_g[1:2], o_w_in[0].astype(BF16),
                        wlr.astype(BF16), n_tanh=0)
    gk2pad = _pad_dirs(gla_gk2[0])
    gkb = gla_gk_b[0].reshape(2, 1, GLA_H * GLA_DK)
    g_ctx, new_gla = _gla_scan(p_o, lr_o, gk2pad, gkb, None, masks,
                               n_seq=CTX_B, seq_len=CTX_T, tok_blk0=0)
    g_smp, _ = _gla_scan(p_o, lr_o, gk2pad, gkb, state_gla[:, 0], masks,
                         n_seq=SMP_B, seq_len=SMP_T, tok_blk0=N_CTX_TOK // SMP_T)
    ind256, bind256 = _group_indicators(GLA_DV)
    y_ctx, y_smp = _odd_out(x, mod3, p_o, g_ctx, g_smp, jnp.tile(gla_g_norm[0], GLA_H).reshape(1, D),
                            ind256, bind256, o_w_out[0].astype(BF16), final_g.reshape(1, D))
    return (y_ctx.reshape(CTX_B, CTX_T, D), y_smp.reshape(SMP_B, SMP_T, D), new_wkv[:, None], new_gla[:, None])
```

```python
import functools
import math

import jax
import jax.numpy as jnp
from jax import lax
from jax.experimental import pallas as pl
from jax.experimental.pallas import tpu as pltpu

F32 = jnp.float32
BF16 = jnp.bfloat16
HI = lax.Precision.HIGHEST

D = 1024
N_CTX_TOK = 16 * 256
N_TOK = 2 * N_CTX_TOK
CTX_B, CTX_T = 16, 256
SMP_B, SMP_T = 2, 2048
GRID_W = 64
WKV_H, WKV_N = 16, 64
GLA_H, GLA_DK, GLA_DV = 4, 128, 256
GLA_GATE_NORM = 16.0
GN_EPS = 64e-5
NORM_EPS = 1e-6
CHUNK = 64
WKV_GROUP = 4
WKV_HEADS_STEP = 8
WKV_CTX_SEG = 4
GLA_CTX_SEG = 4
GLA_GROUP = 8
LANES = 128
TM_PROJ = 1024
TN_PROJ_MAX = 2048
TM_OUT = 256
VMEM_LIMIT = 56 * 1024 * 1024


def _silu(x):
    return x * jax.nn.sigmoid(x)


def _dot(a, b, precision=None):
    return jnp.dot(a, b, preferred_element_type=F32, precision=precision)


def _dot_nt(a, b, precision=None):
    return lax.dot_general(a, b, (((1,), (1,)), ((), ())), preferred_element_type=F32, precision=precision)


def _dot_tn(a, b, precision=None):
    return lax.dot_general(a, b, (((0,), (0,)), ((), ())), preferred_element_type=F32, precision=precision)


def _bdot(a, b):
    return _dot(a.astype(BF16), b.astype(BF16))


def _bdot_nt(a, b):
    return _dot_nt(a.astype(BF16), b.astype(BF16))


def _bdot_tn(a, b):
    return _dot_tn(a.astype(BF16), b.astype(BF16))


def _split2(x):
    hi = x.astype(BF16)
    lo = (x - hi.astype(F32)).astype(BF16)
    return hi, lo


def _scan_cumsum(x, z):
    n = x.shape[0]
    row = lax.broadcasted_iota(jnp.int32, (n, 1), 0)
    pre = x
    s = 1
    while s < n:
        pre = pre + jnp.where(row >= s, pltpu.roll(pre, s, 0), 0.0)
        s *= 2
    tot = pre[n - 1:n, :]
    return jnp.where(z == 0, pre, tot - pre + x), tot


def _tile_row(i, tile):
    ctx_tiles = N_CTX_TOK // tile
    per_req = SMP_T // tile
    return jnp.where(i < ctx_tiles, 0, 1 + (i - ctx_tiles) // per_req)


def _mod_kernel(c_ref, w_ref, b_ref, o_ref):
    cf = _silu(c_ref[...])
    o_ref[0] = _dot(cf, w_ref[0], HI) + b_ref[0]


def _modulation(cvec, ada_w, ada_b):
    depth = ada_w.shape[0]
    return pl.pallas_call(
        _mod_kernel,
        grid=(depth, 3),
        in_specs=[
            pl.BlockSpec((8, D), lambda l, j: (0, 0)),
            pl.BlockSpec((1, D, D), lambda l, j: (l, 0, j)),
            pl.BlockSpec((1, 1, D), lambda l, j: (l, 0, j)),
        ],
        out_specs=pl.BlockSpec((1, 8, D), lambda l, j: (l, 0, j)),
        out_shape=jax.ShapeDtypeStruct((depth, 8, 3 * D), F32),
        compiler_params=pltpu.CompilerParams(
            dimension_semantics=("arbitrary", "arbitrary"), vmem_limit_bytes=VMEM_LIMIT),
        name="adaln_mod",
    )(cvec, ada_w, ada_b.reshape(depth, 1, 3 * D))


def _inproj_kernel(xa_ref, xb_ref, mod_ref, g_ref, w_ref, wlr_ref, p_ref, lr_ref, h_scr, *, n_tanh):
    i = pl.program_id(0)
    j = pl.program_id(1)

    @pl.when(j == 0)
    def _():
        x = jnp.where(i < N_CTX_TOK // TM_PROJ, xa_ref[...], xb_ref[...])
        y = x * lax.rsqrt(jnp.mean(x * x, axis=-1, keepdims=True) + NORM_EPS) * g_ref[...]
        shift = mod_ref[0, 0:1, :]
        scale = mod_ref[0, 1:2, :]
        h = y * (1.0 + scale) + shift
        hb = h.astype(BF16)
        h_scr[...] = hb
        lr = _dot(hb, wlr_ref[...])
        if n_tanh:
            lane = lax.broadcasted_iota(jnp.int32, lr.shape, 1)
            lr = jnp.where(lane < n_tanh, jnp.tanh(lr), lr)
        lr_ref[...] = lr

    p_ref[...] = _dot(h_scr[...], w_ref[...]).astype(p_ref.dtype)


def _inproj(x_ctx, x_smp, smp_tile0, mod3, g, w_bf, wlr_bf, n_tanh):
    n_out = w_bf.shape[1]
    n_lr = wlr_bf.shape[1]
    n_ctx = N_CTX_TOK // TM_PROJ
    tn = max(t for t in range(256, TN_PROJ_MAX + 1, 256) if n_out % t == 0)
    return pl.pallas_call(
        functools.partial(_inproj_kernel, n_tanh=n_tanh),
        grid=(N_TOK // TM_PROJ, n_out // tn),
        in_specs=[
            pl.BlockSpec((TM_PROJ, D), lambda i, j: (jnp.minimum(i, n_ctx - 1), 0)),
            pl.BlockSpec((TM_PROJ, D), lambda i, j: (smp_tile0 + jnp.maximum(i - n_ctx, 0), 0)),
            pl.BlockSpec((1, 3, D), lambda i, j: (_tile_row(i, TM_PROJ), 0, 0)),
            pl.BlockSpec((1, D), lambda i, j: (0, 0)),
            pl.BlockSpec((D, tn), lambda i, j: (0, j)),
            pl.BlockSpec((D, n_lr), lambda i, j: (0, 0)),
        ],
        out_specs=[
            pl.BlockSpec((TM_PROJ, tn), lambda i, j: (i, j)),
            pl.BlockSpec((TM_PROJ, n_lr), lambda i, j: (i, 0)),
        ],
        out_shape=[
            jax.ShapeDtypeStruct((N_TOK, n_out), BF16),
            jax.ShapeDtypeStruct((N_TOK, n_lr), F32),
        ],
        scratch_shapes=[pltpu.VMEM((TM_PROJ, D), BF16)],
        compiler_params=pltpu.CompilerParams(
            dimension_semantics=("arbitrary", "arbitrary"), vmem_limit_bytes=VMEM_LIMIT),
        name="inproj",
    )(x_ctx, x_smp, mod3, g, w_bf, wlr_bf)


def _chunk_masks():
    idx = jnp.arange(CHUNK)
    lower = (idx[:, None] >= idx[None, :]).astype(F32)
    return jnp.stack([lower, lower.T], axis=0)


def _chunk_rows(c):
    return pl.ds(pl.multiple_of(c * CHUNK, CHUNK), CHUNK)


def _inv_unit_tri(lmats, tick=lambda: None):
    row = lax.broadcasted_iota(jnp.int32, (CHUNK, CHUNK), 0)
    col = lax.broadcasted_iota(jnp.int32, (CHUNK, CHUNK), 1)
    eye = (row == col).astype(F32)

    def same_block(n):
        sh = n.bit_length() - 1
        return lax.shift_right_logical(row, sh) == lax.shift_right_logical(col, sh)

    blk16 = same_block(16)
    l16 = [jnp.where(blk16, m, 0.0) for m in lmats]
    x = [eye - m for m in l16]
    p = [_bdot(m, m) for m in l16]
    for step in range(3):
        x = [xi + _bdot(xi, pi) for xi, pi in zip(x, p)]
        if step < 2:
            p = [_bdot(pi, pi) for pi in p]
        if step == 0:
            tick()
    n = 16
    while n < CHUNK:
        off = same_block(2 * n) & jnp.logical_not(same_block(n))
        xl = [_bdot(xi, jnp.where(off, m, 0.0)) for xi, m in zip(x, lmats)]
        x = [xi - _bdot(yi, xi) for xi, yi in zip(x, xl)]
        if n == 16:
            tick()
        n *= 2
    return x


_KT, _RT, _KH, _BH, _KP, _BP = range(6)


def _wkv_kernel(r_ref, k_ref, v_ref, lrw_ref, lra_ref, w2_ref, a2_ref, w0_ref, a0_ref,
                kk_ref, ka_ref, *rest, seq_len, n_seg, zero_init):
    s0_ref = None if zero_init else rest[0]
    o_ref, sout_ref, rp_scr, w_scr, h_scr, d_scr, rt_a, ops_a = rest[-8:]
    z = pl.program_id(0)
    n_chunks = seq_len // CHUNK
    seg_chunks = n_chunks // n_seg
    n_groups = n_chunks // WKV_GROUP
    pairs = range(WKV_HEADS_STEP // 2)
    row = lax.broadcasted_iota(jnp.int32, (CHUNK, LANES), 0)
    lane = lax.broadcasted_iota(jnp.int32, (CHUNK, LANES), 1)
    col = lane & (CHUNK - 1)
    left = lane < WKV_N
    incl2 = ((z == 0) & (row >= col)) | ((z == 1) & (row <= col))
    strict2 = incl2 & (row != col)
    row_b = lax.broadcasted_iota(jnp.int32, (LANES, LANES), 0) < WKV_N
    lane_b = lax.broadcasted_iota(jnp.int32, (LANES, LANES), 1) < WKV_N
    diag_blocks = row_b == lane_b
    zeros_bf = jnp.zeros((CHUNK, LANES), BF16)
    w2 = w2_ref[0].astype(BF16)
    a2 = a2_ref[0].astype(BF16)
    w0 = w0_ref[0]
    a0 = a0_ref[0]
    k_k = kk_ref[...]
    k_a = ka_ref[...]

    def pair_sl(p):
        return slice(p * LANES, (p + 1) * LANES)

    def stack(a, b):
        return jnp.concatenate([a, b], axis=0)

    def prepare_chunk(gi, j, rt_scr, ops_scr):
        c = gi * WKV_GROUP + j
        rows = _chunk_rows(c)
        w_raw = w0 + _bdot(lrw_ref[rows, :], w2)
        a = jax.nn.sigmoid(a0 + _bdot(lra_ref[rows, :], a2))
        lw = -math.exp(-0.5) * jax.nn.sigmoid(w_raw)
        g, gtot = _scan_cumsum(lw, z)
        r = r_ref[rows, :].astype(F32)
        k = k_ref[rows, :].astype(F32)
        kkr = k * k_k
        sq = kkr * kkr
        nrm = []
        for p in pairs:
            sq_p = sq[:, pair_sl(p)]
            n0 = jnp.sum(jnp.where(left, sq_p, 0.0), axis=-1, keepdims=True)
            n1 = jnp.sum(jnp.where(left, 0.0, sq_p), axis=-1, keepdims=True)
            nrm.append(jnp.where(left, jnp.sqrt(n0), jnp.sqrt(n1)))
        kk = kkr / jnp.maximum(jnp.concatenate(nrm, axis=-1), 1e-12)
        b = kk * a
        kmod = k * (1.0 + (a - 1.0) * k_a)
        e_ng = jnp.exp(-g)
        e_gc = jnp.exp(gtot - g)
        rt = r * jnp.exp(g)
        rt_scr[j] = rt
        ops_scr[j, _KT] = (kk * jnp.exp(g - lw)).astype(BF16)
        ops_scr[j, _RT] = rt.astype(BF16)
        ops_scr[j, _KH] = (kmod * e_ng).astype(BF16)
        ops_scr[j, _BH] = (b * e_ng).astype(BF16)
        ops_scr[j, _KP] = (kmod * e_gc).astype(BF16)
        ops_scr[j, _BP] = (b * e_gc).astype(BF16)
        d_scr[c] = jnp.broadcast_to(jnp.exp(gtot), (8, WKV_HEADS_STEP * WKV_N))

    def chains(gi, rt_scr, ops_scr, fillers):
        pending = iter(fillers)

        def tick():
            next(pending, lambda: None)()

        cs = [gi * WKV_GROUP + j for j in range(WKV_GROUP)]
        rows = [_chunk_rows(c) for c in cs]
        cp = []
        for j in range(WKV_GROUP):
            for p in pairs:
                sl = pair_sl(p)
                op = lambda i: ops_scr[j, i, :, sl]
                cp.append(dict(j=j, p=p, rt=rt_scr[j, :, sl], kt=op(_KT), rtb=op(_RT), v=v_ref[rows[j], sl],
                               bk=stack(op(_BH), op(_KH)), kb=stack(op(_KH), op(_BH)),
                               kbp=stack(op(_KP), -op(_BP)), bp=op(_BP)))
        units = [(q, h) for q in range(len(cp)) for h in range(2)]
        pick = [left, jnp.logical_not(left)]
        lm = [jnp.where(strict2, _dot_nt(jnp.where(pick[h], cp[q]["kt"], zeros_bf), cp[q]["bk"]), 0.0)
              for q, h in units]
        rr = [jnp.where(incl2, _dot_nt(jnp.where(pick[h], cp[q]["rtb"], zeros_bf), cp[q]["kb"]), 0.0)
              for q, h in units]
        tick()
        tinv = _inv_unit_tri([m[:, :CHUNK] for m in lm], tick)
        lm_b = [m.astype(BF16) for m in lm]
        rr_b = [m.astype(BF16) for m in rr]
        mv = [_dot(lm_b[u], stack(zeros_bf, cp[q]["v"])) for u, (q, h) in enumerate(units)]
        mrkv = [_dot(rr_b[u], stack(cp[q]["v"], zeros_bf)) for u, (q, h) in enumerate(units)]
        pm = [_bdot(tinv[u], cp[q]["kt"]) for u, (q, h) in enumerate(units)]
        qm = [_bdot(tinv[u], mv[u]) for u in range(len(units))]
        tick()
        corr_p = [_dot(rr_b[u], stack(zeros_bf, pm[u].astype(BF16))) for u in range(len(units))]
        corr_q = [_dot(rr_b[u], stack(zeros_bf, qm[u].astype(BF16))) for u in range(len(units))]

        def merge(xs, q):
            return jnp.where(left, xs[2 * q], xs[2 * q + 1])

        n_cp = range(len(cp))
        rp = [cp[q]["rt"] - merge(corr_p, q) for q in n_cp]
        op = [merge(mrkv, q) - merge(corr_q, q) for q in n_cp]
        p_pair = [merge(pm, q).astype(BF16) for q in n_cp]
        q_pair = [merge(qm, q).astype(BF16) for q in n_cp]
        wfull = [_dot_tn(p_pair[q], cp[q]["bp"]) for q in n_cp]
        hfull = [_dot_tn(stack(cp[q]["v"], q_pair[q]), cp[q]["kbp"]) for q in n_cp]
        for q in n_cp:
            c, p = cs[cp[q]["j"]], cp[q]["p"]
            o_ref[0, rows[cp[q]["j"]], pair_sl(p)] = op[q]
            rp_scr[c, p] = rp[q].astype(BF16)
            w_hi, w_lo = _split2(jnp.where(diag_blocks, wfull[q], 0.0))
            w_scr[c, p, 0] = w_hi
            w_scr[c, p, 1] = w_lo
            h_scr[c, p] = jnp.where(left, hfull[q][:WKV_N], hfull[q][WKV_N:])
        for rest_filler in pending:
            rest_filler()

    def prepare(gi, rt_scr, ops_scr):
        return [functools.partial(prepare_chunk, gi, j, rt_scr, ops_scr) for j in range(WKV_GROUP)]

    def phase1(gi, carry):
        for prep in prepare(gi, rt_a, ops_a):
            prep()
        chains(gi, rt_a, ops_a, [])
        return carry

    lax.fori_loop(0, n_groups, phase1, 0)

    def phase2(pos, carry):
        chains2 = [(sq, p) for sq in range(n_seg) for p in pairs]
        cidx = [sq * seg_chunks + pos + z * (seg_chunks - 1 - 2 * pos) for sq in range(n_seg)]
        split = [_split2(s) for s in carry]
        sw_hi = [_dot(stack(split[i][0], split[i][1]), w_scr[cidx[sq], p, 0]) for i, (sq, p) in enumerate(chains2)]
        sw_lo = [_dot(split[i][0], w_scr[cidx[sq], p, 1]) for i, (sq, p) in enumerate(chains2)]
        new_s = []
        for i, (sq, p) in enumerate(chains2):
            d_tot = d_scr[cidx[sq]][0:1, pair_sl(p)]
            new_s.append(carry[i] * d_tot - (sw_hi[i][:CHUNK] + sw_hi[i][CHUNK:] + sw_lo[i]) + h_scr[cidx[sq], p])
        for i, (sq, p) in enumerate(chains2):
            h_scr[cidx[sq], p] = carry[i]
        return tuple(new_s)

    if zero_init:
        s_init = tuple(jnp.zeros((WKV_N, LANES), F32) for sq in range(n_seg) for p in pairs)
    else:
        s_init = tuple(jnp.concatenate([s0_ref[sq, 0, 2 * p], s0_ref[sq, 0, 2 * p + 1]], axis=-1)
                       for sq in range(n_seg) for p in pairs)
    s_fin = lax.fori_loop(0, seg_chunks, phase2, s_init)
    for sq in range(n_seg):
        for p in pairs:
            s_pair = s_fin[sq * len(pairs) + p]
            sout_ref[sq, 0, 2 * p] = s_pair[:, :WKV_N]
            sout_ref[sq, 0, 2 * p + 1] = s_pair[:, WKV_N:]

    def phase3(it, carry):
        cs = [it * WKV_GROUP + j for j in range(WKV_GROUP)]
        s_bd = [[stack(jnp.where(left, h_scr[c, p], 0.0), jnp.where(left, 0.0, h_scr[c, p])).astype(BF16)
                 for p in pairs] for c in cs]
        inter = [[_dot_nt(rp_scr[c, p], s_bd[j][p]) for p in pairs] for j, c in enumerate(cs)]
        for j, c in enumerate(cs):
            rows = _chunk_rows(c)
            o_ref[0, rows, :] = o_ref[0, rows, :] + jnp.concatenate(inter[j], axis=-1)
        return carry

    lax.fori_loop(0, n_groups, phase3, 0)


def _wkv_scan(p_all, lr, w2pad, a2pad, w0, a0, k_k, k_a, s0, *, n_seq, seq_len, n_seg, tok_blk0):
    def tok(b):
        return tok_blk0 + b

    rows = n_seg * seq_len
    width = WKV_HEADS_STEP * WKV_N
    n_pairs = WKV_HEADS_STEP // 2
    col = lambda base: (lambda z, b, hq: (tok(b), base // width + hq))
    vec = lambda z, b, hq: (z, 0, hq)
    n_chunks = rows // CHUNK
    state_spec = pl.BlockSpec((n_seg, 1, WKV_HEADS_STEP, WKV_N, WKV_N), lambda z, b, hq: (b, z, hq, 0, 0))
    init = () if s0 is None else (s0,)
    return pl.pallas_call(
        functools.partial(_wkv_kernel, seq_len=rows, n_seg=n_seg, zero_init=s0 is None),
        grid=(2, n_seq // n_seg, WKV_H // WKV_HEADS_STEP),
        in_specs=[
            pl.BlockSpec((rows, width), col(4 * D)),
            pl.BlockSpec((rows, width), col(5 * D)),
            pl.BlockSpec((rows, width), col(6 * D)),
            pl.BlockSpec((rows, LANES), lambda z, b, hq: (tok(b), 0)),
            pl.BlockSpec((rows, LANES), lambda z, b, hq: (tok(b), 1)),
            pl.BlockSpec((1, LANES, width), vec),
            pl.BlockSpec((1, LANES, width), vec),
            pl.BlockSpec((1, 1, width), vec),
            pl.BlockSpec((1, 1, width), vec),
            pl.BlockSpec((1, width), lambda z, b, hq: (0, hq)),
            pl.BlockSpec((1, width), lambda z, b, hq: (0, hq)),
        ] + [state_spec] * len(init),
        out_specs=[
            pl.BlockSpec((1, rows, width), lambda z, b, hq: (z, b, hq)),
            state_spec,
        ],
        out_shape=[
            jax.ShapeDtypeStruct((2, n_seq * seq_len, D), F32),
            jax.ShapeDtypeStruct((n_seq, 2, WKV_H, WKV_N, WKV_N), F32),
        ],
        scratch_shapes=[
            pltpu.VMEM((n_chunks, n_pairs, CHUNK, LANES), BF16),
            pltpu.VMEM((n_chunks, n_pairs, 2, LANES, LANES), BF16),
            pltpu.VMEM((n_chunks, n_pairs, CHUNK, LANES), F32),
            pltpu.VMEM((n_chunks, 8, width), F32),
            pltpu.VMEM((WKV_GROUP, CHUNK, width), F32),
            pltpu.VMEM((WKV_GROUP, 6, CHUNK, width), BF16),
        ],
        compiler_params=pltpu.CompilerParams(
            dimension_semantics=("arbitrary", "arbitrary", "arbitrary"), vmem_limit_bytes=VMEM_LIMIT),
        name="wkv_scan_%d" % seq_len,
    )(p_all, p_all, p_all, lr, lr, w2pad, a2pad, w0, a0, k_k, k_a, *init)


def _dot2(x, w_bf):
    hi, lo = _split2(x)
    return _dot(hi, w_bf) + _dot(lo, w_bf)


def _group_stat(x, ind_ref, bind_ref):
    return _dot2(_dot2(x, ind_ref[...]), bind_ref[...])


N_CTX_TILES = N_CTX_TOK // TM_OUT


def _sum_dirs(i, oc_ref, os_ref):
    is_ctx = i < N_CTX_TILES
    return jnp.where(is_ctx, oc_ref[0], os_ref[0]) + jnp.where(is_ctx, oc_ref[1], os_ref[1])


def _ctx_tile(i):
    return (0, jnp.minimum(i, N_CTX_TILES - 1), 0)


def _smp_tile(i):
    return (0, jnp.maximum(i - N_CTX_TILES, 0), 0)


def _even_out_kernel(xc_ref, xs_ref, mod_ref, pc_ref, pw_ref, oc_ref, os_ref, lra_ref, a2_ref, a0_ref, ka_ref,
                     rk_ref, cw_ref, lnw_ref, lnb_ref, ind_ref, bind_ref, wo_ref, out_ref):
    i = pl.program_id(0)
    u = pc_ref[:, 0:D].astype(F32)
    gb = pc_ref[:, D:2 * D].astype(F32)
    gc = pc_ref[:, 2 * D:3 * D].astype(F32)
    zc = pc_ref[:, 3 * D:4 * D].astype(F32)
    xg = gc * u
    row_len = jnp.where(i < N_CTX_TILES, CTX_T, GRID_W)
    pos = lax.broadcasted_iota(jnp.int32, (TM_OUT, 1), 0) & (row_len - 1)
    prev = jnp.where(pos == 0, 0.0, pltpu.roll(xg, 1, 0))
    nxt = jnp.where(pos == row_len - 1, 0.0, pltpu.roll(xg, TM_OUT - 1, 0))
    conv = cw_ref[0:1, :] * prev + cw_ref[1:2, :] * xg + cw_ref[2:3, :] * nxt
    o_conv = _silu(zc) * gb * conv
    r = pw_ref[:, 0:D].astype(F32)
    k = pw_ref[:, D:2 * D].astype(F32)
    v = pw_ref[:, 2 * D:3 * D].astype(F32)
    zw = pw_ref[:, 3 * D:4 * D].astype(F32)
    o = _sum_dirs(i, oc_ref, os_ref)
    mu = _group_stat(o, ind_ref, bind_ref)
    dlt = o - mu
    var = _group_stat(dlt * dlt, ind_ref, bind_ref)
    gn = dlt * lax.rsqrt(var + GN_EPS) * lnw_ref[...] + lnb_ref[...]
    lra = lra_ref[...]
    ic0 = jax.nn.sigmoid(a0_ref[0] + _bdot(lra, a2_ref[0]))
    ic1 = jax.nn.sigmoid(a0_ref[1] + _bdot(lra, a2_ref[1]))
    ksum = k * (2.0 + (ic0 + ic1 - 2.0) * ka_ref[...])
    bonus = _group_stat(r * ksum * rk_ref[...], ind_ref, bind_ref) * float(WKV_N)
    o_wkv = (gn + bonus * v) * _silu(zw)
    y = _dot(o_conv.astype(BF16), wo_ref[0:D, :]) + _dot(o_wkv.astype(BF16), wo_ref[D:2 * D, :])
    gate = mod_ref[0, 2:3, :]
    out_ref[...] = jnp.where(i < N_CTX_TILES, xc_ref[...], xs_ref[...]) + gate * y


def _even_out(x_ctx, x_smp, mod3, p_all, o_ctx, o_smp, lr, a2pad, a0, k_a, r_k, conv_w, ln_w, ln_b, ind, bind,
              wo_bf):
    row = lambda i: (i, 0)
    const2 = lambda i: (0, 0)
    const3 = lambda i: (0, 0, 0)
    return pl.pallas_call(
        _even_out_kernel,
        grid=(N_TOK // TM_OUT,),
        in_specs=[
            pl.BlockSpec((TM_OUT, D), lambda i: _ctx_tile(i)[1:]),
            pl.BlockSpec((TM_OUT, D), lambda i: _smp_tile(i)[1:]),
            pl.BlockSpec((1, 3, D), lambda i: (_tile_row(i, TM_OUT), 0, 0)),
            pl.BlockSpec((TM_OUT, 4 * D), lambda i: (i, 0)),
            pl.BlockSpec((TM_OUT, 4 * D), lambda i: (i, 1)),
            pl.BlockSpec((2, TM_OUT, D), _ctx_tile),
            pl.BlockSpec((2, TM_OUT, D), _smp_tile),
            pl.BlockSpec((TM_OUT, LANES), lambda i: (i, 1)),
            pl.BlockSpec((2, LANES, D), const3),
            pl.BlockSpec((2, 1, D), const3),
            pl.BlockSpec((1, D), const2),
            pl.BlockSpec((1, D), const2),
            pl.BlockSpec((3, D), const2),
            pl.BlockSpec((1, D), const2),
            pl.BlockSpec((1, D), const2),
            pl.BlockSpec((D, LANES), const2),
            pl.BlockSpec((LANES, D), const2),
            pl.BlockSpec((2 * D, D), const2),
        ],
        out_specs=pl.BlockSpec((TM_OUT, D), row),
        out_shape=jax.ShapeDtypeStruct((N_TOK, D), F32),
        compiler_params=pltpu.CompilerParams(
            dimension_semantics=("arbitrary",), vmem_limit_bytes=VMEM_LIMIT),
        name="even_out",
    )(x_ctx, x_smp, mod3, p_all, p_all, o_ctx, o_smp, lr, a2pad, a0, k_a, r_k, conv_w, ln_w, ln_b, ind, bind, wo_bf)


def _gla_kernel(q_ref, k_ref, v_ref, lrg_ref, gk2_ref, gkb_ref, m_ref, *rest, seq_len, n_seg, zero_init):
    s0_ref = None if zero_init else rest[0]
    o_ref, sout_ref, qe_scr, st_scr, e_scr = rest[-5:]
    z = pl.program_id(0)
    n_chunks = seq_len // CHUNK
    seg_chunks = n_chunks // n_seg
    group = min(GLA_GROUP, n_chunks)
    n_groups = n_chunks // group
    incl_b = m_ref[0] > 0.5
    gk2 = gk2_ref[0].astype(BF16)
    gkb = gkb_ref[0]
    scale = GLA_DK ** -0.5

    def phase1(it, carry):
        cs = [it * group + j for j in range(group)]
        rows = [_chunk_rows(c) for c in cs]
        logit = [_bdot(lrg_ref[rw, :], gk2) + gkb for rw in rows]
        g = [jax.nn.log_sigmoid(x) / GLA_GATE_NORM for x in logit]
        bc, b_last = zip(*[_scan_cumsum(x, z) for x in g])
        k = [k_ref[rw, :].astype(F32) for rw in rows]
        qe = [q_ref[rw, :].astype(F32) * scale * jnp.exp(x) for rw, x in zip(rows, bc)]
        ke = [kc * jnp.exp(-x) for kc, x in zip(k, bc)]
        kd = [kc * jnp.exp(bl - x) for kc, x, bl in zip(k, bc, b_last)]
        v = [v_ref[rw, :] for rw in rows]
        att = [jnp.where(incl_b, _bdot_nt(a, b), 0.0) for a, b in zip(qe, ke)]
        o_in = [_bdot(a, b) for a, b in zip(att, v)]
        vtk = [_bdot_tn(a, b) for a, b in zip(v, kd)]
        for j in range(group):
            o_ref[0, rows[j], :] = o_in[j]
            qe_scr[cs[j]] = qe[j]
            st_scr[cs[j]] = vtk[j]
            e_scr[cs[j]] = jnp.broadcast_to(jnp.exp(b_last[j]), (8, GLA_DK))
        return carry

    lax.fori_loop(0, n_groups, phase1, 0)

    def phase2(pos, sts):
        cidx = [sq * seg_chunks + pos + z * (seg_chunks - 1 - 2 * pos) for sq in range(n_seg)]
        new = [st * e_scr[c][0:1, :] + st_scr[c] for st, c in zip(sts, cidx)]
        for st, c in zip(sts, cidx):
            st_scr[c] = st
        return tuple(new)

    if zero_init:
        st_init = tuple(jnp.zeros((GLA_DV, GLA_DK), F32) for sq in range(n_seg))
    else:
        st_init = tuple(s0_ref[sq, 0, 0].T for sq in range(n_seg))
    st_fin = lax.fori_loop(0, seg_chunks, phase2, st_init)
    for sq in range(n_seg):
        sout_ref[sq, 0, 0] = st_fin[sq].T

    def phase3(it, carry):
        cs = [it * group + j for j in range(group)]
        inter = [_bdot_nt(qe_scr[c], st_scr[c]) for c in cs]
        for j in range(group):
            rows = _chunk_rows(cs[j])
            o_ref[0, rows, :] = o_ref[0, rows, :] + inter[j]
        return carry

    lax.fori_loop(0, n_groups, phase3, 0)


def _gla_scan(p_all, lrg, gk2pad, gkb, s0, masks, *, n_seq, seq_len, n_seg, tok_blk0):
    def tok(b):
        return tok_blk0 + b

    rows = n_seg * seq_len
    n_chunks = rows // CHUNK
    state_spec = pl.BlockSpec((n_seg, 1, 1, GLA_DK, GLA_DV), lambda z, b, h: (b, z, h, 0, 0))
    init = () if s0 is None else (s0,)
    return pl.pallas_call(
        functools.partial(_gla_kernel, seq_len=rows, n_seg=n_seg, zero_init=s0 is None),
        grid=(2, n_seq // n_seg, GLA_H),
        in_specs=[
            pl.BlockSpec((rows, GLA_DK), lambda z, b, h: (tok(b), h)),
            pl.BlockSpec((rows, GLA_DK), lambda z, b, h: (tok(b), GLA_H + h)),
            pl.BlockSpec((rows, GLA_DV), lambda z, b, h: (tok(b), GLA_H + h)),
            pl.BlockSpec((rows, LANES), lambda z, b, h: (tok(b), 0)),
            pl.BlockSpec((1, LANES, GLA_DK), lambda z, b, h: (z, 0, h)),
            pl.BlockSpec((1, 1, GLA_DK), lambda z, b, h: (z, 0, h)),
            pl.BlockSpec((1, CHUNK, CHUNK), lambda z, b, h: (z, 0, 0)),
        ] + [state_spec] * len(init),
        out_specs=[
            pl.BlockSpec((1, rows, GLA_DV), lambda z, b, h: (z, b, h)),
            state_spec,
        ],
        out_shape=[
            jax.ShapeDtypeStruct((2, n_seq * seq_len, D), F32),
            jax.ShapeDtypeStruct((n_seq, 2, GLA_H, GLA_DK, GLA_DV), F32),
        ],
        scratch_shapes=[
            pltpu.VMEM((n_chunks, CHUNK, GLA_DK), F32),
            pltpu.VMEM((n_chunks, GLA_DV, GLA_DK), F32),
            pltpu.VMEM((n_chunks, 8, GLA_DK), F32),
        ],
        compiler_params=pltpu.CompilerParams(
            dimension_semantics=("arbitrary", "arbitrary", "arbitrary"), vmem_limit_bytes=VMEM_LIMIT),
        name="gla_scan_%d" % seq_len,
    )(p_all, p_all, p_all, lrg, gk2pad, gkb, masks, *init)


def _odd_out_kernel(x_ref, mod_ref, zg_ref, oc_ref, os_ref, gn_ref, ind_ref, bind_ref, wo_ref, fg_ref,
                    yc_ref, ys_ref):
    i = pl.program_id(0)
    o = _sum_dirs(i, oc_ref, os_ref)
    ms = _group_stat(o * o, ind_ref, bind_ref)
    on = o * lax.rsqrt(ms + NORM_EPS) * gn_ref[...]
    og = on * _silu(zg_ref[...].astype(F32))
    y = _dot(og.astype(BF16), wo_ref[...])
    gate = mod_ref[0, 2:3, :]
    x = x_ref[...] + gate * y
    out = x * lax.rsqrt(jnp.mean(x * x, axis=-1, keepdims=True) + NORM_EPS) * fg_ref[...]

    @pl.when(i < N_CTX_TILES)
    def _():
        yc_ref[...] = out

    @pl.when(i >= N_CTX_TILES)
    def _():
        ys_ref[...] = out


def _odd_out(x, mod3, p_all, o_ctx, o_smp, g_norm, ind, bind, wo_bf, final_g):
    row = lambda i: (i, 0)
    const2 = lambda i: (0, 0)
    return pl.pallas_call(
        _odd_out_kernel,
        grid=(N_TOK // TM_OUT,),
        in_specs=[
            pl.BlockSpec((TM_OUT, D), row),
            pl.BlockSpec((1, 3, D), lambda i: (_tile_row(i, TM_OUT), 0, 0)),
            pl.BlockSpec((TM_OUT, D), lambda i: (i, 2)),
            pl.BlockSpec((2, TM_OUT, D), _ctx_tile),
            pl.BlockSpec((2, TM_OUT, D), _smp_tile),
            pl.BlockSpec((1, D), const2),
            pl.BlockSpec((D, LANES), const2),
            pl.BlockSpec((LANES, D), const2),
            pl.BlockSpec((D, D), const2),
            pl.BlockSpec((1, D), const2),
        ],
        out_specs=[
            pl.BlockSpec((TM_OUT, D), lambda i: _ctx_tile(i)[1:]),
            pl.BlockSpec((TM_OUT, D), lambda i: _smp_tile(i)[1:]),
        ],
        out_shape=[
            jax.ShapeDtypeStruct((N_CTX_TOK, D), F32),
            jax.ShapeDtypeStruct((N_TOK - N_CTX_TOK, D), F32),
        ],
        compiler_params=pltpu.CompilerParams(
            dimension_semantics=("arbitrary",), vmem_limit_bytes=VMEM_LIMIT),
        name="odd_out",
    )(x, mod3, p_all, o_ctx, o_smp, g_norm, ind, bind, wo_bf, final_g)


def _group_indicators(group):
    ch = jnp.arange(D) // group
    lane = jnp.arange(LANES)
    hit = (ch[:, None] == lane[None, :]).astype(F32)
    return (hit / float(group)).astype(BF16), hit.T.astype(BF16)


def _pad_dirs(w):
    r = w.shape[1]
    out = jnp.zeros((2, LANES, w.shape[2]), F32)
    out = out.at[0, 0:r].set(w[0])
    out = out.at[1, r:2 * r].set(w[1])
    return out


def kernel(x_prompt, x_sample, state_wkv, state_gla, c, c_ctx, norm_g, ada_w, ada_b, final_g, e_w_in, e_w_out, conv_w, wkv_w0, wkv_w1, wkv_w2, wkv_a0, wkv_a1, wkv_a2, wkv_k_k, wkv_k_a, wkv_r_k, wkv_ln_w, wkv_ln_b, o_w_in, o_w_out, gla_gk1, gla_gk2, gla_gk_b, gla_g_norm):
    x_ctx = x_prompt.reshape(N_CTX_TOK, D)
    x_smp = x_sample.reshape(N_TOK - N_CTX_TOK, D)
    cvec = jnp.zeros((8, D), F32).at[0].set(c_ctx).at[1:1 + SMP_B].set(c)
    mod = _modulation(cvec, ada_w, ada_b)
    masks = _chunk_masks()

    mod3 = mod[0].reshape(8, 3, D)
    wlr = jnp.concatenate([wkv_w1[0, 0], wkv_w1[0, 1], wkv_a1[0, 0], wkv_a1[0, 1]], axis=1)
    p_e, lr_e = _inproj(x_ctx, x_smp, 0, mod3, norm_g[0:1], e_w_in[0].astype(BF16), wlr.astype(BF16),
                        n_tanh=2 * 64)
    w2pad = _pad_dirs(wkv_w2[0])
    a2pad = _pad_dirs(wkv_a2[0])
    w0 = wkv_w0[0].reshape(2, 1, D)
    a0 = wkv_a0[0].reshape(2, 1, D)
    k_k = wkv_k_k[0].reshape(1, D)
    k_a = wkv_k_a[0].reshape(1, D)
    r_k = wkv_r_k[0].reshape(1, D)
    o_ctx, new_wkv = _wkv_scan(p_e, lr_e, w2pad, a2pad, w0, a0, k_k, k_a, None,
                               n_seq=CTX_B, seq_len=CTX_T, n_seg=WKV_CTX_SEG, tok_blk0=0)
    o_smp, _ = _wkv_scan(p_e, lr_e, w2pad, a2pad, w0, a0, k_k, k_a, state_wkv[:, 0],
                         n_seq=SMP_B, seq_len=SMP_T, n_seg=1, tok_blk0=N_CTX_TOK // SMP_T)
    ind64, bind64 = _group_indicators(WKV_N)
    x = _even_out(x_ctx, x_smp, mod3, p_e, o_ctx, o_smp, lr_e, a2pad, a0, k_a, r_k, conv_w[0],
                  wkv_ln_w[0].reshape(1, D), wkv_ln_b[0].reshape(1, D), ind64, bind64,
                  e_w_out[0].astype(BF16))

    mod3 = mod[1].reshape(8, 3, D)
    rank = gla_gk1.shape[-1]
    wlr = jnp.zeros((D, LANES), F32).at[:, 0:rank].set(gla_gk1[0, 0]).at[:, rank:2 * rank].set(gla_gk1[0, 1])
    p_o, lr_o = _inproj(x, x, N_CTX_TOK // TM_PROJ, mod3, norm_g[1:2], o_w_in[0].astype(BF16),
                        wlr.astype(BF16), n_tanh=0)
    gk2pad = _pad_dirs(gla_gk2[0])
    gkb = gla_gk_b[0].reshape(2, 1, GLA_H * GLA_DK)
    g_ctx, new_gla = _gla_scan(p_o, lr_o, gk2pad, gkb, None, masks,
                               n_seq=CTX_B, seq_len=CTX_T, n_seg=GLA_CTX_SEG, tok_blk0=0)
    g_smp, _ = _gla_scan(p_o, lr_o, gk2pad, gkb, state_gla[:, 0], masks,
                         n_seq=SMP_B, seq_len=SMP_T, n_seg=1, tok_blk0=N_CTX_TOK // SMP_T)
    ind256, bind256 = _group_indicators(GLA_DV)
    y_ctx, y_smp = _odd_out(x, mod3, p_o, g_ctx, g_smp, jnp.tile(gla_g_norm[0], GLA_H).reshape(1, D),
                            ind256, bind256, o_w_out[0].astype(BF16), final_g.reshape(1, D))
    return (y_ctx.reshape(CTX_B, CTX_T, D), y_smp.reshape(SMP_B, SMP_T, D), new_wkv[:, None], new_gla[:, None])
```

```python
import functools
import math

import jax
import jax.numpy as jnp
from jax import lax
from jax.experimental import pallas as pl
from jax.experimental.pallas import tpu as pltpu

F32 = jnp.float32
BF16 = jnp.bfloat16
HI = lax.Precision.HIGHEST

D = 1024
N_CTX_TOK = 16 * 256
N_TOK = 2 * N_CTX_TOK
CTX_B, CTX_T = 16, 256
SMP_B, SMP_T = 2, 2048
GRID_W = 64
WKV_H, WKV_N = 16, 64
GLA_H, GLA_DK, GLA_DV = 4, 128, 256
GLA_GATE_NORM = 16.0
GN_EPS = 64e-5
NORM_EPS = 1e-6
CHUNK = 64
WKV_GROUP = 4
WKV_HEADS_STEP = 8
WKV_CTX_SEG = 4
GLA_CTX_SEG = 4
GLA_GROUP = 8
LANES = 128
TM_PROJ = 1024
TN_PROJ_MAX = 2048
TM_OUT = 256
VMEM_LIMIT = 56 * 1024 * 1024


def _silu(x):
    return x * jax.nn.sigmoid(x)


def _dot(a, b, precision=None):
    return jnp.dot(a, b, preferred_element_type=F32, precision=precision)


def _dot_nt(a, b, precision=None):
    return lax.dot_general(a, b, (((1,), (1,)), ((), ())), preferred_element_type=F32, precision=precision)


def _dot_tn(a, b, precision=None):
    return lax.dot_general(a, b, (((0,), (0,)), ((), ())), preferred_element_type=F32, precision=precision)


def _bdot(a, b):
    return _dot(a.astype(BF16), b.astype(BF16))


def _bdot_nt(a, b):
    return _dot_nt(a.astype(BF16), b.astype(BF16))


def _bdot_tn(a, b):
    return _dot_tn(a.astype(BF16), b.astype(BF16))


def _split2(x):
    hi = x.astype(BF16)
    lo = (x - hi.astype(F32)).astype(BF16)
    return hi, lo


def _scan_cumsum(x, z):
    n = x.shape[0]
    row = lax.broadcasted_iota(jnp.int32, (n, 1), 0)
    pre = x
    s = 1
    while s < n:
        pre = pre + jnp.where(row >= s, pltpu.roll(pre, s, 0), 0.0)
        s *= 2
    tot = pre[n - 1:n, :]
    return jnp.where(z == 0, pre, tot - pre + x), tot


def _tile_row(i, tile):
    ctx_tiles = N_CTX_TOK // tile
    per_req = SMP_T // tile
    return jnp.where(i < ctx_tiles, 0, 1 + (i - ctx_tiles) // per_req)


def _mod_kernel(c_ref, w_ref, b_ref, o_ref):
    cf = _silu(c_ref[...])
    o_ref[0] = _dot(cf, w_ref[0], HI) + b_ref[0]


def _modulation(cvec, ada_w, ada_b):
    depth = ada_w.shape[0]
    return pl.pallas_call(
        _mod_kernel,
        grid=(depth, 3),
        in_specs=[
            pl.BlockSpec((8, D), lambda l, j: (0, 0)),
            pl.BlockSpec((1, D, D), lambda l, j: (l, 0, j)),
            pl.BlockSpec((1, 1, D), lambda l, j: (l, 0, j)),
        ],
        out_specs=pl.BlockSpec((1, 8, D), lambda l, j: (l, 0, j)),
        out_shape=jax.ShapeDtypeStruct((depth, 8, 3 * D), F32),
        compiler_params=pltpu.CompilerParams(
            dimension_semantics=("arbitrary", "arbitrary"), vmem_limit_bytes=VMEM_LIMIT),
        name="adaln_mod",
    )(cvec, ada_w, ada_b.reshape(depth, 1, 3 * D))


def _inproj_kernel(xa_ref, xb_ref, mod_ref, g_ref, w_ref, wlr_ref, p_ref, lr_ref, h_scr, *, n_tanh):
    i = pl.program_id(0)
    j = pl.program_id(1)

    @pl.when(j == 0)
    def _():
        x = jnp.where(i < N_CTX_TOK // TM_PROJ, xa_ref[...], xb_ref[...])
        y = x * lax.rsqrt(jnp.mean(x * x, axis=-1, keepdims=True) + NORM_EPS) * g_ref[...]
        shift = mod_ref[0, 0:1, :]
        scale = mod_ref[0, 1:2, :]
        h = y * (1.0 + scale) + shift
        hb = h.astype(BF16)
        h_scr[...] = hb
        lr = _dot(hb, wlr_ref[...])
        if n_tanh:
            lane = lax.broadcasted_iota(jnp.int32, lr.shape, 1)
            lr = jnp.where(lane < n_tanh, jnp.tanh(lr), lr)
        lr_ref[...] = lr

    p_ref[...] = _dot(h_scr[...], w_ref[...]).astype(p_ref.dtype)


def _inproj(x_ctx, x_smp, smp_tile0, mod3, g, w_bf, wlr_bf, n_tanh):
    n_out = w_bf.shape[1]
    n_lr = wlr_bf.shape[1]
    n_ctx = N_CTX_TOK // TM_PROJ
    tn = max(t for t in range(256, TN_PROJ_MAX + 1, 256) if n_out % t == 0)
    return pl.pallas_call(
        functools.partial(_inproj_kernel, n_tanh=n_tanh),
        grid=(N_TOK // TM_PROJ, n_out // tn),
        in_specs=[
            pl.BlockSpec((TM_PROJ, D), lambda i, j: (jnp.minimum(i, n_ctx - 1), 0)),
            pl.BlockSpec((TM_PROJ, D), lambda i, j: (smp_tile0 + jnp.maximum(i - n_ctx, 0), 0)),
            pl.BlockSpec((1, 3, D), lambda i, j: (_tile_row(i, TM_PROJ), 0, 0)),
            pl.BlockSpec((1, D), lambda i, j: (0, 0)),
            pl.BlockSpec((D, tn), lambda i, j: (0, j)),
            pl.BlockSpec((D, n_lr), lambda i, j: (0, 0)),
        ],
        out_specs=[
            pl.BlockSpec((TM_PROJ, tn), lambda i, j: (i, j)),
            pl.BlockSpec((TM_PROJ, n_lr), lambda i, j: (i, 0)),
        ],
        out_shape=[
            jax.ShapeDtypeStruct((N_TOK, n_out), BF16),
            jax.ShapeDtypeStruct((N_TOK, n_lr), F32),
        ],
        scratch_shapes=[pltpu.VMEM((TM_PROJ, D), BF16)],
        compiler_params=pltpu.CompilerParams(
            dimension_semantics=("arbitrary", "arbitrary"), vmem_limit_bytes=VMEM_LIMIT),
        name="inproj",
    )(x_ctx, x_smp, mod3, g, w_bf, wlr_bf)


def _chunk_masks():
    idx = jnp.arange(CHUNK)
    lower = (idx[:, None] >= idx[None, :]).astype(F32)
    return jnp.stack([lower, lower.T], axis=0)


def _chunk_rows(c):
    return pl.ds(pl.multiple_of(c * CHUNK, CHUNK), CHUNK)


def _inv_unit_tri(lmats, tick=lambda: None):
    row = lax.broadcasted_iota(jnp.int32, (CHUNK, CHUNK), 0)
    col = lax.broadcasted_iota(jnp.int32, (CHUNK, CHUNK), 1)
    eye = (row == col).astype(F32)

    def same_block(n):
        sh = n.bit_length() - 1
        return lax.shift_right_logical(row, sh) == lax.shift_right_logical(col, sh)

    blk16 = same_block(16)
    l16 = [jnp.where(blk16, m, 0.0) for m in lmats]
    x = [eye - m for m in l16]
    p = [_bdot(m, m) for m in l16]
    for step in range(3):
        x = [xi + _bdot(xi, pi) for xi, pi in zip(x, p)]
        if step < 2:
            p = [_bdot(pi, pi) for pi in p]
        if step == 0:
            tick()
    n = 16
    while n < CHUNK:
        off = same_block(2 * n) & jnp.logical_not(same_block(n))
        xl = [_bdot(xi, jnp.where(off, m, 0.0)) for xi, m in zip(x, lmats)]
        x = [xi - _bdot(yi, xi) for xi, yi in zip(x, xl)]
        if n == 16:
            tick()
        n *= 2
    return x


_KT, _RT, _KH, _BH, _KP, _BP = range(6)


def _wkv_kernel(r_ref, k_ref, v_ref, lrw_ref, lra_ref, w2_ref, a2_ref, w0_ref, a0_ref,
                kk_ref, ka_ref, *rest, seq_len, n_seg, zero_init):
    s0_ref = None if zero_init else rest[0]
    o_ref, sout_ref, rp_scr, w_scr, h_scr, d_scr, rt_a, ops_a = rest[-8:]
    z = pl.program_id(0)
    n_chunks = seq_len // CHUNK
    seg_chunks = n_chunks // n_seg
    n_groups = n_chunks // WKV_GROUP
    pairs = range(WKV_HEADS_STEP // 2)
    row = lax.broadcasted_iota(jnp.int32, (CHUNK, LANES), 0)
    lane = lax.broadcasted_iota(jnp.int32, (CHUNK, LANES), 1)
    col = lane & (CHUNK - 1)
    left = lane < WKV_N
    incl2 = ((z == 0) & (row >= col)) | ((z == 1) & (row <= col))
    strict2 = incl2 & (row != col)
    row_b = lax.broadcasted_iota(jnp.int32, (LANES, LANES), 0) < WKV_N
    lane_b = lax.broadcasted_iota(jnp.int32, (LANES, LANES), 1) < WKV_N
    diag_blocks = row_b == lane_b
    zeros_bf = jnp.zeros((CHUNK, LANES), BF16)
    w2 = w2_ref[0].astype(BF16)
    a2 = a2_ref[0].astype(BF16)
    w0 = w0_ref[0]
    a0 = a0_ref[0]
    k_k = kk_ref[...]
    k_a = ka_ref[...]

    def pair_sl(p):
        return slice(p * LANES, (p + 1) * LANES)

    def stack(a, b):
        return jnp.concatenate([a, b], axis=0)

    def prepare_chunk(gi, j, rt_scr, ops_scr):
        c = gi * WKV_GROUP + j
        rows = _chunk_rows(c)
        w_raw = w0 + _bdot(lrw_ref[rows, :], w2)
        a = jax.nn.sigmoid(a0 + _bdot(lra_ref[rows, :], a2))
        lw = -math.exp(-0.5) * jax.nn.sigmoid(w_raw)
        g, gtot = _scan_cumsum(lw, z)
        r = r_ref[rows, :].astype(F32)
        k = k_ref[rows, :].astype(F32)
        kkr = k * k_k
        sq = kkr * kkr
        nrm = []
        for p in pairs:
            sq_p = sq[:, pair_sl(p)]
            n0 = jnp.sum(jnp.where(left, sq_p, 0.0), axis=-1, keepdims=True)
            n1 = jnp.sum(jnp.where(left, 0.0, sq_p), axis=-1, keepdims=True)
            nrm.append(jnp.where(left, jnp.sqrt(n0), jnp.sqrt(n1)))
        kk = kkr / jnp.maximum(jnp.concatenate(nrm, axis=-1), 1e-12)
        b = kk * a
        kmod = k * (1.0 + (a - 1.0) * k_a)
        e_ng = jnp.exp(-g)
        e_gc = jnp.exp(gtot - g)
        rt = r * jnp.exp(g)
        rt_scr[j] = rt
        ops_scr[j, _KT] = (kk * jnp.exp(g - lw)).astype(BF16)
        ops_scr[j, _RT] = rt.astype(BF16)
        ops_scr[j, _KH] = (kmod * e_ng).astype(BF16)
        ops_scr[j, _BH] = (b * e_ng).astype(BF16)
        ops_scr[j, _KP] = (kmod * e_gc).astype(BF16)
        ops_scr[j, _BP] = (b * e_gc).astype(BF16)
        d_scr[c] = jnp.broadcast_to(jnp.exp(gtot), (8, WKV_HEADS_STEP * WKV_N))

    def chains(gi, rt_scr, ops_scr, fillers):
        pending = iter(fillers)

        def tick():
            next(pending, lambda: None)()

        cs = [gi * WKV_GROUP + j for j in range(WKV_GROUP)]
        rows = [_chunk_rows(c) for c in cs]
        cp = []
        for j in range(WKV_GROUP):
            for p in pairs:
                sl = pair_sl(p)
                op = lambda i: ops_scr[j, i, :, sl]
                cp.append(dict(j=j, p=p, rt=rt_scr[j, :, sl], kt=op(_KT), rtb=op(_RT), v=v_ref[rows[j], sl],
                               bk=stack(op(_BH), op(_KH)), kb=stack(op(_KH), op(_BH)),
                               kbp=stack(op(_KP), -op(_BP)), bp=op(_BP)))
        units = [(q, h) for q in range(len(cp)) for h in range(2)]
        pick = [left, jnp.logical_not(left)]
        lm = [jnp.where(strict2, _dot_nt(jnp.where(pick[h], cp[q]["kt"], zeros_bf), cp[q]["bk"]), 0.0)
              for q, h in units]
        rr = [jnp.where(incl2, _dot_nt(jnp.where(pick[h], cp[q]["rtb"], zeros_bf), cp[q]["kb"]), 0.0)
              for q, h in units]
        tick()
        tinv = _inv_unit_tri([m[:, :CHUNK] for m in lm], tick)
        lm_b = [m.astype(BF16) for m in lm]
        rr_b = [m.astype(BF16) for m in rr]
        mv = [_dot(lm_b[u], stack(zeros_bf, cp[q]["v"])) for u, (q, h) in enumerate(units)]
        mrkv = [_dot(rr_b[u], stack(cp[q]["v"], zeros_bf)) for u, (q, h) in enumerate(units)]
        pm = [_bdot(tinv[u], cp[q]["kt"]) for u, (q, h) in enumerate(units)]
        qm = [_bdot(tinv[u], mv[u]) for u in range(len(units))]
        tick()
        corr_p = [_dot(rr_b[u], stack(zeros_bf, pm[u].astype(BF16))) for u in range(len(units))]
        corr_q = [_dot(rr_b[u], stack(zeros_bf, qm[u].astype(BF16))) for u in range(len(units))]

        def merge(xs, q):
            return jnp.where(left, xs[2 * q], xs[2 * q + 1])

        n_cp = range(len(cp))
        rp = [cp[q]["rt"] - merge(corr_p, q) for q in n_cp]
        op = [merge(mrkv, q) - merge(corr_q, q) for q in n_cp]
        p_pair = [merge(pm, q).astype(BF16) for q in n_cp]
        q_pair = [merge(qm, q).astype(BF16) for q in n_cp]
        wfull = [_dot_tn(p_pair[q], cp[q]["bp"]) for q in n_cp]
        hfull = [_dot_tn(stack(cp[q]["v"], q_pair[q]), cp[q]["kbp"]) for q in n_cp]
        for q in n_cp:
            c, p = cs[cp[q]["j"]], cp[q]["p"]
            o_ref[0, rows[cp[q]["j"]], pair_sl(p)] = op[q]
            rp_scr[c, p] = rp[q].astype(BF16)
            w_hi, w_lo = _split2(jnp.where(diag_blocks, wfull[q], 0.0))
            w_scr[c, p, 0] = w_hi
            w_scr[c, p, 1] = w_lo
            h_scr[c, p] = jnp.where(left, hfull[q][:WKV_N], hfull[q][WKV_N:])
        for rest_filler in pending:
            rest_filler()

    def prepare(gi, rt_scr, ops_scr):
        return [functools.partial(prepare_chunk, gi, j, rt_scr, ops_scr) for j in range(WKV_GROUP)]

    def phase1(gi, carry):
        for prep in prepare(gi, rt_a, ops_a):
            prep()
        chains(gi, rt_a, ops_a, [])
        return carry

    lax.fori_loop(0, n_groups, phase1, 0)

    def phase2(pos, carry):
        chains2 = [(sq, p) for sq in range(n_seg) for p in pairs]
        cidx = [sq * seg_chunks + pos + z * (seg_chunks - 1 - 2 * pos) for sq in range(n_seg)]
        split = [_split2(s) for s in carry]
        sw_hi = [_dot(stack(split[i][0], split[i][1]), w_scr[cidx[sq], p, 0]) for i, (sq, p) in enumerate(chains2)]
        sw_lo = [_dot(split[i][0], w_scr[cidx[sq], p, 1]) for i, (sq, p) in enumerate(chains2)]
        s_bd = [stack(jnp.where(left, hi, zeros_bf), jnp.where(left, zeros_bf, hi)) for hi, lo in split]
        inter = [_dot_nt(rp_scr[cidx[sq], p], s_bd[i]) for i, (sq, p) in enumerate(chains2)]
        new_s = []
        for i, (sq, p) in enumerate(chains2):
            d_tot = d_scr[cidx[sq]][0:1, pair_sl(p)]
            new_s.append(carry[i] * d_tot - (sw_hi[i][:CHUNK] + sw_hi[i][CHUNK:] + sw_lo[i]) + h_scr[cidx[sq], p])
        for sq in range(n_seg):
            rows = _chunk_rows(cidx[sq])
            o_ref[0, rows, :] = o_ref[0, rows, :] + jnp.concatenate(
                [inter[sq * len(pairs) + p] for p in pairs], axis=-1)
        return tuple(new_s)

    if zero_init:
        s_init = tuple(jnp.zeros((WKV_N, LANES), F32) for sq in range(n_seg) for p in pairs)
    else:
        s_init = tuple(jnp.concatenate([s0_ref[sq, 0, 2 * p], s0_ref[sq, 0, 2 * p + 1]], axis=-1)
                       for sq in range(n_seg) for p in pairs)
    s_fin = lax.fori_loop(0, seg_chunks, phase2, s_init)
    for sq in range(n_seg):
        for p in pairs:
            s_pair = s_fin[sq * len(pairs) + p]
            sout_ref[sq, 0, 2 * p] = s_pair[:, :WKV_N]
            sout_ref[sq, 0, 2 * p + 1] = s_pair[:, WKV_N:]


def _wkv_scan(p_all, lr, w2pad, a2pad, w0, a0, k_k, k_a, s0, *, n_seq, seq_len, n_seg, tok_blk0):
    def tok(b):
        return tok_blk0 + b

    rows = n_seg * seq_len
    width = WKV_HEADS_STEP * WKV_N
    n_pairs = WKV_HEADS_STEP // 2
    col = lambda base: (lambda z, b, hq: (tok(b), base // width + hq))
    vec = lambda z, b, hq: (z, 0, hq)
    n_chunks = rows // CHUNK
    state_spec = pl.BlockSpec((n_seg, 1, WKV_HEADS_STEP, WKV_N, WKV_N), lambda z, b, hq: (b, z, hq, 0, 0))
    init = () if s0 is None else (s0,)
    return pl.pallas_call(
        functools.partial(_wkv_kernel, seq_len=rows, n_seg=n_seg, zero_init=s0 is None),
        grid=(2, n_seq // n_seg, WKV_H // WKV_HEADS_STEP),
        in_specs=[
            pl.BlockSpec((rows, width), col(4 * D)),
            pl.BlockSpec((rows, width), col(5 * D)),
            pl.BlockSpec((rows, width), col(6 * D)),
            pl.BlockSpec((rows, LANES), lambda z, b, hq: (tok(b), 0)),
            pl.BlockSpec((rows, LANES), lambda z, b, hq: (tok(b), 1)),
            pl.BlockSpec((1, LANES, width), vec),
            pl.BlockSpec((1, LANES, width), vec),
            pl.BlockSpec((1, 1, width), vec),
            pl.BlockSpec((1, 1, width), vec),
            pl.BlockSpec((1, width), lambda z, b, hq: (0, hq)),
            pl.BlockSpec((1, width), lambda z, b, hq: (0, hq)),
        ] + [state_spec] * len(init),
        out_specs=[
            pl.BlockSpec((1, rows, width), lambda z, b, hq: (z, b, hq)),
            state_spec,
        ],
        out_shape=[
            jax.ShapeDtypeStruct((2, n_seq * seq_len, D), F32),
            jax.ShapeDtypeStruct((n_seq, 2, WKV_H, WKV_N, WKV_N), F32),
        ],
        scratch_shapes=[
            pltpu.VMEM((n_chunks, n_pairs, CHUNK, LANES), BF16),
            pltpu.VMEM((n_chunks, n_pairs, 2, LANES, LANES), BF16),
            pltpu.VMEM((n_chunks, n_pairs, CHUNK, LANES), F32),
            pltpu.VMEM((n_chunks, 8, width), F32),
            pltpu.VMEM((WKV_GROUP, CHUNK, width), F32),
            pltpu.VMEM((WKV_GROUP, 6, CHUNK, width), BF16),
        ],
        compiler_params=pltpu.CompilerParams(
            dimension_semantics=("arbitrary", "arbitrary", "arbitrary"), vmem_limit_bytes=VMEM_LIMIT),
        name="wkv_scan_%d" % seq_len,
    )(p_all, p_all, p_all, lr, lr, w2pad, a2pad, w0, a0, k_k, k_a, *init)


def _dot2(x, w_bf):
    hi, lo = _split2(x)
    return _dot(hi, w_bf) + _dot(lo, w_bf)


def _group_stat(x, ind_ref, bind_ref):
    return _dot2(_dot2(x, ind_ref[...]), bind_ref[...])


N_CTX_TILES = N_CTX_TOK // TM_OUT


def _sum_dirs(i, oc_ref, os_ref):
    is_ctx = i < N_CTX_TILES
    return jnp.where(is_ctx, oc_ref[0], os_ref[0]) + jnp.where(is_ctx, oc_ref[1], os_ref[1])


def _ctx_tile(i):
    return (0, jnp.minimum(i, N_CTX_TILES - 1), 0)


def _smp_tile(i):
    return (0, jnp.maximum(i - N_CTX_TILES, 0), 0)


def _even_out_kernel(xc_ref, xs_ref, mod_ref, pc_ref, pw_ref, oc_ref, os_ref, lra_ref, a2_ref, a0_ref, ka_ref,
                     rk_ref, cw_ref, lnw_ref, lnb_ref, ind_ref, bind_ref, wo_ref, out_ref):
    i = pl.program_id(0)
    u = pc_ref[:, 0:D].astype(F32)
    gb = pc_ref[:, D:2 * D].astype(F32)
    gc = pc_ref[:, 2 * D:3 * D].astype(F32)
    zc = pc_ref[:, 3 * D:4 * D].astype(F32)
    xg = gc * u
    row_len = jnp.where(i < N_CTX_TILES, CTX_T, GRID_W)
    pos = lax.broadcasted_iota(jnp.int32, (TM_OUT, 1), 0) & (row_len - 1)
    prev = jnp.where(pos == 0, 0.0, pltpu.roll(xg, 1, 0))
    nxt = jnp.where(pos == row_len - 1, 0.0, pltpu.roll(xg, TM_OUT - 1, 0))
    conv = cw_ref[0:1, :] * prev + cw_ref[1:2, :] * xg + cw_ref[2:3, :] * nxt
    o_conv = _silu(zc) * gb * conv
    r = pw_ref[:, 0:D].astype(F32)
    k = pw_ref[:, D:2 * D].astype(F32)
    v = pw_ref[:, 2 * D:3 * D].astype(F32)
    zw = pw_ref[:, 3 * D:4 * D].astype(F32)
    o = _sum_dirs(i, oc_ref, os_ref)
    mu = _group_stat(o, ind_ref, bind_ref)
    dlt = o - mu
    var = _group_stat(dlt * dlt, ind_ref, bind_ref)
    gn = dlt * lax.rsqrt(var + GN_EPS) * lnw_ref[...] + lnb_ref[...]
    lra = lra_ref[...]
    ic0 = jax.nn.sigmoid(a0_ref[0] + _bdot(lra, a2_ref[0]))
    ic1 = jax.nn.sigmoid(a0_ref[1] + _bdot(lra, a2_ref[1]))
    ksum = k * (2.0 + (ic0 + ic1 - 2.0) * ka_ref[...])
    bonus = _group_stat(r * ksum * rk_ref[...], ind_ref, bind_ref) * float(WKV_N)
    o_wkv = (gn + bonus * v) * _silu(zw)
    y = _dot(o_conv.astype(BF16), wo_ref[0:D, :]) + _dot(o_wkv.astype(BF16), wo_ref[D:2 * D, :])
    gate = mod_ref[0, 2:3, :]
    out_ref[...] = jnp.where(i < N_CTX_TILES, xc_ref[...], xs_ref[...]) + gate * y


def _even_out(x_ctx, x_smp, mod3, p_all, o_ctx, o_smp, lr, a2pad, a0, k_a, r_k, conv_w, ln_w, ln_b, ind, bind,
              wo_bf):
    row = lambda i: (i, 0)
    const2 = lambda i: (0, 0)
    const3 = lambda i: (0, 0, 0)
    return pl.pallas_call(
        _even_out_kernel,
        grid=(N_TOK // TM_OUT,),
        in_specs=[
            pl.BlockSpec((TM_OUT, D), lambda i: _ctx_tile(i)[1:]),
            pl.BlockSpec((TM_OUT, D), lambda i: _smp_tile(i)[1:]),
            pl.BlockSpec((1, 3, D), lambda i: (_tile_row(i, TM_OUT), 0, 0)),
            pl.BlockSpec((TM_OUT, 4 * D), lambda i: (i, 0)),
            pl.BlockSpec((TM_OUT, 4 * D), lambda i: (i, 1)),
            pl.BlockSpec((2, TM_OUT, D), _ctx_tile),
            pl.BlockSpec((2, TM_OUT, D), _smp_tile),
            pl.BlockSpec((TM_OUT, LANES), lambda i: (i, 1)),
            pl.BlockSpec((2, LANES, D), const3),
            pl.BlockSpec((2, 1, D), const3),
            pl.BlockSpec((1, D), const2),
            pl.BlockSpec((1, D), const2),
            pl.BlockSpec((3, D), const2),
            pl.BlockSpec((1, D), const2),
            pl.BlockSpec((1, D), const2),
            pl.BlockSpec((D, LANES), const2),
            pl.BlockSpec((LANES, D), const2),
            pl.BlockSpec((2 * D, D), const2),
        ],
        out_specs=pl.BlockSpec((TM_OUT, D), row),
        out_shape=jax.ShapeDtypeStruct((N_TOK, D), F32),
        compiler_params=pltpu.CompilerParams(
            dimension_semantics=("arbitrary",), vmem_limit_bytes=VMEM_LIMIT),
        name="even_out",
    )(x_ctx, x_smp, mod3, p_all, p_all, o_ctx, o_smp, lr, a2pad, a0, k_a, r_k, conv_w, ln_w, ln_b, ind, bind, wo_bf)


def _gla_kernel(q_ref, k_ref, v_ref, lrg_ref, gk2_ref, gkb_ref, m_ref, *rest, seq_len, n_seg, zero_init):
    s0_ref = None if zero_init else rest[0]
    o_ref, sout_ref, qe_scr, st_scr, e_scr = rest[-5:]
    z = pl.program_id(0)
    n_chunks = seq_len // CHUNK
    seg_chunks = n_chunks // n_seg
    group = min(GLA_GROUP, n_chunks)
    n_groups = n_chunks // group
    incl_b = m_ref[0] > 0.5
    gk2 = gk2_ref[0].astype(BF16)
    gkb = gkb_ref[0]
    scale = GLA_DK ** -0.5

    def phase1(it, carry):
        cs = [it * group + j for j in range(group)]
        rows = [_chunk_rows(c) for c in cs]
        logit = [_bdot(lrg_ref[rw, :], gk2) + gkb for rw in rows]
        g = [jax.nn.log_sigmoid(x) / GLA_GATE_NORM for x in logit]
        bc, b_last = zip(*[_scan_cumsum(x, z) for x in g])
        k = [k_ref[rw, :].astype(F32) for rw in rows]
        qe = [q_ref[rw, :].astype(F32) * scale * jnp.exp(x) for rw, x in zip(rows, bc)]
        ke = [kc * jnp.exp(-x) for kc, x in zip(k, bc)]
        kd = [kc * jnp.exp(bl - x) for kc, x, bl in zip(k, bc, b_last)]
        v = [v_ref[rw, :] for rw in rows]
        att = [jnp.where(incl_b, _bdot_nt(a, b), 0.0) for a, b in zip(qe, ke)]
        o_in = [_bdot(a, b) for a, b in zip(att, v)]
        vtk = [_bdot_tn(a, b) for a, b in zip(v, kd)]
        for j in range(group):
            o_ref[0, rows[j], :] = o_in[j]
            qe_scr[cs[j]] = qe[j]
            st_scr[cs[j]] = vtk[j]
            e_scr[cs[j]] = jnp.broadcast_to(jnp.exp(b_last[j]), (8, GLA_DK))
        return carry

    lax.fori_loop(0, n_groups, phase1, 0)

    def phase2(pos, sts):
        cidx = [sq * seg_chunks + pos + z * (seg_chunks - 1 - 2 * pos) for sq in range(n_seg)]
        new = [st * e_scr[c][0:1, :] + st_scr[c] for st, c in zip(sts, cidx)]
        for st, c in zip(sts, cidx):
            st_scr[c] = st
        return tuple(new)

    if zero_init:
        st_init = tuple(jnp.zeros((GLA_DV, GLA_DK), F32) for sq in range(n_seg))
    else:
        st_init = tuple(s0_ref[sq, 0, 0].T for sq in range(n_seg))
    st_fin = lax.fori_loop(0, seg_chunks, phase2, st_init)
    for sq in range(n_seg):
        sout_ref[sq, 0, 0] = st_fin[sq].T

    def phase3(it, carry):
        cs = [it * group + j for j in range(group)]
        inter = [_bdot_nt(qe_scr[c], st_scr[c]) for c in cs]
        for j in range(group):
            rows = _chunk_rows(cs[j])
            o_ref[0, rows, :] = o_ref[0, rows, :] + inter[j]
        return carry

    lax.fori_loop(0, n_groups, phase3, 0)


def _gla_scan(p_all, lrg, gk2pad, gkb, s0, masks, *, n_seq, seq_len, n_seg, tok_blk0):
    def tok(b):
        return tok_blk0 + b

    rows = n_seg * seq_len
    n_chunks = rows // CHUNK
    state_spec = pl.BlockSpec((n_seg, 1, 1, GLA_DK, GLA_DV), lambda z, b, h: (b, z, h, 0, 0))
    init = () if s0 is None else (s0,)
    return pl.pallas_call(
        functools.partial(_gla_kernel, seq_len=rows, n_seg=n_seg, zero_init=s0 is None),
        grid=(2, n_seq // n_seg, GLA_H),
        in_specs=[
            pl.BlockSpec((rows, GLA_DK), lambda z, b, h: (tok(b), h)),
            pl.BlockSpec((rows, GLA_DK), lambda z, b, h: (tok(b), GLA_H + h)),
            pl.BlockSpec((rows, GLA_DV), lambda z, b, h: (tok(b), GLA_H + h)),
            pl.BlockSpec((rows, LANES), lambda z, b, h: (tok(b), 0)),
            pl.BlockSpec((1, LANES, GLA_DK), lambda z, b, h: (z, 0, h)),
            pl.BlockSpec((1, 1, GLA_DK), lambda z, b, h: (z, 0, h)),
            pl.BlockSpec((1, CHUNK, CHUNK), lambda z, b, h: (z, 0, 0)),
        ] + [state_spec] * len(init),
        out_specs=[
            pl.BlockSpec((1, rows, GLA_DV), lambda z, b, h: (z, b, h)),
            state_spec,
        ],
        out_shape=[
            jax.ShapeDtypeStruct((2, n_seq * seq_len, D), F32),
            jax.ShapeDtypeStruct((n_seq, 2, GLA_H, GLA_DK, GLA_DV), F32),
        ],
        scratch_shapes=[
            pltpu.VMEM((n_chunks, CHUNK, GLA_DK), F32),
            pltpu.VMEM((n_chunks, GLA_DV, GLA_DK), F32),
            pltpu.VMEM((n_chunks, 8, GLA_DK), F32),
        ],
        compiler_params=pltpu.CompilerParams(
            dimension_semantics=("arbitrary", "arbitrary", "arbitrary"), vmem_limit_bytes=VMEM_LIMIT),
        name="gla_scan_%d" % seq_len,
    )(p_all, p_all, p_all, lrg, gk2pad, gkb, masks, *init)


def _odd_out_kernel(x_ref, mod_ref, zg_ref, oc_ref, os_ref, gn_ref, ind_ref, bind_ref, wo_ref, fg_ref,
                    yc_ref, ys_ref):
    i = pl.program_id(0)
    o = _sum_dirs(i, oc_ref, os_ref)
    ms = _group_stat(o * o, ind_ref, bind_ref)
    on = o * lax.rsqrt(ms + NORM_EPS) * gn_ref[...]
    og = on * _silu(zg_ref[...].astype(F32))
    y = _dot(og.astype(BF16), wo_ref[...])
    gate = mod_ref[0, 2:3, :]
    x = x_ref[...] + gate * y
    out = x * lax.rsqrt(jnp.mean(x * x, axis=-1, keepdims=True) + NORM_EPS) * fg_ref[...]

    @pl.when(i < N_CTX_TILES)
    def _():
        yc_ref[...] = out

    @pl.when(i >= N_CTX_TILES)
    def _():
        ys_ref[...] = out


def _odd_out(x, mod3, p_all, o_ctx, o_smp, g_norm, ind, bind, wo_bf, final_g):
    row = lambda i: (i, 0)
    const2 = lambda i: (0, 0)
    return pl.pallas_call(
        _odd_out_kernel,
        grid=(N_TOK // TM_OUT,),
        in_specs=[
            pl.BlockSpec((TM_OUT, D), row),
            pl.BlockSpec((1, 3, D), lambda i: (_tile_row(i, TM_OUT), 0, 0)),
            pl.BlockSpec((TM_OUT, D), lambda i: (i, 2)),
            pl.BlockSpec((2, TM_OUT, D), _ctx_tile),
            pl.BlockSpec((2, TM_OUT, D), _smp_tile),
            pl.BlockSpec((1, D), const2),
            pl.BlockSpec((D, LANES), const2),
            pl.BlockSpec((LANES, D), const2),
            pl.BlockSpec((D, D), const2),
            pl.BlockSpec((1, D), const2),
        ],
        out_specs=[
            pl.BlockSpec((TM_OUT, D), lambda i: _ctx_tile(i)[1:]),
            pl.BlockSpec((TM_OUT, D), lambda i: _smp_tile(i)[1:]),
        ],
        out_shape=[
            jax.ShapeDtypeStruct((N_CTX_TOK, D), F32),
            jax.ShapeDtypeStruct((N_TOK - N_CTX_TOK, D), F32),
        ],
        compiler_params=pltpu.CompilerParams(
            dimension_semantics=("arbitrary",), vmem_limit_bytes=VMEM_LIMIT),
        name="odd_out",
    )(x, mod3, p_all, o_ctx, o_smp, g_norm, ind, bind, wo_bf, final_g)


def _group_indicators(group):
    ch = jnp.arange(D) // group
    lane = jnp.arange(LANES)
    hit = (ch[:, None] == lane[None, :]).astype(F32)
    return (hit / float(group)).astype(BF16), hit.T.astype(BF16)


def _pad_dirs(w):
    r = w.shape[1]
    out = jnp.zeros((2, LANES, w.shape[2]), F32)
    out = out.at[0, 0:r].set(w[0])
    out = out.at[1, r:2 * r].set(w[1])
    return out


def kernel(x_prompt, x_sample, state_wkv, state_gla, c, c_ctx, norm_g, ada_w, ada_b, final_g, e_w_in, e_w_out, conv_w, wkv_w0, wkv_w1, wkv_w2, wkv_a0, wkv_a1, wkv_a2, wkv_k_k, wkv_k_a, wkv_r_k, wkv_ln_w, wkv_ln_b, o_w_in, o_w_out, gla_gk1, gla_gk2, gla_gk_b, gla_g_norm):
    x_ctx = x_prompt.reshape(N_CTX_TOK, D)
    x_smp = x_sample.reshape(N_TOK - N_CTX_TOK, D)
    cvec = jnp.zeros((8, D), F32).at[0].set(c_ctx).at[1:1 + SMP_B].set(c)
    mod = _modulation(cvec, ada_w, ada_b)
    masks = _chunk_masks()

    mod3 = mod[0].reshape(8, 3, D)
    wlr = jnp.concatenate([wkv_w1[0, 0], wkv_w1[0, 1], wkv_a1[0, 0], wkv_a1[0, 1]], axis=1)
    p_e, lr_e = _inproj(x_ctx, x_smp, 0, mod3, norm_g[0:1], e_w_in[0].astype(BF16), wlr.astype(BF16),
                        n_tanh=2 * 64)
    w2pad = _pad_dirs(wkv_w2[0])
    a2pad = _pad_dirs(wkv_a2[0])
    w0 = wkv_w0[0].reshape(2, 1, D)
    a0 = wkv_a0[0].reshape(2, 1, D)
    k_k = wkv_k_k[0].reshape(1, D)
    k_a = wkv_k_a[0].reshape(1, D)
    r_k = wkv_r_k[0].reshape(1, D)
    o_ctx, new_wkv = _wkv_scan(p_e, lr_e, w2pad, a2pad, w0, a0, k_k, k_a, None,
                               n_seq=CTX_B, seq_len=CTX_T, n_seg=WKV_CTX_SEG, tok_blk0=0)
    o_smp, _ = _wkv_scan(p_e, lr_e, w2pad, a2pad, w0, a0, k_k, k_a, state_wkv[:, 0],
                         n_seq=SMP_B, seq_len=SMP_T, n_seg=1, tok_blk0=N_CTX_TOK // SMP_T)
    ind64, bind64 = _group_indicators(WKV_N)
    x = _even_out(x_ctx, x_smp, mod3, p_e, o_ctx, o_smp, lr_e, a2pad, a0, k_a, r_k, conv_w[0],
                  wkv_ln_w[0].reshape(1, D), wkv_ln_b[0].reshape(1, D), ind64, bind64,
                  e_w_out[0].astype(BF16))

    mod3 = mod[1].reshape(8, 3, D)
    rank = gla_gk1.shape[-1]
    wlr = jnp.zeros((D, LANES), F32).at[:, 0:rank].set(gla_gk1[0, 0]).at[:, rank:2 * rank].set(gla_gk1[0, 1])
    p_o, lr_o = _inproj(x, x, N_CTX_TOK // TM_PROJ, mod3, norm_g[1:2], o_w_in[0].astype(BF16),
                        wlr.astype(BF16), n_tanh=0)
    gk2pad = _pad_dirs(gla_gk2[0])
    gkb = gla_gk_b[0].reshape(2, 1, GLA_H * GLA_DK)
    g_ctx, new_gla = _gla_scan(p_o, lr_o, gk2pad, gkb, None, masks,
                               n_seq=CTX_B, seq_len=CTX_T, n_seg=GLA_CTX_SEG, tok_blk0=0)
    g_smp, _ = _gla_scan(p_o, lr_o, gk2pad, gkb, state_gla[:, 0], masks,
                         n_seq=SMP_B, seq_len=SMP_T, n_seg=1, tok_blk0=N_CTX_TOK // SMP_T)
    ind256, bind256 = _group_indicators(GLA_DV)
    y_ctx, y_smp = _odd_out(x, mod3, p_o, g_ctx, g_smp, jnp.tile(gla_g_norm[0], GLA_H).reshape(1, D),
                            ind256, bind256, o_w_out[0].astype(BF16), final_g.reshape(1, D))
    return (y_ctx.reshape(CTX_B, CTX_T, D), y_smp.reshape(SMP_B, SMP_T, D), new_wkv[:, None], new_gla[:, None])
```

```python
import functools
import math

import jax
import jax.numpy as jnp
from jax import lax
from jax.experimental import pallas as pl
from jax.experimental.pallas import tpu as pltpu

F32 = jnp.float32
BF16 = jnp.bfloat16
HI = lax.Precision.HIGHEST

D = 1024
N_CTX_TOK = 16 * 256
N_TOK = 2 * N_CTX_TOK
CTX_B, CTX_T = 16, 256
SMP_B, SMP_T = 2, 2048
GRID_W = 64
WKV_H, WKV_N = 16, 64
GLA_H, GLA_DK, GLA_DV = 4, 128, 256
GLA_GATE_NORM = 16.0
GN_EPS = 64e-5
NORM_EPS = 1e-6
CHUNK = 64
WKV_GROUP = 4
WKV_HEADS_STEP = 8
WKV_CTX_SEG = 4
GLA_CTX_SEG = 4
GLA_GROUP = 8
LANES = 128
TM_PROJ = 1024
TN_PROJ_MAX = 2048
TM_OUT = 256
VMEM_LIMIT = 56 * 1024 * 1024


def _silu(x):
    return x * jax.nn.sigmoid(x)


def _dot(a, b, precision=None):
    return jnp.dot(a, b, preferred_element_type=F32, precision=precision)


def _dot_nt(a, b, precision=None):
    return lax.dot_general(a, b, (((1,), (1,)), ((), ())), preferred_element_type=F32, precision=precision)


def _dot_tn(a, b, precision=None):
    return lax.dot_general(a, b, (((0,), (0,)), ((), ())), preferred_element_type=F32, precision=precision)


def _bdot(a, b):
    return _dot(a.astype(BF16), b.astype(BF16))


def _bdot_nt(a, b):
    return _dot_nt(a.astype(BF16), b.astype(BF16))


def _bdot_tn(a, b):
    return _dot_tn(a.astype(BF16), b.astype(BF16))


def _split2(x):
    hi = x.astype(BF16)
    lo = (x - hi.astype(F32)).astype(BF16)
    return hi, lo


def _scan_cumsum(x, z):
    n = x.shape[0]
    row = lax.broadcasted_iota(jnp.int32, (n, 1), 0)
    pre = x
    s = 1
    while s < n:
        pre = pre + jnp.where(row >= s, pltpu.roll(pre, s, 0), 0.0)
        s *= 2
    tot = pre[n - 1:n, :]
    return jnp.where(z == 0, pre, tot - pre + x), tot


def _tile_row(i, tile):
    ctx_tiles = N_CTX_TOK // tile
    per_req = SMP_T // tile
    return jnp.where(i < ctx_tiles, 0, 1 + (i - ctx_tiles) // per_req)


def _mod_kernel(c_ref, w_ref, b_ref, o_ref):
    cf = _silu(c_ref[...])
    o_ref[0] = _dot(cf, w_ref[0], HI) + b_ref[0]


def _modulation(cvec, ada_w, ada_b):
    depth = ada_w.shape[0]
    return pl.pallas_call(
        _mod_kernel,
        grid=(depth, 3),
        in_specs=[
            pl.BlockSpec((8, D), lambda l, j: (0, 0)),
            pl.BlockSpec((1, D, D), lambda l, j: (l, 0, j)),
            pl.BlockSpec((1, 1, D), lambda l, j: (l, 0, j)),
        ],
        out_specs=pl.BlockSpec((1, 8, D), lambda l, j: (l, 0, j)),
        out_shape=jax.ShapeDtypeStruct((depth, 8, 3 * D), F32),
        compiler_params=pltpu.CompilerParams(
            dimension_semantics=("arbitrary", "arbitrary"), vmem_limit_bytes=VMEM_LIMIT),
        name="adaln_mod",
    )(cvec, ada_w, ada_b.reshape(depth, 1, 3 * D))


def _inproj_kernel(xa_ref, xb_ref, mod_ref, g_ref, w_ref, wlr_ref, p_ref, lr_ref, h_scr, *, n_tanh):
    i = pl.program_id(0)
    j = pl.program_id(1)

    @pl.when(j == 0)
    def _():
        x = jnp.where(i < N_CTX_TOK // TM_PROJ, xa_ref[...], xb_ref[...])
        y = x * lax.rsqrt(jnp.mean(x * x, axis=-1, keepdims=True) + NORM_EPS) * g_ref[...]
        shift = mod_ref[0, 0:1, :]
        scale = mod_ref[0, 1:2, :]
        h = y * (1.0 + scale) + shift
        hb = h.astype(BF16)
        h_scr[...] = hb
        lr = _dot(hb, wlr_ref[...])
        if n_tanh:
            lane = lax.broadcasted_iota(jnp.int32, lr.shape, 1)
            lr = jnp.where(lane < n_tanh, jnp.tanh(lr), lr)
        lr_ref[...] = lr

    p_ref[...] = _dot(h_scr[...], w_ref[...]).astype(p_ref.dtype)


def _inproj(x_ctx, x_smp, smp_tile0, mod3, g, w_bf, wlr_bf, n_tanh):
    n_out = w_bf.shape[1]
    n_lr = wlr_bf.shape[1]
    n_ctx = N_CTX_TOK // TM_PROJ
    tn = max(t for t in range(256, TN_PROJ_MAX + 1, 256) if n_out % t == 0)
    return pl.pallas_call(
        functools.partial(_inproj_kernel, n_tanh=n_tanh),
        grid=(N_TOK // TM_PROJ, n_out // tn),
        in_specs=[
            pl.BlockSpec((TM_PROJ, D), lambda i, j: (jnp.minimum(i, n_ctx - 1), 0)),
            pl.BlockSpec((TM_PROJ, D), lambda i, j: (smp_tile0 + jnp.maximum(i - n_ctx, 0), 0)),
            pl.BlockSpec((1, 3, D), lambda i, j: (_tile_row(i, TM_PROJ), 0, 0)),
            pl.BlockSpec((1, D), lambda i, j: (0, 0)),
            pl.BlockSpec((D, tn), lambda i, j: (0, j)),
            pl.BlockSpec((D, n_lr), lambda i, j: (0, 0)),
        ],
        out_specs=[
            pl.BlockSpec((TM_PROJ, tn), lambda i, j: (i, j)),
            pl.BlockSpec((TM_PROJ, n_lr), lambda i, j: (i, 0)),
        ],
        out_shape=[
            jax.ShapeDtypeStruct((N_TOK, n_out), BF16),
            jax.ShapeDtypeStruct((N_TOK, n_lr), F32),
        ],
        scratch_shapes=[pltpu.VMEM((TM_PROJ, D), BF16)],
        compiler_params=pltpu.CompilerParams(
            dimension_semantics=("arbitrary", "arbitrary"), vmem_limit_bytes=VMEM_LIMIT),
        name="inproj",
    )(x_ctx, x_smp, mod3, g, w_bf, wlr_bf)


def _chunk_masks():
    idx = jnp.arange(CHUNK)
    lower = (idx[:, None] >= idx[None, :]).astype(F32)
    return jnp.stack([lower, lower.T], axis=0)


def _chunk_rows(c):
    return pl.ds(pl.multiple_of(c * CHUNK, CHUNK), CHUNK)


def _inv_unit_tri(lmats, tick=lambda: None):
    row = lax.broadcasted_iota(jnp.int32, (CHUNK, CHUNK), 0)
    col = lax.broadcasted_iota(jnp.int32, (CHUNK, CHUNK), 1)
    eye = (row == col).astype(F32)

    def same_block(n):
        sh = n.bit_length() - 1
        return lax.shift_right_logical(row, sh) == lax.shift_right_logical(col, sh)

    blk16 = same_block(16)
    l16 = [jnp.where(blk16, m, 0.0) for m in lmats]
    x = [eye - m for m in l16]
    p = [_bdot(m, m) for m in l16]
    for step in range(3):
        x = [xi + _bdot(xi, pi) for xi, pi in zip(x, p)]
        if step < 2:
            p = [_bdot(pi, pi) for pi in p]
        if step == 0:
            tick()
    n = 16
    while n < CHUNK:
        off = same_block(2 * n) & jnp.logical_not(same_block(n))
        xl = [_bdot(xi, jnp.where(off, m, 0.0)) for xi, m in zip(x, lmats)]
        x = [xi - _bdot(yi, xi) for xi, yi in zip(x, xl)]
        if n == 16:
            tick()
        n *= 2
    return x


_KT, _RT, _KH, _BH, _KP, _BP = range(6)


def _wkv_kernel(r_ref, k_ref, v_ref, lrw_ref, lra_ref, w2_ref, a2_ref, w0_ref, a0_ref,
                kk_ref, ka_ref, *rest, seq_len, n_seg, zero_init):
    s0_ref = None if zero_init else rest[0]
    o_ref, sout_ref, rp_scr, w_scr, h_scr, d_scr, rt_a, ops_a = rest[-8:]
    z = pl.program_id(0)
    n_chunks = seq_len // CHUNK
    seg_chunks = n_chunks // n_seg
    n_groups = n_chunks // WKV_GROUP
    pairs = range(WKV_HEADS_STEP // 2)
    row = lax.broadcasted_iota(jnp.int32, (CHUNK, LANES), 0)
    lane = lax.broadcasted_iota(jnp.int32, (CHUNK, LANES), 1)
    col = lane & (CHUNK - 1)
    left = lane < WKV_N
    incl2 = ((z == 0) & (row >= col)) | ((z == 1) & (row <= col))
    strict2 = incl2 & (row != col)
    row_b = lax.broadcasted_iota(jnp.int32, (LANES, LANES), 0) < WKV_N
    lane_b = lax.broadcasted_iota(jnp.int32, (LANES, LANES), 1) < WKV_N
    diag_blocks = row_b == lane_b
    zeros_bf = jnp.zeros((CHUNK, LANES), BF16)
    w2 = w2_ref[0].astype(BF16)
    a2 = a2_ref[0].astype(BF16)
    w0 = w0_ref[0]
    a0 = a0_ref[0]
    k_k = kk_ref[...]
    k_a = ka_ref[...]

    def pair_sl(p):
        return slice(p * LANES, (p + 1) * LANES)

    def stack(a, b):
        return jnp.concatenate([a, b], axis=0)

    def prepare_chunk(gi, j, rt_scr, ops_scr):
        c = gi * WKV_GROUP + j
        rows = _chunk_rows(c)
        w_raw = w0 + _bdot(lrw_ref[rows, :], w2)
        a = jax.nn.sigmoid(a0 + _bdot(lra_ref[rows, :], a2))
        lw = -math.exp(-0.5) * jax.nn.sigmoid(w_raw)
        g, gtot = _scan_cumsum(lw, z)
        r = r_ref[rows, :].astype(F32)
        k = k_ref[rows, :].astype(F32)
        kkr = k * k_k
        sq = kkr * kkr
        nrm = []
        for p in pairs:
            sq_p = sq[:, pair_sl(p)]
            n0 = jnp.sum(jnp.where(left, sq_p, 0.0), axis=-1, keepdims=True)
            n1 = jnp.sum(jnp.where(left, 0.0, sq_p), axis=-1, keepdims=True)
            nrm.append(jnp.where(left, jnp.sqrt(n0), jnp.sqrt(n1)))
        kk = kkr / jnp.maximum(jnp.concatenate(nrm, axis=-1), 1e-12)
        b = kk * a
        kmod = k * (1.0 + (a - 1.0) * k_a)
        e_ng = jnp.exp(-g)
        e_gc = jnp.exp(gtot - g)
        rt = r * jnp.exp(g)
        rt_scr[j] = rt
        ops_scr[j, _KT] = (kk * jnp.exp(g - lw)).astype(BF16)
        ops_scr[j, _RT] = rt.astype(BF16)
        ops_scr[j, _KH] = (kmod * e_ng).astype(BF16)
        ops_scr[j, _BH] = (b * e_ng).astype(BF16)
        ops_scr[j, _KP] = (kmod * e_gc).astype(BF16)
        ops_scr[j, _BP] = (b * e_gc).astype(BF16)
        d_scr[c] = jnp.broadcast_to(jnp.exp(gtot), (8, WKV_HEADS_STEP * WKV_N))

    def chains(gi, rt_scr, ops_scr, fillers):
        pending = iter(fillers)

        def tick():
            next(pending, lambda: None)()

        cs = [gi * WKV_GROUP + j for j in range(WKV_GROUP)]
        rows = [_chunk_rows(c) for c in cs]
        cp = []
        for j in range(WKV_GROUP):
            for p in pairs:
                sl = pair_sl(p)
                op = lambda i: ops_scr[j, i, :, sl]
                cp.append(dict(j=j, p=p, rt=rt_scr[j, :, sl], kt=op(_KT), rtb=op(_RT), v=v_ref[rows[j], sl],
                               bk=stack(op(_BH), op(_KH)), kb=stack(op(_KH), op(_BH)),
                               kbp=stack(op(_KP), -op(_BP)), bp=op(_BP)))
        units = [(q, h) for q in range(len(cp)) for h in range(2)]
        pick = [left, jnp.logical_not(left)]
        lm = [jnp.where(strict2, _dot_nt(jnp.where(pick[h], cp[q]["kt"], zeros_bf), cp[q]["bk"]), 0.0)
              for q, h in units]
        rr = [jnp.where(incl2, _dot_nt(jnp.where(pick[h], cp[q]["rtb"], zeros_bf), cp[q]["kb"]), 0.0)
              for q, h in units]
        tick()
        tinv = _inv_unit_tri([m[:, :CHUNK] for m in lm], tick)
        lm_b = [m.astype(BF16) for m in lm]
        rr_b = [m.astype(BF16) for m in rr]
        mv = [_dot(lm_b[u], stack(zeros_bf, cp[q]["v"])) for u, (q, h) in enumerate(units)]
        pm =[_bdot(tinv[u], cp[q]["kt"]) for u, (q, h) in enumerate(units)]
        qm = [_bdot(tinv[u], mv[u]) for u in range(len(units))]
        tick()
        corr_p = [_dot(rr_b[u], stack(zeros_bf, pm[u].astype(BF16))) for u in range(len(units))]
        op_u = [_dot(rr_b[u], stack(cp[q]["v"], -qm[u].astype(BF16))) for u, (q, h) in enumerate(units)]

        def merge(xs, q):
            return jnp.where(left, xs[2 * q], xs[2 * q + 1])

        n_cp = range(len(cp))
        rp = [cp[q]["rt"] - merge(corr_p, q) for q in n_cp]
        op = [merge(op_u, q) for q in n_cp]
        p_pair = [merge(pm, q).astype(BF16) for q in n_cp]
        q_pair = [merge(qm, q).astype(BF16) for q in n_cp]
        wfull = [_dot_tn(p_pair[q], cp[q]["bp"]) for q in n_cp]
        hfull = [_dot_tn(stack(cp[q]["v"], q_pair[q]), cp[q]["kbp"]) for q in n_cp]
        for q in n_cp:
            c, p = cs[cp[q]["j"]], cp[q]["p"]
            o_ref[0, rows[cp[q]["j"]], pair_sl(p)] = op[q]
            rp_scr[c, p] = rp[q].astype(BF16)
            w_hi, w_lo = _split2(jnp.where(diag_blocks, wfull[q], 0.0))
            w_scr[c, p, 0] = w_hi
            w_scr[c, p, 1] = w_lo
            h_scr[c, p] = jnp.where(left, hfull[q][:WKV_N], hfull[q][WKV_N:])
        for rest_filler in pending:
            rest_filler()

    def prepare(gi, rt_scr, ops_scr):
        return [functools.partial(prepare_chunk, gi, j, rt_scr, ops_scr) for j in range(WKV_GROUP)]

    def phase1(gi, carry):
        for prep in prepare(gi, rt_a, ops_a):
            prep()
        chains(gi, rt_a, ops_a, [])
        return carry

    lax.fori_loop(0, n_groups, phase1, 0)

    def phase2(pos, carry):
        chains2 = [(sq, p) for sq in range(n_seg) for p in pairs]
        cidx = [sq * seg_chunks + pos + z * (seg_chunks - 1 - 2 * pos) for sq in range(n_seg)]
        split = [_split2(s) for s in carry]
        sw_hi = [_dot(stack(split[i][0], split[i][1]), w_scr[cidx[sq], p, 0]) for i, (sq, p) in enumerate(chains2)]
        sw_lo = [_dot(split[i][0], w_scr[cidx[sq], p, 1]) for i, (sq, p) in enumerate(chains2)]
        s_bd = [stack(jnp.where(left, hi, zeros_bf), jnp.where(left, zeros_bf, hi)) for hi, lo in split]
        inter = [_dot_nt(rp_scr[cidx[sq], p], s_bd[i]) for i, (sq, p) in enumerate(chains2)]
        new_s = []
        for i, (sq, p) in enumerate(chains2):
            d_tot = d_scr[cidx[sq]][0:1, pair_sl(p)]
            new_s.append(carry[i] * d_tot - (sw_hi[i][:CHUNK] + sw_hi[i][CHUNK:] + sw_lo[i]) + h_scr[cidx[sq], p])
        for sq in range(n_seg):
            rows = _chunk_rows(cidx[sq])
            o_ref[0, rows, :] = o_ref[0, rows, :] + jnp.concatenate(
                [inter[sq * len(pairs) + p] for p in pairs], axis=-1)
        return tuple(new_s)

    if zero_init:
        s_init = tuple(jnp.zeros((WKV_N, LANES), F32) for sq in range(n_seg) for p in pairs)
    else:
        s_init = tuple(jnp.concatenate([s0_ref[sq, 0, 2 * p], s0_ref[sq, 0, 2 * p + 1]], axis=-1)
                       for sq in range(n_seg) for p in pairs)
    s_fin = lax.fori_loop(0, seg_chunks, phase2, s_init)
    for sq in range(n_seg):
        for p in pairs:
            s_pair = s_fin[sq * len(pairs) + p]
            sout_ref[sq, 0, 2 * p] = s_pair[:, :WKV_N]
            sout_ref[sq, 0, 2 * p + 1] = s_pair[:, WKV_N:]


def _wkv_scan(p_all, lr, w2pad, a2pad, w0, a0, k_k, k_a, s0, *, n_seq, seq_len, n_seg, tok_blk0):
    def tok(b):
        return tok_blk0 + b

    rows = n_seg * seq_len
    width = WKV_HEADS_STEP * WKV_N
    n_pairs = WKV_HEADS_STEP // 2
    col = lambda base: (lambda z, b, hq: (tok(b), base // width + hq))
    vec = lambda z, b, hq: (z, 0, hq)
    n_chunks = rows // CHUNK
    state_spec = pl.BlockSpec((n_seg, 1, WKV_HEADS_STEP, WKV_N, WKV_N), lambda z, b, hq: (b, z, hq, 0, 0))
    init = () if s0 is None else (s0,)
    return pl.pallas_call(
        functools.partial(_wkv_kernel, seq_len=rows, n_seg=n_seg, zero_init=s0 is None),
        grid=(2, n_seq // n_seg, WKV_H // WKV_HEADS_STEP),
        in_specs=[
            pl.BlockSpec((rows, width), col(4 * D)),
            pl.BlockSpec((rows, width), col(5 * D)),
            pl.BlockSpec((rows, width), col(6 * D)),
            pl.BlockSpec((rows, LANES), lambda z, b, hq: (tok(b), 0)),
            pl.BlockSpec((rows, LANES), lambda z, b, hq: (tok(b), 1)),
            pl.BlockSpec((1, LANES, width), vec),
            pl.BlockSpec((1, LANES, width), vec),
            pl.BlockSpec((1, 1, width), vec),
            pl.BlockSpec((1, 1, width), vec),
            pl.BlockSpec((1, width), lambda z, b, hq: (0, hq)),
            pl.BlockSpec((1, width), lambda z, b, hq: (0, hq)),
        ] + [state_spec] * len(init),
        out_specs=[
            pl.BlockSpec((1, rows, width), lambda z, b, hq: (z, b, hq)),
            state_spec,
        ],
        out_shape=[
            jax.ShapeDtypeStruct((2, n_seq * seq_len, D), F32),
            jax.ShapeDtypeStruct((n_seq, 2, WKV_H, WKV_N, WKV_N), F32),
        ],
        scratch_shapes=[
            pltpu.VMEM((n_chunks, n_pairs, CHUNK, LANES), BF16),
            pltpu.VMEM((n_chunks, n_pairs, 2, LANES, LANES), BF16),
            pltpu.VMEM((n_chunks, n_pairs, CHUNK, LANES), F32),
            pltpu.VMEM((n_chunks, 8, width), F32),
            pltpu.VMEM((WKV_GROUP, CHUNK, width), F32),
            pltpu.VMEM((WKV_GROUP, 6, CHUNK, width), BF16),
        ],
        compiler_params=pltpu.CompilerParams(
            dimension_semantics=("arbitrary", "arbitrary", "arbitrary"), vmem_limit_bytes=VMEM_LIMIT),
        name="wkv_scan_%d" % seq_len,
    )(p_all, p_all, p_all, lr, lr, w2pad, a2pad, w0, a0, k_k, k_a, *init)


def _dot2(x, w_bf):
    hi, lo = _split2(x)
    return _dot(hi, w_bf) + _dot(lo, w_bf)


def _group_stat(x, ind_ref, bind_ref, split_bcast=False):
    per_group = _dot(x.astype(BF16), ind_ref[...])
    if split_bcast:
        return _dot2(per_group, bind_ref[...])
    return _dot(per_group.astype(BF16), bind_ref[...])


N_CTX_TILES = N_CTX_TOK // TM_OUT


def _sum_dirs(i, oc_ref, os_ref):
    is_ctx = i < N_CTX_TILES
    return jnp.where(is_ctx, oc_ref[0], os_ref[0]) + jnp.where(is_ctx, oc_ref[1], os_ref[1])


def _ctx_tile(i):
    return (0, jnp.minimum(i, N_CTX_TILES - 1), 0)


def _smp_tile(i):
    return (0, jnp.maximum(i - N_CTX_TILES, 0), 0)


def _even_out_kernel(xc_ref, xs_ref, mod_ref, pc_ref, pw_ref, oc_ref, os_ref, lra_ref, a2_ref, a0_ref, ka_ref,
                     rk_ref, cw_ref, lnw_ref, lnb_ref, ind_ref, bind_ref, wo_ref, out_ref):
    i = pl.program_id(0)
    u = pc_ref[:, 0:D].astype(F32)
    gb = pc_ref[:, D:2 * D].astype(F32)
    gc = pc_ref[:, 2 * D:3 * D].astype(F32)
    zc = pc_ref[:, 3 * D:4 * D].astype(F32)
    xg = gc * u
    row_len = jnp.where(i < N_CTX_TILES, CTX_T, GRID_W)
    pos = lax.broadcasted_iota(jnp.int32, (TM_OUT, 1), 0) & (row_len - 1)
    prev = jnp.where(pos == 0, 0.0, pltpu.roll(xg, 1, 0))
    nxt = jnp.where(pos == row_len - 1, 0.0, pltpu.roll(xg, TM_OUT - 1, 0))
    conv = cw_ref[0:1, :] * prev + cw_ref[1:2, :] * xg + cw_ref[2:3, :] * nxt
    o_conv = _silu(zc) * gb * conv
    r = pw_ref[:, 0:D].astype(F32)
    k = pw_ref[:, D:2 * D].astype(F32)
    v = pw_ref[:, 2 * D:3 * D].astype(F32)
    zw = pw_ref[:, 3 * D:4 * D].astype(F32)
    o = _sum_dirs(i, oc_ref, os_ref)
    mu = _group_stat(o, ind_ref, bind_ref, split_bcast=True)
    dlt = o - mu
    var = _group_stat(dlt * dlt, ind_ref, bind_ref)
    gn = dlt * lax.rsqrt(var + GN_EPS) * lnw_ref[...] + lnb_ref[...]
    lra = lra_ref[...]
    ic0 = jax.nn.sigmoid(a0_ref[0] + _bdot(lra, a2_ref[0]))
    ic1 = jax.nn.sigmoid(a0_ref[1] + _bdot(lra, a2_ref[1]))
    ksum = k * (2.0 + (ic0 + ic1 - 2.0) * ka_ref[...])
    bonus = _group_stat(r * ksum * rk_ref[...], ind_ref, bind_ref) * float(WKV_N)
    o_wkv = (gn + bonus * v) * _silu(zw)
    y = _dot(o_conv.astype(BF16), wo_ref[0:D, :]) + _dot(o_wkv.astype(BF16), wo_ref[D:2 * D, :])
    gate = mod_ref[0, 2:3, :]
    out_ref[...] = jnp.where(i < N_CTX_TILES, xc_ref[...], xs_ref[...]) + gate * y


def _even_out(x_ctx, x_smp, mod3, p_all, o_ctx, o_smp, lr, a2pad, a0, k_a, r_k, conv_w, ln_w, ln_b, ind, bind,
              wo_bf):
    row = lambda i: (i, 0)
    const2 = lambda i: (0, 0)
    const3 = lambda i: (0, 0, 0)
    return pl.pallas_call(
        _even_out_kernel,
        grid=(N_TOK // TM_OUT,),
        in_specs=[
            pl.BlockSpec((TM_OUT, D), lambda i: _ctx_tile(i)[1:]),
            pl.BlockSpec((TM_OUT, D), lambda i: _smp_tile(i)[1:]),
            pl.BlockSpec((1, 3, D), lambda i: (_tile_row(i, TM_OUT), 0, 0)),
            pl.BlockSpec((TM_OUT, 4 * D), lambda i: (i, 0)),
            pl.BlockSpec((TM_OUT, 4 * D), lambda i: (i, 1)),
            pl.BlockSpec((2, TM_OUT, D), _ctx_tile),
            pl.BlockSpec((2, TM_OUT, D), _smp_tile),
            pl.BlockSpec((TM_OUT, LANES), lambda i: (i, 1)),
            pl.BlockSpec((2, LANES, D), const3),
            pl.BlockSpec((2, 1, D), const3),
            pl.BlockSpec((1, D), const2),
            pl.BlockSpec((1, D), const2),
            pl.BlockSpec((3, D), const2),
            pl.BlockSpec((1, D), const2),
            pl.BlockSpec((1, D), const2),
            pl.BlockSpec((D, LANES), const2),
            pl.BlockSpec((LANES, D), const2),
            pl.BlockSpec((2 * D, D), const2),
        ],
        out_specs=pl.BlockSpec((TM_OUT, D), row),
        out_shape=jax.ShapeDtypeStruct((N_TOK, D), F32),
        compiler_params=pltpu.CompilerParams(
            dimension_semantics=("arbitrary",), vmem_limit_bytes=VMEM_LIMIT),
        name="even_out",
    )(x_ctx, x_smp, mod3, p_all, p_all, o_ctx, o_smp, lr, a2pad, a0, k_a, r_k, conv_w, ln_w, ln_b, ind, bind, wo_bf)


def _gla_kernel(q_ref, k_ref, v_ref, lrg_ref, gk2_ref, gkb_ref, m_ref, *rest, seq_len, n_seg, zero_init):
    s0_ref = None if zero_init else rest[0]
    o_ref, sout_ref, qe_scr, st_scr, e_scr = rest[-5:]
    z = pl.program_id(0)
    n_chunks = seq_len // CHUNK
    seg_chunks = n_chunks // n_seg
    group = min(GLA_GROUP, n_chunks)
    n_groups = n_chunks // group
    incl_b = m_ref[0] > 0.5
    gk2 = gk2_ref[0].astype(BF16)
    gkb = gkb_ref[0]
    scale = GLA_DK ** -0.5

    def phase1(it, carry):
        cs = [it * group + j for j in range(group)]
        rows = [_chunk_rows(c) for c in cs]
        logit = [_bdot(lrg_ref[rw, :], gk2) + gkb for rw in rows]
        g = [jax.nn.log_sigmoid(x) / GLA_GATE_NORM for x in logit]
        bc, b_last = zip(*[_scan_cumsum(x, z) for x in g])
        k = [k_ref[rw, :].astype(F32) for rw in rows]
        qe = [q_ref[rw, :].astype(F32) * scale * jnp.exp(x) for rw, x in zip(rows, bc)]
        ke = [kc * jnp.exp(-x) for kc, x in zip(k, bc)]
        kd = [kc * jnp.exp(bl - x) for kc, x, bl in zip(k, bc, b_last)]
        v = [v_ref[rw, :] for rw in rows]
        att = [jnp.where(incl_b, _bdot_nt(a, b), 0.0) for a, b in zip(qe, ke)]
        o_in = [_bdot(a, b) for a, b in zip(att, v)]
        vtk = [_bdot_tn(a, b) for a, b in zip(v, kd)]
        for j in range(group):
            o_ref[0, rows[j], :] = o_in[j]
            qe_scr[cs[j]] = qe[j]
            st_scr[cs[j]] = vtk[j]
            e_scr[cs[j]] = jnp.broadcast_to(jnp.exp(b_last[j]), (8, GLA_DK))
        return carry

    lax.fori_loop(0, n_groups, phase1, 0)

    def phase2(pos, sts):
        cidx = [sq * seg_chunks + pos + z * (seg_chunks - 1 - 2 * pos) for sq in range(n_seg)]
        new = [st * e_scr[c][0:1, :] + st_scr[c] for st, c in zip(sts, cidx)]
        for st, c in zip(sts, cidx):
            st_scr[c] = st
        return tuple(new)

    if zero_init:
        st_init = tuple(jnp.zeros((GLA_DV, GLA_DK), F32) for sq in range(n_seg))
    else:
        st_init = tuple(s0_ref[sq, 0, 0].T for sq in range(n_seg))
    st_fin = lax.fori_loop(0, seg_chunks, phase2, st_init)
    for sq in range(n_seg):
        sout_ref[sq, 0, 0] = st_fin[sq].T

    def phase3(it, carry):
        cs = [it * group + j for j in range(group)]
        inter = [_bdot_nt(qe_scr[c], st_scr[c]) for c in cs]
        for j in range(group):
            rows = _chunk_rows(cs[j])
            o_ref[0, rows, :] = o_ref[0, rows, :] + inter[j]
        return carry

    lax.fori_loop(0, n_groups, phase3, 0)


def _gla_scan(p_all, lrg, gk2pad, gkb, s0, masks, *, n_seq, seq_len, n_seg, tok_blk0):
    def tok(b):
        return tok_blk0 + b

    rows = n_seg * seq_len
    n_chunks = rows // CHUNK
    state_spec = pl.BlockSpec((n_seg, 1, 1, GLA_DK, GLA_DV), lambda z, b, h: (b, z, h, 0, 0))
    init = () if s0 is None else (s0,)
    return pl.pallas_call(
        functools.partial(_gla_kernel, seq_len=rows, n_seg=n_seg, zero_init=s0 is None),
        grid=(2, n_seq // n_seg, GLA_H),
        in_specs=[
            pl.BlockSpec((rows, GLA_DK), lambda z, b, h: (tok(b), h)),
            pl.BlockSpec((rows, GLA_DK), lambda z, b, h: (tok(b), GLA_H + h)),
            pl.BlockSpec((rows, GLA_DV), lambda z, b, h: (tok(b), GLA_H + h)),
            pl.BlockSpec((rows, LANES), lambda z, b, h: (tok(b), 0)),
            pl.BlockSpec((1, LANES, GLA_DK), lambda z, b, h: (z, 0, h)),
            pl.BlockSpec((1, 1, GLA_DK), lambda z, b, h: (z, 0, h)),
            pl.BlockSpec((1, CHUNK, CHUNK), lambda z, b, h: (z, 0, 0)),
        ] + [state_spec] * len(init),
        out_specs=[
            pl.BlockSpec((1, rows, GLA_DV), lambda z, b, h: (z, b, h)),
            state_spec,
        ],
        out_shape=[
            jax.ShapeDtypeStruct((2, n_seq * seq_len, D), F32),
            jax.ShapeDtypeStruct((n_seq, 2, GLA_H, GLA_DK, GLA_DV), F32),
        ],
        scratch_shapes=[
            pltpu.VMEM((n_chunks, CHUNK, GLA_DK), F32),
            pltpu.VMEM((n_chunks, GLA_DV, GLA_DK), F32),
            pltpu.VMEM((n_chunks, 8, GLA_DK), F32),
        ],
        compiler_params=pltpu.CompilerParams(
            dimension_semantics=("arbitrary", "arbitrary", "arbitrary"), vmem_limit_bytes=VMEM_LIMIT),
        name="gla_scan_%d" % seq_len,
    )(p_all, p_all, p_all, lrg, gk2pad, gkb, masks, *init)


def _odd_out_kernel(x_ref, mod_ref, zg_ref, oc_ref, os_ref, gn_ref, ind_ref, bind_ref, wo_ref, fg_ref,
                    yc_ref, ys_ref):
    i = pl.program_id(0)
    o = _sum_dirs(i, oc_ref, os_ref)
    ms = _group_stat(o * o, ind_ref, bind_ref)
    on = o * lax.rsqrt(ms + NORM_EPS) * gn_ref[...]
    og = on * _silu(zg_ref[...].astype(F32))
    y = _dot(og.astype(BF16), wo_ref[...])
    gate = mod_ref[0, 2:3, :]
    x = x_ref[...] + gate * y
    out = x * lax.rsqrt(jnp.mean(x * x, axis=-1, keepdims=True) + NORM_EPS) * fg_ref[...]

    @pl.when(i < N_CTX_TILES)
    def _():
        yc_ref[...] = out

    @pl.when(i >= N_CTX_TILES)
    def _():
        ys_ref[...] = out


def _odd_out(x, mod3, p_all, o_ctx, o_smp, g_norm, ind, bind, wo_bf, final_g):
    row = lambda i: (i, 0)
    const2 = lambda i: (0, 0)
    return pl.pallas_call(
        _odd_out_kernel,
        grid=(N_TOK // TM_OUT,),
        in_specs=[
            pl.BlockSpec((TM_OUT, D), row),
            pl.BlockSpec((1, 3, D), lambda i: (_tile_row(i, TM_OUT), 0, 0)),
            pl.BlockSpec((TM_OUT, D), lambda i: (i, 2)),
            pl.BlockSpec((2, TM_OUT, D), _ctx_tile),
            pl.BlockSpec((2, TM_OUT, D), _smp_tile),
            pl.BlockSpec((1, D), const2),
            pl.BlockSpec((D, LANES), const2),
            pl.BlockSpec((LANES, D), const2),
            pl.BlockSpec((D, D), const2),
            pl.BlockSpec((1, D), const2),
        ],
        out_specs=[
            pl.BlockSpec((TM_OUT, D), lambda i: _ctx_tile(i)[1:]),
            pl.BlockSpec((TM_OUT, D), lambda i: _smp_tile(i)[1:]),
        ],
        out_shape=[
            jax.ShapeDtypeStruct((N_CTX_TOK, D), F32),
            jax.ShapeDtypeStruct((N_TOK - N_CTX_TOK, D), F32),
        ],
        compiler_params=pltpu.CompilerParams(
            dimension_semantics=("arbitrary",), vmem_limit_bytes=VMEM_LIMIT),
        name="odd_out",
    )(x, mod3, p_all, o_ctx, o_smp, g_norm, ind, bind, wo_bf, final_g)


def _group_indicators(group):
    ch = jnp.arange(D) // group
    lane = jnp.arange(LANES)
    hit = (ch[:, None] == lane[None, :]).astype(F32)
    return (hit / float(group)).astype(BF16), hit.T.astype(BF16)


def _pad_dirs(w):
    r = w.shape[1]
    out = jnp.zeros((2, LANES, w.shape[2]), F32)
    out = out.at[0, 0:r].set(w[0])
    out = out.at[1, r:2 * r].set(w[1])
    return out


def kernel(x_prompt, x_sample, state_wkv, state_gla, c, c_ctx, norm_g, ada_w, ada_b, final_g, e_w_in, e_w_out, conv_w, wkv_w0, wkv_w1, wkv_w2, wkv_a0, wkv_a1, wkv_a2, wkv_k_k, wkv_k_a, wkv_r_k, wkv_ln_w, wkv_ln_b, o_w_in, o_w_out, gla_gk1, gla_gk2, gla_gk_b, gla_g_norm):
    x_ctx = x_prompt.reshape(N_CTX_TOK, D)
    x_smp = x_sample.reshape(N_TOK - N_CTX_TOK, D)
    cvec = jnp.zeros((8, D), F32).at[0].set(c_ctx).at[1:1 + SMP_B].set(c)
    mod = _modulation(cvec, ada_w, ada_b)
    masks = _chunk_masks()

    mod3 = mod[0].reshape(8, 3, D)
    wlr = jnp.concatenate([wkv_w1[0, 0], wkv_w1[0, 1], wkv_a1[0, 0], wkv_a1[0, 1]], axis=1)
    p_e, lr_e = _inproj(x_ctx, x_smp, 0, mod3, norm_g[0:1], e_w_in[0].astype(BF16), wlr.astype(BF16),
                        n_tanh=2 * 64)
    w2pad = _pad_dirs(wkv_w2[0])
    a2pad = _pad_dirs(wkv_a2[0])
    w0 = wkv_w0[0].reshape(2, 1, D)
    a0 = wkv_a0[0].reshape(2, 1, D)
    k_k = wkv_k_k[0].reshape(1, D)
    k_a = wkv_k_a[0].reshape(1, D)
    r_k = wkv_r_k[0].reshape(1, D)
    o_ctx, new_wkv = _wkv_scan(p_e, lr_e, w2pad, a2pad, w0, a0, k_k, k_a, None,
                               n_seq=CTX_B, seq_len=CTX_T, n_seg=WKV_CTX_SEG, tok_blk0=0)
    o_smp, _ = _wkv_scan(p_e, lr_e, w2pad, a2pad, w0, a0, k_k, k_a, state_wkv[:, 0],
                         n_seq=SMP_B, seq_len=SMP_T, n_seg=1, tok_blk0=N_CTX_TOK // SMP_T)
    ind64, bind64 = _group_indicators(WKV_N)
    x = _even_out(x_ctx, x_smp, mod3, p_e, o_ctx, o_smp, lr_e, a2pad, a0, k_a, r_k, conv_w[0],
                  wkv_ln_w[0].reshape(1, D), wkv_ln_b[0].reshape(1, D), ind64, bind64,
                  e_w_out[0].astype(BF16))

    mod3 = mod[1].reshape(8, 3, D)
    rank = gla_gk1.shape[-1]
    wlr = jnp.zeros((D, LANES), F32).at[:, 0:rank].set(gla_gk1[0, 0]).at[:, rank:2 * rank].set(gla_gk1[0, 1])
    p_o, lr_o = _inproj(x, x, N_CTX_TOK // TM_PROJ, mod3, norm_g[1:2], o_w_in[0].astype(BF16),
                        wlr.astype(BF16), n_tanh=0)
    gk2pad = _pad_dirs(gla_gk2[0])
    gkb = gla_gk_b[0].reshape(2, 1, GLA_H * GLA_DK)
    g_ctx, new_gla = _gla_scan(p_o, lr_o, gk2pad, gkb, None, masks,
                               n_seq=CTX_B, seq_len=CTX_T, n_seg=GLA_CTX_SEG, tok_blk0=0)
    g_smp, _ = _gla_scan(p_o, lr_o, gk2pad, gkb, state_gla[:, 0], masks,
                         n_seq=SMP_B, seq_len=SMP_T, n_seg=1, tok_blk0=N_CTX_TOK // SMP_T)
    ind256, bind256 = _group_indicators(GLA_DV)
    y_ctx, y_smp = _odd_out(x, mod3, p_o, g_ctx, g_smp, jnp.tile(gla_g_norm[0], GLA_H).reshape(1, D),
                            ind256, bind256, o_w_out[0].astype(BF16), final_g.reshape(1, D))
    return (y_ctx.reshape(CTX_B, CTX_T, D), y_smp.reshape(SMP_B, SMP_T, D), new_wkv[:, None], new_gla[:, None])
```

```python
import functools
import math

import jax
import jax.numpy as jnp
from jax import lax
from jax.experimental import pallas as pl
from jax.experimental.pallas import tpu as pltpu

F32 = jnp.float32
BF16 = jnp.bfloat16
HI = lax.Precision.HIGHEST

D = 1024
N_CTX_TOK = 16 * 256
N_TOK = 2 * N_CTX_TOK
CTX_B, CTX_T = 16, 256
SMP_B, SMP_T = 2, 2048
GRID_W = 64
WKV_H, WKV_N = 16, 64
GLA_H, GLA_DK, GLA_DV = 4, 128, 256
GLA_GATE_NORM = 16.0
GN_EPS = 64e-5
NORM_EPS = 1e-6
CHUNK = 64
WKV_GROUP = 4
WKV_HEADS_STEP = 8
WKV_CTX_SEG = 4
GLA_CTX_SEG = 4
GLA_GROUP = 8
LANES = 128
TM_PROJ = 1024
TN_PROJ_MAX = 2048
TM_OUT = 256
VMEM_LIMIT = 56 * 1024 * 1024


def _silu(x):
    return x * jax.nn.sigmoid(x)


def _dot(a, b, precision=None):
    return jnp.dot(a, b, preferred_element_type=F32, precision=precision)


def _dot_nt(a, b, precision=None):
    return lax.dot_general(a, b, (((1,), (1,)), ((), ())), preferred_element_type=F32, precision=precision)


def _dot_tn(a, b, precision=None):
    return lax.dot_general(a, b, (((0,), (0,)), ((), ())), preferred_element_type=F32, precision=precision)


def _bdot(a, b):
    return _dot(a.astype(BF16), b.astype(BF16))


def _bdot_nt(a, b):
    return _dot_nt(a.astype(BF16), b.astype(BF16))


def _bdot_tn(a, b):
    return _dot_tn(a.astype(BF16), b.astype(BF16))


def _split2(x):
    hi = x.astype(BF16)
    lo = (x - hi.astype(F32)).astype(BF16)
    return hi, lo


def _scan_cumsum(x, z):
    n = x.shape[0]
    row = lax.broadcasted_iota(jnp.int32, (n, 1), 0)
    pre = x
    s = 1
    while s < n:
        pre = pre + jnp.where(row >= s, pltpu.roll(pre, s, 0), 0.0)
        s *= 2
    tot = pre[n - 1:n, :]
    return jnp.where(z == 0, pre, tot - pre + x), tot


def _tile_row(i, tile):
    ctx_tiles = N_CTX_TOK // tile
    per_req = SMP_T // tile
    return jnp.where(i < ctx_tiles, 0, 1 + (i - ctx_tiles) // per_req)


def _mod_kernel(c_ref, w_ref, b_ref, o_ref):
    cf = _silu(c_ref[...])
    o_ref[0] = _dot(cf, w_ref[0], HI) + b_ref[0]


def _modulation(cvec, ada_w, ada_b):
    depth = ada_w.shape[0]
    return pl.pallas_call(
        _mod_kernel,
        grid=(depth, 3),
        in_specs=[
            pl.BlockSpec((8, D), lambda l, j: (0, 0)),
            pl.BlockSpec((1, D, D), lambda l, j: (l, 0, j)),
            pl.BlockSpec((1, 1, D), lambda l, j: (l, 0, j)),
        ],
        out_specs=pl.BlockSpec((1, 8, D), lambda l, j: (l, 0, j)),
        out_shape=jax.ShapeDtypeStruct((depth, 8, 3 * D), F32),
        compiler_params=pltpu.CompilerParams(
            dimension_semantics=("arbitrary", "arbitrary"), vmem_limit_bytes=VMEM_LIMIT),
        name="adaln_mod",
    )(cvec, ada_w, ada_b.reshape(depth, 1, 3 * D))


def _inproj_kernel(xa_ref, xb_ref, mod_ref, g_ref, w_ref, wlr_ref, p_ref, lr_ref, h_scr, *, n_tanh):
    i = pl.program_id(0)
    j = pl.program_id(1)

    @pl.when(j == 0)
    def _():
        x = jnp.where(i < N_CTX_TOK // TM_PROJ, xa_ref[...], xb_ref[...])
        y = x * lax.rsqrt(jnp.mean(x * x, axis=-1, keepdims=True) + NORM_EPS) * g_ref[...]
        shift = mod_ref[0, 0:1, :]
        scale = mod_ref[0, 1:2, :]
        h = y * (1.0 + scale) + shift
        hb = h.astype(BF16)
        h_scr[...] = hb
        lr = _dot(hb, wlr_ref[...])
        if n_tanh:
            lane = lax.broadcasted_iota(jnp.int32, lr.shape, 1)
            lr = jnp.where(lane < n_tanh, jnp.tanh(lr), lr)
        lr_ref[...] = lr

    p_ref[...] = _dot(h_scr[...], w_ref[...]).astype(p_ref.dtype)


def _inproj(x_ctx, x_smp, smp_tile0, mod3, g, w_bf, wlr_bf, n_tanh):
    n_out = w_bf.shape[1]
    n_lr = wlr_bf.shape[1]
    n_ctx = N_CTX_TOK // TM_PROJ
    tn = max(t for t in range(256, TN_PROJ_MAX + 1, 256) if n_out % t == 0)
    return pl.pallas_call(
        functools.partial(_inproj_kernel, n_tanh=n_tanh),
        grid=(N_TOK // TM_PROJ, n_out // tn),
        in_specs=[
            pl.BlockSpec((TM_PROJ, D), lambda i, j: (jnp.minimum(i, n_ctx - 1), 0)),
            pl.BlockSpec((TM_PROJ, D), lambda i, j: (smp_tile0 + jnp.maximum(i - n_ctx, 0), 0)),
            pl.BlockSpec((1, 3, D), lambda i, j: (_tile_row(i, TM_PROJ), 0, 0)),
            pl.BlockSpec((1, D), lambda i, j: (0, 0)),
            pl.BlockSpec((D, tn), lambda i, j: (0, j)),
            pl.BlockSpec((D, n_lr), lambda i, j: (0, 0)),
        ],
        out_specs=[
            pl.BlockSpec((TM_PROJ, tn), lambda i, j: (i, j)),
            pl.BlockSpec((TM_PROJ, n_lr), lambda i, j: (i, 0)),
        ],
        out_shape=[
            jax.ShapeDtypeStruct((N_TOK, n_out), BF16),
            jax.ShapeDtypeStruct((N_TOK, n_lr), F32),
        ],
        scratch_shapes=[pltpu.VMEM((TM_PROJ, D), BF16)],
        compiler_params=pltpu.CompilerParams(
            dimension_semantics=("arbitrary", "arbitrary"), vmem_limit_bytes=VMEM_LIMIT),
        name="inproj",
    )(x_ctx, x_smp, mod3, g, w_bf, wlr_bf)


def _chunk_masks():
    idx = jnp.arange(CHUNK)
    lower = (idx[:, None] >= idx[None, :]).astype(F32)
    return jnp.stack([lower, lower.T], axis=0)


def _chunk_rows(c):
    return pl.ds(pl.multiple_of(c * CHUNK, CHUNK), CHUNK)


def _inv_unit_tri(lmats):
    row = lax.broadcasted_iota(jnp.int32, (CHUNK, CHUNK), 0)
    col = lax.broadcasted_iota(jnp.int32, (CHUNK, CHUNK), 1)
    eye = (row == col).astype(F32)

    def same_block(n):
        sh = n.bit_length() - 1
        return lax.shift_right_logical(row, sh) == lax.shift_right_logical(col, sh)

    blk16 = same_block(16)
    l16 = [jnp.where(blk16, m, 0.0) for m in lmats]
    x = [eye - m for m in l16]
    p = [_bdot(m, m) for m in l16]
    for step in range(3):
        x = [xi + _bdot(xi, pi) for xi, pi in zip(x, p)]
        if step < 2:
            p = [_bdot(pi, pi) for pi in p]
    n = 16
    while n < CHUNK:
        off = same_block(2 * n) & jnp.logical_not(same_block(n))
        xl = [_bdot(xi, jnp.where(off, m, 0.0)) for xi, m in zip(x, lmats)]
        x = [xi - _bdot(yi, xi) for xi, yi in zip(x, xl)]
        n *= 2
    return x


_KT, _RT, _KH, _BH, _KP, _BP = range(6)


def _wkv_kernel(r_ref, k_ref, v_ref, lrw_ref, lra_ref, w2_ref, a2_ref, w0_ref, a0_ref,
                kk_ref, ka_ref, *rest, seq_len, n_seg, zero_init):
    s0_ref = None if zero_init else rest[0]
    o_ref, sout_ref, rp_scr, w_scr, h_scr, d_scr, rt_a, ops_a = rest[-8:]
    z = pl.program_id(0)
    n_chunks = seq_len // CHUNK
    seg_chunks = n_chunks // n_seg
    n_groups = n_chunks // WKV_GROUP
    pairs = range(WKV_HEADS_STEP // 2)
    row = lax.broadcasted_iota(jnp.int32, (CHUNK, LANES), 0)
    lane = lax.broadcasted_iota(jnp.int32, (CHUNK, LANES), 1)
    col = lane & (CHUNK - 1)
    left = lane < WKV_N
    incl2 = ((z == 0) & (row >= col)) | ((z == 1) & (row <= col))
    strict2 = incl2 & (row != col)
    row_b = lax.broadcasted_iota(jnp.int32, (LANES, LANES), 0) < WKV_N
    lane_b = lax.broadcasted_iota(jnp.int32, (LANES, LANES), 1) < WKV_N
    diag_blocks = row_b == lane_b
    zeros_bf = jnp.zeros((CHUNK, LANES), BF16)
    w2 = w2_ref[0].astype(BF16)
    a2 = a2_ref[0].astype(BF16)
    w0 = w0_ref[0]
    a0 = a0_ref[0]
    k_k = kk_ref[...]
    k_a = ka_ref[...]

    def pair_sl(p):
        return slice(p * LANES, (p + 1) * LANES)

    def stack(a, b):
        return jnp.concatenate([a, b], axis=0)

    def prepare_chunk(gi, j, rt_scr, ops_scr):
        c = gi * WKV_GROUP + j
        rows = _chunk_rows(c)
        w_raw = w0 + _bdot(lrw_ref[rows, :], w2)
        a = jax.nn.sigmoid(a0 + _bdot(lra_ref[rows, :], a2))
        lw = -math.exp(-0.5) * jax.nn.sigmoid(w_raw)
        g, gtot = _scan_cumsum(lw, z)
        r = r_ref[rows, :].astype(F32)
        k = k_ref[rows, :].astype(F32)
        kkr = k * k_k
        sq = kkr * kkr
        nrm = []
        for p in pairs:
            sq_p = sq[:, pair_sl(p)]
            n0 = jnp.sum(jnp.where(left, sq_p, 0.0), axis=-1, keepdims=True)
            n1 = jnp.sum(jnp.where(left, 0.0, sq_p), axis=-1, keepdims=True)
            nrm.append(jnp.where(left, jnp.sqrt(n0), jnp.sqrt(n1)))
        kk = kkr / jnp.maximum(jnp.concatenate(nrm, axis=-1), 1e-12)
        b = kk * a
        kmod = k * (1.0 + (a - 1.0) * k_a)
        e_ng = jnp.exp(-g)
        e_gc = jnp.exp(gtot - g)
        rt = r * jnp.exp(g)
        rt_scr[j] = rt
        ops_scr[j, _KT] = (kk * jnp.exp(g - lw)).astype(BF16)
        ops_scr[j, _RT] = rt.astype(BF16)
        ops_scr[j, _KH] = (kmod * e_ng).astype(BF16)
        ops_scr[j, _BH] = (b * e_ng).astype(BF16)
        ops_scr[j, _KP] = (kmod * e_gc).astype(BF16)
        ops_scr[j, _BP] = (b * e_gc).astype(BF16)
        d_scr[c] = jnp.broadcast_to(jnp.exp(gtot), (8, WKV_HEADS_STEP * WKV_N))

    def chains(gi, rt_scr, ops_scr):
        cs = [gi * WKV_GROUP + j for j in range(WKV_GROUP)]
        rows = [_chunk_rows(c) for c in cs]
        cps = [(j, p) for j in range(WKV_GROUP) for p in pairs]
        units = [(q, h) for q in range(len(cps)) for h in range(2)]
        pick = [left, jnp.logical_not(left)]

        def opnd(q, i):
            j, p = cps[q]
            return ops_scr[j, i, :, pair_sl(p)]

        def vals(q):
            j, p = cps[q]
            return v_ref[rows[j], pair_sl(p)]

        def merge(xs, q):
            return jnp.where(left, xs[2 * q], xs[2 * q + 1])

        lm = [jnp.where(strict2, _dot_nt(jnp.where(pick[h], opnd(q, _KT), zeros_bf),
                                         stack(opnd(q, _BH), opnd(q, _KH))), 0.0) for q, h in units]
        tinv = _inv_unit_tri([m[:, :CHUNK] for m in lm])
        mv = [_dot(lm[u].astype(BF16), stack(zeros_bf, vals(q))) for u, (q, h) in enumerate(units)]
        pm = [_bdot(tinv[u], opnd(q, _KT)) for u, (q, h) in enumerate(units)]
        qm = [_bdot(tinv[u], mv[u]) for u in range(len(units))]
        rr_b = [jnp.where(incl2, _dot_nt(jnp.where(pick[h], opnd(q, _RT), zeros_bf),
                                         stack(opnd(q, _KH), opnd(q, _BH))), 0.0).astype(BF16) for q, h in units]
        corr_p = [_dot(rr_b[u], stack(zeros_bf, pm[u].astype(BF16))) for u in range(len(units))]
        op_u = [_dot(rr_b[u], stack(vals(q), -qm[u].astype(BF16))) for u, (q, h) in enumerate(units)]
        n_cp = range(len(cps))
        p_pair = [merge(pm, q).astype(BF16) for q in n_cp]
        q_pair = [merge(qm, q).astype(BF16) for q in n_cp]
        wfull = [_dot_tn(p_pair[q], opnd(q, _BP)) for q in n_cp]
        hfull = [_dot_tn(stack(vals(q), q_pair[q]), stack(opnd(q, _KP), -opnd(q, _BP))) for q in n_cp]
        for q in n_cp:
            j, p = cps[q]
            o_ref[0, rows[j], pair_sl(p)] = merge(op_u, q)
            rp_scr[cs[j], p] = (rt_scr[j, :, pair_sl(p)] - merge(corr_p, q)).astype(BF16)
            w_hi, w_lo = _split2(jnp.where(diag_blocks, wfull[q], 0.0))
            w_scr[cs[j], p, 0] = w_hi
            w_scr[cs[j], p, 1] = w_lo
            h_scr[cs[j], p] = jnp.where(left, hfull[q][:WKV_N], hfull[q][WKV_N:])

    def phase1(gi, carry):
        for j in range(WKV_GROUP):
            prepare_chunk(gi, j, rt_a, ops_a)
        chains(gi, rt_a, ops_a)
        return carry

    lax.fori_loop(0, n_groups, phase1, 0)

    def phase2(pos, carry):
        chains2 = [(sq, p) for sq in range(n_seg) for p in pairs]
        cidx = [sq * seg_chunks + pos + z * (seg_chunks - 1 - 2 * pos) for sq in range(n_seg)]
        split = [_split2(s) for s in carry]
        sw_hi = [_dot(stack(split[i][0], split[i][1]), w_scr[cidx[sq], p, 0]) for i, (sq, p) in enumerate(chains2)]
        sw_lo = [_dot(split[i][0], w_scr[cidx[sq], p, 1]) for i, (sq, p) in enumerate(chains2)]
        s_bd = [stack(jnp.where(left, hi, zeros_bf), jnp.where(left, zeros_bf, hi)) for hi, lo in split]
        inter = [_dot_nt(rp_scr[cidx[sq], p], s_bd[i]) for i, (sq, p) in enumerate(chains2)]
        new_s = []
        for i, (sq, p) in enumerate(chains2):
            d_tot = d_scr[cidx[sq]][0:1, pair_sl(p)]
            new_s.append(carry[i] * d_tot - (sw_hi[i][:CHUNK] + sw_hi[i][CHUNK:] + sw_lo[i]) + h_scr[cidx[sq], p])
        for sq in range(n_seg):
            rows = _chunk_rows(cidx[sq])
            o_ref[0, rows, :] = o_ref[0, rows, :] + jnp.concatenate(
                [inter[sq * len(pairs) + p] for p in pairs], axis=-1)
        return tuple(new_s)

    if zero_init:
        s_init = tuple(jnp.zeros((WKV_N, LANES), F32) for sq in range(n_seg) for p in pairs)
    else:
        s_init = tuple(jnp.concatenate([s0_ref[sq, 0, 2 * p], s0_ref[sq, 0, 2 * p + 1]], axis=-1)
                       for sq in range(n_seg) for p in pairs)
    s_fin = lax.fori_loop(0, seg_chunks, phase2, s_init)
    for sq in range(n_seg):
        for p in pairs:
            s_pair = s_fin[sq * len(pairs) + p]
            sout_ref[sq, 0, 2 * p] = s_pair[:, :WKV_N]
            sout_ref[sq, 0, 2 * p + 1] = s_pair[:, WKV_N:]


def _wkv_scan(p_all, lr, w2pad, a2pad, w0, a0, k_k, k_a, s0, *, n_seq, seq_len, n_seg, tok_blk0):
    def tok(b):
        return tok_blk0 + b

    rows = n_seg * seq_len
    width = WKV_HEADS_STEP * WKV_N
    n_pairs = WKV_HEADS_STEP // 2
    col = lambda base: (lambda z, b, hq: (tok(b), base // width + hq))
    vec = lambda z, b, hq: (z, 0, hq)
    n_chunks = rows // CHUNK
    state_spec = pl.BlockSpec((n_seg, 1, WKV_HEADS_STEP, WKV_N, WKV_N), lambda z, b, hq: (b, z, hq, 0, 0))
    init = () if s0 is None else (s0,)
    return pl.pallas_call(
        functools.partial(_wkv_kernel, seq_len=rows, n_seg=n_seg, zero_init=s0 is None),
        grid=(2, n_seq // n_seg, WKV_H // WKV_HEADS_STEP),
        in_specs=[
            pl.BlockSpec((rows, width), col(4 * D)),
            pl.BlockSpec((rows, width), col(5 * D)),
            pl.BlockSpec((rows, width), col(6 * D)),
            pl.BlockSpec((rows, LANES), lambda z, b, hq: (tok(b), 0)),
            pl.BlockSpec((rows, LANES), lambda z, b, hq: (tok(b), 1)),
            pl.BlockSpec((1, LANES, width), vec),
            pl.BlockSpec((1, LANES, width), vec),
            pl.BlockSpec((1, 1, width), vec),
            pl.BlockSpec((1, 1, width), vec),
            pl.BlockSpec((1, width), lambda z, b, hq: (0, hq)),
            pl.BlockSpec((1, width), lambda z, b, hq: (0, hq)),
        ] + [state_spec] * len(init),
        out_specs=[
            pl.BlockSpec((1, rows, width), lambda z, b, hq: (z, b, hq)),
            state_spec,
        ],
        out_shape=[
            jax.ShapeDtypeStruct((2, n_seq * seq_len, D), F32),
            jax.ShapeDtypeStruct((n_seq, 2, WKV_H, WKV_N, WKV_N), F32),
        ],
        scratch_shapes=[
            pltpu.VMEM((n_chunks, n_pairs, CHUNK, LANES), BF16),
            pltpu.VMEM((n_chunks, n_pairs, 2, LANES, LANES), BF16),
            pltpu.VMEM((n_chunks, n_pairs, CHUNK, LANES), F32),
            pltpu.VMEM((n_chunks, 8, width), F32),
            pltpu.VMEM((WKV_GROUP, CHUNK, width), F32),
            pltpu.VMEM((WKV_GROUP, 6, CHUNK, width), BF16),
        ],
        compiler_params=pltpu.CompilerParams(
            dimension_semantics=("arbitrary", "arbitrary", "arbitrary"), vmem_limit_bytes=VMEM_LIMIT),
        name="wkv_scan_%d" % seq_len,
    )(p_all, p_all, p_all, lr, lr, w2pad, a2pad, w0, a0, k_k, k_a, *init)


def _dot2(x, w_bf):
    hi, lo = _split2(x)
    return _dot(hi, w_bf) + _dot(lo, w_bf)


def _group_stat(x, ind_ref, bind_ref, split_bcast=False):
    per_group = _dot(x.astype(BF16), ind_ref[...])
    if split_bcast:
        return _dot2(per_group, bind_ref[...])
    return _dot(per_group.astype(BF16), bind_ref[...])


N_CTX_TILES = N_CTX_TOK // TM_OUT


def _sum_dirs(i, oc_ref, os_ref):
    is_ctx = i < N_CTX_TILES
    return jnp.where(is_ctx, oc_ref[0], os_ref[0]) + jnp.where(is_ctx, oc_ref[1], os_ref[1])


def _ctx_tile(i):
    return (0, jnp.minimum(i, N_CTX_TILES - 1), 0)


def _smp_tile(i):
    return (0, jnp.maximum(i - N_CTX_TILES, 0), 0)


def _even_out_kernel(xc_ref, xs_ref, mod_ref, pc_ref, pw_ref, oc_ref, os_ref, lra_ref, a2_ref, a0_ref, ka_ref,
                     rk_ref, cw_ref, lnw_ref, lnb_ref, ind_ref, bind_ref, wo_ref, out_ref):
    i = pl.program_id(0)
    u = pc_ref[:, 0:D].astype(F32)
    gb = pc_ref[:, D:2 * D].astype(F32)
    gc = pc_ref[:, 2 * D:3 * D].astype(F32)
    zc = pc_ref[:, 3 * D:4 * D].astype(F32)
    xg = gc * u
    row_len = jnp.where(i < N_CTX_TILES, CTX_T, GRID_W)
    pos = lax.broadcasted_iota(jnp.int32, (TM_OUT, 1), 0) & (row_len - 1)
    prev = jnp.where(pos == 0, 0.0, pltpu.roll(xg, 1, 0))
    nxt = jnp.where(pos == row_len - 1, 0.0, pltpu.roll(xg, TM_OUT - 1, 0))
    conv = cw_ref[0:1, :] * prev + cw_ref[1:2, :] * xg + cw_ref[2:3, :] * nxt
    o_conv = _silu(zc) * gb * conv
    r = pw_ref[:, 0:D].astype(F32)
    k = pw_ref[:, D:2 * D].astype(F32)
    v = pw_ref[:, 2 * D:3 * D].astype(F32)
    zw = pw_ref[:, 3 * D:4 * D].astype(F32)
    o = _sum_dirs(i, oc_ref, os_ref)
    mu = _group_stat(o, ind_ref, bind_ref, split_bcast=True)
    dlt = o - mu
    var = _group_stat(dlt * dlt, ind_ref, bind_ref)
    gn = dlt * lax.rsqrt(var + GN_EPS) * lnw_ref[...] + lnb_ref[...]
    lra = lra_ref[...]
    ic0 = jax.nn.sigmoid(a0_ref[0] + _bdot(lra, a2_ref[0]))
    ic1 = jax.nn.sigmoid(a0_ref[1] + _bdot(lra, a2_ref[1]))
    ksum = k * (2.0 + (ic0 + ic1 - 2.0) * ka_ref[...])
    bonus = _group_stat(r * ksum * rk_ref[...], ind_ref, bind_ref) * float(WKV_N)
    o_wkv = (gn + bonus * v) * _silu(zw)
    y = _dot(o_conv.astype(BF16), wo_ref[0:D, :]) + _dot(o_wkv.astype(BF16), wo_ref[D:2 * D, :])
    gate = mod_ref[0, 2:3, :]
    out_ref[...] = jnp.where(i < N_CTX_TILES, xc_ref[...], xs_ref[...]) + gate * y


def _even_out(x_ctx, x_smp, mod3, p_all, o_ctx, o_smp, lr, a2pad, a0, k_a, r_k, conv_w, ln_w, ln_b, ind, bind,
              wo_bf):
    row = lambda i: (i, 0)
    const2 = lambda i: (0, 0)
    const3 = lambda i: (0, 0, 0)
    return pl.pallas_call(
        _even_out_kernel,
        grid=(N_TOK // TM_OUT,),
        in_specs=[
            pl.BlockSpec((TM_OUT, D), lambda i: _ctx_tile(i)[1:]),
            pl.BlockSpec((TM_OUT, D), lambda i: _smp_tile(i)[1:]),
            pl.BlockSpec((1, 3, D), lambda i: (_tile_row(i, TM_OUT), 0, 0)),
            pl.BlockSpec((TM_OUT, 4 * D), lambda i: (i, 0)),
            pl.BlockSpec((TM_OUT, 4 * D), lambda i: (i, 1)),
            pl.BlockSpec((2, TM_OUT, D), _ctx_tile),
            pl.BlockSpec((2, TM_OUT, D), _smp_tile),
            pl.BlockSpec((TM_OUT, LANES), lambda i: (i, 1)),
            pl.BlockSpec((2, LANES, D), const3),
            pl.BlockSpec((2, 1, D), const3),
            pl.BlockSpec((1, D), const2),
            pl.BlockSpec((1, D), const2),
            pl.BlockSpec((3, D), const2),
            pl.BlockSpec((1, D), const2),
            pl.BlockSpec((1, D), const2),
            pl.BlockSpec((D, LANES), const2),
            pl.BlockSpec((LANES, D), const2),
            pl.BlockSpec((2 * D, D), const2),
        ],
        out_specs=pl.BlockSpec((TM_OUT, D), row),
        out_shape=jax.ShapeDtypeStruct((N_TOK, D), F32),
        compiler_params=pltpu.CompilerParams(
            dimension_semantics=("arbitrary",), vmem_limit_bytes=VMEM_LIMIT),
        name="even_out",
    )(x_ctx, x_smp, mod3, p_all, p_all, o_ctx, o_smp, lr, a2pad, a0, k_a, r_k, conv_w, ln_w, ln_b, ind, bind, wo_bf)


def _gla_kernel(q_ref, k_ref, v_ref, lrg_ref, gk2_ref, gkb_ref, m_ref, *rest, seq_len, n_seg, zero_init):
    s0_ref = None if zero_init else rest[0]
    o_ref, sout_ref, qe_scr, st_scr, e_scr = rest[-5:]
    z = pl.program_id(0)
    n_chunks = seq_len // CHUNK
    seg_chunks = n_chunks // n_seg
    group = min(GLA_GROUP, n_chunks)
    n_groups = n_chunks // group
    incl_b = m_ref[0] > 0.5
    gk2 = gk2_ref[0].astype(BF16)
    gkb = gkb_ref[0]
    scale = GLA_DK ** -0.5

    def phase1(it, carry):
        cs = [it * group + j for j in range(group)]
        rows = [_chunk_rows(c) for c in cs]
        logit = [_bdot(lrg_ref[rw, :], gk2) + gkb for rw in rows]
        g = [jax.nn.log_sigmoid(x) / GLA_GATE_NORM for x in logit]
        bc, b_last = zip(*[_scan_cumsum(x, z) for x in g])
        k = [k_ref[rw, :].astype(F32) for rw in rows]
        qe = [q_ref[rw, :].astype(F32) * scale * jnp.exp(x) for rw, x in zip(rows, bc)]
        ke = [kc * jnp.exp(-x) for kc, x in zip(k, bc)]
        kd = [kc * jnp.exp(bl - x) for kc, x, bl in zip(k, bc, b_last)]
        v = [v_ref[rw, :] for rw in rows]
        att = [jnp.where(incl_b, _bdot_nt(a, b), 0.0) for a, b in zip(qe, ke)]
        o_in = [_bdot(a, b) for a, b in zip(att, v)]
        vtk = [_bdot_tn(a, b) for a, b in zip(v, kd)]
        for j in range(group):
            o_ref[0, rows[j], :] = o_in[j]
            qe_scr[cs[j]] = qe[j]
            st_scr[cs[j]] = vtk[j]
            e_scr[cs[j]] = jnp.broadcast_to(jnp.exp(b_last[j]), (8, GLA_DK))
        return carry

    lax.fori_loop(0, n_groups, phase1, 0)

    def phase2(pos, sts):
        cidx = [sq * seg_chunks + pos + z * (seg_chunks - 1 - 2 * pos) for sq in range(n_seg)]
        new = [st * e_scr[c][0:1, :] + st_scr[c] for st, c in zip(sts, cidx)]
        for st, c in zip(sts, cidx):
            st_scr[c] = st
        return tuple(new)

    if zero_init:
        st_init = tuple(jnp.zeros((GLA_DV, GLA_DK), F32) for sq in range(n_seg))
    else:
        st_init = tuple(s0_ref[sq, 0, 0].T for sq in range(n_seg))
    st_fin = lax.fori_loop(0, seg_chunks, phase2, st_init)
    for sq in range(n_seg):
        sout_ref[sq, 0, 0] = st_fin[sq].T

    def phase3(it, carry):
        cs = [it * group + j for j in range(group)]
        inter = [_bdot_nt(qe_scr[c], st_scr[c]) for c in cs]
        for j in range(group):
            rows = _chunk_rows(cs[j])
            o_ref[0, rows, :] = o_ref[0, rows, :] + inter[j]
        return carry

    lax.fori_loop(0, n_groups, phase3, 0)


def _gla_scan(p_all, lrg, gk2pad, gkb, s0, masks, *, n_seq, seq_len, n_seg, tok_blk0):
    def tok(b):
        return tok_blk0 + b

    rows = n_seg * seq_len
    n_chunks = rows // CHUNK
    state_spec = pl.BlockSpec((n_seg, 1, 1, GLA_DK, GLA_DV), lambda z, b, h: (b, z, h, 0, 0))
    init = () if s0 is None else (s0,)
    return pl.pallas_call(
        functools.partial(_gla_kernel, seq_len=rows, n_seg=n_seg, zero_init=s0 is None),
        grid=(2, n_seq // n_seg, GLA_H),
        in_specs=[
            pl.BlockSpec((rows, GLA_DK), lambda z, b, h: (tok(b), h)),
            pl.BlockSpec((rows, GLA_DK), lambda z, b, h: (tok(b), GLA_H + h)),
            pl.BlockSpec((rows, GLA_DV), lambda z, b, h: (tok(b), GLA_H + h)),
            pl.BlockSpec((rows, LANES), lambda z, b, h: (tok(b), 0)),
            pl.BlockSpec((1, LANES, GLA_DK), lambda z, b, h: (z, 0, h)),
            pl.BlockSpec((1, 1, GLA_DK), lambda z, b, h: (z, 0, h)),
            pl.BlockSpec((1, CHUNK, CHUNK), lambda z, b, h: (z, 0, 0)),
        ] + [state_spec] * len(init),
        out_specs=[
            pl.BlockSpec((1, rows, GLA_DV), lambda z, b, h: (z, b, h)),
            state_spec,
        ],
        out_shape=[
            jax.ShapeDtypeStruct((2, n_seq * seq_len, D), F32),
            jax.ShapeDtypeStruct((n_seq, 2, GLA_H, GLA_DK, GLA_DV), F32),
        ],
        scratch_shapes=[
            pltpu.VMEM((n_chunks, CHUNK, GLA_DK), F32),
            pltpu.VMEM((n_chunks, GLA_DV, GLA_DK), F32),
            pltpu.VMEM((n_chunks, 8, GLA_DK), F32),
        ],
        compiler_params=pltpu.CompilerParams(
            dimension_semantics=("arbitrary", "arbitrary", "arbitrary"), vmem_limit_bytes=VMEM_LIMIT),
        name="gla_scan_%d" % seq_len,
    )(p_all, p_all, p_all, lrg, gk2pad, gkb, masks, *init)


def _odd_out_kernel(x_ref, mod_ref, zg_ref, oc_ref, os_ref, gn_ref, ind_ref, bind_ref, wo_ref, fg_ref,
                    yc_ref, ys_ref):
    i = pl.program_id(0)
    o = _sum_dirs(i, oc_ref, os_ref)
    ms = _group_stat(o * o, ind_ref, bind_ref)
    on = o * lax.rsqrt(ms + NORM_EPS) * gn_ref[...]
    og = on * _silu(zg_ref[...].astype(F32))
    y = _dot(og.astype(BF16), wo_ref[...])
    gate = mod_ref[0, 2:3, :]
    x = x_ref[...] + gate * y
    out = x * lax.rsqrt(jnp.mean(x * x, axis=-1, keepdims=True) + NORM_EPS) * fg_ref[...]

    @pl.when(i < N_CTX_TILES)
    def _():
        yc_ref[...] = out

    @pl.when(i >= N_CTX_TILES)
    def _():
        ys_ref[...] = out


def _odd_out(x, mod3, p_all, o_ctx, o_smp, g_norm, ind, bind, wo_bf, final_g):
    row = lambda i: (i, 0)
    const2 = lambda i: (0, 0)
    return pl.pallas_call(
        _odd_out_kernel,
        grid=(N_TOK // TM_OUT,),
        in_specs=[
            pl.BlockSpec((TM_OUT, D), row),
            pl.BlockSpec((1, 3, D), lambda i: (_tile_row(i, TM_OUT), 0, 0)),
            pl.BlockSpec((TM_OUT, D), lambda i: (i, 2)),
            pl.BlockSpec((2, TM_OUT, D), _ctx_tile),
            pl.BlockSpec((2, TM_OUT, D), _smp_tile),
            pl.BlockSpec((1, D), const2),
            pl.BlockSpec((D, LANES), const2),
            pl.BlockSpec((LANES, D), const2),
            pl.BlockSpec((D, D), const2),
            pl.BlockSpec((1, D), const2),
        ],
        out_specs=[
            pl.BlockSpec((TM_OUT, D), lambda i: _ctx_tile(i)[1:]),
            pl.BlockSpec((TM_OUT, D), lambda i: _smp_tile(i)[1:]),
        ],
        out_shape=[
            jax.ShapeDtypeStruct((N_CTX_TOK, D), F32),
            jax.ShapeDtypeStruct((N_TOK - N_CTX_TOK, D), F32),
        ],
        compiler_params=pltpu.CompilerParams(
            dimension_semantics=("arbitrary",), vmem_limit_bytes=VMEM_LIMIT),
        name="odd_out",
    )(x, mod3, p_all, o_ctx, o_smp, g_norm, ind, bind, wo_bf, final_g)


def _group_indicators(group):
    ch = jnp.arange(D) // group
    lane = jnp.arange(LANES)
    hit = (ch[:, None] == lane[None, :]).astype(F32)
    return (hit / float(group)).astype(BF16), hit.T.astype(BF16)


def _pad_dirs(w):
    r = w.shape[1]
    out = jnp.zeros((2, LANES, w.shape[2]), F32)
    out = out.at[0, 0:r].set(w[0])
    out = out.at[1, r:2 * r].set(w[1])
    return out


def kernel(x_prompt, x_sample, state_wkv, state_gla, c, c_ctx, norm_g, ada_w, ada_b, final_g, e_w_in, e_w_out, conv_w, wkv_w0, wkv_w1, wkv_w2, wkv_a0, wkv_a1, wkv_a2, wkv_k_k, wkv_k_a, wkv_r_k, wkv_ln_w, wkv_ln_b, o_w_in, o_w_out, gla_gk1, gla_gk2, gla_gk_b, gla_g_norm):
    x_ctx = x_prompt.reshape(N_CTX_TOK, D)
    x_smp = x_sample.reshape(N_TOK - N_CTX_TOK, D)
    cvec = jnp.zeros((8, D), F32).at[0].set(c_ctx).at[1:1 + SMP_B].set(c)
    mod = _modulation(cvec, ada_w, ada_b)
    masks = _chunk_masks()

    mod3 = mod[0].reshape(8, 3, D)
    wlr = jnp.concatenate([wkv_w1[0, 0], wkv_w1[0, 1], wkv_a1[0, 0], wkv_a1[0, 1]], axis=1)
    p_e, lr_e = _inproj(x_ctx, x_smp, 0, mod3, norm_g[0:1], e_w_in[0].astype(BF16), wlr.astype(BF16),
                        n_tanh=2 * 64)
    w2pad = _pad_dirs(wkv_w2[0])
    a2pad = _pad_dirs(wkv_a2[0])
    w0 = wkv_w0[0].reshape(2, 1, D)
    a0 = wkv_a0[0].reshape(2, 1, D)
    k_k = wkv_k_k[0].reshape(1, D)
    k_a = wkv_k_a[0].reshape(1, D)
    r_k = wkv_r_k[0].reshape(1, D)
    o_ctx, new_wkv = _wkv_scan(p_e, lr_e, w2pad, a2pad, w0, a0, k_k, k_a, None,
                               n_seq=CTX_B, seq_len=CTX_T, n_seg=WKV_CTX_SEG, tok_blk0=0)
    o_smp, _ = _wkv_scan(p_e, lr_e, w2pad, a2pad, w0, a0, k_k, k_a, state_wkv[:, 0],
                         n_seq=SMP_B, seq_len=SMP_T, n_seg=1, tok_blk0=N_CTX_TOK // SMP_T)
    ind64, bind64 = _group_indicators(WKV_N)
    x = _even_out(x_ctx, x_smp, mod3, p_e, o_ctx, o_smp, lr_e, a2pad, a0, k_a, r_k, conv_w[0],
                  wkv_ln_w[0].reshape(1, D), wkv_ln_b[0].reshape(1, D), ind64, bind64,
                  e_w_out[0].astype(BF16))

    mod3 = mod[1].reshape(8, 3, D)
    rank = gla_gk1.shape[-1]
    wlr = jnp.zeros((D, LANES), F32).at[:, 0:rank].set(gla_gk1[0, 0]).at[:, rank:2 * rank].set(gla_gk1[0, 1])
    p_o, lr_o = _inproj(x, x, N_CTX_TOK // TM_PROJ, mod3, norm_g[1:2], o_w_in[0].astype(BF16),
                        wlr.astype(BF16), n_tanh=0)
    gk2pad = _pad_dirs(gla_gk2[0])
    gkb = gla_gk_b[0].reshape(2, 1, GLA_H * GLA_DK)
    g_ctx, new_gla = _gla_scan(p_o, lr_o, gk2pad, gkb, None, masks,
                               n_seq=CTX_B, seq_len=CTX_T, n_seg=GLA_CTX_SEG, tok_blk0=0)
    g_smp, _ = _gla_scan(p_o, lr_o, gk2pad, gkb, state_gla[:, 0], masks,
                         n_seq=SMP_B, seq_len=SMP_T, n_seg=1, tok_blk0=N_CTX_TOK // SMP_T)
    ind256, bind256 = _group_indicators(GLA_DV)
    y_ctx, y_smp = _odd_out(x, mod3, p_o, g_ctx, g_smp, jnp.tile(gla_g_norm[0], GLA_H).reshape(1, D),
                            ind256, bind256, o_w_out[0].astype(BF16), final_g.reshape(1, D))
    return (y_ctx.reshape(CTX_B, CTX_T, D), y_smp.reshape(SMP_B, SMP_T, D), new_wkv[:, None], new_gla[:, None])
```

```python
import functools
import math

import jax
import jax.numpy as jnp
from jax import lax
from jax.experimental import pallas as pl
from jax.experimental.pallas import tpu as pltpu

F32 = jnp.float32
BF16 = jnp.bfloat16
HI = lax.Precision.HIGHEST

D = 1024
N_CTX_TOK = 16 * 256
N_TOK = 2 * N_CTX_TOK
CTX_B, CTX_T = 16, 256
SMP_B, SMP_T = 2, 2048
GRID_W = 64
WKV_H, WKV_N = 16, 64
GLA_H, GLA_DK, GLA_DV = 4, 128, 256
GLA_GATE_NORM = 16.0
GN_EPS = 64e-5
NORM_EPS = 1e-6
CHUNK = 64
WKV_GROUP = 4
WKV_HEADS_STEP = 8
WKV_CTX_SEG = 4
GLA_CTX_SEG = 4
GLA_GROUP = 8
LANES = 128
TM_PROJ = 1024
TN_PROJ_MAX = 2048
TM_OUT = 256
VMEM_LIMIT = 56 * 1024 * 1024


def _silu(x):
    return x * jax.nn.sigmoid(x)


def _dot(a, b, precision=None):
    return jnp.dot(a, b, preferred_element_type=F32, precision=precision)


def _dot_nt(a, b, precision=None):
    return lax.dot_general(a, b, (((1,), (1,)), ((), ())), preferred_element_type=F32, precision=precision)


def _dot_tn(a, b, precision=None):
    return lax.dot_general(a, b, (((0,), (0,)), ((), ())), preferred_element_type=F32, precision=precision)


def _bdot(a, b):
    return _dot(a.astype(BF16), b.astype(BF16))


def _bdot_nt(a, b):
    return _dot_nt(a.astype(BF16), b.astype(BF16))


def _bdot_tn(a, b):
    return _dot_tn(a.astype(BF16), b.astype(BF16))


def _split2(x):
    hi = x.astype(BF16)
    lo = (x - hi.astype(F32)).astype(BF16)
    return hi, lo


def _scan_cumsum(x, z):
    n = x.shape[0]
    row = lax.broadcasted_iota(jnp.int32, (n, 1), 0)
    pre = x
    s = 1
    while s < n:
        pre = pre + jnp.where(row >= s, pltpu.roll(pre, s, 0), 0.0)
        s *= 2
    tot = pre[n - 1:n, :]
    return jnp.where(z == 0, pre, tot - pre + x), tot


def _tile_row(i, tile):
    ctx_tiles = N_CTX_TOK // tile
    per_req = SMP_T // tile
    return jnp.where(i < ctx_tiles, 0, 1 + (i - ctx_tiles) // per_req)


def _mod_kernel(c_ref, w_ref, b_ref, o_ref):
    cf = _silu(c_ref[...])
    o_ref[0] = _dot(cf, w_ref[0], HI) + b_ref[0]


def _modulation(cvec, ada_w, ada_b):
    depth = ada_w.shape[0]
    return pl.pallas_call(
        _mod_kernel,
        grid=(depth, 3),
        in_specs=[
            pl.BlockSpec((8, D), lambda l, j: (0, 0)),
            pl.BlockSpec((1, D, D), lambda l, j: (l, 0, j)),
            pl.BlockSpec((1, 1, D), lambda l, j: (l, 0, j)),
        ],
        out_specs=pl.BlockSpec((1, 8, D), lambda l, j: (l, 0, j)),
        out_shape=jax.ShapeDtypeStruct((depth, 8, 3 * D), F32),
        compiler_params=pltpu.CompilerParams(
            dimension_semantics=("arbitrary", "arbitrary"), vmem_limit_bytes=VMEM_LIMIT),
        name="adaln_mod",
    )(cvec, ada_w, ada_b.reshape(depth, 1, 3 * D))


def _inproj_kernel(xa_ref, xb_ref, mod_ref, g_ref, w_ref, wlr_ref, p_ref, lr_ref, h_scr, *, n_tanh):
    i = pl.program_id(0)
    j = pl.program_id(1)

    @pl.when(j == 0)
    def _():
        x = jnp.where(i < N_CTX_TOK // TM_PROJ, xa_ref[...], xb_ref[...])
        y = x * lax.rsqrt(jnp.mean(x * x, axis=-1, keepdims=True) + NORM_EPS) * g_ref[...]
        shift = mod_ref[0, 0:1, :]
        scale = mod_ref[0, 1:2, :]
        h = y * (1.0 + scale) + shift
        hb = h.astype(BF16)
        h_scr[...] = hb
        lr = _dot(hb, wlr_ref[...])
        if n_tanh:
            lane = lax.broadcasted_iota(jnp.int32, lr.shape, 1)
            lr = jnp.where(lane < n_tanh, jnp.tanh(lr), lr)
        lr_ref[...] = lr

    p_ref[...] = _dot(h_scr[...], w_ref[...]).astype(p_ref.dtype)


def _inproj(x_ctx, x_smp, smp_tile0, mod3, g, w_bf, wlr_bf, n_tanh):
    n_out = w_bf.shape[1]
    n_lr = wlr_bf.shape[1]
    n_ctx = N_CTX_TOK // TM_PROJ
    tn = max(t for t in range(256, TN_PROJ_MAX + 1, 256) if n_out % t == 0)
    return pl.pallas_call(
        functools.partial(_inproj_kernel, n_tanh=n_tanh),
        grid=(N_TOK // TM_PROJ, n_out // tn),
        in_specs=[
            pl.BlockSpec((TM_PROJ, D), lambda i, j: (jnp.minimum(i, n_ctx - 1), 0)),
            pl.BlockSpec((TM_PROJ, D), lambda i, j: (smp_tile0 + jnp.maximum(i - n_ctx, 0), 0)),
            pl.BlockSpec((1, 3, D), lambda i, j: (_tile_row(i, TM_PROJ), 0, 0)),
            pl.BlockSpec((1, D), lambda i, j: (0, 0)),
            pl.BlockSpec((D, tn), lambda i, j: (0, j)),
            pl.BlockSpec((D, n_lr), lambda i, j: (0, 0)),
        ],
        out_specs=[
            pl.BlockSpec((TM_PROJ, tn), lambda i, j: (i, j)),
            pl.BlockSpec((TM_PROJ, n_lr), lambda i, j: (i, 0)),
        ],
        out_shape=[
            jax.ShapeDtypeStruct((N_TOK, n_out), BF16),
            jax.ShapeDtypeStruct((N_TOK, n_lr), F32),
        ],
        scratch_shapes=[pltpu.VMEM((TM_PROJ, D), BF16)],
        compiler_params=pltpu.CompilerParams(
            dimension_semantics=("arbitrary", "arbitrary"), vmem_limit_bytes=VMEM_LIMIT),
        name="inproj",
    )(x_ctx, x_smp, mod3, g, w_bf, wlr_bf)


def _chunk_masks():
    idx = jnp.arange(CHUNK)
    lower = (idx[:, None] >= idx[None, :]).astype(F32)
    return jnp.stack([lower, lower.T], axis=0)


def _chunk_rows(c):
    return pl.ds(pl.multiple_of(c * CHUNK, CHUNK), CHUNK)


def _inv_unit_tri(lmats):
    row = lax.broadcasted_iota(jnp.int32, (CHUNK, CHUNK), 0)
    col = lax.broadcasted_iota(jnp.int32, (CHUNK, CHUNK), 1)
    eye = (row == col).astype(F32)

    def same_block(n):
        sh = n.bit_length() - 1
        return lax.shift_right_logical(row, sh) == lax.shift_right_logical(col, sh)

    blk16 = same_block(16)
    l16 = [jnp.where(blk16, m, 0.0) for m in lmats]
    x = [eye - m for m in l16]
    p = [_bdot(m, m) for m in l16]
    for step in range(3):
        x = [xi + _bdot(xi, pi) for xi, pi in zip(x, p)]
        if step < 2:
            p = [_bdot(pi, pi) for pi in p]
    n = 16
    while n < CHUNK:
        off = same_block(2 * n) & jnp.logical_not(same_block(n))
        xl = [_bdot(xi, jnp.where(off, m, 0.0)) for xi, m in zip(x, lmats)]
        x = [xi - _bdot(yi, xi) for xi, yi in zip(x, xl)]
        n *= 2
    return x


_KT, _RT, _KH, _BH, _KP, _BP = range(6)


def _wkv_kernel(r_ref, k_ref, v_ref, lrw_ref, lra_ref, w2_ref, a2_ref, w0_ref, a0_ref,
                kk_ref, ka_ref, *rest, seq_len, n_seg, zero_init):
    s0_ref = None if zero_init else rest[0]
    o_ref, sout_ref, rp_scr, w_scr, h_scr, d_scr, rt_a, ops_a = rest[-8:]
    z = pl.program_id(0)
    n_chunks = seq_len // CHUNK
    seg_chunks = n_chunks // n_seg
    n_groups = n_chunks // WKV_GROUP
    pairs = range(WKV_HEADS_STEP // 2)
    row = lax.broadcasted_iota(jnp.int32, (CHUNK, LANES), 0)
    lane = lax.broadcasted_iota(jnp.int32, (CHUNK, LANES), 1)
    col = lane & (CHUNK - 1)
    left = lane < WKV_N
    incl2 = ((z == 0) & (row >= col)) | ((z == 1) & (row <= col))
    strict2 = incl2 & (row != col)
    row_b = lax.broadcasted_iota(jnp.int32, (LANES, LANES), 0) < WKV_N
    lane_b = lax.broadcasted_iota(jnp.int32, (LANES, LANES), 1) < WKV_N
    diag_blocks = row_b == lane_b
    zeros_bf = jnp.zeros((CHUNK, LANES), BF16)
    w2 = w2_ref[0].astype(BF16)
    a2 = a2_ref[0].astype(BF16)
    w0 = w0_ref[0]
    a0 = a0_ref[0]
    k_k = kk_ref[...]
    k_a = ka_ref[...]

    def pair_sl(p):
        return slice(p * LANES, (p + 1) * LANES)

    def stack(a, b):
        return jnp.concatenate([a, b], axis=0)

    def prepare_chunk(gi, j, rt_scr, ops_scr):
        c = gi * WKV_GROUP + j
        rows = _chunk_rows(c)
        w_raw = w0 + _bdot(lrw_ref[rows, :], w2)
        a = jax.nn.sigmoid(a0 + _bdot(lra_ref[rows, :], a2))
        lw = -math.exp(-0.5) * jax.nn.sigmoid(w_raw)
        g, gtot = _scan_cumsum(lw, z)
        r = r_ref[rows, :].astype(F32)
        k = k_ref[rows, :].astype(F32)
        kkr = k * k_k
        sq = kkr * kkr
        nrm = []
        for p in pairs:
            sq_p = sq[:, pair_sl(p)]
            n0 = jnp.sum(jnp.where(left, sq_p, 0.0), axis=-1, keepdims=True)
            n1 = jnp.sum(jnp.where(left, 0.0, sq_p), axis=-1, keepdims=True)
            nrm.append(jnp.where(left, jnp.sqrt(n0), jnp.sqrt(n1)))
        kk = kkr / jnp.maximum(jnp.concatenate(nrm, axis=-1), 1e-12)
        b = kk * a
        kmod = k * (1.0 + (a - 1.0) * k_a)
        e_ng = jnp.exp(-g)
        e_gc = jnp.exp(gtot - g)
        rt = r * jnp.exp(g)
        rt_scr[j] = rt
        ops_scr[j, _KT] = (kk * jnp.exp(g - lw)).astype(BF16)
        ops_scr[j, _RT] = rt.astype(BF16)
        ops_scr[j, _KH] = (kmod * e_ng).astype(BF16)
        ops_scr[j, _BH] = (b * e_ng).astype(BF16)
        ops_scr[j, _KP] = (kmod * e_gc).astype(BF16)
        ops_scr[j, _BP] = (b * e_gc).astype(BF16)
        d_scr[c] = jnp.broadcast_to(jnp.exp(gtot), (8, WKV_HEADS_STEP * WKV_N))

    def chains(gi, rt_scr, ops_scr):
        cs = [gi * WKV_GROUP + j for j in range(WKV_GROUP)]
        rows = [_chunk_rows(c) for c in cs]
        cps = [(j, p) for j in range(WKV_GROUP) for p in pairs]
        units = [(q, h) for q in range(len(cps)) for h in range(2)]
        pick = [left, jnp.logical_not(left)]

        def opnd(q, i):
            j, p = cps[q]
            return ops_scr[j, i, :, pair_sl(p)]

        def vals(q):
            j, p = cps[q]
            return v_ref[rows[j], pair_sl(p)]

        n_cp = range(len(cps))
        lm = [jnp.where(strict2, _dot_nt(jnp.where(pick[h], opnd(q, _KT), zeros_bf),
                                         stack(opnd(q, _BH), opnd(q, _KH))), 0.0) for q, h in units]
        tinv = _inv_unit_tri([m[:, :CHUNK] for m in lm])

        def halves(x0, x1):
            return jnp.where(left, x0, zeros_bf), jnp.where(left, zeros_bf, x1)

        def rows4(a, b, c, d):
            return jnp.concatenate([a, b, c, d], axis=0)

        v_lr = [halves(vals(q), vals(q)) for q in n_cp]
        lm2 = [jnp.concatenate([lm[2 * q], lm[2 * q + 1]], axis=-1).astype(BF16) for q in n_cp]
        mv = [_dot(lm2[q], rows4(zeros_bf, v_lr[q][0], zeros_bf, v_lr[q][1])) for q in n_cp]
        pm = [_bdot(tinv[u], opnd(q, _KT)) for u, (q, h) in enumerate(units)]
        qm = [_bdot(tinv[u], mv[q]) for u, (q, h) in enumerate(units)]
        rr2 = [jnp.concatenate(
            [jnp.where(incl2, _dot_nt(jnp.where(pick[h], opnd(q, _RT), zeros_bf),
                                      stack(opnd(q, _KH), opnd(q, _BH))), 0.0).astype(BF16) for h in range(2)],
            axis=-1) for q in n_cp]
        p_lr = [halves(pm[2 * q].astype(BF16), pm[2 * q + 1].astype(BF16)) for q in n_cp]
        q_lr = [halves(qm[2 * q].astype(BF16), qm[2 * q + 1].astype(BF16)) for q in n_cp]
        corr_p = [_dot(rr2[q], rows4(zeros_bf, p_lr[q][0], zeros_bf, p_lr[q][1])) for q in n_cp]
        op_m = [_dot(rr2[q], rows4(v_lr[q][0], -q_lr[q][0], v_lr[q][1], -q_lr[q][1])) for q in n_cp]
        p_pair = [p_lr[q][0] + p_lr[q][1] for q in n_cp]
        q_pair = [q_lr[q][0] + q_lr[q][1] for q in n_cp]
        wfull = [_dot_tn(p_pair[q], opnd(q, _BP)) for q in n_cp]
        hfull = [_dot_tn(stack(vals(q), q_pair[q]), stack(opnd(q, _KP), -opnd(q, _BP))) for q in n_cp]
        for q in n_cp:
            j, p = cps[q]
            o_ref[0, rows[j], pair_sl(p)] = op_m[q]
            rp_scr[cs[j], p] = (rt_scr[j, :, pair_sl(p)] - corr_p[q]).astype(BF16)
            w_hi, w_lo = _split2(jnp.where(diag_blocks, wfull[q], 0.0))
            w_scr[cs[j], p, 0] = w_hi
            w_scr[cs[j], p, 1] = w_lo
            h_scr[cs[j], p] = jnp.where(left, hfull[q][:WKV_N], hfull[q][WKV_N:])

    def phase1(gi, carry):
        for j in range(WKV_GROUP):
            prepare_chunk(gi, j, rt_a, ops_a)
        chains(gi, rt_a, ops_a)
        return carry

    lax.fori_loop(0, n_groups, phase1, 0)

    def phase2(pos, carry):
        chains2 = [(sq, p) for sq in range(n_seg) for p in pairs]
        cidx = [sq * seg_chunks + pos + z * (seg_chunks - 1 - 2 * pos) for sq in range(n_seg)]
        split = [_split2(s) for s in carry]
        sw_hi = [_dot(stack(split[i][0], split[i][1]), w_scr[cidx[sq], p, 0]) for i, (sq, p) in enumerate(chains2)]
        sw_lo = [_dot(split[i][0], w_scr[cidx[sq], p, 1]) for i, (sq, p) in enumerate(chains2)]
        s_bd = [stack(jnp.where(left, hi, zeros_bf), jnp.where(left, zeros_bf, hi)) for hi, lo in split]
        inter = [_dot_nt(rp_scr[cidx[sq], p], s_bd[i]) for i, (sq, p) in enumerate(chains2)]
        new_s = []
        for i, (sq, p) in enumerate(chains2):
            d_tot = d_scr[cidx[sq]][0:1, pair_sl(p)]
            new_s.append(carry[i] * d_tot - (sw_hi[i][:CHUNK] + sw_hi[i][CHUNK:] + sw_lo[i]) + h_scr[cidx[sq], p])
        for sq in range(n_seg):
            rows = _chunk_rows(cidx[sq])
            o_ref[0, rows, :] = o_ref[0, rows, :] + jnp.concatenate(
                [inter[sq * len(pairs) + p] for p in pairs], axis=-1)
        return tuple(new_s)

    if zero_init:
        s_init = tuple(jnp.zeros((WKV_N, LANES), F32) for sq in range(n_seg) for p in pairs)
    else:
        s_init = tuple(jnp.concatenate([s0_ref[sq, 0, 2 * p], s0_ref[sq, 0, 2 * p + 1]], axis=-1)
                       for sq in range(n_seg) for p in pairs)
    s_fin = lax.fori_loop(0, seg_chunks, phase2, s_init)
    for sq in range(n_seg):
        for p in pairs:
            s_pair = s_fin[sq * len(pairs) + p]
            sout_ref[sq, 0, 2 * p] = s_pair[:, :WKV_N]
            sout_ref[sq, 0, 2 * p + 1] = s_pair[:, WKV_N:]


def _wkv_scan(p_all, lr, w2pad, a2pad, w0, a0, k_k, k_a, s0, *, n_seq, seq_len, n_seg, tok_blk0):
    def tok(b):
        return tok_blk0 + b

    rows = n_seg * seq_len
    width = WKV_HEADS_STEP * WKV_N
    n_pairs = WKV_HEADS_STEP // 2
    col = lambda base: (lambda z, b, hq: (tok(b), base // width + hq))
    vec = lambda z, b, hq: (z, 0, hq)
    n_chunks = rows // CHUNK
    state_spec = pl.BlockSpec((n_seg, 1, WKV_HEADS_STEP, WKV_N, WKV_N), lambda z, b, hq: (b, z, hq, 0, 0))
    init = () if s0 is None else (s0,)
    return pl.pallas_call(
        functools.partial(_wkv_kernel, seq_len=rows, n_seg=n_seg, zero_init=s0 is None),
        grid=(2, n_seq // n_seg, WKV_H // WKV_HEADS_STEP),
        in_specs=[
            pl.BlockSpec((rows, width), col(4 * D)),
            pl.BlockSpec((rows, width), col(5 * D)),
            pl.BlockSpec((rows, width), col(6 * D)),
            pl.BlockSpec((rows, LANES), lambda z, b, hq: (tok(b), 0)),
            pl.BlockSpec((rows, LANES), lambda z, b, hq: (tok(b), 1)),
            pl.BlockSpec((1, LANES, width), vec),
            pl.BlockSpec((1, LANES, width), vec),
            pl.BlockSpec((1, 1, width), vec),
            pl.BlockSpec((1, 1, width), vec),
            pl.BlockSpec((1, width), lambda z, b, hq: (0, hq)),
            pl.BlockSpec((1, width), lambda z, b, hq: (0, hq)),
        ] + [state_spec] * len(init),
        out_specs=[
            pl.BlockSpec((1, rows, width), lambda z, b, hq: (z, b, hq)),
            state_spec,
        ],
        out_shape=[
            jax.ShapeDtypeStruct((2, n_seq * seq_len, D), F32),
            jax.ShapeDtypeStruct((n_seq, 2, WKV_H, WKV_N, WKV_N), F32),
        ],
        scratch_shapes=[
            pltpu.VMEM((n_chunks, n_pairs, CHUNK, LANES), BF16),
            pltpu.VMEM((n_chunks, n_pairs, 2, LANES, LANES), BF16),
            pltpu.VMEM((n_chunks, n_pairs, CHUNK, LANES), F32),
            pltpu.VMEM((n_chunks, 8, width), F32),
            pltpu.VMEM((WKV_GROUP, CHUNK, width), F32),
            pltpu.VMEM((WKV_GROUP, 6, CHUNK, width), BF16),
        ],
        compiler_params=pltpu.CompilerParams(
            dimension_semantics=("arbitrary", "arbitrary", "arbitrary"), vmem_limit_bytes=VMEM_LIMIT),
        name="wkv_scan_%d" % seq_len,
    )(p_all, p_all, p_all, lr, lr, w2pad, a2pad, w0, a0, k_k, k_a, *init)


def _dot2(x, w_bf):
    hi, lo = _split2(x)
    return _dot(hi, w_bf) + _dot(lo, w_bf)


def _group_stat(x, ind_ref, bind_ref, split_bcast=False):
    per_group = _dot(x.astype(BF16), ind_ref[...])
    if split_bcast:
        return _dot2(per_group, bind_ref[...])
    return _dot(per_group.astype(BF16), bind_ref[...])


N_CTX_TILES = N_CTX_TOK // TM_OUT


def _sum_dirs(i, oc_ref, os_ref):
    is_ctx = i < N_CTX_TILES
    return jnp.where(is_ctx, oc_ref[0], os_ref[0]) + jnp.where(is_ctx, oc_ref[1], os_ref[1])


def _ctx_tile(i):
    return (0, jnp.minimum(i, N_CTX_TILES - 1), 0)


def _smp_tile(i):
    return (0, jnp.maximum(i - N_CTX_TILES, 0), 0)


def _even_out_kernel(xc_ref, xs_ref, mod_ref, pc_ref, pw_ref, oc_ref, os_ref, lra_ref, a2_ref, a0_ref, ka_ref,
                     rk_ref, cw_ref, lnw_ref, lnb_ref, ind_ref, bind_ref, wo_ref, out_ref):
    i = pl.program_id(0)
    u = pc_ref[:, 0:D].astype(F32)
    gb = pc_ref[:, D:2 * D].astype(F32)
    gc = pc_ref[:, 2 * D:3 * D].astype(F32)
    zc = pc_ref[:, 3 * D:4 * D].astype(F32)
    xg = gc * u
    row_len = jnp.where(i < N_CTX_TILES, CTX_T, GRID_W)
    pos = lax.broadcasted_iota(jnp.int32, (TM_OUT, 1), 0) & (row_len - 1)
    prev = jnp.where(pos == 0, 0.0, pltpu.roll(xg, 1, 0))
    nxt = jnp.where(pos == row_len - 1, 0.0, pltpu.roll(xg, TM_OUT - 1, 0))
    conv = cw_ref[0:1, :] * prev + cw_ref[1:2, :] * xg + cw_ref[2:3, :] * nxt
    o_conv = _silu(zc) * gb * conv
    r = pw_ref[:, 0:D].astype(F32)
    k = pw_ref[:, D:2 * D].astype(F32)
    v = pw_ref[:, 2 * D:3 * D].astype(F32)
    zw = pw_ref[:, 3 * D:4 * D].astype(F32)
    o = _sum_dirs(i, oc_ref, os_ref)
    mu = _group_stat(o, ind_ref, bind_ref, split_bcast=True)
    dlt = o - mu
    var = _group_stat(dlt * dlt, ind_ref, bind_ref)
    gn = dlt * lax.rsqrt(var + GN_EPS) * lnw_ref[...] + lnb_ref[...]
    lra = lra_ref[...]
    ic0 = jax.nn.sigmoid(a0_ref[0] + _bdot(lra, a2_ref[0]))
    ic1 = jax.nn.sigmoid(a0_ref[1] + _bdot(lra, a2_ref[1]))
    ksum = k * (2.0 + (ic0 + ic1 - 2.0) * ka_ref[...])
    bonus = _group_stat(r * ksum * rk_ref[...], ind_ref, bind_ref) * float(WKV_N)
    o_wkv = (gn + bonus * v) * _silu(zw)
    y = _dot(o_conv.astype(BF16), wo_ref[0:D, :]) + _dot(o_wkv.astype(BF16), wo_ref[D:2 * D, :])
    gate = mod_ref[0, 2:3, :]
    out_ref[...] = jnp.where(i < N_CTX_TILES, xc_ref[...], xs_ref[...]) + gate * y


def _even_out(x_ctx, x_smp, mod3, p_all, o_ctx, o_smp, lr, a2pad, a0, k_a, r_k, conv_w, ln_w, ln_b, ind, bind,
              wo_bf):
    row = lambda i: (i, 0)
    const2 = lambda i: (0, 0)
    const3 = lambda i: (0, 0, 0)
    return pl.pallas_call(
        _even_out_kernel,
        grid=(N_TOK // TM_OUT,),
        in_specs=[
            pl.BlockSpec((TM_OUT, D), lambda i: _ctx_tile(i)[1:]),
            pl.BlockSpec((TM_OUT, D), lambda i: _smp_tile(i)[1:]),
            pl.BlockSpec((1, 3, D), lambda i: (_tile_row(i, TM_OUT), 0, 0)),
            pl.BlockSpec((TM_OUT, 4 * D), lambda i: (i, 0)),
            pl.BlockSpec((TM_OUT, 4 * D), lambda i: (i, 1)),
            pl.BlockSpec((2, TM_OUT, D), _ctx_tile),
            pl.BlockSpec((2, TM_OUT, D), _smp_tile),
            pl.BlockSpec((TM_OUT, LANES), lambda i: (i, 1)),
            pl.BlockSpec((2, LANES, D), const3),
            pl.BlockSpec((2, 1, D), const3),
            pl.BlockSpec((1, D), const2),
            pl.BlockSpec((1, D), const2),
            pl.BlockSpec((3, D), const2),
            pl.BlockSpec((1, D), const2),
            pl.BlockSpec((1, D), const2),
            pl.BlockSpec((D, LANES), const2),
            pl.BlockSpec((LANES, D), const2),
            pl.BlockSpec((2 * D, D), const2),
        ],
        out_specs=pl.BlockSpec((TM_OUT, D), row),
        out_shape=jax.ShapeDtypeStruct((N_TOK, D), F32),
        compiler_params=pltpu.CompilerParams(
            dimension_semantics=("arbitrary",), vmem_limit_bytes=VMEM_LIMIT),
        name="even_out",
    )(x_ctx, x_smp, mod3, p_all, p_all, o_ctx, o_smp, lr, a2pad, a0, k_a, r_k, conv_w, ln_w, ln_b, ind, bind, wo_bf)


def _gla_kernel(q_ref, k_ref, v_ref, lrg_ref, gk2_ref, gkb_ref, m_ref, *rest, seq_len, n_seg, zero_init):
    s0_ref = None if zero_init else rest[0]
    o_ref, sout_ref, qe_scr, st_scr, e_scr = rest[-5:]
    z = pl.program_id(0)
    n_chunks = seq_len // CHUNK
    seg_chunks = n_chunks // n_seg
    group = min(GLA_GROUP, n_chunks)
    n_groups = n_chunks // group
    incl_b = m_ref[0] > 0.5
    gk2 = gk2_ref[0].astype(BF16)
    gkb = gkb_ref[0]
    scale = GLA_DK ** -0.5

    def phase1(it, carry):
        cs = [it * group + j for j in range(group)]
        rows = [_chunk_rows(c) for c in cs]
        logit = [_bdot(lrg_ref[rw, :], gk2) + gkb for rw in rows]
        g = [jax.nn.log_sigmoid(x) / GLA_GATE_NORM for x in logit]
        bc, b_last = zip(*[_scan_cumsum(x, z) for x in g])
        k = [k_ref[rw, :].astype(F32) for rw in rows]
        qe = [q_ref[rw, :].astype(F32) * scale * jnp.exp(x) for rw, x in zip(rows, bc)]
        ke = [kc * jnp.exp(-x) for kc, x in zip(k, bc)]
        kd = [kc * jnp.exp(bl - x) for kc, x, bl in zip(k, bc, b_last)]
        v = [v_ref[rw, :] for rw in rows]
        att = [jnp.where(incl_b, _bdot_nt(a, b), 0.0) for a, b in zip(qe, ke)]
        o_in = [_bdot(a, b) for a, b in zip(att, v)]
        vtk = [_bdot_tn(a, b) for a, b in zip(v, kd)]
        for j in range(group):
            o_ref[0, rows[j], :] = o_in[j]
            qe_scr[cs[j]] = qe[j]
            st_scr[cs[j]] = vtk[j]
            e_scr[cs[j]] = jnp.broadcast_to(jnp.exp(b_last[j]), (8, GLA_DK))
        return carry

    lax.fori_loop(0, n_groups, phase1, 0)

    def phase2(pos, sts):
        cidx = [sq * seg_chunks + pos + z * (seg_chunks - 1 - 2 * pos) for sq in range(n_seg)]
        new = [st * e_scr[c][0:1, :] + st_scr[c] for st, c in zip(sts, cidx)]
        for st, c in zip(sts, cidx):
            st_scr[c] = st
        return tuple(new)

    if zero_init:
        st_init = tuple(jnp.zeros((GLA_DV, GLA_DK), F32) for sq in range(n_seg))
    else:
        st_init = tuple(s0_ref[sq, 0, 0].T for sq in range(n_seg))
    st_fin = lax.fori_loop(0, seg_chunks, phase2, st_init)
    for sq in range(n_seg):
        sout_ref[sq, 0, 0] = st_fin[sq].T

    def phase3(it, carry):
        cs = [it * group + j for j in range(group)]
        inter = [_bdot_nt(qe_scr[c], st_scr[c]) for c in cs]
        for j in range(group):
            rows = _chunk_rows(cs[j])
            o_ref[0, rows, :] = o_ref[0, rows, :] + inter[j]
        return carry

    lax.fori_loop(0, n_groups, phase3, 0)


def _gla_scan(p_all, lrg, gk2pad, gkb, s0, masks, *, n_seq, seq_len, n_seg, tok_blk0):
    def tok(b):
        return tok_blk0 + b

    rows = n_seg * seq_len
    n_chunks = rows // CHUNK
    state_spec = pl.BlockSpec((n_seg, 1, 1, GLA_DK, GLA_DV), lambda z, b, h: (b, z, h, 0, 0))
    init = () if s0 is None else (s0,)
    return pl.pallas_call(
        functools.partial(_gla_kernel, seq_len=rows, n_seg=n_seg, zero_init=s0 is None),
        grid=(2, n_seq // n_seg, GLA_H),
        in_specs=[
            pl.BlockSpec((rows, GLA_DK), lambda z, b, h: (tok(b), h)),
            pl.BlockSpec((rows, GLA_DK), lambda z, b, h: (tok(b), GLA_H + h)),
            pl.BlockSpec((rows, GLA_DV), lambda z, b, h: (tok(b), GLA_H + h)),
            pl.BlockSpec((rows, LANES), lambda z, b, h: (tok(b), 0)),
            pl.BlockSpec((1, LANES, GLA_DK), lambda z, b, h: (z, 0, h)),
            pl.BlockSpec((1, 1, GLA_DK), lambda z, b, h: (z, 0, h)),
            pl.BlockSpec((1, CHUNK, CHUNK), lambda z, b, h: (z, 0, 0)),
        ] + [state_spec] * len(init),
        out_specs=[
            pl.BlockSpec((1, rows, GLA_DV), lambda z, b, h: (z, b, h)),
            state_spec,
        ],
        out_shape=[
            jax.ShapeDtypeStruct((2, n_seq * seq_len, D), F32),
            jax.ShapeDtypeStruct((n_seq, 2, GLA_H, GLA_DK, GLA_DV), F32),
        ],
        scratch_shapes=[
            pltpu.VMEM((n_chunks, CHUNK, GLA_DK), F32),
            pltpu.VMEM((n_chunks, GLA_DV, GLA_DK), F32),
            pltpu.VMEM((n_chunks, 8, GLA_DK), F32),
        ],
        compiler_params=pltpu.CompilerParams(
            dimension_semantics=("arbitrary", "arbitrary", "arbitrary"), vmem_limit_bytes=VMEM_LIMIT),
        name="gla_scan_%d" % seq_len,
    )(p_all, p_all, p_all, lrg, gk2pad, gkb, masks, *init)


def _odd_out_kernel(x_ref, mod_ref, zg_ref, oc_ref, os_ref, gn_ref, ind_ref, bind_ref, wo_ref, fg_ref,
                    yc_ref, ys_ref):
    i = pl.program_id(0)
    o = _sum_dirs(i, oc_ref, os_ref)
    ms = _group_stat(o * o, ind_ref, bind_ref)
    on = o * lax.rsqrt(ms + NORM_EPS) * gn_ref[...]
    og = on * _silu(zg_ref[...].astype(F32))
    y = _dot(og.astype(BF16), wo_ref[...])
    gate = mod_ref[0, 2:3, :]
    x = x_ref[...] + gate * y
    out = x * lax.rsqrt(jnp.mean(x * x, axis=-1, keepdims=True) + NORM_EPS) * fg_ref[...]

    @pl.when(i < N_CTX_TILES)
    def _():
        yc_ref[...] = out

    @pl.when(i >= N_CTX_TILES)
    def _():
        ys_ref[...] = out


def _odd_out(x, mod3, p_all, o_ctx, o_smp, g_norm, ind, bind, wo_bf, final_g):
    row = lambda i: (i, 0)
    const2 = lambda i: (0, 0)
    return pl.pallas_call(
        _odd_out_kernel,
        grid=(N_TOK // TM_OUT,),
        in_specs=[
            pl.BlockSpec((TM_OUT, D), row),
            pl.BlockSpec((1, 3, D), lambda i: (_tile_row(i, TM_OUT), 0, 0)),
            pl.BlockSpec((TM_OUT, D), lambda i: (i, 2)),
            pl.BlockSpec((2, TM_OUT, D), _ctx_tile),
            pl.BlockSpec((2, TM_OUT, D), _smp_tile),
            pl.BlockSpec((1, D), const2),
            pl.BlockSpec((D, LANES), const2),
            pl.BlockSpec((LANES, D), const2),
            pl.BlockSpec((D, D), const2),
            pl.BlockSpec((1, D), const2),
        ],
        out_specs=[
            pl.BlockSpec((TM_OUT, D), lambda i: _ctx_tile(i)[1:]),
            pl.BlockSpec((TM_OUT, D), lambda i: _smp_tile(i)[1:]),
        ],
        out_shape=[
            jax.ShapeDtypeStruct((N_CTX_TOK, D), F32),
            jax.ShapeDtypeStruct((N_TOK - N_CTX_TOK, D), F32),
        ],
        compiler_params=pltpu.CompilerParams(
            dimension_semantics=("arbitrary",), vmem_limit_bytes=VMEM_LIMIT),
        name="odd_out",
    )(x, mod3, p_all, o_ctx, o_smp, g_norm, ind, bind, wo_bf, final_g)


def _group_indicators(group):
    ch = jnp.arange(D) // group
    lane = jnp.arange(LANES)
    hit = (ch[:, None] == lane[None, :]).astype(F32)
    return (hit / float(group)).astype(BF16), hit.T.astype(BF16)


def _pad_dirs(w):
    r = w.shape[1]
    out = jnp.zeros((2, LANES, w.shape[2]), F32)
    out = out.at[0, 0:r].set(w[0])
    out = out.at[1, r:2 * r].set(w[1])
    return out


def kernel(x_prompt, x_sample, state_wkv, state_gla, c, c_ctx, norm_g, ada_w, ada_b, final_g, e_w_in, e_w_out, conv_w, wkv_w0, wkv_w1, wkv_w2, wkv_a0, wkv_a1, wkv_a2, wkv_k_k, wkv_k_a, wkv_r_k, wkv_ln_w, wkv_ln_b, o_w_in, o_w_out, gla_gk1, gla_gk2, gla_gk_b, gla_g_norm):
    x_ctx = x_prompt.reshape(N_CTX_TOK, D)
    x_smp = x_sample.reshape(N_TOK - N_CTX_TOK, D)
    cvec = jnp.zeros((8, D), F32).at[0].set(c_ctx).at[1:1 + SMP_B].set(c)
    mod = _modulation(cvec, ada_w, ada_b)
    masks = _chunk_masks()

    mod3 = mod[0].reshape(8, 3, D)
    wlr = jnp.concatenate([wkv_w1[0, 0], wkv_w1[0, 1], wkv_a1[0, 0], wkv_a1[0, 1]], axis=1)
    p_e, lr_e = _inproj(x_ctx, x_smp, 0, mod3, norm_g[0:1], e_w_in[0].astype(BF16), wlr.astype(BF16),
                        n_tanh=2 * 64)
    w2pad = _pad_dirs(wkv_w2[0])
    a2pad = _pad_dirs(wkv_a2[0])
    w0 = wkv_w0[0].reshape(2, 1, D)
    a0 = wkv_a0[0].reshape(2, 1, D)
    k_k = wkv_k_k[0].reshape(1, D)
    k_a = wkv_k_a[0].reshape(1, D)
    r_k = wkv_r_k[0].reshape(1, D)
    o_ctx, new_wkv = _wkv_scan(p_e, lr_e, w2pad, a2pad, w0, a0, k_k, k_a, None,
                               n_seq=CTX_B, seq_len=CTX_T, n_seg=WKV_CTX_SEG, tok_blk0=0)
    o_smp, _ = _wkv_scan(p_e, lr_e, w2pad, a2pad, w0, a0, k_k, k_a, state_wkv[:, 0],
                         n_seq=SMP_B, seq_len=SMP_T, n_seg=1, tok_blk0=N_CTX_TOK // SMP_T)
    ind64, bind64 = _group_indicators(WKV_N)
    x = _even_out(x_ctx, x_smp, mod3, p_e, o_ctx, o_smp, lr_e, a2pad, a0, k_a, r_k, conv_w[0],
                  wkv_ln_w[0].reshape(1, D), wkv_ln_b[0].reshape(1, D), ind64, bind64,
                  e_w_out[0].astype(BF16))

    mod3 = mod[1].reshape(8, 3, D)
    rank = gla_gk1.shape[-1]
    wlr = jnp.zeros((D, LANES), F32).at[:, 0:rank].set(gla_gk1[0, 0]).at[:, rank:2 * rank].set(gla_gk1[0, 1])
    p_o, lr_o = _inproj(x, x, N_CTX_TOK // TM_PROJ, mod3, norm_g[1:2], o_w_in[0].astype(BF16),
                        wlr.astype(BF16), n_tanh=0)
    gk2pad = _pad_dirs(gla_gk2[0])
    gkb = gla_gk_b[0].reshape(2, 1, GLA_H * GLA_DK)
    g_ctx, new_gla = _gla_scan(p_o, lr_o, gk2pad, gkb, None, masks,
                               n_seq=CTX_B, seq_len=CTX_T, n_seg=GLA_CTX_SEG, tok_blk0=0)
    g_smp, _ = _gla_scan(p_o, lr_o, gk2pad, gkb, state_gla[:, 0], masks,
                         n_seq=SMP_B, seq_len=SMP_T, n_seg=1, tok_blk0=N_CTX_TOK // SMP_T)
    ind256, bind256 = _group_indicators(GLA_DV)
    y_ctx, y_smp = _odd_out(x, mod3, p_o, g_ctx, g_smp, jnp.tile(gla_g_norm[0], GLA_H).reshape(1, D),
                            ind256, bind256, o_w_out[0].astype(BF16), final_g.reshape(1, D))
    return (y_ctx.reshape(CTX_B, CTX_T, D), y_smp.reshape(SMP_B, SMP_T, D), new_wkv[:, None], new_gla[:, None])
```

```python
import functools
import math

import jax
import jax.numpy as jnp
from jax import lax
from jax.experimental import pallas as pl
from jax.experimental.pallas import tpu as pltpu

F32 = jnp.float32
BF16 = jnp.bfloat16
HI = lax.Precision.HIGHEST

D = 1024
N_CTX_TOK = 16 * 256
N_TOK = 2 * N_CTX_TOK
CTX_B, CTX_T = 16, 256
SMP_B, SMP_T = 2, 2048
GRID_W = 64
WKV_H, WKV_N = 16, 64
GLA_H, GLA_DK, GLA_DV = 4, 128, 256
GLA_GATE_NORM = 16.0
GN_EPS = 64e-5
NORM_EPS = 1e-6
CHUNK = 64
WKV_GROUP = 4
WKV_HEADS_STEP = 8
WKV_CTX_SEG = 4
GLA_CTX_SEG = 4
GLA_GROUP = 8
LANES = 128
TM_PROJ = 1024
TN_PROJ_MAX = 2048
TM_OUT = 256
VMEM_LIMIT = 56 * 1024 * 1024


def _silu(x):
    return x * jax.nn.sigmoid(x)


def _dot(a, b, precision=None):
    return jnp.dot(a, b, preferred_element_type=F32, precision=precision)


def _dot_nt(a, b, precision=None):
    return lax.dot_general(a, b, (((1,), (1,)), ((), ())), preferred_element_type=F32, precision=precision)


def _dot_tn(a, b, precision=None):
    return lax.dot_general(a, b, (((0,), (0,)), ((), ())), preferred_element_type=F32, precision=precision)


def _bdot(a, b):
    return _dot(a.astype(BF16), b.astype(BF16))


def _bdot_nt(a, b):
    return _dot_nt(a.astype(BF16), b.astype(BF16))


def _bdot_tn(a, b):
    return _dot_tn(a.astype(BF16), b.astype(BF16))


def _split2(x):
    hi = x.astype(BF16)
    lo = (x - hi.astype(F32)).astype(BF16)
    return hi, lo


def _scan_cumsum(x, z):
    n = x.shape[0]
    row = lax.broadcasted_iota(jnp.int32, (n, 1), 0)
    pre = x
    s = 1
    while s < n:
        pre = pre + jnp.where(row >= s, pltpu.roll(pre, s, 0), 0.0)
        s *= 2
    tot = pre[n - 1:n, :]
    return jnp.where(z == 0, pre, tot - pre + x), tot


def _tile_row(i, tile):
    ctx_tiles = N_CTX_TOK // tile
    per_req = SMP_T // tile
    return jnp.where(i < ctx_tiles, 0, 1 + (i - ctx_tiles) // per_req)


def _mod_kernel(c_ref, w_ref, b_ref, o_ref):
    cf = _silu(c_ref[...])
    o_ref[0] = _dot(cf, w_ref[0], HI) + b_ref[0]


def _modulation(cvec, ada_w, ada_b):
    depth = ada_w.shape[0]
    return pl.pallas_call(
        _mod_kernel,
        grid=(depth, 3),
        in_specs=[
            pl.BlockSpec((8, D), lambda l, j: (0, 0)),
            pl.BlockSpec((1, D, D), lambda l, j: (l, 0, j)),
            pl.BlockSpec((1, 1, D), lambda l, j: (l, 0, j)),
        ],
        out_specs=pl.BlockSpec((1, 8, D), lambda l, j: (l, 0, j)),
        out_shape=jax.ShapeDtypeStruct((depth, 8, 3 * D), F32),
        compiler_params=pltpu.CompilerParams(
            dimension_semantics=("arbitrary", "arbitrary"), vmem_limit_bytes=VMEM_LIMIT),
        name="adaln_mod",
    )(cvec, ada_w, ada_b.reshape(depth, 1, 3 * D))


def _inproj_kernel(xa_ref, xb_ref, mod_ref, g_ref, w_ref, wlr_ref, p_ref, lr_ref, h_scr, *, n_tanh):
    i = pl.program_id(0)
    j = pl.program_id(1)

    @pl.when(j == 0)
    def _():
        x = jnp.where(i < N_CTX_TOK // TM_PROJ, xa_ref[...], xb_ref[...])
        y = x * lax.rsqrt(jnp.mean(x * x, axis=-1, keepdims=True) + NORM_EPS) * g_ref[...]
        shift = mod_ref[0, 0:1, :]
        scale = mod_ref[0, 1:2, :]
        h = y * (1.0 + scale) + shift
        hb = h.astype(BF16)
        h_scr[...] = hb
        lr = _dot(hb, wlr_ref[...])
        if n_tanh:
            lane = lax.broadcasted_iota(jnp.int32, lr.shape, 1)
            lr = jnp.where(lane < n_tanh, jnp.tanh(lr), lr)
        lr_ref[...] = lr

    p_ref[...] = _dot(h_scr[...], w_ref[...]).astype(p_ref.dtype)


def _inproj(x_ctx, x_smp, smp_tile0, mod3, g, w_bf, wlr_bf, n_tanh):
    n_out = w_bf.shape[1]
    n_lr = wlr_bf.shape[1]
    n_ctx = N_CTX_TOK // TM_PROJ
    tn = max(t for t in range(256, TN_PROJ_MAX + 1, 256) if n_out % t == 0)
    return pl.pallas_call(
        functools.partial(_inproj_kernel, n_tanh=n_tanh),
        grid=(N_TOK // TM_PROJ, n_out // tn),
        in_specs=[
            pl.BlockSpec((TM_PROJ, D), lambda i, j: (jnp.minimum(i, n_ctx - 1), 0)),
            pl.BlockSpec((TM_PROJ, D), lambda i, j: (smp_tile0 + jnp.maximum(i - n_ctx, 0), 0)),
            pl.BlockSpec((1, 3, D), lambda i, j: (_tile_row(i, TM_PROJ), 0, 0)),
            pl.BlockSpec((1, D), lambda i, j: (0, 0)),
            pl.BlockSpec((D, tn), lambda i, j: (0, j)),
            pl.BlockSpec((D, n_lr), lambda i, j: (0, 0)),
        ],
        out_specs=[
            pl.BlockSpec((TM_PROJ, tn), lambda i, j: (i, j)),
            pl.BlockSpec((TM_PROJ, n_lr), lambda i, j: (i, 0)),
        ],
        out_shape=[
            jax.ShapeDtypeStruct((N_TOK, n_out), BF16),
            jax.ShapeDtypeStruct((N_TOK, n_lr), F32),
        ],
        scratch_shapes=[pltpu.VMEM((TM_PROJ, D), BF16)],
        compiler_params=pltpu.CompilerParams(
            dimension_semantics=("arbitrary", "arbitrary"), vmem_limit_bytes=VMEM_LIMIT),
        name="inproj",
    )(x_ctx, x_smp, mod3, g, w_bf, wlr_bf)


def _chunk_masks():
    idx = jnp.arange(CHUNK)
    lower = (idx[:, None] >= idx[None, :]).astype(F32)
    return jnp.stack([lower, lower.T], axis=0)


def _chunk_rows(c):
    return pl.ds(pl.multiple_of(c * CHUNK, CHUNK), CHUNK)


def _pair_blockdiag(x):
    left = lax.broadcasted_iota(jnp.int32, x.shape, 1) < WKV_N
    xb = x.astype(BF16)
    zero = jnp.zeros_like(xb)
    return jnp.concatenate([jnp.where(left, xb, zero), jnp.where(left, zero, xb)], axis=0)


def _pair_mm(a, b):
    return _dot(a.astype(BF16), _pair_blockdiag(b))


def _inv_unit_tri(lmats):
    row = lax.broadcasted_iota(jnp.int32, (CHUNK, LANES), 0)
    col = lax.broadcasted_iota(jnp.int32, (CHUNK, LANES), 1) & (CHUNK - 1)
    eye = (row == col).astype(F32)

    def same_block(n):
        sh = n.bit_length() - 1
        return lax.shift_right_logical(row, sh) == lax.shift_right_logical(col, sh)

    blk16 = same_block(16)
    l16 = [jnp.where(blk16, m, 0.0) for m in lmats]
    x = [eye - m for m in l16]
    p = [_pair_mm(m, m) for m in l16]
    for step in range(3):
        x = [xi + _pair_mm(xi, pi) for xi, pi in zip(x, p)]
        if step < 2:
            p = [_pair_mm(pi, pi) for pi in p]
    n = 16
    while n < CHUNK:
        off = same_block(2 * n) & jnp.logical_not(same_block(n))
        xl = [_pair_mm(xi, jnp.where(off, m, 0.0)) for xi, m in zip(x, lmats)]
        x = [xi - _pair_mm(yi, xi) for xi, yi in zip(x, xl)]
        n *= 2
    return x


_KT, _RT, _KH, _BH, _KP, _BP = range(6)


def _wkv_kernel(r_ref, k_ref, v_ref, lrw_ref, lra_ref, w2_ref, a2_ref, w0_ref, a0_ref,
                kk_ref, ka_ref, *rest, seq_len, n_seg, zero_init):
    s0_ref = None if zero_init else rest[0]
    o_ref, sout_ref, rp_scr, w_scr, h_scr, d_scr, rt_a, ops_a = rest[-8:]
    z = pl.program_id(0)
    n_chunks = seq_len // CHUNK
    seg_chunks = n_chunks // n_seg
    n_groups = n_chunks // WKV_GROUP
    pairs = range(WKV_HEADS_STEP // 2)
    row = lax.broadcasted_iota(jnp.int32, (CHUNK, LANES), 0)
    lane = lax.broadcasted_iota(jnp.int32, (CHUNK, LANES), 1)
    col = lane & (CHUNK - 1)
    left = lane < WKV_N
    incl2 = ((z == 0) & (row >= col)) | ((z == 1) & (row <= col))
    strict2 = incl2 & (row != col)
    row_b = lax.broadcasted_iota(jnp.int32, (LANES, LANES), 0) < WKV_N
    lane_b = lax.broadcasted_iota(jnp.int32, (LANES, LANES), 1) < WKV_N
    diag_blocks = row_b == lane_b
    zeros_bf = jnp.zeros((CHUNK, LANES), BF16)
    w2 = w2_ref[0].astype(BF16)
    a2 = a2_ref[0].astype(BF16)
    w0 = w0_ref[0]
    a0 = a0_ref[0]
    k_k = kk_ref[...]
    k_a = ka_ref[...]

    def pair_sl(p):
        return slice(p * LANES, (p + 1) * LANES)

    def stack(a, b):
        return jnp.concatenate([a, b], axis=0)

    def prepare_chunk(gi, j, rt_scr, ops_scr):
        c = gi * WKV_GROUP + j
        rows = _chunk_rows(c)
        w_raw = w0 + _bdot(lrw_ref[rows, :], w2)
        a = jax.nn.sigmoid(a0 + _bdot(lra_ref[rows, :], a2))
        lw = -math.exp(-0.5) * jax.nn.sigmoid(w_raw)
        g, gtot = _scan_cumsum(lw, z)
        r = r_ref[rows, :].astype(F32)
        k = k_ref[rows, :].astype(F32)
        kkr = k * k_k
        sq = kkr * kkr
        nrm = []
        for p in pairs:
            sq_p = sq[:, pair_sl(p)]
            n0 = jnp.sum(jnp.where(left, sq_p, 0.0), axis=-1, keepdims=True)
            n1 = jnp.sum(jnp.where(left, 0.0, sq_p), axis=-1, keepdims=True)
            nrm.append(jnp.where(left, jnp.sqrt(n0), jnp.sqrt(n1)))
        kk = kkr / jnp.maximum(jnp.concatenate(nrm, axis=-1), 1e-12)
        b = kk * a
        kmod = k * (1.0 + (a - 1.0) * k_a)
        e_ng = jnp.exp(-g)
        e_gc = jnp.exp(gtot - g)
        rt = r * jnp.exp(g)
        rt_scr[j] = rt
        ops_scr[j, _KT] = (kk * jnp.exp(g - lw)).astype(BF16)
        ops_scr[j, _RT] = rt.astype(BF16)
        ops_scr[j, _KH] = (kmod * e_ng).astype(BF16)
        ops_scr[j, _BH] = (b * e_ng).astype(BF16)
        ops_scr[j, _KP] = (kmod * e_gc).astype(BF16)
        ops_scr[j, _BP] = (b * e_gc).astype(BF16)
        d_scr[c] = jnp.broadcast_to(jnp.exp(gtot), (8, WKV_HEADS_STEP * WKV_N))

    def chains(gi, rt_scr, ops_scr):
        cs = [gi * WKV_GROUP + j for j in range(WKV_GROUP)]
        rows = [_chunk_rows(c) for c in cs]
        cps = [(j, p) for j in range(WKV_GROUP) for p in pairs]
        n_cp = range(len(cps))

        def opnd(q, i):
            j, p = cps[q]
            return ops_scr[j, i, :, pair_sl(p)]

        def vals(q):
            j, p = cps[q]
            return v_ref[rows[j], pair_sl(p)]

        bk = [stack(_pair_blockdiag(opnd(q, _BH)), _pair_blockdiag(opnd(q, _KH))) for q in n_cp]
        g_k = [_dot_nt(opnd(q, _KT), bk[q]) for q in n_cp]
        g_r = [_dot_nt(opnd(q, _RT), bk[q]) for q in n_cp]
        lmat = [jnp.where(strict2, g[:, :LANES], 0.0) for g in g_k]
        mkk = [jnp.where(strict2, g[:, LANES:], 0.0) for g in g_k]
        mrb = [jnp.where(incl2, g[:, :LANES], 0.0).astype(BF16) for g in g_r]
        mrk = [jnp.where(incl2, g[:, LANES:], 0.0).astype(BF16) for g in g_r]
        tinv = _inv_unit_tri(lmat)
        mv = [_pair_mm(mkk[q], vals(q)) for q in n_cp]
        pm = [_pair_mm(tinv[q], opnd(q, _KT)).astype(BF16) for q in n_cp]
        qm = [_pair_mm(tinv[q], mv[q]).astype(BF16) for q in n_cp]
        corr_p = [_dot(mrb[q], _pair_blockdiag(pm[q])) for q in n_cp]
        op_m = [_dot(jnp.concatenate([mrk[q], mrb[q]], axis=-1),
                     stack(_pair_blockdiag(vals(q)), -_pair_blockdiag(qm[q]))) for q in n_cp]
        wfull = [_dot_tn(pm[q], opnd(q, _BP)) for q in n_cp]
        hfull = [_dot_tn(stack(vals(q), qm[q]), stack(opnd(q, _KP), -opnd(q, _BP))) for q in n_cp]
        for q in n_cp:
            j, p = cps[q]
            o_ref[0, rows[j], pair_sl(p)] = op_m[q]
            rp_scr[cs[j], p] = (rt_scr[j, :, pair_sl(p)] - corr_p[q]).astype(BF16)
            w_hi, w_lo = _split2(jnp.where(diag_blocks, wfull[q], 0.0))
            w_scr[cs[j], p, 0] = w_hi
            w_scr[cs[j], p, 1] = w_lo
            h_scr[cs[j], p] = jnp.where(left, hfull[q][:WKV_N], hfull[q][WKV_N:])

    def phase1(gi, carry):
        for j in range(WKV_GROUP):
            prepare_chunk(gi, j, rt_a, ops_a)
        chains(gi, rt_a, ops_a)
        return carry

    lax.fori_loop(0, n_groups, phase1, 0)

    def phase2(pos, carry):
        chains2 = [(sq, p) for sq in range(n_seg) for p in pairs]
        cidx = [sq * seg_chunks + pos + z * (seg_chunks - 1 - 2 * pos) for sq in range(n_seg)]
        split = [_split2(s) for s in carry]
        sw_hi = [_dot(stack(split[i][0], split[i][1]), w_scr[cidx[sq], p, 0]) for i, (sq, p) in enumerate(chains2)]
        sw_lo = [_dot(split[i][0], w_scr[cidx[sq], p, 1]) for i, (sq, p) in enumerate(chains2)]
        s_bd = [stack(jnp.where(left, hi, zeros_bf), jnp.where(left, zeros_bf, hi)) for hi, lo in split]
        inter = [_dot_nt(rp_scr[cidx[sq], p], s_bd[i]) for i, (sq, p) in enumerate(chains2)]
        new_s = []
        for i, (sq, p) in enumerate(chains2):
            d_tot = d_scr[cidx[sq]][0:1, pair_sl(p)]
            new_s.append(carry[i] * d_tot - (sw_hi[i][:CHUNK] + sw_hi[i][CHUNK:] + sw_lo[i]) + h_scr[cidx[sq], p])
        for sq in range(n_seg):
            rows = _chunk_rows(cidx[sq])
            o_ref[0, rows, :] = o_ref[0, rows, :] + jnp.concatenate(
                [inter[sq * len(pairs) + p] for p in pairs], axis=-1)
        return tuple(new_s)

    if zero_init:
        s_init = tuple(jnp.zeros((WKV_N, LANES), F32) for sq in range(n_seg) for p in pairs)
    else:
        s_init = tuple(jnp.concatenate([s0_ref[sq, 0, 2 * p], s0_ref[sq, 0, 2 * p + 1]], axis=-1)
                       for sq in range(n_seg) for p in pairs)
    s_fin = lax.fori_loop(0, seg_chunks, phase2, s_init)
    for sq in range(n_seg):
        for p in pairs:
            s_pair = s_fin[sq * len(pairs) + p]
            sout_ref[sq, 0, 2 * p] = s_pair[:, :WKV_N]
            sout_ref[sq, 0, 2 * p + 1] = s_pair[:, WKV_N:]


def _wkv_scan(p_all, lr, w2pad, a2pad, w0, a0, k_k, k_a, s0, *, n_seq, seq_len, n_seg, tok_blk0):
    def tok(b):
        return tok_blk0 + b

    rows = n_seg * seq_len
    width = WKV_HEADS_STEP * WKV_N
    n_pairs = WKV_HEADS_STEP // 2
    col = lambda base: (lambda z, b, hq: (tok(b), base // width + hq))
    vec = lambda z, b, hq: (z, 0, hq)
    n_chunks = rows // CHUNK
    state_spec = pl.BlockSpec((n_seg, 1, WKV_HEADS_STEP, WKV_N, WKV_N), lambda z, b, hq: (b, z, hq, 0, 0))
    init = () if s0 is None else (s0,)
    return pl.pallas_call(
        functools.partial(_wkv_kernel, seq_len=rows, n_seg=n_seg, zero_init=s0 is None),
        grid=(2, n_seq // n_seg, WKV_H // WKV_HEADS_STEP),
        in_specs=[
            pl.BlockSpec((rows, width), col(4 * D)),
            pl.BlockSpec((rows, width), col(5 * D)),
            pl.BlockSpec((rows, width), col(6 * D)),
            pl.BlockSpec((rows, LANES), lambda z, b, hq: (tok(b), 0)),
            pl.BlockSpec((rows, LANES), lambda z, b, hq: (tok(b), 1)),
            pl.BlockSpec((1, LANES, width), vec),
            pl.BlockSpec((1, LANES, width), vec),
            pl.BlockSpec((1, 1, width), vec),
            pl.BlockSpec((1, 1, width), vec),
            pl.BlockSpec((1, width), lambda z, b, hq: (0, hq)),
            pl.BlockSpec((1, width), lambda z, b, hq: (0, hq)),
        ] + [state_spec] * len(init),
        out_specs=[
            pl.BlockSpec((1, rows, width), lambda z, b, hq: (z, b, hq)),
            state_spec,
        ],
        out_shape=[
            jax.ShapeDtypeStruct((2, n_seq * seq_len, D), F32),
            jax.ShapeDtypeStruct((n_seq, 2, WKV_H, WKV_N, WKV_N), F32),
        ],
        scratch_shapes=[
            pltpu.VMEM((n_chunks, n_pairs, CHUNK, LANES), BF16),
            pltpu.VMEM((n_chunks, n_pairs, 2, LANES, LANES), BF16),
            pltpu.VMEM((n_chunks, n_pairs, CHUNK, LANES), F32),
            pltpu.VMEM((n_chunks, 8, width), F32),
            pltpu.VMEM((WKV_GROUP, CHUNK, width), F32),
            pltpu.VMEM((WKV_GROUP, 6, CHUNK, width), BF16),
        ],
        compiler_params=pltpu.CompilerParams(
            dimension_semantics=("arbitrary", "arbitrary", "arbitrary"), vmem_limit_bytes=VMEM_LIMIT),
        name="wkv_scan_%d" % seq_len,
    )(p_all, p_all, p_all, lr, lr, w2pad, a2pad, w0, a0, k_k, k_a, *init)


def _dot2(x, w_bf):
    hi, lo = _split2(x)
    return _dot(hi, w_bf) + _dot(lo, w_bf)


def _group_stat(x, ind_ref, bind_ref, split_bcast=False):
    per_group = _dot(x.astype(BF16), ind_ref[...])
    if split_bcast:
        return _dot2(per_group, bind_ref[...])
    return _dot(per_group.astype(BF16), bind_ref[...])


N_CTX_TILES = N_CTX_TOK // TM_OUT


def _sum_dirs(i, oc_ref, os_ref):
    is_ctx = i < N_CTX_TILES
    return jnp.where(is_ctx, oc_ref[0], os_ref[0]) + jnp.where(is_ctx, oc_ref[1], os_ref[1])


def _ctx_tile(i):
    return (0, jnp.minimum(i, N_CTX_TILES - 1), 0)


def _smp_tile(i):
    return (0, jnp.maximum(i - N_CTX_TILES, 0), 0)


def _even_out_kernel(xc_ref, xs_ref, mod_ref, pc_ref, pw_ref, oc_ref, os_ref, lra_ref, a2_ref, a0_ref, ka_ref,
                     rk_ref, cw_ref, lnw_ref, lnb_ref, ind_ref, bind_ref, wo_ref, out_ref):
    i = pl.program_id(0)
    u = pc_ref[:, 0:D].astype(F32)
    gb = pc_ref[:, D:2 * D].astype(F32)
    gc = pc_ref[:, 2 * D:3 * D].astype(F32)
    zc = pc_ref[:, 3 * D:4 * D].astype(F32)
    xg = gc * u
    row_len = jnp.where(i < N_CTX_TILES, CTX_T, GRID_W)
    pos = lax.broadcasted_iota(jnp.int32, (TM_OUT, 1), 0) & (row_len - 1)
    prev = jnp.where(pos == 0, 0.0, pltpu.roll(xg, 1, 0))
    nxt = jnp.where(pos == row_len - 1, 0.0, pltpu.roll(xg, TM_OUT - 1, 0))
    conv = cw_ref[0:1, :] * prev + cw_ref[1:2, :] * xg + cw_ref[2:3, :] * nxt
    o_conv = _silu(zc) * gb * conv
    r = pw_ref[:, 0:D].astype(F32)
    k = pw_ref[:, D:2 * D].astype(F32)
    v = pw_ref[:, 2 * D:3 * D].astype(F32)
    zw = pw_ref[:, 3 * D:4 * D].astype(F32)
    o = _sum_dirs(i, oc_ref, os_ref)
    mu = _group_stat(o, ind_ref, bind_ref, split_bcast=True)
    dlt = o - mu
    var = _group_stat(dlt * dlt, ind_ref, bind_ref)
    gn = dlt * lax.rsqrt(var + GN_EPS) * lnw_ref[...] + lnb_ref[...]
    lra = lra_ref[...]
    ic0 = jax.nn.sigmoid(a0_ref[0] + _bdot(lra, a2_ref[0]))
    ic1 = jax.nn.sigmoid(a0_ref[1] + _bdot(lra, a2_ref[1]))
    ksum = k * (2.0 + (ic0 + ic1 - 2.0) * ka_ref[...])
    bonus = _group_stat(r * ksum * rk_ref[...], ind_ref, bind_ref) * float(WKV_N)
    o_wkv = (gn + bonus * v) * _silu(zw)
    y = _dot(o_conv.astype(BF16), wo_ref[0:D, :]) + _dot(o_wkv.astype(BF16), wo_ref[D:2 * D, :])
    gate = mod_ref[0, 2:3, :]
    out_ref[...] = jnp.where(i < N_CTX_TILES, xc_ref[...], xs_ref[...]) + gate * y


def _even_out(x_ctx, x_smp, mod3, p_all, o_ctx, o_smp, lr, a2pad, a0, k_a, r_k, conv_w, ln_w, ln_b, ind, bind,
              wo_bf):
    row = lambda i: (i, 0)
    const2 = lambda i: (0, 0)
    const3 = lambda i: (0, 0, 0)
    return pl.pallas_call(
        _even_out_kernel,
        grid=(N_TOK // TM_OUT,),
        in_specs=[
            pl.BlockSpec((TM_OUT, D), lambda i: _ctx_tile(i)[1:]),
            pl.BlockSpec((TM_OUT, D), lambda i: _smp_tile(i)[1:]),
            pl.BlockSpec((1, 3, D), lambda i: (_tile_row(i, TM_OUT), 0, 0)),
            pl.BlockSpec((TM_OUT, 4 * D), lambda i: (i, 0)),
            pl.BlockSpec((TM_OUT, 4 * D), lambda i: (i, 1)),
            pl.BlockSpec((2, TM_OUT, D), _ctx_tile),
            pl.BlockSpec((2, TM_OUT, D), _smp_tile),
            pl.BlockSpec((TM_OUT, LANES), lambda i: (i, 1)),
            pl.BlockSpec((2, LANES, D), const3),
            pl.BlockSpec((2, 1, D), const3),
            pl.BlockSpec((1, D), const2),
            pl.BlockSpec((1, D), const2),
            pl.BlockSpec((3, D), const2),
            pl.BlockSpec((1, D), const2),
            pl.BlockSpec((1, D), const2),
            pl.BlockSpec((D, LANES), const2),
            pl.BlockSpec((LANES, D), const2),
            pl.BlockSpec((2 * D, D), const2),
        ],
        out_specs=pl.BlockSpec((TM_OUT, D), row),
        out_shape=jax.ShapeDtypeStruct((N_TOK, D), F32),
        compiler_params=pltpu.CompilerParams(
            dimension_semantics=("arbitrary",), vmem_limit_bytes=VMEM_LIMIT),
        name="even_out",
    )(x_ctx, x_smp, mod3, p_all, p_all, o_ctx, o_smp, lr, a2pad, a0, k_a, r_k, conv_w, ln_w, ln_b, ind, bind, wo_bf)


def _gla_kernel(q_ref, k_ref, v_ref, lrg_ref, gk2_ref, gkb_ref, m_ref, *rest, seq_len, n_seg, zero_init):
    s0_ref = None if zero_init else rest[0]
    o_ref, sout_ref, qe_scr, st_scr, e_scr = rest[-5:]
    z = pl.program_id(0)
    n_chunks = seq_len // CHUNK
    seg_chunks = n_chunks // n_seg
    group = min(GLA_GROUP, n_chunks)
    n_groups = n_chunks // group
    incl_b = m_ref[0] > 0.5
    gk2 = gk2_ref[0].astype(BF16)
    gkb = gkb_ref[0]
    scale = GLA_DK ** -0.5

    def phase1(it, carry):
        cs = [it * group + j for j in range(group)]
        rows = [_chunk_rows(c) for c in cs]
        logit = [_bdot(lrg_ref[rw, :], gk2) + gkb for rw in rows]
        g = [jax.nn.log_sigmoid(x) / GLA_GATE_NORM for x in logit]
        bc, b_last = zip(*[_scan_cumsum(x, z) for x in g])
        k = [k_ref[rw, :].astype(F32) for rw in rows]
        qe = [q_ref[rw, :].astype(F32) * scale * jnp.exp(x) for rw, x in zip(rows, bc)]
        ke = [kc * jnp.exp(-x) for kc, x in zip(k, bc)]
        kd = [kc * jnp.exp(bl - x) for kc, x, bl in zip(k, bc, b_last)]
        v = [v_ref[rw, :] for rw in rows]
        att = [jnp.where(incl_b, _bdot_nt(a, b), 0.0) for a, b in zip(qe, ke)]
        o_in = [_bdot(a, b) for a, b in zip(att, v)]
        vtk = [_bdot_tn(a, b) for a, b in zip(v, kd)]
        for j in range(group):
            o_ref[0, rows[j], :] = o_in[j]
            qe_scr[cs[j]] = qe[j]
            st_scr[cs[j]] = vtk[j]
            e_scr[cs[j]] = jnp.broadcast_to(jnp.exp(b_last[j]), (8, GLA_DK))
        return carry

    lax.fori_loop(0, n_groups, phase1, 0)

    def phase2(pos, sts):
        cidx = [sq * seg_chunks + pos + z * (seg_chunks - 1 - 2 * pos) for sq in range(n_seg)]
        new = [st * e_scr[c][0:1, :] + st_scr[c] for st, c in zip(sts, cidx)]
        for st, c in zip(sts, cidx):
            st_scr[c] = st
        return tuple(new)

    if zero_init:
        st_init = tuple(jnp.zeros((GLA_DV, GLA_DK), F32) for sq in range(n_seg))
    else:
        st_init = tuple(s0_ref[sq, 0, 0].T for sq in range(n_seg))
    st_fin = lax.fori_loop(0, seg_chunks, phase2, st_init)
    for sq in range(n_seg):
        sout_ref[sq, 0, 0] = st_fin[sq].T

    def phase3(it, carry):
        cs = [it * group + j for j in range(group)]
        inter = [_bdot_nt(qe_scr[c], st_scr[c]) for c in cs]
        for j in range(group):
            rows = _chunk_rows(cs[j])
            o_ref[0, rows, :] = o_ref[0, rows, :] + inter[j]
        return carry

    lax.fori_loop(0, n_groups, phase3, 0)


def _gla_scan(p_all, lrg, gk2pad, gkb, s0, masks, *, n_seq, seq_len, n_seg, tok_blk0):
    def tok(b):
        return tok_blk0 + b

    rows = n_seg * seq_len
    n_chunks = rows // CHUNK
    state_spec = pl.BlockSpec((n_seg, 1, 1, GLA_DK, GLA_DV), lambda z, b, h: (b, z, h, 0, 0))
    init = () if s0 is None else (s0,)
    return pl.pallas_call(
        functools.partial(_gla_kernel, seq_len=rows, n_seg=n_seg, zero_init=s0 is None),
        grid=(2, n_seq // n_seg, GLA_H),
        in_specs=[
            pl.BlockSpec((rows, GLA_DK), lambda z, b, h: (tok(b), h)),
            pl.BlockSpec((rows, GLA_DK), lambda z, b, h: (tok(b), GLA_H + h)),
            pl.BlockSpec((rows, GLA_DV), lambda z, b, h: (tok(b), GLA_H + h)),
            pl.BlockSpec((rows, LANES), lambda z, b, h: (tok(b), 0)),
            pl.BlockSpec((1, LANES, GLA_DK), lambda z, b, h: (z, 0, h)),
            pl.BlockSpec((1, 1, GLA_DK), lambda z, b, h: (z, 0, h)),
            pl.BlockSpec((1, CHUNK, CHUNK), lambda z, b, h: (z, 0, 0)),
        ] + [state_spec] * len(init),
        out_specs=[
            pl.BlockSpec((1, rows, GLA_DV), lambda z, b, h: (z, b, h)),
            state_spec,
        ],
        out_shape=[
            jax.ShapeDtypeStruct((2, n_seq * seq_len, D), F32),
            jax.ShapeDtypeStruct((n_seq, 2, GLA_H, GLA_DK, GLA_DV), F32),
        ],
        scratch_shapes=[
            pltpu.VMEM((n_chunks, CHUNK, GLA_DK), F32),
            pltpu.VMEM((n_chunks, GLA_DV, GLA_DK), F32),
            pltpu.VMEM((n_chunks, 8, GLA_DK), F32),
        ],
        compiler_params=pltpu.CompilerParams(
            dimension_semantics=("arbitrary", "arbitrary", "arbitrary"), vmem_limit_bytes=VMEM_LIMIT),
        name="gla_scan_%d" % seq_len,
    )(p_all, p_all, p_all, lrg, gk2pad, gkb, masks, *init)


def _odd_out_kernel(x_ref, mod_ref, zg_ref, oc_ref, os_ref, gn_ref, ind_ref, bind_ref, wo_ref, fg_ref,
                    yc_ref, ys_ref):
    i = pl.program_id(0)
    o = _sum_dirs(i, oc_ref, os_ref)
    ms = _group_stat(o * o, ind_ref, bind_ref)
    on = o * lax.rsqrt(ms + NORM_EPS) * gn_ref[...]
    og = on * _silu(zg_ref[...].astype(F32))
    y = _dot(og.astype(BF16), wo_ref[...])
    gate = mod_ref[0, 2:3, :]
    x = x_ref[...] + gate * y
    out = x * lax.rsqrt(jnp.mean(x * x, axis=-1, keepdims=True) + NORM_EPS) * fg_ref[...]

    @pl.when(i < N_CTX_TILES)
    def _():
        yc_ref[...] = out

    @pl.when(i >= N_CTX_TILES)
    def _():
        ys_ref[...] = out


def _odd_out(x, mod3, p_all, o_ctx, o_smp, g_norm, ind, bind, wo_bf, final_g):
    row = lambda i: (i, 0)
    const2 = lambda i: (0, 0)
    return pl.pallas_call(
        _odd_out_kernel,
        grid=(N_TOK // TM_OUT,),
        in_specs=[
            pl.BlockSpec((TM_OUT, D), row),
            pl.BlockSpec((1, 3, D), lambda i: (_tile_row(i, TM_OUT), 0, 0)),
            pl.BlockSpec((TM_OUT, D), lambda i: (i, 2)),
            pl.BlockSpec((2, TM_OUT, D), _ctx_tile),
            pl.BlockSpec((2, TM_OUT, D), _smp_tile),
            pl.BlockSpec((1, D), const2),
            pl.BlockSpec((D, LANES), const2),
            pl.BlockSpec((LANES, D), const2),
            pl.BlockSpec((D, D), const2),
            pl.BlockSpec((1, D), const2),
        ],
        out_specs=[
            pl.BlockSpec((TM_OUT, D), lambda i: _ctx_tile(i)[1:]),
            pl.BlockSpec((TM_OUT, D), lambda i: _smp_tile(i)[1:]),
        ],
        out_shape=[
            jax.ShapeDtypeStruct((N_CTX_TOK, D), F32),
            jax.ShapeDtypeStruct((N_TOK - N_CTX_TOK, D), F32),
        ],
        compiler_params=pltpu.CompilerParams(
            dimension_semantics=("arbitrary",), vmem_limit_bytes=VMEM_LIMIT),
        name="odd_out",
    )(x, mod3, p_all, o_ctx, o_smp, g_norm, ind, bind, wo_bf, final_g)


def _group_indicators(group):
    ch = jnp.arange(D) // group
    lane = jnp.arange(LANES)
    hit = (ch[:, None] == lane[None, :]).astype(F32)
    return (hit / float(group)).astype(BF16), hit.T.astype(BF16)


def _pad_dirs(w):
    r = w.shape[1]
    out = jnp.zeros((2, LANES, w.shape[2]), F32)
    out = out.at[0, 0:r].set(w[0])
    out = out.at[1, r:2 * r].set(w[1])
    return out


def kernel(x_prompt, x_sample, state_wkv, state_gla, c, c_ctx, norm_g, ada_w, ada_b, final_g, e_w_in, e_w_out, conv_w, wkv_w0, wkv_w1, wkv_w2, wkv_a0, wkv_a1, wkv_a2, wkv_k_k, wkv_k_a, wkv_r_k, wkv_ln_w, wkv_ln_b, o_w_in, o_w_out, gla_gk1, gla_gk2, gla_gk_b, gla_g_norm):
    x_ctx = x_prompt.reshape(N_CTX_TOK, D)
    x_smp = x_sample.reshape(N_TOK - N_CTX_TOK, D)
    cvec = jnp.zeros((8, D), F32).at[0].set(c_ctx).at[1:1 + SMP_B].set(c)
    mod = _modulation(cvec, ada_w, ada_b)
    masks = _chunk_masks()

    mod3 = mod[0].reshape(8, 3, D)
    wlr = jnp.concatenate([wkv_w1[0, 0], wkv_w1[0, 1], wkv_a1[0, 0], wkv_a1[0, 1]], axis=1)
    p_e, lr_e = _inproj(x_ctx, x_smp, 0, mod3, norm_g[0:1], e_w_in[0].astype(BF16), wlr.astype(BF16),
                        n_tanh=2 * 64)
    w2pad = _pad_dirs(wkv_w2[0])
    a2pad = _pad_dirs(wkv_a2[0])
    w0 = wkv_w0[0].reshape(2, 1, D)
    a0 = wkv_a0[0].reshape(2, 1, D)
    k_k = wkv_k_k[0].reshape(1, D)
    k_a = wkv_k_a[0].reshape(1, D)
    r_k = wkv_r_k[0].reshape(1, D)
    o_ctx, new_wkv = _wkv_scan(p_e, lr_e, w2pad, a2pad, w0, a0, k_k, k_a, None,
                               n_seq=CTX_B, seq_len=CTX_T, n_seg=WKV_CTX_SEG, tok_blk0=0)
    o_smp, _ = _wkv_scan(p_e, lr_e, w2pad, a2pad, w0, a0, k_k, k_a, state_wkv[:, 0],
                         n_seq=SMP_B, seq_len=SMP_T, n_seg=1, tok_blk0=N_CTX_TOK // SMP_T)
    ind64, bind64 = _group_indicators(WKV_N)
    x = _even_out(x_ctx, x_smp, mod3, p_e, o_ctx, o_smp, lr_e, a2pad, a0, k_a, r_k, conv_w[0],
                  wkv_ln_w[0].reshape(1, D), wkv_ln_b[0].reshape(1, D), ind64, bind64,
                  e_w_out[0].astype(BF16))

    mod3 = mod[1].reshape(8, 3, D)
    rank = gla_gk1.shape[-1]
    wlr = jnp.zeros((D, LANES), F32).at[:, 0:rank].set(gla_gk1[0, 0]).at[:, rank:2 * rank].set(gla_gk1[0, 1])
    p_o, lr_o = _inproj(x, x, N_CTX_TOK // TM_PROJ, mod3, norm_g[1:2], o_w_in[0].astype(BF16),
                        wlr.astype(BF16), n_tanh=0)
    gk2pad = _pad_dirs(gla_gk2[0])
    gkb = gla_gk_b[0].reshape(2, 1, GLA_H * GLA_DK)
    g_ctx, new_gla = _gla_scan(p_o, lr_o, gk2pad, gkb, None, masks,
                               n_seq=CTX_B, seq_len=CTX_T, n_seg=GLA_CTX_SEG, tok_blk0=0)
    g_smp, _ = _gla_scan(p_o, lr_o, gk2pad, gkb, state_gla[:, 0], masks,
                         n_seq=SMP_B, seq_len=SMP_T, n_seg=1, tok_blk0=N_CTX_TOK // SMP_T)
    ind256, bind256 = _group_indicators(GLA_DV)
    y_ctx, y_smp = _odd_out(x, mod3, p_o, g_ctx, g_smp, jnp.tile(gla_g_norm[0], GLA_H).reshape(1, D),
                            ind256, bind256, o_w_out[0].astype(BF16), final_g.reshape(1, D))
    return (y_ctx.reshape(CTX_B, CTX_T, D), y_smp.reshape(SMP_B, SMP_T, D), new_wkv[:, None], new_gla[:, None])
```

```python
import functools
import math

import jax
import jax.numpy as jnp
from jax import lax
from jax.experimental import pallas as pl
from jax.experimental.pallas import tpu as pltpu

F32 = jnp.float32
BF16 = jnp.bfloat16
HI = lax.Precision.HIGHEST

D = 1024
N_CTX_TOK = 16 * 256
N_TOK = 2 * N_CTX_TOK
CTX_B, CTX_T = 16, 256
SMP_B, SMP_T = 2, 2048
GRID_W = 64
WKV_H, WKV_N = 16, 64
GLA_H, GLA_DK, GLA_DV = 4, 128, 256
GLA_GATE_NORM = 16.0
GN_EPS = 64e-5
NORM_EPS = 1e-6
CHUNK = 64
WKV_GROUP = 4
WKV_HEADS_STEP = 8
WKV_CTX_SEG = 4
GLA_CTX_SEG = 4
GLA_GROUP = 8
LANES = 128
TM_PROJ = 1024
TN_PROJ_MAX = 2048
TM_OUT = 256
VMEM_LIMIT = 56 * 1024 * 1024


def _silu(x):
    return x * jax.nn.sigmoid(x)


def _dot(a, b, precision=None):
    return jnp.dot(a, b, preferred_element_type=F32, precision=precision)


def _dot_nt(a, b, precision=None):
    return lax.dot_general(a, b, (((1,), (1,)), ((), ())), preferred_element_type=F32, precision=precision)


def _dot_tn(a, b, precision=None):
    return lax.dot_general(a, b, (((0,), (0,)), ((), ())), preferred_element_type=F32, precision=precision)


def _bdot(a, b):
    return _dot(a.astype(BF16), b.astype(BF16))


def _bdot_nt(a, b):
    return _dot_nt(a.astype(BF16), b.astype(BF16))


def _bdot_tn(a, b):
    return _dot_tn(a.astype(BF16), b.astype(BF16))


def _split2(x):
    hi = x.astype(BF16)
    lo = (x - hi.astype(F32)).astype(BF16)
    return hi, lo


def _scan_cumsum(x, z):
    n = x.shape[0]
    row = lax.broadcasted_iota(jnp.int32, (n, 1), 0)
    pre = x
    s = 1
    while s < n:
        pre = pre + jnp.where(row >= s, pltpu.roll(pre, s, 0), 0.0)
        s *= 2
    tot = pre[n - 1:n, :]
    return jnp.where(z == 0, pre, tot - pre + x), tot


def _tile_row(i, tile):
    ctx_tiles = N_CTX_TOK // tile
    per_req = SMP_T // tile
    return jnp.where(i < ctx_tiles, 0, 1 + (i - ctx_tiles) // per_req)


def _mod_kernel(c_ref, w_ref, b_ref, o_ref):
    cf = _silu(c_ref[...])
    o_ref[0] = _dot(cf, w_ref[0], HI) + b_ref[0]


def _modulation(cvec, ada_w, ada_b):
    depth = ada_w.shape[0]
    return pl.pallas_call(
        _mod_kernel,
        grid=(depth, 3),
        in_specs=[
            pl.BlockSpec((8, D), lambda l, j: (0, 0)),
            pl.BlockSpec((1, D, D), lambda l, j: (l, 0, j)),
            pl.BlockSpec((1, 1, D), lambda l, j: (l, 0, j)),
        ],
        out_specs=pl.BlockSpec((1, 8, D), lambda l, j: (l, 0, j)),
        out_shape=jax.ShapeDtypeStruct((depth, 8, 3 * D), F32),
        compiler_params=pltpu.CompilerParams(
            dimension_semantics=("arbitrary", "arbitrary"), vmem_limit_bytes=VMEM_LIMIT),
        name="adaln_mod",
    )(cvec, ada_w, ada_b.reshape(depth, 1, 3 * D))


def _inproj_kernel(xa_ref, xb_ref, mod_ref, g_ref, w_ref, wlr_ref, p_ref, lr_ref, h_scr, *, n_tanh):
    i = pl.program_id(0)
    j = pl.program_id(1)

    @pl.when(j == 0)
    def _():
        x = jnp.where(i < N_CTX_TOK // TM_PROJ, xa_ref[...], xb_ref[...])
        y = x * lax.rsqrt(jnp.mean(x * x, axis=-1, keepdims=True) + NORM_EPS) * g_ref[...]
        shift = mod_ref[0, 0:1, :]
        scale = mod_ref[0, 1:2, :]
        h = y * (1.0 + scale) + shift
        hb = h.astype(BF16)
        h_scr[...] = hb
        lr = _dot(hb, wlr_ref[...])
        if n_tanh:
            lane = lax.broadcasted_iota(jnp.int32, lr.shape, 1)
            lr = jnp.where(lane < n_tanh, jnp.tanh(lr), lr)
        lr_ref[...] = lr

    p_ref[...] = _dot(h_scr[...], w_ref[...]).astype(p_ref.dtype)


def _inproj(x_ctx, x_smp, smp_tile0, mod3, g, w_bf, wlr_bf, n_tanh):
    n_out = w_bf.shape[1]
    n_lr = wlr_bf.shape[1]
    n_ctx = N_CTX_TOK // TM_PROJ
    tn = max(t for t in range(256, TN_PROJ_MAX + 1, 256) if n_out % t == 0)
    return pl.pallas_call(
        functools.partial(_inproj_kernel, n_tanh=n_tanh),
        grid=(N_TOK // TM_PROJ, n_out // tn),
        in_specs=[
            pl.BlockSpec((TM_PROJ, D), lambda i, j: (jnp.minimum(i, n_ctx - 1), 0)),
            pl.BlockSpec((TM_PROJ, D), lambda i, j: (smp_tile0 + jnp.maximum(i - n_ctx, 0), 0)),
            pl.BlockSpec((1, 3, D), lambda i, j: (_tile_row(i, TM_PROJ), 0, 0)),
            pl.BlockSpec((1, D), lambda i, j: (0, 0)),
            pl.BlockSpec((D, tn), lambda i, j: (0, j)),
            pl.BlockSpec((D, n_lr), lambda i, j: (0, 0)),
        ],
        out_specs=[
            pl.BlockSpec((TM_PROJ, tn), lambda i, j: (i, j)),
            pl.BlockSpec((TM_PROJ, n_lr), lambda i, j: (i, 0)),
        ],
        out_shape=[
            jax.ShapeDtypeStruct((N_TOK, n_out), BF16),
            jax.ShapeDtypeStruct((N_TOK, n_lr), F32),
        ],
        scratch_shapes=[pltpu.VMEM((TM_PROJ, D), BF16)],
        compiler_params=pltpu.CompilerParams(
            dimension_semantics=("arbitrary", "arbitrary"), vmem_limit_bytes=VMEM_LIMIT),
        name="inproj",
    )(x_ctx, x_smp, mod3, g, w_bf, wlr_bf)


def _chunk_masks():
    idx = jnp.arange(CHUNK)
    lower = (idx[:, None] >= idx[None, :]).astype(F32)
    return jnp.stack([lower, lower.T], axis=0)


def _chunk_rows(c):
    return pl.ds(pl.multiple_of(c * CHUNK, CHUNK), CHUNK)


def _pair_blockdiag(x):
    left = lax.broadcasted_iota(jnp.int32, x.shape, 1) < WKV_N
    xb = x.astype(BF16)
    zero = jnp.zeros_like(xb)
    return jnp.concatenate([jnp.where(left, xb, zero), jnp.where(left, zero, xb)], axis=0)


def _pair_mm(a, b):
    return _dot(a.astype(BF16), _pair_blockdiag(b))


def _inv_unit_tri(lmats, tick=lambda: None):
    row = lax.broadcasted_iota(jnp.int32, (CHUNK, LANES), 0)
    col = lax.broadcasted_iota(jnp.int32, (CHUNK, LANES), 1) & (CHUNK - 1)
    eye = (row == col).astype(F32)

    def same_block(n):
        sh = n.bit_length() - 1
        return lax.shift_right_logical(row, sh) == lax.shift_right_logical(col, sh)

    blk16 = same_block(16)
    l16 = [jnp.where(blk16, m, 0.0) for m in lmats]
    x = [eye - m for m in l16]
    p = [_pair_mm(m, m) for m in l16]
    for step in range(3):
        x = [xi + _pair_mm(xi, pi) for xi, pi in zip(x, p)]
        if step < 2:
            p = [_pair_mm(pi, pi) for pi in p]
        if step == 0:
            tick()
    n = 16
    while n < CHUNK:
        off = same_block(2 * n) & jnp.logical_not(same_block(n))
        xl = [_pair_mm(xi, jnp.where(off, m, 0.0)) for xi, m in zip(x, lmats)]
        x = [xi - _pair_mm(yi, xi) for xi, yi in zip(x, xl)]
        if n == 16:
            tick()
        n *= 2
    return x


_KT, _RT, _KH, _BH, _KP, _BP = range(6)


def _wkv_kernel(r_ref, k_ref, v_ref, lrw_ref, lra_ref, w2_ref, a2_ref, w0_ref, a0_ref,
                kk_ref, ka_ref, *rest, seq_len, n_seg, zero_init):
    s0_ref = None if zero_init else rest[0]
    o_ref, sout_ref, rp_scr, w_scr, h_scr, d_scr, rt_a, ops_a = rest[-8:]
    z = pl.program_id(0)
    n_chunks = seq_len // CHUNK
    seg_chunks = n_chunks // n_seg
    n_groups = n_chunks // WKV_GROUP
    pairs = range(WKV_HEADS_STEP // 2)
    row = lax.broadcasted_iota(jnp.int32, (CHUNK, LANES), 0)
    lane = lax.broadcasted_iota(jnp.int32, (CHUNK, LANES), 1)
    col = lane & (CHUNK - 1)
    left = lane < WKV_N
    incl2 = ((z == 0) & (row >= col)) | ((z == 1) & (row <= col))
    strict2 = incl2 & (row != col)
    row_b = lax.broadcasted_iota(jnp.int32, (LANES, LANES), 0) < WKV_N
    lane_b = lax.broadcasted_iota(jnp.int32, (LANES, LANES), 1) < WKV_N
    diag_blocks = row_b == lane_b
    w2 = w2_ref[0].astype(BF16)
    a2 = a2_ref[0].astype(BF16)
    w0 = w0_ref[0]
    a0 = a0_ref[0]
    k_k = kk_ref[...]
    k_a = ka_ref[...]

    def pair_sl(p):
        return slice(p * LANES, (p + 1) * LANES)

    def stack(a, b):
        return jnp.concatenate([a, b], axis=0)

    def prepare_chunk(gi, j, rt_scr, ops_scr):
        c = gi * WKV_GROUP + j
        rows = _chunk_rows(c)
        w_raw = w0 + _bdot(lrw_ref[rows, :], w2)
        a = jax.nn.sigmoid(a0 + _bdot(lra_ref[rows, :], a2))
        lw = -math.exp(-0.5) * jax.nn.sigmoid(w_raw)
        g, gtot = _scan_cumsum(lw, z)
        r = r_ref[rows, :].astype(F32)
        k = k_ref[rows, :].astype(F32)
        kkr = k * k_k
        sq = kkr * kkr
        nrm = []
        for p in pairs:
            sq_p = sq[:, pair_sl(p)]
            n0 = jnp.sum(jnp.where(left, sq_p, 0.0), axis=-1, keepdims=True)
            n1 = jnp.sum(jnp.where(left, 0.0, sq_p), axis=-1, keepdims=True)
            nrm.append(jnp.where(left, jnp.sqrt(n0), jnp.sqrt(n1)))
        kk = kkr / jnp.maximum(jnp.concatenate(nrm, axis=-1), 1e-12)
        b = kk * a
        kmod = k * (1.0 + (a - 1.0) * k_a)
        e_ng = jnp.exp(-g)
        e_gc = jnp.exp(gtot - g)
        rt = r * jnp.exp(g)
        rt_scr[j] = rt
        ops_scr[j, _KT] = (kk * jnp.exp(g - lw)).astype(BF16)
        ops_scr[j, _RT] = rt.astype(BF16)
        ops_scr[j, _KH] = (kmod * e_ng).astype(BF16)
        ops_scr[j, _BH] = (b * e_ng).astype(BF16)
        ops_scr[j, _KP] = (kmod * e_gc).astype(BF16)
        ops_scr[j, _BP] = (b * e_gc).astype(BF16)
        d_scr[c] = jnp.broadcast_to(jnp.exp(gtot), (8, WKV_HEADS_STEP * WKV_N))

    def chains(gi, rt_scr, ops_scr, fillers):
        pending = iter(fillers)

        def tick():
            next(pending, lambda: None)()

        cs = [gi * WKV_GROUP + j for j in range(WKV_GROUP)]
        rows = [_chunk_rows(c) for c in cs]
        cps = [(j, p) for j in range(WKV_GROUP) for p in pairs]
        n_cp = range(len(cps))

        def opnd(q, i):
            j, p = cps[q]
            return ops_scr[j, i, :, pair_sl(p)]

        def vals(q):
            j, p = cps[q]
            return v_ref[rows[j], pair_sl(p)]

        bk = [stack(_pair_blockdiag(opnd(q, _BH)), _pair_blockdiag(opnd(q, _KH))) for q in n_cp]
        g_k = [_dot_nt(opnd(q, _KT), bk[q]) for q in n_cp]
        g_r = [_dot_nt(opnd(q, _RT), bk[q]) for q in n_cp]
        lmat = [jnp.where(strict2, g[:, :LANES], 0.0) for g in g_k]
        mkk = [jnp.where(strict2, g[:, LANES:], 0.0) for g in g_k]
        mrb = [jnp.where(incl2, g[:, :LANES], 0.0).astype(BF16) for g in g_r]
        mrk = [jnp.where(incl2, g[:, LANES:], 0.0).astype(BF16) for g in g_r]
        tick()
        tinv = _inv_unit_tri(lmat, tick)
        mv = [_pair_mm(mkk[q], vals(q)) for q in n_cp]
        pm = [_pair_mm(tinv[q], opnd(q, _KT)).astype(BF16) for q in n_cp]
        qm = [_pair_mm(tinv[q], mv[q]).astype(BF16) for q in n_cp]
        tick()
        corr_p = [_dot(mrb[q], _pair_blockdiag(pm[q])) for q in n_cp]
        op_m = [_dot(jnp.concatenate([mrk[q], mrb[q]], axis=-1),
                     stack(_pair_blockdiag(vals(q)), -_pair_blockdiag(qm[q]))) for q in n_cp]
        wfull = [_dot_tn(pm[q], opnd(q, _BP)) for q in n_cp]
        hfull = [_dot_tn(stack(vals(q), qm[q]), stack(opnd(q, _KP), -opnd(q, _BP))) for q in n_cp]
        for q in n_cp:
            j, p = cps[q]
            o_ref[0, rows[j], pair_sl(p)] = op_m[q]
            rp_scr[cs[j], p] = (rt_scr[j, :, pair_sl(p)] - corr_p[q]).astype(BF16)
            w_hi, w_lo = _split2(jnp.where(diag_blocks, wfull[q], 0.0))
            w_scr[cs[j], p, 0] = w_hi
            w_scr[cs[j], p, 1] = w_lo
            h_scr[cs[j], p] = jnp.where(left, hfull[q][:WKV_N], hfull[q][WKV_N:])

        for rest_filler in pending:
            rest_filler()

    groups_per_seq = seg_chunks // WKV_GROUP

    def scan_group(i):
        return i + z * (n_groups - 1 - 2 * i)

    def init_state(sq, p):
        if zero_init:
            return jnp.zeros((WKV_N, LANES), F32)
        return jnp.concatenate([s0_ref[sq, 0, 2 * p], s0_ref[sq, 0, 2 * p + 1]], axis=-1)

    def walk_steps(i, state):
        g = scan_group(i)
        sq = g // groups_per_seq
        first = (i % groups_per_seq) == 0

        def step(jj):
            if jj == 0:
                state[:] = [jnp.where(first, init_state(sq, p), state[p]) for p in pairs]
            c = g * WKV_GROUP + jj + z * (WKV_GROUP - 1 - 2 * jj)
            split = [_split2(sp) for sp in state]
            sw_hi = [_dot(stack(split[p][0], split[p][1]), w_scr[c, p, 0]) for p in pairs]
            sw_lo = [_dot(split[p][0], w_scr[c, p, 1]) for p in pairs]
            inter = [_dot_nt(rp_scr[c, p], _pair_blockdiag(split[p][0])) for p in pairs]
            d_tot = d_scr[c][0:1, :]
            state[:] = [state[p] * d_tot[:, pair_sl(p)] - (sw_hi[p][:CHUNK] + sw_hi[p][CHUNK:] + sw_lo[p])
                        + h_scr[c, p] for p in pairs]
            rows = _chunk_rows(c)
            o_ref[0, rows, :] = o_ref[0, rows, :] + jnp.concatenate(inter, axis=-1)
            if jj == WKV_GROUP - 1:
                for p in pairs:
                    sout_ref[sq, 0, 2 * p] = state[p][:, :WKV_N]
                    sout_ref[sq, 0, 2 * p + 1] = state[p][:, WKV_N:]

        return [functools.partial(step, jj) for jj in range(WKV_GROUP)]

    def group_terms(i, fillers):
        g = scan_group(i)
        for j in range(WKV_GROUP):
            prepare_chunk(g, j, rt_a, ops_a)
        chains(g, rt_a, ops_a, fillers)

    group_terms(0, [])

    def body(i, carry):
        state = list(carry)
        group_terms(i, walk_steps(i - 1, state))
        return tuple(state)

    carry = lax.fori_loop(1, n_groups, body, tuple(jnp.zeros((WKV_N, LANES), F32) for p in pairs))
    state = list(carry)
    for last_step in walk_steps(n_groups - 1, state):
        last_step()


def _wkv_scan(p_all, lr, w2pad, a2pad, w0, a0, k_k, k_a, s0, *, n_seq, seq_len, n_seg, tok_blk0):
    def tok(b):
        return tok_blk0 + b

    rows = n_seg * seq_len
    width = WKV_HEADS_STEP * WKV_N
    n_pairs = WKV_HEADS_STEP // 2
    col = lambda base: (lambda z, b, hq: (tok(b), base // width + hq))
    vec = lambda z, b, hq: (z, 0, hq)
    n_chunks = rows // CHUNK
    state_spec = pl.BlockSpec((n_seg, 1, WKV_HEADS_STEP, WKV_N, WKV_N), lambda z, b, hq: (b, z, hq, 0, 0))
    init = () if s0 is None else (s0,)
    return pl.pallas_call(
        functools.partial(_wkv_kernel, seq_len=rows, n_seg=n_seg, zero_init=s0 is None),
        grid=(2, n_seq // n_seg, WKV_H // WKV_HEADS_STEP),
        in_specs=[
            pl.BlockSpec((rows, width), col(4 * D)),
            pl.BlockSpec((rows, width), col(5 * D)),
            pl.BlockSpec((rows, width), col(6 * D)),
            pl.BlockSpec((rows, LANES), lambda z, b, hq: (tok(b), 0)),
            pl.BlockSpec((rows, LANES), lambda z, b, hq: (tok(b), 1)),
            pl.BlockSpec((1, LANES, width), vec),
            pl.BlockSpec((1, LANES, width), vec),
            pl.BlockSpec((1, 1, width), vec),
            pl.BlockSpec((1, 1, width), vec),
            pl.BlockSpec((1, width), lambda z, b, hq: (0, hq)),
            pl.BlockSpec((1, width), lambda z, b, hq: (0, hq)),
        ] + [state_spec] * len(init),
        out_specs=[
            pl.BlockSpec((1, rows, width), lambda z, b, hq: (z, b, hq)),
            state_spec,
        ],
        out_shape=[
            jax.ShapeDtypeStruct((2, n_seq * seq_len, D), F32),
            jax.ShapeDtypeStruct((n_seq, 2, WKV_H, WKV_N, WKV_N), F32),
        ],
        scratch_shapes=[
            pltpu.VMEM((n_chunks, n_pairs, CHUNK, LANES), BF16),
            pltpu.VMEM((n_chunks, n_pairs, 2, LANES, LANES), BF16),
            pltpu.VMEM((n_chunks, n_pairs, CHUNK, LANES), F32),
            pltpu.VMEM((n_chunks, 8, width), F32),
            pltpu.VMEM((WKV_GROUP, CHUNK, width), F32),
            pltpu.VMEM((WKV_GROUP, 6, CHUNK, width), BF16),
        ],
        compiler_params=pltpu.CompilerParams(
            dimension_semantics=("arbitrary", "arbitrary", "arbitrary"), vmem_limit_bytes=VMEM_LIMIT),
        name="wkv_scan_%d" % seq_len,
    )(p_all, p_all, p_all, lr, lr, w2pad, a2pad, w0, a0, k_k, k_a, *init)


def _dot2(x, w_bf):
    hi, lo = _split2(x)
    return _dot(hi, w_bf) + _dot(lo, w_bf)


def _group_stat(x, ind_ref, bind_ref, split_bcast=False):
    per_group = _dot(x.astype(BF16), ind_ref[...])
    if split_bcast:
        return _dot2(per_group, bind_ref[...])
    return _dot(per_group.astype(BF16), bind_ref[...])


N_CTX_TILES = N_CTX_TOK // TM_OUT


def _sum_dirs(i, oc_ref, os_ref):
    is_ctx = i < N_CTX_TILES
    return jnp.where(is_ctx, oc_ref[0], os_ref[0]) + jnp.where(is_ctx, oc_ref[1], os_ref[1])


def _ctx_tile(i):
    return (0, jnp.minimum(i, N_CTX_TILES - 1), 0)


def _smp_tile(i):
    return (0, jnp.maximum(i - N_CTX_TILES, 0), 0)


def _even_out_kernel(xc_ref, xs_ref, mod_ref, pc_ref, pw_ref, oc_ref, os_ref, lra_ref, a2_ref, a0_ref, ka_ref,
                     rk_ref, cw_ref, lnw_ref, lnb_ref, ind_ref, bind_ref, wo_ref, out_ref):
    i = pl.program_id(0)
    u = pc_ref[:, 0:D].astype(F32)
    gb = pc_ref[:, D:2 * D].astype(F32)
    gc = pc_ref[:, 2 * D:3 * D].astype(F32)
    zc = pc_ref[:, 3 * D:4 * D].astype(F32)
    xg = gc * u
    row_len = jnp.where(i < N_CTX_TILES, CTX_T, GRID_W)
    pos = lax.broadcasted_iota(jnp.int32, (TM_OUT, 1), 0) & (row_len - 1)
    prev = jnp.where(pos == 0, 0.0, pltpu.roll(xg, 1, 0))
    nxt = jnp.where(pos == row_len - 1, 0.0, pltpu.roll(xg, TM_OUT - 1, 0))
    conv = cw_ref[0:1, :] * prev + cw_ref[1:2, :] * xg + cw_ref[2:3, :] * nxt
    o_conv = _silu(zc) * gb * conv
    r = pw_ref[:, 0:D].astype(F32)
    k = pw_ref[:, D:2 * D].astype(F32)
    v = pw_ref[:, 2 * D:3 * D].astype(F32)
    zw = pw_ref[:, 3 * D:4 * D].astype(F32)
    o = _sum_dirs(i, oc_ref, os_ref)
    mu = _group_stat(o, ind_ref, bind_ref, split_bcast=True)
    dlt = o - mu
    var = _group_stat(dlt * dlt, ind_ref, bind_ref)
    gn = dlt * lax.rsqrt(var + GN_EPS) * lnw_ref[...] + lnb_ref[...]
    lra = lra_ref[...]
    ic0 = jax.nn.sigmoid(a0_ref[0] + _bdot(lra, a2_ref[0]))
    ic1 = jax.nn.sigmoid(a0_ref[1] + _bdot(lra, a2_ref[1]))
    ksum = k * (2.0 + (ic0 + ic1 - 2.0) * ka_ref[...])
    bonus = _group_stat(r * ksum * rk_ref[...], ind_ref, bind_ref) * float(WKV_N)
    o_wkv = (gn + bonus * v) * _silu(zw)
    y = _dot(o_conv.astype(BF16), wo_ref[0:D, :]) + _dot(o_wkv.astype(BF16), wo_ref[D:2 * D, :])
    gate = mod_ref[0, 2:3, :]
    out_ref[...] = jnp.where(i < N_CTX_TILES, xc_ref[...], xs_ref[...]) + gate * y


def _even_out(x_ctx, x_smp, mod3, p_all, o_ctx, o_smp, lr, a2pad, a0, k_a, r_k, conv_w, ln_w, ln_b, ind, bind,
              wo_bf):
    row = lambda i: (i, 0)
    const2 = lambda i: (0, 0)
    const3 = lambda i: (0, 0, 0)
    return pl.pallas_call(
        _even_out_kernel,
        grid=(N_TOK // TM_OUT,),
        in_specs=[
            pl.BlockSpec((TM_OUT, D), lambda i: _ctx_tile(i)[1:]),
            pl.BlockSpec((TM_OUT, D), lambda i: _smp_tile(i)[1:]),
            pl.BlockSpec((1, 3, D), lambda i: (_tile_row(i, TM_OUT), 0, 0)),
            pl.BlockSpec((TM_OUT, 4 * D), lambda i: (i, 0)),
            pl.BlockSpec((TM_OUT, 4 * D), lambda i: (i, 1)),
            pl.BlockSpec((2, TM_OUT, D), _ctx_tile),
            pl.BlockSpec((2, TM_OUT, D), _smp_tile),
            pl.BlockSpec((TM_OUT, LANES), lambda i: (i, 1)),
            pl.BlockSpec((2, LANES, D), const3),
            pl.BlockSpec((2, 1, D), const3),
            pl.BlockSpec((1, D), const2),
            pl.BlockSpec((1, D), const2),
            pl.BlockSpec((3, D), const2),
            pl.BlockSpec((1, D), const2),
            pl.BlockSpec((1, D), const2),
            pl.BlockSpec((D, LANES), const2),
            pl.BlockSpec((LANES, D), const2),
            pl.BlockSpec((2 * D, D), const2),
        ],
        out_specs=pl.BlockSpec((TM_OUT, D), row),
        out_shape=jax.ShapeDtypeStruct((N_TOK, D), F32),
        compiler_params=pltpu.CompilerParams(
            dimension_semantics=("arbitrary",), vmem_limit_bytes=VMEM_LIMIT),
        name="even_out",
    )(x_ctx, x_smp, mod3, p_all, p_all, o_ctx, o_smp, lr, a2pad, a0, k_a, r_k, conv_w, ln_w, ln_b, ind, bind, wo_bf)


def _gla_kernel(q_ref, k_ref, v_ref, lrg_ref, gk2_ref, gkb_ref, m_ref, *rest, seq_len, n_seg, zero_init):
    s0_ref = None if zero_init else rest[0]
    o_ref, sout_ref, qe_scr, st_scr, e_scr = rest[-5:]
    z = pl.program_id(0)
    n_chunks = seq_len // CHUNK
    seg_chunks = n_chunks // n_seg
    group = min(GLA_GROUP, n_chunks)
    n_groups = n_chunks // group
    incl_b = m_ref[0] > 0.5
    gk2 = gk2_ref[0].astype(BF16)
    gkb = gkb_ref[0]
    scale = GLA_DK ** -0.5

    def phase1(it, carry):
        cs = [it * group + j for j in range(group)]
        rows = [_chunk_rows(c) for c in cs]
        logit = [_bdot(lrg_ref[rw, :], gk2) + gkb for rw in rows]
        g = [jax.nn.log_sigmoid(x) / GLA_GATE_NORM for x in logit]
        bc, b_last = zip(*[_scan_cumsum(x, z) for x in g])
        k = [k_ref[rw, :].astype(F32) for rw in rows]
        qe = [q_ref[rw, :].astype(F32) * scale * jnp.exp(x) for rw, x in zip(rows, bc)]
        ke = [kc * jnp.exp(-x) for kc, x in zip(k, bc)]
        kd = [kc * jnp.exp(bl - x) for kc, x, bl in zip(k, bc, b_last)]
        v = [v_ref[rw, :] for rw in rows]
        att = [jnp.where(incl_b, _bdot_nt(a, b), 0.0) for a, b in zip(qe, ke)]
        o_in = [_bdot(a, b) for a, b in zip(att, v)]
        vtk = [_bdot_tn(a, b) for a, b in zip(v, kd)]
        for j in range(group):
            o_ref[0, rows[j], :] = o_in[j]
            qe_scr[cs[j]] = qe[j]
            st_scr[cs[j]] = vtk[j]
            e_scr[cs[j]] = jnp.broadcast_to(jnp.exp(b_last[j]), (8, GLA_DK))
        return carry

    lax.fori_loop(0, n_groups, phase1, 0)

    def phase2(pos, sts):
        cidx = [sq * seg_chunks + pos + z * (seg_chunks - 1 - 2 * pos) for sq in range(n_seg)]
        new = [st * e_scr[c][0:1, :] + st_scr[c] for st, c in zip(sts, cidx)]
        for st, c in zip(sts, cidx):
            st_scr[c] = st
        return tuple(new)

    if zero_init:
        st_init = tuple(jnp.zeros((GLA_DV, GLA_DK), F32) for sq in range(n_seg))
    else:
        st_init = tuple(s0_ref[sq, 0, 0].T for sq in range(n_seg))
    st_fin = lax.fori_loop(0, seg_chunks, phase2, st_init)
    for sq in range(n_seg):
        sout_ref[sq, 0, 0] = st_fin[sq].T

    def phase3(it, carry):
        cs = [it * group + j for j in range(group)]
        inter = [_bdot_nt(qe_scr[c], st_scr[c]) for c in cs]
        for j in range(group):
            rows = _chunk_rows(cs[j])
            o_ref[0, rows, :] = o_ref[0, rows, :] + inter[j]
        return carry

    lax.fori_loop(0, n_groups, phase3, 0)


def _gla_scan(p_all, lrg, gk2pad, gkb, s0, masks, *, n_seq, seq_len, n_seg, tok_blk0):
    def tok(b):
        return tok_blk0 + b

    rows = n_seg * seq_len
    n_chunks = rows // CHUNK
    state_spec = pl.BlockSpec((n_seg, 1, 1, GLA_DK, GLA_DV), lambda z, b, h: (b, z, h, 0, 0))
    init = () if s0 is None else (s0,)
    return pl.pallas_call(
        functools.partial(_gla_kernel, seq_len=rows, n_seg=n_seg, zero_init=s0 is None),
        grid=(2, n_seq // n_seg, GLA_H),
        in_specs=[
            pl.BlockSpec((rows, GLA_DK), lambda z, b, h: (tok(b), h)),
            pl.BlockSpec((rows, GLA_DK), lambda z, b, h: (tok(b), GLA_H + h)),
            pl.BlockSpec((rows, GLA_DV), lambda z, b, h: (tok(b), GLA_H + h)),
            pl.BlockSpec((rows, LANES), lambda z, b, h: (tok(b), 0)),
            pl.BlockSpec((1, LANES, GLA_DK), lambda z, b, h: (z, 0, h)),
            pl.BlockSpec((1, 1, GLA_DK), lambda z, b, h: (z, 0, h)),
            pl.BlockSpec((1, CHUNK, CHUNK), lambda z, b, h: (z, 0, 0)),
        ] + [state_spec] * len(init),
        out_specs=[
            pl.BlockSpec((1, rows, GLA_DV), lambda z, b, h: (z, b, h)),
            state_spec,
        ],
        out_shape=[
            jax.ShapeDtypeStruct((2, n_seq * seq_len, D), F32),
            jax.ShapeDtypeStruct((n_seq, 2, GLA_H, GLA_DK, GLA_DV), F32),
        ],
        scratch_shapes=[
            pltpu.VMEM((n_chunks, CHUNK, GLA_DK), F32),
            pltpu.VMEM((n_chunks, GLA_DV, GLA_DK), F32),
            pltpu.VMEM((n_chunks, 8, GLA_DK), F32),
        ],
        compiler_params=pltpu.CompilerParams(
            dimension_semantics=("arbitrary", "arbitrary", "arbitrary"), vmem_limit_bytes=VMEM_LIMIT),
        name="gla_scan_%d" % seq_len,
    )(p_all, p_all, p_all, lrg, gk2pad, gkb, masks, *init)


def _odd_out_kernel(x_ref, mod_ref, zg_ref, oc_ref, os_ref, gn_ref, ind_ref, bind_ref, wo_ref, fg_ref,
                    yc_ref, ys_ref):
    i = pl.program_id(0)
    o = _sum_dirs(i, oc_ref, os_ref)
    ms = _group_stat(o * o, ind_ref, bind_ref)
    on = o * lax.rsqrt(ms + NORM_EPS) * gn_ref[...]
    og = on * _silu(zg_ref[...].astype(F32))
    y = _dot(og.astype(BF16), wo_ref[...])
    gate = mod_ref[0, 2:3, :]
    x = x_ref[...] + gate * y
    out = x * lax.rsqrt(jnp.mean(x * x, axis=-1, keepdims=True) + NORM_EPS) * fg_ref[...]

    @pl.when(i < N_CTX_TILES)
    def _():
        yc_ref[...] = out

    @pl.when(i >= N_CTX_TILES)
    def _():
        ys_ref[...] = out


def _odd_out(x, mod3, p_all, o_ctx, o_smp, g_norm, ind, bind, wo_bf, final_g):
    row = lambda i: (i, 0)
    const2 = lambda i: (0, 0)
    return pl.pallas_call(
        _odd_out_kernel,
        grid=(N_TOK // TM_OUT,),
        in_specs=[
            pl.BlockSpec((TM_OUT, D), row),
            pl.BlockSpec((1, 3, D), lambda i: (_tile_row(i, TM_OUT), 0, 0)),
            pl.BlockSpec((TM_OUT, D), lambda i: (i, 2)),
            pl.BlockSpec((2, TM_OUT, D), _ctx_tile),
            pl.BlockSpec((2, TM_OUT, D), _smp_tile),
            pl.BlockSpec((1, D), const2),
            pl.BlockSpec((D, LANES), const2),
            pl.BlockSpec((LANES, D), const2),
            pl.BlockSpec((D, D), const2),
            pl.BlockSpec((1, D), const2),
        ],
        out_specs=[
            pl.BlockSpec((TM_OUT, D), lambda i: _ctx_tile(i)[1:]),
            pl.BlockSpec((TM_OUT, D), lambda i: _smp_tile(i)[1:]),
        ],
        out_shape=[
            jax.ShapeDtypeStruct((N_CTX_TOK, D), F32),
            jax.ShapeDtypeStruct((N_TOK - N_CTX_TOK, D), F32),
        ],
        compiler_params=pltpu.CompilerParams(
            dimension_semantics=("arbitrary",), vmem_limit_bytes=VMEM_LIMIT),
        name="odd_out",
    )(x, mod3, p_all, o_ctx, o_smp, g_norm, ind, bind, wo_bf, final_g)


def _group_indicators(group):
    ch = jnp.arange(D) // group
    lane = jnp.arange(LANES)
    hit = (ch[:, None] == lane[None, :]).astype(F32)
    return (hit / float(group)).astype(BF16), hit.T.astype(BF16)


def _pad_dirs(w):
    r = w.shape[1]
    out = jnp.zeros((2, LANES, w.shape[2]), F32)
    out = out.at[0, 0:r].set(w[0])
    out = out.at[1, r:2 * r].set(w[1])
    return out


def kernel(x_prompt, x_sample, state_wkv, state_gla, c, c_ctx, norm_g, ada_w, ada_b, final_g, e_w_in, e_w_out, conv_w, wkv_w0, wkv_w1, wkv_w2, wkv_a0, wkv_a1, wkv_a2, wkv_k_k, wkv_k_a, wkv_r_k, wkv_ln_w, wkv_ln_b, o_w_in, o_w_out, gla_gk1, gla_gk2, gla_gk_b, gla_g_norm):
    x_ctx = x_prompt.reshape(N_CTX_TOK, D)
    x_smp = x_sample.reshape(N_TOK - N_CTX_TOK, D)
    cvec = jnp.zeros((8, D), F32).at[0].set(c_ctx).at[1:1 + SMP_B].set(c)
    mod = _modulation(cvec, ada_w, ada_b)
    masks = _chunk_masks()

    mod3 = mod[0].reshape(8, 3, D)
    wlr = jnp.concatenate([wkv_w1[0, 0], wkv_w1[0, 1], wkv_a1[0, 0], wkv_a1[0, 1]], axis=1)
    p_e, lr_e = _inproj(x_ctx, x_smp, 0, mod3, norm_g[0:1], e_w_in[0].astype(BF16), wlr.astype(BF16),
                        n_tanh=2 * 64)
    w2pad = _pad_dirs(wkv_w2[0])
    a2pad = _pad_dirs(wkv_a2[0])
    w0 = wkv_w0[0].reshape(2, 1, D)
    a0 = wkv_a0[0].reshape(2, 1, D)
    k_k = wkv_k_k[0].reshape(1, D)
    k_a = wkv_k_a[0].reshape(1, D)
    r_k = wkv_r_k[0].reshape(1, D)
    o_ctx, new_wkv = _wkv_scan(p_e, lr_e, w2pad, a2pad, w0, a0, k_k, k_a, None,
                               n_seq=CTX_B, seq_len=CTX_T, n_seg=WKV_CTX_SEG, tok_blk0=0)
    o_smp, _ = _wkv_scan(p_e, lr_e, w2pad, a2pad, w0, a0, k_k, k_a, state_wkv[:, 0],
                         n_seq=SMP_B, seq_len=SMP_T, n_seg=1, tok_blk0=N_CTX_TOK // SMP_T)
    ind64, bind64 = _group_indicators(WKV_N)
    x = _even_out(x_ctx, x_smp, mod3, p_e, o_ctx, o_smp, lr_e, a2pad, a0, k_a, r_k, conv_w[0],
                  wkv_ln_w[0].reshape(1, D), wkv_ln_b[0].reshape(1, D), ind64, bind64,
                  e_w_out[0].astype(BF16))

    mod3 = mod[1].reshape(8, 3, D)
    rank = gla_gk1.shape[-1]
    wlr = jnp.zeros((D, LANES), F32).at[:, 0:rank].set(gla_gk1[0, 0]).at[:, rank:2 * rank].set(gla_gk1[0, 1])
    p_o, lr_o = _inproj(x, x, N_CTX_TOK // TM_PROJ, mod3, norm_g[1:2], o_w_in[0].astype(BF16),
                        wlr.astype(BF16), n_tanh=0)
    gk2pad = _pad_dirs(gla_gk2[0])
    gkb = gla_gk_b[0].reshape(2, 1, GLA_H * GLA_DK)
    g_ctx, new_gla = _gla_scan(p_o, lr_o, gk2pad, gkb, None, masks,
                               n_seq=CTX_B, seq_len=CTX_T, n_seg=GLA_CTX_SEG, tok_blk0=0)
    g_smp, _ = _gla_scan(p_o, lr_o, gk2pad, gkb, state_gla[:, 0], masks,
                         n_seq=SMP_B, seq_len=SMP_T, n_seg=1, tok_blk0=N_CTX_TOK // SMP_T)
    ind256, bind256 = _group_indicators(GLA_DV)
    y_ctx, y_smp = _odd_out(x, mod3, p_o, g_ctx, g_smp, jnp.tile(gla_g_norm[0], GLA_H).reshape(1, D),
                            ind256, bind256, o_w_out[0].astype(BF16), final_g.reshape(1, D))
    return (y_ctx.reshape(CTX_B, CTX_T, D), y_smp.reshape(SMP_B, SMP_T, D), new_wkv[:, None], new_gla[:, None])
```

```python
import functools
import math

import jax
import jax.numpy as jnp
from jax import lax
from jax.experimental import pallas as pl
from jax.experimental.pallas import tpu as pltpu

F32 = jnp.float32
BF16 = jnp.bfloat16
HI = lax.Precision.HIGHEST

D = 1024
N_CTX_TOK = 16 * 256
N_TOK = 2 * N_CTX_TOK
CTX_B, CTX_T = 16, 256
SMP_B, SMP_T = 2, 2048
GRID_W = 64
WKV_H, WKV_N = 16, 64
GLA_H, GLA_DK, GLA_DV = 4, 128, 256
GLA_GATE_NORM = 16.0
GN_EPS = 64e-5
NORM_EPS = 1e-6
CHUNK = 64
WKV_GROUP = 4
WKV_HEADS_STEP = 8
WKV_CTX_SEG = 8
GLA_CTX_SEG = 8
GLA_GROUP = 8
LANES = 128
TM_PROJ = 1024
TN_PROJ_MAX = 2048
TM_OUT = 256
VMEM_LIMIT = 56 * 1024 * 1024


def _silu(x):
    return x * jax.nn.sigmoid(x)


def _dot(a, b, precision=None):
    return jnp.dot(a, b, preferred_element_type=F32, precision=precision)


def _dot_nt(a, b, precision=None):
    return lax.dot_general(a, b, (((1,), (1,)), ((), ())), preferred_element_type=F32, precision=precision)


def _dot_tn(a, b, precision=None):
    return lax.dot_general(a, b, (((0,), (0,)), ((), ())), preferred_element_type=F32, precision=precision)


def _bdot(a, b):
    return _dot(a.astype(BF16), b.astype(BF16))


def _bdot_nt(a, b):
    return _dot_nt(a.astype(BF16), b.astype(BF16))


def _bdot_tn(a, b):
    return _dot_tn(a.astype(BF16), b.astype(BF16))


def _split2(x):
    hi = x.astype(BF16)
    lo = (x - hi.astype(F32)).astype(BF16)
    return hi, lo


def _scan_cumsum(x, z):
    n = x.shape[0]
    row = lax.broadcasted_iota(jnp.int32, (n, 1), 0)
    pre = x
    s = 1
    while s < n:
        pre = pre + jnp.where(row >= s, pltpu.roll(pre, s, 0), 0.0)
        s *= 2
    tot = pre[n - 1:n, :]
    return jnp.where(z == 0, pre, tot - pre + x), tot


def _tile_row(i, tile):
    ctx_tiles = N_CTX_TOK // tile
    per_req = SMP_T // tile
    return jnp.where(i < ctx_tiles, 0, 1 + (i - ctx_tiles) // per_req)


def _mod_kernel(c_ref, w_ref, b_ref, o_ref):
    cf = _silu(c_ref[...])
    o_ref[0] = _dot(cf, w_ref[0], HI) + b_ref[0]


def _modulation(cvec, ada_w, ada_b):
    depth = ada_w.shape[0]
    return pl.pallas_call(
        _mod_kernel,
        grid=(depth, 3),
        in_specs=[
            pl.BlockSpec((8, D), lambda l, j: (0, 0)),
            pl.BlockSpec((1, D, D), lambda l, j: (l, 0, j)),
            pl.BlockSpec((1, 1, D), lambda l, j: (l, 0, j)),
        ],
        out_specs=pl.BlockSpec((1, 8, D), lambda l, j: (l, 0, j)),
        out_shape=jax.ShapeDtypeStruct((depth, 8, 3 * D), F32),
        compiler_params=pltpu.CompilerParams(
            dimension_semantics=("arbitrary", "arbitrary"), vmem_limit_bytes=VMEM_LIMIT),
        name="adaln_mod",
    )(cvec, ada_w, ada_b.reshape(depth, 1, 3 * D))


def _inproj_kernel(xa_ref, xb_ref, mod_ref, g_ref, w_ref, wlr_ref, p_ref, lr_ref, h_scr, *, n_tanh):
    i = pl.program_id(0)
    j = pl.program_id(1)

    @pl.when(j == 0)
    def _():
        x = jnp.where(i < N_CTX_TOK // TM_PROJ, xa_ref[...], xb_ref[...])
        y = x * lax.rsqrt(jnp.mean(x * x, axis=-1, keepdims=True) + NORM_EPS) * g_ref[...]
        shift = mod_ref[0, 0:1, :]
        scale = mod_ref[0, 1:2, :]
        h = y * (1.0 + scale) + shift
        hb = h.astype(BF16)
        h_scr[...] = hb
        lr = _dot(hb, wlr_ref[...])
        if n_tanh:
            lane = lax.broadcasted_iota(jnp.int32, lr.shape, 1)
            lr = jnp.where(lane < n_tanh, jnp.tanh(lr), lr)
        lr_ref[...] = lr

    p_ref[...] = _dot(h_scr[...], w_ref[...]).astype(p_ref.dtype)


def _inproj(x_ctx, x_smp, smp_tile0, mod3, g, w_bf, wlr_bf, n_tanh):
    n_out = w_bf.shape[1]
    n_lr = wlr_bf.shape[1]
    n_ctx = N_CTX_TOK // TM_PROJ
    tn = max(t for t in range(256, TN_PROJ_MAX + 1, 256) if n_out % t == 0)
    return pl.pallas_call(
        functools.partial(_inproj_kernel, n_tanh=n_tanh),
        grid=(N_TOK // TM_PROJ, n_out // tn),
        in_specs=[
            pl.BlockSpec((TM_PROJ, D), lambda i, j: (jnp.minimum(i, n_ctx - 1), 0)),
            pl.BlockSpec((TM_PROJ, D), lambda i, j: (smp_tile0 + jnp.maximum(i - n_ctx, 0), 0)),
            pl.BlockSpec((1, 3, D), lambda i, j: (_tile_row(i, TM_PROJ), 0, 0)),
            pl.BlockSpec((1, D), lambda i, j: (0, 0)),
            pl.BlockSpec((D, tn), lambda i, j: (0, j)),
            pl.BlockSpec((D, n_lr), lambda i, j: (0, 0)),
        ],
        out_specs=[
            pl.BlockSpec((TM_PROJ, tn), lambda i, j: (i, j)),
            pl.BlockSpec((TM_PROJ, n_lr), lambda i, j: (i, 0)),
        ],
        out_shape=[
            jax.ShapeDtypeStruct((N_TOK, n_out), BF16),
            jax.ShapeDtypeStruct((N_TOK, n_lr), F32),
        ],
        scratch_shapes=[pltpu.VMEM((TM_PROJ, D), BF16)],
        compiler_params=pltpu.CompilerParams(
            dimension_semantics=("arbitrary", "arbitrary"), vmem_limit_bytes=VMEM_LIMIT),
        name="inproj",
    )(x_ctx, x_smp, mod3, g, w_bf, wlr_bf)


def _chunk_masks():
    idx = jnp.arange(CHUNK)
    lower = (idx[:, None] >= idx[None, :]).astype(F32)
    return jnp.stack([lower, lower.T], axis=0)


def _chunk_rows(c):
    return pl.ds(pl.multiple_of(c * CHUNK, CHUNK), CHUNK)


def _pair_blockdiag(x):
    left = lax.broadcasted_iota(jnp.int32, x.shape, 1) < WKV_N
    xb = x.astype(BF16)
    zero = jnp.zeros_like(xb)
    return jnp.concatenate([jnp.where(left, xb, zero), jnp.where(left, zero, xb)], axis=0)


def _pair_mm(a, b):
    return _dot(a.astype(BF16), _pair_blockdiag(b))


def _inv_unit_tri(lmats, tick=lambda: None):
    row = lax.broadcasted_iota(jnp.int32, (CHUNK, LANES), 0)
    col = lax.broadcasted_iota(jnp.int32, (CHUNK, LANES), 1) & (CHUNK - 1)
    eye = (row == col).astype(F32)

    def same_block(n):
        sh = n.bit_length() - 1
        return lax.shift_right_logical(row, sh) == lax.shift_right_logical(col, sh)

    blk16 = same_block(16)
    l16 = [jnp.where(blk16, m, 0.0) for m in lmats]
    x = [eye - m for m in l16]
    p = [_pair_mm(m, m) for m in l16]
    for step in range(3):
        x = [xi + _pair_mm(xi, pi) for xi, pi in zip(x, p)]
        if step < 2:
            p = [_pair_mm(pi, pi) for pi in p]
        if step == 0:
            tick()
    n = 16
    while n < CHUNK:
        off = same_block(2 * n) & jnp.logical_not(same_block(n))
        xl = [_pair_mm(xi, jnp.where(off, m, 0.0)) for xi, m in zip(x, lmats)]
        x = [xi - _pair_mm(yi, xi) for xi, yi in zip(x, xl)]
        if n == 16:
            tick()
        n *= 2
    return x


_KT, _RT, _KH, _BH, _KP, _BP = range(6)


def _wkv_kernel(r_ref, k_ref, v_ref, lrw_ref, lra_ref, w2_ref, a2_ref, w0_ref, a0_ref,
                kk_ref, ka_ref, *rest, seq_len, n_seg, zero_init):
    s0_ref = None if zero_init else rest[0]
    o_ref, sout_ref, rp_scr, w_scr, h_scr, d_scr, rt_a, ops_a = rest[-8:]
    z = pl.program_id(0)
    n_chunks = seq_len // CHUNK
    seg_chunks = n_chunks // n_seg
    n_groups = n_chunks // WKV_GROUP
    pairs = range(WKV_HEADS_STEP // 2)
    row = lax.broadcasted_iota(jnp.int32, (CHUNK, LANES), 0)
    lane = lax.broadcasted_iota(jnp.int32, (CHUNK, LANES), 1)
    col = lane & (CHUNK - 1)
    left = lane < WKV_N
    incl2 = ((z == 0) & (row >= col)) | ((z == 1) & (row <= col))
    strict2 = incl2 & (row != col)
    row_b = lax.broadcasted_iota(jnp.int32, (LANES, LANES), 0) < WKV_N
    lane_b = lax.broadcasted_iota(jnp.int32, (LANES, LANES), 1) < WKV_N
    diag_blocks = row_b == lane_b
    w2 = w2_ref[0].astype(BF16)
    a2 = a2_ref[0].astype(BF16)
    w0 = w0_ref[0]
    a0 = a0_ref[0]
    k_k = kk_ref[...]
    k_a = ka_ref[...]

    def pair_sl(p):
        return slice(p * LANES, (p + 1) * LANES)

    def stack(a, b):
        return jnp.concatenate([a, b], axis=0)

    def prepare_chunk(gi, j, rt_scr, ops_scr):
        c = gi * WKV_GROUP + j
        rows = _chunk_rows(c)
        w_raw = w0 + _bdot(lrw_ref[rows, :], w2)
        a = jax.nn.sigmoid(a0 + _bdot(lra_ref[rows, :], a2))
        lw = -math.exp(-0.5) * jax.nn.sigmoid(w_raw)
        g, gtot = _scan_cumsum(lw, z)
        r = r_ref[rows, :].astype(F32)
        k = k_ref[rows, :].astype(F32)
        kkr = k * k_k
        sq = kkr * kkr
        nrm = []
        for p in pairs:
            sq_p = sq[:, pair_sl(p)]
            n0 = jnp.sum(jnp.where(left, sq_p, 0.0), axis=-1, keepdims=True)
            n1 = jnp.sum(jnp.where(left, 0.0, sq_p), axis=-1, keepdims=True)
            nrm.append(jnp.where(left, jnp.sqrt(n0), jnp.sqrt(n1)))
        kk = kkr / jnp.maximum(jnp.concatenate(nrm, axis=-1), 1e-12)
        b = kk * a
        kmod = k * (1.0 + (a - 1.0) * k_a)
        e_ng = jnp.exp(-g)
        e_gc = jnp.exp(gtot - g)
        rt = r * jnp.exp(g)
        rt_scr[j] = rt
        ops_scr[j, _KT] = (kk * jnp.exp(g - lw)).astype(BF16)
        ops_scr[j, _RT] = rt.astype(BF16)
        ops_scr[j, _KH] = (kmod * e_ng).astype(BF16)
        ops_scr[j, _BH] = (b * e_ng).astype(BF16)
        ops_scr[j, _KP] = (kmod * e_gc).astype(BF16)
        ops_scr[j, _BP] = (b * e_gc).astype(BF16)
        d_scr[c] = jnp.broadcast_to(jnp.exp(gtot), (8, WKV_HEADS_STEP * WKV_N))

    def chains(gi, rt_scr, ops_scr, fillers):
        pending = iter(fillers)

        def tick():
            next(pending, lambda: None)()

        cs = [gi * WKV_GROUP + j for j in range(WKV_GROUP)]
        rows = [_chunk_rows(c) for c in cs]
        cps = [(j, p) for j in range(WKV_GROUP) for p in pairs]
        n_cp = range(len(cps))

        def opnd(q, i):
            j, p = cps[q]
            return ops_scr[j, i, :, pair_sl(p)]

        def vals(q):
            j, p = cps[q]
            return v_ref[rows[j], pair_sl(p)]

        bk = [stack(_pair_blockdiag(opnd(q, _BH)), _pair_blockdiag(opnd(q, _KH))) for q in n_cp]
        g_k = [_dot_nt(opnd(q, _KT), bk[q]) for q in n_cp]
        g_r = [_dot_nt(opnd(q, _RT), bk[q]) for q in n_cp]
        lmat = [jnp.where(strict2, g[:, :LANES], 0.0) for g in g_k]
        mkk = [jnp.where(strict2, g[:, LANES:], 0.0) for g in g_k]
        mrb = [jnp.where(incl2, g[:, :LANES], 0.0).astype(BF16) for g in g_r]
        mrk = [jnp.where(incl2, g[:, LANES:], 0.0).astype(BF16) for g in g_r]
        tick()
        tinv = _inv_unit_tri(lmat, tick)
        mv = [_pair_mm(mkk[q], vals(q)) for q in n_cp]
        pm = [_pair_mm(tinv[q], opnd(q, _KT)).astype(BF16) for q in n_cp]
        qm = [_pair_mm(tinv[q], mv[q]).astype(BF16) for q in n_cp]
        tick()
        corr_p = [_dot(mrb[q], _pair_blockdiag(pm[q])) for q in n_cp]
        op_m = [_dot(jnp.concatenate([mrk[q], mrb[q]], axis=-1),
                     stack(_pair_blockdiag(vals(q)), -_pair_blockdiag(qm[q]))) for q in n_cp]
        wfull = [_dot_tn(pm[q], opnd(q, _BP)) for q in n_cp]
        hfull = [_dot_tn(stack(vals(q), qm[q]), stack(opnd(q, _KP), -opnd(q, _BP))) for q in n_cp]
        for q in n_cp:
            j, p = cps[q]
            o_ref[0, rows[j], pair_sl(p)] = op_m[q]
            rp_scr[cs[j], p] = (rt_scr[j, :, pair_sl(p)] - corr_p[q]).astype(BF16)
            w_hi, w_lo = _split2(jnp.where(diag_blocks, wfull[q], 0.0))
            w_scr[cs[j], p, 0] = w_hi
            w_scr[cs[j], p, 1] = w_lo
            h_scr[cs[j], p] = jnp.where(left, hfull[q][:WKV_N], hfull[q][WKV_N:])

        for rest_filler in pending:
            rest_filler()

    groups_per_seq = seg_chunks // WKV_GROUP

    def scan_group(i):
        return i + z * (n_groups - 1 - 2 * i)

    def init_state(sq, p):
        if zero_init:
            return jnp.zeros((WKV_N, LANES), F32)
        return jnp.concatenate([s0_ref[sq, 0, 2 * p], s0_ref[sq, 0, 2 * p + 1]], axis=-1)

    def walk_steps(i, state):
        g = scan_group(i)
        sq = g // groups_per_seq
        first = (i % groups_per_seq) == 0

        def step(jj):
            if jj == 0:
                state[:] = [jnp.where(first, init_state(sq, p), state[p]) for p in pairs]
            c = g * WKV_GROUP + jj + z * (WKV_GROUP - 1 - 2 * jj)
            split = [_split2(sp) for sp in state]
            sw_hi = [_dot(stack(split[p][0], split[p][1]), w_scr[c, p, 0]) for p in pairs]
            sw_lo = [_dot(split[p][0], w_scr[c, p, 1]) for p in pairs]
            inter = [_dot_nt(rp_scr[c, p], _pair_blockdiag(split[p][0])) for p in pairs]
            d_tot = d_scr[c][0:1, :]
            state[:] = [state[p] * d_tot[:, pair_sl(p)] - (sw_hi[p][:CHUNK] + sw_hi[p][CHUNK:] + sw_lo[p])
                        + h_scr[c, p] for p in pairs]
            rows = _chunk_rows(c)
            o_ref[0, rows, :] = o_ref[0, rows, :] + jnp.concatenate(inter, axis=-1)
            if jj == WKV_GROUP - 1:
                for p in pairs:
                    sout_ref[sq, 0, 2 * p] = state[p][:, :WKV_N]
                    sout_ref[sq, 0, 2 * p + 1] = state[p][:, WKV_N:]

        return [functools.partial(step, jj) for jj in range(WKV_GROUP)]

    def group_terms(i, fillers):
        g = scan_group(i)
        for j in range(WKV_GROUP):
            prepare_chunk(g, j, rt_a, ops_a)
        chains(g, rt_a, ops_a, fillers)

    group_terms(0, [])

    def body(i, carry):
        state = list(carry)
        group_terms(i, walk_steps(i - 1, state))
        return tuple(state)

    carry = lax.fori_loop(1, n_groups, body, tuple(jnp.zeros((WKV_N, LANES), F32) for p in pairs))
    state = list(carry)
    for last_step in walk_steps(n_groups - 1, state):
        last_step()


def _wkv_scan(p_all, lr, w2pad, a2pad, w0, a0, k_k, k_a, s0, *, n_seq, seq_len, n_seg, tok_blk0):
    def tok(b):
        return tok_blk0 + b

    rows = n_seg * seq_len
    width = WKV_HEADS_STEP * WKV_N
    n_pairs = WKV_HEADS_STEP // 2
    col = lambda base: (lambda z, b, hq: (tok(b), base // width + hq))
    vec = lambda z, b, hq: (z, 0, hq)
    n_chunks = rows // CHUNK
    state_spec = pl.BlockSpec((n_seg, 1, WKV_HEADS_STEP, WKV_N, WKV_N), lambda z, b, hq: (b, z, hq, 0, 0))
    init = () if s0 is None else (s0,)
    return pl.pallas_call(
        functools.partial(_wkv_kernel, seq_len=rows, n_seg=n_seg, zero_init=s0 is None),
        grid=(2, n_seq // n_seg, WKV_H // WKV_HEADS_STEP),
        in_specs=[
            pl.BlockSpec((rows, width), col(4 * D)),
            pl.BlockSpec((rows, width), col(5 * D)),
            pl.BlockSpec((rows, width), col(6 * D)),
            pl.BlockSpec((rows, LANES), lambda z, b, hq: (tok(b), 0)),
            pl.BlockSpec((rows, LANES), lambda z, b, hq: (tok(b), 1)),
            pl.BlockSpec((1, LANES, width), vec),
            pl.BlockSpec((1, LANES, width), vec),
            pl.BlockSpec((1, 1, width), vec),
            pl.BlockSpec((1, 1, width), vec),
            pl.BlockSpec((1, width), lambda z, b, hq: (0, hq)),
            pl.BlockSpec((1, width), lambda z, b, hq: (0, hq)),
        ] + [state_spec] * len(init),
        out_specs=[
            pl.BlockSpec((1, rows, width), lambda z, b, hq: (z, b, hq)),
            state_spec,
        ],
        out_shape=[
            jax.ShapeDtypeStruct((2, n_seq * seq_len, D), F32),
            jax.ShapeDtypeStruct((n_seq, 2, WKV_H, WKV_N, WKV_N), F32),
        ],
        scratch_shapes=[
            pltpu.VMEM((n_chunks, n_pairs, CHUNK, LANES), BF16),
            pltpu.VMEM((n_chunks, n_pairs, 2, LANES, LANES), BF16),
            pltpu.VMEM((n_chunks, n_pairs, CHUNK, LANES), F32),
            pltpu.VMEM((n_chunks, 8, width), F32),
            pltpu.VMEM((WKV_GROUP, CHUNK, width), F32),
            pltpu.VMEM((WKV_GROUP, 6, CHUNK, width), BF16),
        ],
        compiler_params=pltpu.CompilerParams(
            dimension_semantics=("arbitrary", "arbitrary", "arbitrary"), vmem_limit_bytes=VMEM_LIMIT),
        name="wkv_scan_%d" % seq_len,
    )(p_all, p_all, p_all, lr, lr, w2pad, a2pad, w0, a0, k_k, k_a, *init)


def _dot2(x, w_bf):
    hi, lo = _split2(x)
    return _dot(hi, w_bf) + _dot(lo, w_bf)


def _group_stat(x, ind_ref, bind_ref, split_bcast=False):
    per_group = _dot(x.astype(BF16), ind_ref[...])
    if split_bcast:
        return _dot2(per_group, bind_ref[...])
    return _dot(per_group.astype(BF16), bind_ref[...])


N_CTX_TILES = N_CTX_TOK // TM_OUT


def _sum_dirs(i, oc_ref, os_ref):
    is_ctx = i < N_CTX_TILES
    return jnp.where(is_ctx, oc_ref[0], os_ref[0]) + jnp.where(is_ctx, oc_ref[1], os_ref[1])


def _ctx_tile(i):
    return (0, jnp.minimum(i, N_CTX_TILES - 1), 0)


def _smp_tile(i):
    return (0, jnp.maximum(i - N_CTX_TILES, 0), 0)


def _even_out_kernel(xc_ref, xs_ref, mod_ref, pc_ref, pw_ref, oc_ref, os_ref, lra_ref, a2_ref, a0_ref, ka_ref,
                     rk_ref, cw_ref, lnw_ref, lnb_ref, ind_ref, bind_ref, wo_ref, out_ref):
    i = pl.program_id(0)
    u = pc_ref[:, 0:D].astype(F32)
    gb = pc_ref[:, D:2 * D].astype(F32)
    gc = pc_ref[:, 2 * D:3 * D].astype(F32)
    zc = pc_ref[:, 3 * D:4 * D].astype(F32)
    xg = gc * u
    row_len = jnp.where(i < N_CTX_TILES, CTX_T, GRID_W)
    pos = lax.broadcasted_iota(jnp.int32, (TM_OUT, 1), 0) & (row_len - 1)
    prev = jnp.where(pos == 0, 0.0, pltpu.roll(xg, 1, 0))
    nxt = jnp.where(pos == row_len - 1, 0.0, pltpu.roll(xg, TM_OUT - 1, 0))
    conv = cw_ref[0:1, :] * prev + cw_ref[1:2, :] * xg + cw_ref[2:3, :] * nxt
    o_conv = _silu(zc) * gb * conv
    r = pw_ref[:, 0:D].astype(F32)
    k = pw_ref[:, D:2 * D].astype(F32)
    v = pw_ref[:, 2 * D:3 * D].astype(F32)
    zw = pw_ref[:, 3 * D:4 * D].astype(F32)
    o = _sum_dirs(i, oc_ref, os_ref)
    mu = _group_stat(o, ind_ref, bind_ref, split_bcast=True)
    dlt = o - mu
    var = _group_stat(dlt * dlt, ind_ref, bind_ref)
    gn = dlt * lax.rsqrt(var + GN_EPS) * lnw_ref[...] + lnb_ref[...]
    lra = lra_ref[...]
    ic0 = jax.nn.sigmoid(a0_ref[0] + _bdot(lra, a2_ref[0]))
    ic1 = jax.nn.sigmoid(a0_ref[1] + _bdot(lra, a2_ref[1]))
    ksum = k * (2.0 + (ic0 + ic1 - 2.0) * ka_ref[...])
    bonus = _group_stat(r * ksum * rk_ref[...], ind_ref, bind_ref) * float(WKV_N)
    o_wkv = (gn + bonus * v) * _silu(zw)
    y = _dot(o_conv.astype(BF16), wo_ref[0:D, :]) + _dot(o_wkv.astype(BF16), wo_ref[D:2 * D, :])
    gate = mod_ref[0, 2:3, :]
    out_ref[...] = jnp.where(i < N_CTX_TILES, xc_ref[...], xs_ref[...]) + gate * y


def _even_out(x_ctx, x_smp, mod3, p_all, o_ctx, o_smp, lr, a2pad, a0, k_a, r_k, conv_w, ln_w, ln_b, ind, bind,
              wo_bf):
    row = lambda i: (i, 0)
    const2 = lambda i: (0, 0)
    const3 = lambda i: (0, 0, 0)
    return pl.pallas_call(
        _even_out_kernel,
        grid=(N_TOK // TM_OUT,),
        in_specs=[
            pl.BlockSpec((TM_OUT, D), lambda i: _ctx_tile(i)[1:]),
            pl.BlockSpec((TM_OUT, D), lambda i: _smp_tile(i)[1:]),
            pl.BlockSpec((1, 3, D), lambda i: (_tile_row(i, TM_OUT), 0, 0)),
            pl.BlockSpec((TM_OUT, 4 * D), lambda i: (i, 0)),
            pl.BlockSpec((TM_OUT, 4 * D), lambda i: (i, 1)),
            pl.BlockSpec((2, TM_OUT, D), _ctx_tile),
            pl.BlockSpec((2, TM_OUT, D), _smp_tile),
            pl.BlockSpec((TM_OUT, LANES), lambda i: (i, 1)),
            pl.BlockSpec((2, LANES, D), const3),
            pl.BlockSpec((2, 1, D), const3),
            pl.BlockSpec((1, D), const2),
            pl.BlockSpec((1, D), const2),
            pl.BlockSpec((3, D), const2),
            pl.BlockSpec((1, D), const2),
            pl.BlockSpec((1, D), const2),
            pl.BlockSpec((D, LANES), const2),
            pl.BlockSpec((LANES, D), const2),
            pl.BlockSpec((2 * D, D), const2),
        ],
        out_specs=pl.BlockSpec((TM_OUT, D), row),
        out_shape=jax.ShapeDtypeStruct((N_TOK, D), F32),
        compiler_params=pltpu.CompilerParams(
            dimension_semantics=("arbitrary",), vmem_limit_bytes=VMEM_LIMIT),
        name="even_out",
    )(x_ctx, x_smp, mod3, p_all, p_all, o_ctx, o_smp, lr, a2pad, a0, k_a, r_k, conv_w, ln_w, ln_b, ind, bind, wo_bf)


def _gla_kernel(q_ref, k_ref, v_ref, lrg_ref, gk2_ref, gkb_ref, m_ref, *rest, seq_len, n_seg, zero_init):
    s0_ref = None if zero_init else rest[0]
    o_ref, sout_ref, qe_scr, st_scr, e_scr = rest[-5:]
    z = pl.program_id(0)
    n_chunks = seq_len // CHUNK
    seg_chunks = n_chunks // n_seg
    group = min(GLA_GROUP, n_chunks)
    n_groups = n_chunks // group
    incl_b = m_ref[0] > 0.5
    gk2 = gk2_ref[0].astype(BF16)
    gkb = gkb_ref[0]
    scale = GLA_DK ** -0.5

    def phase1(it, carry):
        cs = [it * group + j for j in range(group)]
        rows = [_chunk_rows(c) for c in cs]
        logit = [_bdot(lrg_ref[rw, :], gk2) + gkb for rw in rows]
        g = [jax.nn.log_sigmoid(x) / GLA_GATE_NORM for x in logit]
        bc, b_last = zip(*[_scan_cumsum(x, z) for x in g])
        k = [k_ref[rw, :].astype(F32) for rw in rows]
        qe = [q_ref[rw, :].astype(F32) * scale * jnp.exp(x) for rw, x in zip(rows, bc)]
        ke = [kc * jnp.exp(-x) for kc, x in zip(k, bc)]
        kd = [kc * jnp.exp(bl - x) for kc, x, bl in zip(k, bc, b_last)]
        v = [v_ref[rw, :] for rw in rows]
        att = [jnp.where(incl_b, _bdot_nt(a, b), 0.0) for a, b in zip(qe, ke)]
        o_in = [_bdot(a, b) for a, b in zip(att, v)]
        vtk = [_bdot_tn(a, b) for a, b in zip(v, kd)]
        for j in range(group):
            o_ref[0, rows[j], :] = o_in[j]
            qe_scr[cs[j]] = qe[j]
            st_scr[cs[j]] = vtk[j]
            e_scr[cs[j]] = jnp.broadcast_to(jnp.exp(b_last[j]), (8, GLA_DK))
        return carry

    lax.fori_loop(0, n_groups, phase1, 0)

    def phase2(pos, sts):
        cidx = [sq * seg_chunks + pos + z * (seg_chunks - 1 - 2 * pos) for sq in range(n_seg)]
        new = [st * e_scr[c][0:1, :] + st_scr[c] for st, c in zip(sts, cidx)]
        for st, c in zip(sts, cidx):
            st_scr[c] = st
        return tuple(new)

    if zero_init:
        st_init = tuple(jnp.zeros((GLA_DV, GLA_DK), F32) for sq in range(n_seg))
    else:
        st_init = tuple(s0_ref[sq, 0, 0].T for sq in range(n_seg))
    st_fin = lax.fori_loop(0, seg_chunks, phase2, st_init)
    for sq in range(n_seg):
        sout_ref[sq, 0, 0] = st_fin[sq].T

    def phase3(it, carry):
        cs = [it * group + j for j in range(group)]
        inter = [_bdot_nt(qe_scr[c], st_scr[c]) for c in cs]
        for j in range(group):
            rows = _chunk_rows(cs[j])
            o_ref[0, rows, :] = o_ref[0, rows, :] + inter[j]
        return carry

    lax.fori_loop(0, n_groups, phase3, 0)


def _gla_scan(p_all, lrg, gk2pad, gkb, s0, masks, *, n_seq, seq_len, n_seg, tok_blk0):
    def tok(b):
        return tok_blk0 + b

    rows = n_seg * seq_len
    n_chunks = rows // CHUNK
    state_spec = pl.BlockSpec((n_seg, 1, 1, GLA_DK, GLA_DV), lambda z, b, h: (b, z, h, 0, 0))
    init = () if s0 is None else (s0,)
    return pl.pallas_call(
        functools.partial(_gla_kernel, seq_len=rows, n_seg=n_seg, zero_init=s0 is None),
        grid=(2, n_seq // n_seg, GLA_H),
        in_specs=[
            pl.BlockSpec((rows, GLA_DK), lambda z, b, h: (tok(b), h)),
            pl.BlockSpec((rows, GLA_DK), lambda z, b, h: (tok(b), GLA_H + h)),
            pl.BlockSpec((rows, GLA_DV), lambda z, b, h: (tok(b), GLA_H + h)),
            pl.BlockSpec((rows, LANES), lambda z, b, h: (tok(b), 0)),
            pl.BlockSpec((1, LANES, GLA_DK), lambda z, b, h: (z, 0, h)),
            pl.BlockSpec((1, 1, GLA_DK), lambda z, b, h: (z, 0, h)),
            pl.BlockSpec((1, CHUNK, CHUNK), lambda z, b, h: (z, 0, 0)),
        ] + [state_spec] * len(init),
        out_specs=[
            pl.BlockSpec((1, rows, GLA_DV), lambda z, b, h: (z, b, h)),
            state_spec,
        ],
        out_shape=[
            jax.ShapeDtypeStruct((2, n_seq * seq_len, D), F32),
            jax.ShapeDtypeStruct((n_seq, 2, GLA_H, GLA_DK, GLA_DV), F32),
        ],
        scratch_shapes=[
            pltpu.VMEM((n_chunks, CHUNK, GLA_DK), F32),
            pltpu.VMEM((n_chunks, GLA_DV, GLA_DK), F32),
            pltpu.VMEM((n_chunks, 8, GLA_DK), F32),
        ],
        compiler_params=pltpu.CompilerParams(
            dimension_semantics=("arbitrary", "arbitrary", "arbitrary"), vmem_limit_bytes=VMEM_LIMIT),
        name="gla_scan_%d" % seq_len,
    )(p_all, p_all, p_all, lrg, gk2pad, gkb, masks, *init)


def _odd_out_kernel(x_ref, mod_ref, zg_ref, oc_ref, os_ref, gn_ref, ind_ref, bind_ref, wo_ref, fg_ref,
                    yc_ref, ys_ref):
    i = pl.program_id(0)
    o = _sum_dirs(i, oc_ref, os_ref)
    ms = _group_stat(o * o, ind_ref, bind_ref)
    on = o * lax.rsqrt(ms + NORM_EPS) * gn_ref[...]
    og = on * _silu(zg_ref[...].astype(F32))
    y = _dot(og.astype(BF16), wo_ref[...])
    gate = mod_ref[0, 2:3, :]
    x = x_ref[...] + gate * y
    out = x * lax.rsqrt(jnp.mean(x * x, axis=-1, keepdims=True) + NORM_EPS) * fg_ref[...]

    @pl.when(i < N_CTX_TILES)
    def _():
        yc_ref[...] = out

    @pl.when(i >= N_CTX_TILES)
    def _():
        ys_ref[...] = out


def _odd_out(x, mod3, p_all, o_ctx, o_smp, g_norm, ind, bind, wo_bf, final_g):
    row = lambda i: (i, 0)
    const2 = lambda i: (0, 0)
    return pl.pallas_call(
        _odd_out_kernel,
        grid=(N_TOK // TM_OUT,),
        in_specs=[
            pl.BlockSpec((TM_OUT, D), row),
            pl.BlockSpec((1, 3, D), lambda i: (_tile_row(i, TM_OUT), 0, 0)),
            pl.BlockSpec((TM_OUT, D), lambda i: (i, 2)),
            pl.BlockSpec((2, TM_OUT, D), _ctx_tile),
            pl.BlockSpec((2, TM_OUT, D), _smp_tile),
            pl.BlockSpec((1, D), const2),
            pl.BlockSpec((D, LANES), const2),
            pl.BlockSpec((LANES, D), const2),
            pl.BlockSpec((D, D), const2),
            pl.BlockSpec((1, D), const2),
        ],
        out_specs=[
            pl.BlockSpec((TM_OUT, D), lambda i: _ctx_tile(i)[1:]),
            pl.BlockSpec((TM_OUT, D), lambda i: _smp_tile(i)[1:]),
        ],
        out_shape=[
            jax.ShapeDtypeStruct((N_CTX_TOK, D), F32),
            jax.ShapeDtypeStruct((N_TOK - N_CTX_TOK, D), F32),
        ],
        compiler_params=pltpu.CompilerParams(
            dimension_semantics=("arbitrary",), vmem_limit_bytes=VMEM_LIMIT),
        name="odd_out",
    )(x, mod3, p_all, o_ctx, o_smp, g_norm, ind, bind, wo_bf, final_g)


def _group_indicators(group):
    ch = jnp.arange(D) // group
    lane = jnp.arange(LANES)
    hit = (ch[:, None] == lane[None, :]).astype(F32)
    return (hit / float(group)).astype(BF16), hit.T.astype(BF16)


def _pad_dirs(w):
    r = w.shape[1]
    out = jnp.zeros((2, LANES, w.shape[2]), F32)
    out = out.at[0, 0:r].set(w[0])
    out = out.at[1, r:2 * r].set(w[1])
    return out


def kernel(x_prompt, x_sample, state_wkv, state_gla, c, c_ctx, norm_g, ada_w, ada_b, final_g, e_w_in, e_w_out, conv_w, wkv_w0, wkv_w1, wkv_w2, wkv_a0, wkv_a1, wkv_a2, wkv_k_k, wkv_k_a, wkv_r_k, wkv_ln_w, wkv_ln_b, o_w_in, o_w_out, gla_gk1, gla_gk2, gla_gk_b, gla_g_norm):
    x_ctx = x_prompt.reshape(N_CTX_TOK, D)
    x_smp = x_sample.reshape(N_TOK - N_CTX_TOK, D)
    cvec = jnp.zeros((8, D), F32).at[0].set(c_ctx).at[1:1 + SMP_B].set(c)
    mod = _modulation(cvec, ada_w, ada_b)
    masks = _chunk_masks()

    mod3 = mod[0].reshape(8, 3, D)
    wlr = jnp.concatenate([wkv_w1[0, 0], wkv_w1[0, 1], wkv_a1[0, 0], wkv_a1[0, 1]], axis=1)
    p_e, lr_e = _inproj(x_ctx, x_smp, 0, mod3, norm_g[0:1], e_w_in[0].astype(BF16), wlr.astype(BF16),
                        n_tanh=2 * 64)
    w2pad = _pad_dirs(wkv_w2[0])
    a2pad = _pad_dirs(wkv_a2[0])
    w0 = wkv_w0[0].reshape(2, 1, D)
    a0 = wkv_a0[0].reshape(2, 1, D)
    k_k = wkv_k_k[0].reshape(1, D)
    k_a = wkv_k_a[0].reshape(1, D)
    r_k = wkv_r_k[0].reshape(1, D)
    o_ctx, new_wkv = _wkv_scan(p_e, lr_e, w2pad, a2pad, w0, a0, k_k, k_a, None,
                               n_seq=CTX_B, seq_len=CTX_T, n_seg=WKV_CTX_SEG, tok_blk0=0)
    o_smp, _ = _wkv_scan(p_e, lr_e, w2pad, a2pad, w0, a0, k_k, k_a, state_wkv[:, 0],
                         n_seq=SMP_B, seq_len=SMP_T, n_seg=1, tok_blk0=N_CTX_TOK // SMP_T)
    ind64, bind64 = _group_indicators(WKV_N)
    x = _even_out(x_ctx, x_smp, mod3, p_e, o_ctx, o_smp, lr_e, a2pad, a0, k_a, r_k, conv_w[0],
                  wkv_ln_w[0].reshape(1, D), wkv_ln_b[0].reshape(1, D), ind64, bind64,
                  e_w_out[0].astype(BF16))

    mod3 = mod[1].reshape(8, 3, D)
    rank = gla_gk1.shape[-1]
    wlr = jnp.zeros((D, LANES), F32).at[:, 0:rank].set(gla_gk1[0, 0]).at[:, rank:2 * rank].set(gla_gk1[0, 1])
    p_o, lr_o = _inproj(x, x, N_CTX_TOK // TM_PROJ, mod3, norm_g[1:2], o_w_in[0].astype(BF16),
                        wlr.astype(BF16), n_tanh=0)
    gk2pad = _pad_dirs(gla_gk2[0])
    gkb = gla_gk_b[0].reshape(2, 1, GLA_H * GLA_DK)
    g_ctx, new_gla = _gla_scan(p_o, lr_o, gk2pad, gkb, None, masks,
                               n_seq=CTX_B, seq_len=CTX_T, n_seg=GLA_CTX_SEG, tok_blk0=0)
    g_smp, _ = _gla_scan(p_o, lr_o, gk2pad, gkb, state_gla[:, 0], masks,
                         n_seq=SMP_B, seq_len=SMP_T, n_seg=1, tok_blk0=N_CTX_TOK // SMP_T)
    ind256, bind256 = _group_indicators(GLA_DV)
    y_ctx, y_smp = _odd_out(x, mod3, p_o, g_ctx, g_smp, jnp.tile(gla_g_norm[0], GLA_H).reshape(1, D),
                            ind256, bind256, o_w_out[0].astype(BF16), final_g.reshape(1, D))
    return (y_ctx.reshape(CTX_B, CTX_T, D), y_smp.reshape(SMP_B, SMP_T, D), new_wkv[:, None], new_gla[:, None])
```

```python
import functools
import math

import jax
import jax.numpy as jnp
from jax import lax
from jax.experimental import pallas as pl
from jax.experimental.pallas import tpu as pltpu

F32 = jnp.float32
BF16 = jnp.bfloat16
HI = lax.Precision.HIGHEST

D = 1024
N_CTX_TOK = 16 * 256
N_TOK = 2 * N_CTX_TOK
CTX_B, CTX_T = 16, 256
SMP_B, SMP_T = 2, 2048
GRID_W = 64
WKV_H, WKV_N = 16, 64
GLA_H, GLA_DK, GLA_DV = 4, 128, 256
GLA_GATE_NORM = 16.0
GN_EPS = 64e-5
NORM_EPS = 1e-6
CHUNK = 64
WKV_GROUP = 4
WKV_HEADS_STEP = 8
WKV_CTX_SEG = 8
GLA_CTX_SEG = 8
GLA_GROUP = 16
LANES = 128
TM_PROJ = 1024
TN_PROJ_MAX = 2048
TM_OUT = 256
VMEM_LIMIT = 56 * 1024 * 1024


def _silu(x):
    return x * jax.nn.sigmoid(x)


def _dot(a, b, precision=None):
    return jnp.dot(a, b, preferred_element_type=F32, precision=precision)


def _dot_nt(a, b, precision=None):
    return lax.dot_general(a, b, (((1,), (1,)), ((), ())), preferred_element_type=F32, precision=precision)


def _dot_tn(a, b, precision=None):
    return lax.dot_general(a, b, (((0,), (0,)), ((), ())), preferred_element_type=F32, precision=precision)


def _bdot(a, b):
    return _dot(a.astype(BF16), b.astype(BF16))


def _bdot_nt(a, b):
    return _dot_nt(a.astype(BF16), b.astype(BF16))


def _bdot_tn(a, b):
    return _dot_tn(a.astype(BF16), b.astype(BF16))


def _split2(x):
    hi = x.astype(BF16)
    lo = (x - hi.astype(F32)).astype(BF16)
    return hi, lo


def _scan_cumsum(x, z):
    n = x.shape[0]
    row = lax.broadcasted_iota(jnp.int32, (n, 1), 0)
    pre = x
    s = 1
    while s < n:
        pre = pre + jnp.where(row >= s, pltpu.roll(pre, s, 0), 0.0)
        s *= 2
    tot = pre[n - 1:n, :]
    return jnp.where(z == 0, pre, tot - pre + x), tot


def _tile_row(i, tile):
    ctx_tiles = N_CTX_TOK // tile
    per_req = SMP_T // tile
    return jnp.where(i < ctx_tiles, 0, 1 + (i - ctx_tiles) // per_req)


def _mod_kernel(c_ref, w_ref, b_ref, o_ref):
    cf = _silu(c_ref[...])
    o_ref[0] = _dot(cf, w_ref[0], HI) + b_ref[0]


def _modulation(cvec, ada_w, ada_b):
    depth = ada_w.shape[0]
    return pl.pallas_call(
        _mod_kernel,
        grid=(depth, 3),
        in_specs=[
            pl.BlockSpec((8, D), lambda l, j: (0, 0)),
            pl.BlockSpec((1, D, D), lambda l, j: (l, 0, j)),
            pl.BlockSpec((1, 1, D), lambda l, j: (l, 0, j)),
        ],
        out_specs=pl.BlockSpec((1, 8, D), lambda l, j: (l, 0, j)),
        out_shape=jax.ShapeDtypeStruct((depth, 8, 3 * D), F32),
        compiler_params=pltpu.CompilerParams(
            dimension_semantics=("arbitrary", "arbitrary"), vmem_limit_bytes=VMEM_LIMIT),
        name="adaln_mod",
    )(cvec, ada_w, ada_b.reshape(depth, 1, 3 * D))


def _inproj_kernel(xa_ref, xb_ref, mod_ref, g_ref, w_ref, wlr_ref, p_ref, lr_ref, h_scr, *, n_tanh):
    i = pl.program_id(0)
    j = pl.program_id(1)

    @pl.when(j == 0)
    def _():
        x = jnp.where(i < N_CTX_TOK // TM_PROJ, xa_ref[...], xb_ref[...])
        y = x * lax.rsqrt(jnp.mean(x * x, axis=-1, keepdims=True) + NORM_EPS) * g_ref[...]
        shift = mod_ref[0, 0:1, :]
        scale = mod_ref[0, 1:2, :]
        h = y * (1.0 + scale) + shift
        hb = h.astype(BF16)
        h_scr[...] = hb
        lr = _dot(hb, wlr_ref[...])
        if n_tanh:
            lane = lax.broadcasted_iota(jnp.int32, lr.shape, 1)
            lr = jnp.where(lane < n_tanh, jnp.tanh(lr), lr)
        lr_ref[...] = lr

    p_ref[...] = _dot(h_scr[...], w_ref[...]).astype(p_ref.dtype)


def _inproj(x_ctx, x_smp, smp_tile0, mod3, g, w_bf, wlr_bf, n_tanh):
    n_out = w_bf.shape[1]
    n_lr = wlr_bf.shape[1]
    n_ctx = N_CTX_TOK // TM_PROJ
    tn = max(t for t in range(256, TN_PROJ_MAX + 1, 256) if n_out % t == 0)
    return pl.pallas_call(
        functools.partial(_inproj_kernel, n_tanh=n_tanh),
        grid=(N_TOK // TM_PROJ, n_out // tn),
        in_specs=[
            pl.BlockSpec((TM_PROJ, D), lambda i, j: (jnp.minimum(i, n_ctx - 1), 0)),
            pl.BlockSpec((TM_PROJ, D), lambda i, j: (smp_tile0 + jnp.maximum(i - n_ctx, 0), 0)),
            pl.BlockSpec((1, 3, D), lambda i, j: (_tile_row(i, TM_PROJ), 0, 0)),
            pl.BlockSpec((1, D), lambda i, j: (0, 0)),
            pl.BlockSpec((D, tn), lambda i, j: (0, j)),
            pl.BlockSpec((D, n_lr), lambda i, j: (0, 0)),
        ],
        out_specs=[
            pl.BlockSpec((TM_PROJ, tn), lambda i, j: (i, j)),
            pl.BlockSpec((TM_PROJ, n_lr), lambda i, j: (i, 0)),
        ],
        out_shape=[
            jax.ShapeDtypeStruct((N_TOK, n_out), BF16),
            jax.ShapeDtypeStruct((N_TOK, n_lr), F32),
        ],
        scratch_shapes=[pltpu.VMEM((TM_PROJ, D), BF16)],
        compiler_params=pltpu.CompilerParams(
            dimension_semantics=("arbitrary", "arbitrary"), vmem_limit_bytes=VMEM_LIMIT),
        name="inproj",
    )(x_ctx, x_smp, mod3, g, w_bf, wlr_bf)


def _chunk_masks():
    idx = jnp.arange(CHUNK)
    lower = (idx[:, None] >= idx[None, :]).astype(F32)
    return jnp.stack([lower, lower.T], axis=0)


def _chunk_rows(c):
    return pl.ds(pl.multiple_of(c * CHUNK, CHUNK), CHUNK)


def _pair_blockdiag(x):
    left = lax.broadcasted_iota(jnp.int32, x.shape, 1) < WKV_N
    xb = x.astype(BF16)
    zero = jnp.zeros_like(xb)
    return jnp.concatenate([jnp.where(left, xb, zero), jnp.where(left, zero, xb)], axis=0)


def _pair_mm(a, b):
    return _dot(a.astype(BF16), _pair_blockdiag(b))


def _inv_unit_tri(lmats, tick=lambda: None):
    row = lax.broadcasted_iota(jnp.int32, (CHUNK, LANES), 0)
    col = lax.broadcasted_iota(jnp.int32, (CHUNK, LANES), 1) & (CHUNK - 1)
    eye = (row == col).astype(F32)

    def same_block(n):
        sh = n.bit_length() - 1
        return lax.shift_right_logical(row, sh) == lax.shift_right_logical(col, sh)

    blk16 = same_block(16)
    l16 = [jnp.where(blk16, m, 0.0) for m in lmats]
    x = [eye - m for m in l16]
    p = [_pair_mm(m, m) for m in l16]
    for step in range(3):
        x = [xi + _pair_mm(xi, pi) for xi, pi in zip(x, p)]
        if step < 2:
            p = [_pair_mm(pi, pi) for pi in p]
        if step == 0:
            tick()
    n = 16
    while n < CHUNK:
        off = same_block(2 * n) & jnp.logical_not(same_block(n))
        xl = [_pair_mm(xi, jnp.where(off, m, 0.0)) for xi, m in zip(x, lmats)]
        x = [xi - _pair_mm(yi, xi) for xi, yi in zip(x, xl)]
        if n == 16:
            tick()
        n *= 2
    return x


_KT, _RT, _KH, _BH, _KP, _BP = range(6)


def _wkv_kernel(r_ref, k_ref, v_ref, lrw_ref, lra_ref, w2_ref, a2_ref, w0_ref, a0_ref,
                kk_ref, ka_ref, *rest, seq_len, n_seg, zero_init):
    s0_ref = None if zero_init else rest[0]
    o_ref, sout_ref, rp_scr, w_scr, h_scr, d_scr, rt_a, ops_a = rest[-8:]
    z = pl.program_id(0)
    n_chunks = seq_len // CHUNK
    seg_chunks = n_chunks // n_seg
    n_groups = n_chunks // WKV_GROUP
    pairs = range(WKV_HEADS_STEP // 2)
    row = lax.broadcasted_iota(jnp.int32, (CHUNK, LANES), 0)
    lane = lax.broadcasted_iota(jnp.int32, (CHUNK, LANES), 1)
    col = lane & (CHUNK - 1)
    left = lane < WKV_N
    incl2 = ((z == 0) & (row >= col)) | ((z == 1) & (row <= col))
    strict2 = incl2 & (row != col)
    row_b = lax.broadcasted_iota(jnp.int32, (LANES, LANES), 0) < WKV_N
    lane_b = lax.broadcasted_iota(jnp.int32, (LANES, LANES), 1) < WKV_N
    diag_blocks = row_b == lane_b
    w2 = w2_ref[0].astype(BF16)
    a2 = a2_ref[0].astype(BF16)
    w0 = w0_ref[0]
    a0 = a0_ref[0]
    k_k = kk_ref[...]
    k_a = ka_ref[...]

    def pair_sl(p):
        return slice(p * LANES, (p + 1) * LANES)

    def stack(a, b):
        return jnp.concatenate([a, b], axis=0)

    def prepare_chunk(gi, j, rt_scr, ops_scr):
        c = gi * WKV_GROUP + j
        rows = _chunk_rows(c)
        w_raw = w0 + _bdot(lrw_ref[rows, :], w2)
        a = jax.nn.sigmoid(a0 + _bdot(lra_ref[rows, :], a2))
        lw = -math.exp(-0.5) * jax.nn.sigmoid(w_raw)
        g, gtot = _scan_cumsum(lw, z)
        r = r_ref[rows, :].astype(F32)
        k = k_ref[rows, :].astype(F32)
        kkr = k * k_k
        sq = kkr * kkr
        nrm = []
        for p in pairs:
            sq_p = sq[:, pair_sl(p)]
            n0 = jnp.sum(jnp.where(left, sq_p, 0.0), axis=-1, keepdims=True)
            n1 = jnp.sum(jnp.where(left, 0.0, sq_p), axis=-1, keepdims=True)
            nrm.append(jnp.where(left, jnp.sqrt(n0), jnp.sqrt(n1)))
        kk = kkr / jnp.maximum(jnp.concatenate(nrm, axis=-1), 1e-12)
        b = kk * a
        kmod = k * (1.0 + (a - 1.0) * k_a)
        e_ng = jnp.exp(-g)
        e_gc = jnp.exp(gtot - g)
        rt = r * jnp.exp(g)
        rt_scr[j] = rt
        ops_scr[j, _KT] = (kk * jnp.exp(g - lw)).astype(BF16)
        ops_scr[j, _RT] = rt.astype(BF16)
        ops_scr[j, _KH] = (kmod * e_ng).astype(BF16)
        ops_scr[j, _BH] = (b * e_ng).astype(BF16)
        ops_scr[j, _KP] = (kmod * e_gc).astype(BF16)
        ops_scr[j, _BP] = (b * e_gc).astype(BF16)
        d_scr[c] = jnp.broadcast_to(jnp.exp(gtot), (8, WKV_HEADS_STEP * WKV_N))

    def chains(gi, rt_scr, ops_scr, fillers):
        pending = iter(fillers)

        def tick():
            next(pending, lambda: None)()

        cs = [gi * WKV_GROUP + j for j in range(WKV_GROUP)]
        rows = [_chunk_rows(c) for c in cs]
        cps = [(j, p) for j in range(WKV_GROUP) for p in pairs]
        n_cp = range(len(cps))

        def opnd(q, i):
            j, p = cps[q]
            return ops_scr[j, i, :, pair_sl(p)]

        def vals(q):
            j, p = cps[q]
            return v_ref[rows[j], pair_sl(p)]

        bk = [stack(_pair_blockdiag(opnd(q, _BH)), _pair_blockdiag(opnd(q, _KH))) for q in n_cp]
        g_k = [_dot_nt(opnd(q, _KT), bk[q]) for q in n_cp]
        g_r = [_dot_nt(opnd(q, _RT), bk[q]) for q in n_cp]
        lmat = [jnp.where(strict2, g[:, :LANES], 0.0) for g in g_k]
        mkk = [jnp.where(strict2, g[:, LANES:], 0.0) for g in g_k]
        mrb = [jnp.where(incl2, g[:, :LANES], 0.0).astype(BF16) for g in g_r]
        mrk = [jnp.where(incl2, g[:, LANES:], 0.0).astype(BF16) for g in g_r]
        tick()
        tinv = _inv_unit_tri(lmat, tick)
        mv = [_pair_mm(mkk[q], vals(q)) for q in n_cp]
        pm = [_pair_mm(tinv[q], opnd(q, _KT)).astype(BF16) for q in n_cp]
        qm = [_pair_mm(tinv[q], mv[q]).astype(BF16) for q in n_cp]
        tick()
        corr_p = [_dot(mrb[q], _pair_blockdiag(pm[q])) for q in n_cp]
        op_m = [_dot(jnp.concatenate([mrk[q], mrb[q]], axis=-1),
                     stack(_pair_blockdiag(vals(q)), -_pair_blockdiag(qm[q]))) for q in n_cp]
        wfull = [_dot_tn(pm[q], opnd(q, _BP)) for q in n_cp]
        hfull = [_dot_tn(stack(vals(q), qm[q]), stack(opnd(q, _KP), -opnd(q, _BP))) for q in n_cp]
        for q in n_cp:
            j, p = cps[q]
            o_ref[0, rows[j], pair_sl(p)] = op_m[q]
            rp_scr[cs[j], p] = (rt_scr[j, :, pair_sl(p)] - corr_p[q]).astype(BF16)
            w_hi, w_lo = _split2(jnp.where(diag_blocks, wfull[q], 0.0))
            w_scr[cs[j], p, 0] = w_hi
            w_scr[cs[j], p, 1] = w_lo
            h_scr[cs[j], p] = jnp.where(left, hfull[q][:WKV_N], hfull[q][WKV_N:])

        for rest_filler in pending:
            rest_filler()

    groups_per_seq = seg_chunks // WKV_GROUP

    def scan_group(i):
        return i + z * (n_groups - 1 - 2 * i)

    def init_state(sq, p):
        if zero_init:
            return jnp.zeros((WKV_N, LANES), F32)
        return jnp.concatenate([s0_ref[sq, 0, 2 * p], s0_ref[sq, 0, 2 * p + 1]], axis=-1)

    def walk_steps(i, state):
        g = scan_group(i)
        sq = g // groups_per_seq
        first = (i % groups_per_seq) == 0

        def step(jj):
            if jj == 0:
                state[:] = [jnp.where(first, init_state(sq, p), state[p]) for p in pairs]
            c = g * WKV_GROUP + jj + z * (WKV_GROUP - 1 - 2 * jj)
            split = [_split2(sp) for sp in state]
            sw_hi = [_dot(stack(split[p][0], split[p][1]), w_scr[c, p, 0]) for p in pairs]
            sw_lo = [_dot(split[p][0], w_scr[c, p, 1]) for p in pairs]
            inter = [_dot_nt(rp_scr[c, p], _pair_blockdiag(split[p][0])) for p in pairs]
            d_tot = d_scr[c][0:1, :]
            state[:] = [state[p] * d_tot[:, pair_sl(p)] - (sw_hi[p][:CHUNK] + sw_hi[p][CHUNK:] + sw_lo[p])
                        + h_scr[c, p] for p in pairs]
            rows = _chunk_rows(c)
            o_ref[0, rows, :] = o_ref[0, rows, :] + jnp.concatenate(inter, axis=-1)
            if jj == WKV_GROUP - 1:
                for p in pairs:
                    sout_ref[sq, 0, 2 * p] = state[p][:, :WKV_N]
                    sout_ref[sq, 0, 2 * p + 1] = state[p][:, WKV_N:]

        return [functools.partial(step, jj) for jj in range(WKV_GROUP)]

    def group_terms(i, fillers):
        g = scan_group(i)
        for j in range(WKV_GROUP):
            prepare_chunk(g, j, rt_a, ops_a)
        chains(g, rt_a, ops_a, fillers)

    group_terms(0, [])

    def body(i, carry):
        state = list(carry)
        group_terms(i, walk_steps(i - 1, state))
        return tuple(state)

    carry = lax.fori_loop(1, n_groups, body, tuple(jnp.zeros((WKV_N, LANES), F32) for p in pairs))
    state = list(carry)
    for last_step in walk_steps(n_groups - 1, state):
        last_step()


def _wkv_scan(p_all, lr, w2pad, a2pad, w0, a0, k_k, k_a, s0, *, n_seq, seq_len, n_seg, tok_blk0):
    def tok(b):
        return tok_blk0 + b

    rows = n_seg * seq_len
    width = WKV_HEADS_STEP * WKV_N
    n_pairs = WKV_HEADS_STEP // 2
    col = lambda base: (lambda z, b, hq: (tok(b), base // width + hq))
    vec = lambda z, b, hq: (z, 0, hq)
    n_chunks = rows // CHUNK
    state_spec = pl.BlockSpec((n_seg, 1, WKV_HEADS_STEP, WKV_N, WKV_N), lambda z, b, hq: (b, z, hq, 0, 0))
    init = () if s0 is None else (s0,)
    return pl.pallas_call(
        functools.partial(_wkv_kernel, seq_len=rows, n_seg=n_seg, zero_init=s0 is None),
        grid=(2, n_seq // n_seg, WKV_H // WKV_HEADS_STEP),
        in_specs=[
            pl.BlockSpec((rows, width), col(4 * D)),
            pl.BlockSpec((rows, width), col(5 * D)),
            pl.BlockSpec((rows, width), col(6 * D)),
            pl.BlockSpec((rows, LANES), lambda z, b, hq: (tok(b), 0)),
            pl.BlockSpec((rows, LANES), lambda z, b, hq: (tok(b), 1)),
            pl.BlockSpec((1, LANES, width), vec),
            pl.BlockSpec((1, LANES, width), vec),
            pl.BlockSpec((1, 1, width), vec),
            pl.BlockSpec((1, 1, width), vec),
            pl.BlockSpec((1, width), lambda z, b, hq: (0, hq)),
            pl.BlockSpec((1, width), lambda z, b, hq: (0, hq)),
        ] + [state_spec] * len(init),
        out_specs=[
            pl.BlockSpec((1, rows, width), lambda z, b, hq: (z, b, hq)),
            state_spec,
        ],
        out_shape=[
            jax.ShapeDtypeStruct((2, n_seq * seq_len, D), F32),
            jax.ShapeDtypeStruct((n_seq, 2, WKV_H, WKV_N, WKV_N), F32),
        ],
        scratch_shapes=[
            pltpu.VMEM((n_chunks, n_pairs, CHUNK, LANES), BF16),
            pltpu.VMEM((n_chunks, n_pairs, 2, LANES, LANES), BF16),
            pltpu.VMEM((n_chunks, n_pairs, CHUNK, LANES), F32),
            pltpu.VMEM((n_chunks, 8, width), F32),
            pltpu.VMEM((WKV_GROUP, CHUNK, width), F32),
            pltpu.VMEM((WKV_GROUP, 6, CHUNK, width), BF16),
        ],
        compiler_params=pltpu.CompilerParams(
            dimension_semantics=("arbitrary", "arbitrary", "arbitrary"), vmem_limit_bytes=VMEM_LIMIT),
        name="wkv_scan_%d" % seq_len,
    )(p_all, p_all, p_all, lr, lr, w2pad, a2pad, w0, a0, k_k, k_a, *init)


def _dot2(x, w_bf):
    hi, lo = _split2(x)
    return _dot(hi, w_bf) + _dot(lo, w_bf)


def _group_stat(x, ind_ref, bind_ref, split_bcast=False):
    per_group = _dot(x.astype(BF16), ind_ref[...])
    if split_bcast:
        return _dot2(per_group, bind_ref[...])
    return _dot(per_group.astype(BF16), bind_ref[...])


N_CTX_TILES = N_CTX_TOK // TM_OUT


def _sum_dirs(i, oc_ref, os_ref):
    is_ctx = i < N_CTX_TILES
    return jnp.where(is_ctx, oc_ref[0], os_ref[0]) + jnp.where(is_ctx, oc_ref[1], os_ref[1])


def _ctx_tile(i):
    return (0, jnp.minimum(i, N_CTX_TILES - 1), 0)


def _smp_tile(i):
    return (0, jnp.maximum(i - N_CTX_TILES, 0), 0)


def _even_out_kernel(xc_ref, xs_ref, mod_ref, pc_ref, pw_ref, oc_ref, os_ref, lra_ref, a2_ref, a0_ref, ka_ref,
                     rk_ref, cw_ref, lnw_ref, lnb_ref, ind_ref, bind_ref, wo_ref, out_ref):
    i = pl.program_id(0)
    u = pc_ref[:, 0:D].astype(F32)
    gb = pc_ref[:, D:2 * D].astype(F32)
    gc = pc_ref[:, 2 * D:3 * D].astype(F32)
    zc = pc_ref[:, 3 * D:4 * D].astype(F32)
    xg = gc * u
    row_len = jnp.where(i < N_CTX_TILES, CTX_T, GRID_W)
    pos = lax.broadcasted_iota(jnp.int32, (TM_OUT, 1), 0) & (row_len - 1)
    prev = jnp.where(pos == 0, 0.0, pltpu.roll(xg, 1, 0))
    nxt = jnp.where(pos == row_len - 1, 0.0, pltpu.roll(xg, TM_OUT - 1, 0))
    conv = cw_ref[0:1, :] * prev + cw_ref[1:2, :] * xg + cw_ref[2:3, :] * nxt
    o_conv = _silu(zc) * gb * conv
    r = pw_ref[:, 0:D].astype(F32)
    k = pw_ref[:, D:2 * D].astype(F32)
    v = pw_ref[:, 2 * D:3 * D].astype(F32)
    zw = pw_ref[:, 3 * D:4 * D].astype(F32)
    o = _sum_dirs(i, oc_ref, os_ref)
    mu = _group_stat(o, ind_ref, bind_ref, split_bcast=True)
    dlt = o - mu
    var = _group_stat(dlt * dlt, ind_ref, bind_ref)
    gn = dlt * lax.rsqrt(var + GN_EPS) * lnw_ref[...] + lnb_ref[...]
    lra = lra_ref[...]
    ic0 = jax.nn.sigmoid(a0_ref[0] + _bdot(lra, a2_ref[0]))
    ic1 = jax.nn.sigmoid(a0_ref[1] + _bdot(lra, a2_ref[1]))
    ksum = k * (2.0 + (ic0 + ic1 - 2.0) * ka_ref[...])
    bonus = _group_stat(r * ksum * rk_ref[...], ind_ref, bind_ref) * float(WKV_N)
    o_wkv = (gn + bonus * v) * _silu(zw)
    y = _dot(o_conv.astype(BF16), wo_ref[0:D, :]) + _dot(o_wkv.astype(BF16), wo_ref[D:2 * D, :])
    gate = mod_ref[0, 2:3, :]
    out_ref[...] = jnp.where(i < N_CTX_TILES, xc_ref[...], xs_ref[...]) + gate * y


def _even_out(x_ctx, x_smp, mod3, p_all, o_ctx, o_smp, lr, a2pad, a0, k_a, r_k, conv_w, ln_w, ln_b, ind, bind,
              wo_bf):
    row = lambda i: (i, 0)
    const2 = lambda i: (0, 0)
    const3 = lambda i: (0, 0, 0)
    return pl.pallas_call(
        _even_out_kernel,
        grid=(N_TOK // TM_OUT,),
        in_specs=[
            pl.BlockSpec((TM_OUT, D), lambda i: _ctx_tile(i)[1:]),
            pl.BlockSpec((TM_OUT, D), lambda i: _smp_tile(i)[1:]),
            pl.BlockSpec((1, 3, D), lambda i: (_tile_row(i, TM_OUT), 0, 0)),
            pl.BlockSpec((TM_OUT, 4 * D), lambda i: (i, 0)),
            pl.BlockSpec((TM_OUT, 4 * D), lambda i: (i, 1)),
            pl.BlockSpec((2, TM_OUT, D), _ctx_tile),
            pl.BlockSpec((2, TM_OUT, D), _smp_tile),
            pl.BlockSpec((TM_OUT, LANES), lambda i: (i, 1)),
            pl.BlockSpec((2, LANES, D), const3),
            pl.BlockSpec((2, 1, D), const3),
            pl.BlockSpec((1, D), const2),
            pl.BlockSpec((1, D), const2),
            pl.BlockSpec((3, D), const2),
            pl.BlockSpec((1, D), const2),
            pl.BlockSpec((1, D), const2),
            pl.BlockSpec((D, LANES), const2),
            pl.BlockSpec((LANES, D), const2),
            pl.BlockSpec((2 * D, D), const2),
        ],
        out_specs=pl.BlockSpec((TM_OUT, D), row),
        out_shape=jax.ShapeDtypeStruct((N_TOK, D), F32),
        compiler_params=pltpu.CompilerParams(
            dimension_semantics=("arbitrary",), vmem_limit_bytes=VMEM_LIMIT),
        name="even_out",
    )(x_ctx, x_smp, mod3, p_all, p_all, o_ctx, o_smp, lr, a2pad, a0, k_a, r_k, conv_w, ln_w, ln_b, ind, bind, wo_bf)


def _gla_kernel(q_ref, k_ref, v_ref, lrg_ref, gk2_ref, gkb_ref, m_ref, *rest, seq_len, n_seg, zero_init):
    s0_ref = None if zero_init else rest[0]
    o_ref, sout_ref, qe_scr, st_scr, e_scr = rest[-5:]
    z = pl.program_id(0)
    n_chunks = seq_len // CHUNK
    seg_chunks = n_chunks // n_seg
    group = min(GLA_GROUP, n_chunks)
    n_groups = n_chunks // group
    incl_b = m_ref[0] > 0.5
    gk2 = gk2_ref[0].astype(BF16)
    gkb = gkb_ref[0]
    scale = GLA_DK ** -0.5

    def phase1(it, carry):
        cs = [it * group + j for j in range(group)]
        rows = [_chunk_rows(c) for c in cs]
        logit = [_bdot(lrg_ref[rw, :], gk2) + gkb for rw in rows]
        g = [jax.nn.log_sigmoid(x) / GLA_GATE_NORM for x in logit]
        bc, b_last = zip(*[_scan_cumsum(x, z) for x in g])
        k = [k_ref[rw, :].astype(F32) for rw in rows]
        qe = [q_ref[rw, :].astype(F32) * scale * jnp.exp(x) for rw, x in zip(rows, bc)]
        ke = [kc * jnp.exp(-x) for kc, x in zip(k, bc)]
        kd = [kc * jnp.exp(bl - x) for kc, x, bl in zip(k, bc, b_last)]
        v = [v_ref[rw, :] for rw in rows]
        att = [jnp.where(incl_b, _bdot_nt(a, b), 0.0) for a, b in zip(qe, ke)]
        o_in = [_bdot(a, b) for a, b in zip(att, v)]
        vtk = [_bdot_tn(a, b) for a, b in zip(v, kd)]
        for j in range(group):
            o_ref[0, rows[j], :] = o_in[j]
            qe_scr[cs[j]] = qe[j]
            st_scr[cs[j]] = vtk[j]
            e_scr[cs[j]] = jnp.broadcast_to(jnp.exp(b_last[j]), (8, GLA_DK))
        return carry

    lax.fori_loop(0, n_groups, phase1, 0)

    def phase2(pos, sts):
        cidx = [sq * seg_chunks + pos + z * (seg_chunks - 1 - 2 * pos) for sq in range(n_seg)]
        new = [st * e_scr[c][0:1, :] + st_scr[c] for st, c in zip(sts, cidx)]
        for st, c in zip(sts, cidx):
            st_scr[c] = st
        return tuple(new)

    if zero_init:
        st_init = tuple(jnp.zeros((GLA_DV, GLA_DK), F32) for sq in range(n_seg))
    else:
        st_init = tuple(s0_ref[sq, 0, 0].T for sq in range(n_seg))
    st_fin = lax.fori_loop(0, seg_chunks, phase2, st_init)
    for sq in range(n_seg):
        sout_ref[sq, 0, 0] = st_fin[sq].T

    def phase3(it, carry):
        cs = [it * group + j for j in range(group)]
        inter = [_bdot_nt(qe_scr[c], st_scr[c]) for c in cs]
        for j in range(group):
            rows = _chunk_rows(cs[j])
            o_ref[0, rows, :] = o_ref[0, rows, :] + inter[j]
        return carry

    lax.fori_loop(0, n_groups, phase3, 0)


def _gla_scan(p_all, lrg, gk2pad, gkb, s0, masks, *, n_seq, seq_len, n_seg, tok_blk0):
    def tok(b):
        return tok_blk0 + b

    rows = n_seg * seq_len
    n_chunks = rows // CHUNK
    state_spec = pl.BlockSpec((n_seg, 1, 1, GLA_DK, GLA_DV), lambda z, b, h: (b, z, h, 0, 0))
    init = () if s0 is None else (s0,)
    return pl.pallas_call(
        functools.partial(_gla_kernel, seq_len=rows, n_seg=n_seg, zero_init=s0 is None),
        grid=(2, n_seq // n_seg, GLA_H),
        in_specs=[
            pl.BlockSpec((rows, GLA_DK), lambda z, b, h: (tok(b), h)),
            pl.BlockSpec((rows, GLA_DK), lambda z, b, h: (tok(b), GLA_H + h)),
            pl.BlockSpec((rows, GLA_DV), lambda z, b, h: (tok(b), GLA_H + h)),
            pl.BlockSpec((rows, LANES), lambda z, b, h: (tok(b), 0)),
            pl.BlockSpec((1, LANES, GLA_DK), lambda z, b, h: (z, 0, h)),
            pl.BlockSpec((1, 1, GLA_DK), lambda z, b, h: (z, 0, h)),
            pl.BlockSpec((1, CHUNK, CHUNK), lambda z, b, h: (z, 0, 0)),
        ] + [state_spec] * len(init),
        out_specs=[
            pl.BlockSpec((1, rows, GLA_DV), lambda z, b, h: (z, b, h)),
            state_spec,
        ],
        out_shape=[
            jax.ShapeDtypeStruct((2, n_seq * seq_len, D), F32),
            jax.ShapeDtypeStruct((n_seq, 2, GLA_H, GLA_DK, GLA_DV), F32),
        ],
        scratch_shapes=[
            pltpu.VMEM((n_chunks, CHUNK, GLA_DK), F32),
            pltpu.VMEM((n_chunks, GLA_DV, GLA_DK), F32),
            pltpu.VMEM((n_chunks, 8, GLA_DK), F32),
        ],
        compiler_params=pltpu.CompilerParams(
            dimension_semantics=("arbitrary", "arbitrary", "arbitrary"), vmem_limit_bytes=VMEM_LIMIT),
        name="gla_scan_%d" % seq_len,
    )(p_all, p_all, p_all, lrg, gk2pad, gkb, masks, *init)


def _odd_out_kernel(x_ref, mod_ref, zg_ref, oc_ref, os_ref, gn_ref, ind_ref, bind_ref, wo_ref, fg_ref,
                    yc_ref, ys_ref):
    i = pl.program_id(0)
    o = _sum_dirs(i, oc_ref, os_ref)
    ms = _group_stat(o * o, ind_ref, bind_ref)
    on = o * lax.rsqrt(ms + NORM_EPS) * gn_ref[...]
    og = on * _silu(zg_ref[...].astype(F32))
    y = _dot(og.astype(BF16), wo_ref[...])
    gate = mod_ref[0, 2:3, :]
    x = x_ref[...] + gate * y
    out = x * lax.rsqrt(jnp.mean(x * x, axis=-1, keepdims=True) + NORM_EPS) * fg_ref[...]

    @pl.when(i < N_CTX_TILES)
    def _():
        yc_ref[...] = out

    @pl.when(i >= N_CTX_TILES)
    def _():
        ys_ref[...] = out


def _odd_out(x, mod3, p_all, o_ctx, o_smp, g_norm, ind, bind, wo_bf, final_g):
    row = lambda i: (i, 0)
    const2 = lambda i: (0, 0)
    return pl.pallas_call(
        _odd_out_kernel,
        grid=(N_TOK // TM_OUT,),
        in_specs=[
            pl.BlockSpec((TM_OUT, D), row),
            pl.BlockSpec((1, 3, D), lambda i: (_tile_row(i, TM_OUT), 0, 0)),
            pl.BlockSpec((TM_OUT, D), lambda i: (i, 2)),
            pl.BlockSpec((2, TM_OUT, D), _ctx_tile),
            pl.BlockSpec((2, TM_OUT, D), _smp_tile),
            pl.BlockSpec((1, D), const2),
            pl.BlockSpec((D, LANES), const2),
            pl.BlockSpec((LANES, D), const2),
            pl.BlockSpec((D, D), const2),
            pl.BlockSpec((1, D), const2),
        ],
        out_specs=[
            pl.BlockSpec((TM_OUT, D), lambda i: _ctx_tile(i)[1:]),
            pl.BlockSpec((TM_OUT, D), lambda i: _smp_tile(i)[1:]),
        ],
        out_shape=[
            jax.ShapeDtypeStruct((N_CTX_TOK, D), F32),
            jax.ShapeDtypeStruct((N_TOK - N_CTX_TOK, D), F32),
        ],
        compiler_params=pltpu.CompilerParams(
            dimension_semantics=("arbitrary",), vmem_limit_bytes=VMEM_LIMIT),
        name="odd_out",
    )(x, mod3, p_all, o_ctx, o_smp, g_norm, ind, bind, wo_bf, final_g)


def _group_indicators(group):
    ch = jnp.arange(D) // group
    lane = jnp.arange(LANES)
    hit = (ch[:, None] == lane[None, :]).astype(F32)
    return (hit / float(group)).astype(BF16), hit.T.astype(BF16)


def _pad_dirs(w):
    r = w.shape[1]
    out = jnp.zeros((2, LANES, w.shape[2]), F32)
    out = out.at[0, 0:r].set(w[0])
    out = out.at[1, r:2 * r].set(w[1])
    return out


def kernel(x_prompt, x_sample, state_wkv, state_gla, c, c_ctx, norm_g, ada_w, ada_b, final_g, e_w_in, e_w_out, conv_w, wkv_w0, wkv_w1, wkv_w2, wkv_a0, wkv_a1, wkv_a2, wkv_k_k, wkv_k_a, wkv_r_k, wkv_ln_w, wkv_ln_b, o_w_in, o_w_out, gla_gk1, gla_gk2, gla_gk_b, gla_g_norm):
    x_ctx = x_prompt.reshape(N_CTX_TOK, D)
    x_smp = x_sample.reshape(N_TOK - N_CTX_TOK, D)
    cvec = jnp.zeros((8, D), F32).at[0].set(c_ctx).at[1:1 + SMP_B].set(c)
    mod = _modulation(cvec, ada_w, ada_b)
    masks = _chunk_masks()

    mod3 = mod[0].reshape(8, 3, D)
    wlr = jnp.concatenate([wkv_w1[0, 0], wkv_w1[0, 1], wkv_a1[0, 0], wkv_a1[0, 1]], axis=1)
    p_e, lr_e = _inproj(x_ctx, x_smp, 0, mod3, norm_g[0:1], e_w_in[0].astype(BF16), wlr.astype(BF16),
                        n_tanh=2 * 64)
    w2pad = _pad_dirs(wkv_w2[0])
    a2pad = _pad_dirs(wkv_a2[0])
    w0 = wkv_w0[0].reshape(2, 1, D)
    a0 = wkv_a0[0].reshape(2, 1, D)
    k_k = wkv_k_k[0].reshape(1, D)
    k_a = wkv_k_a[0].reshape(1, D)
    r_k = wkv_r_k[0].reshape(1, D)
    o_ctx, new_wkv = _wkv_scan(p_e, lr_e, w2pad, a2pad, w0, a0, k_k, k_a, None,
                               n_seq=CTX_B, seq_len=CTX_T, n_seg=WKV_CTX_SEG, tok_blk0=0)
    o_smp, _ = _wkv_scan(p_e, lr_e, w2pad, a2pad, w0, a0, k_k, k_a, state_wkv[:, 0],
                         n_seq=SMP_B, seq_len=SMP_T, n_seg=1, tok_blk0=N_CTX_TOK // SMP_T)
    ind64, bind64 = _group_indicators(WKV_N)
    x = _even_out(x_ctx, x_smp, mod3, p_e, o_ctx, o_smp, lr_e, a2pad, a0, k_a, r_k, conv_w[0],
                  wkv_ln_w[0].reshape(1, D), wkv_ln_b[0].reshape(1, D), ind64, bind64,
                  e_w_out[0].astype(BF16))

    mod3 = mod[1].reshape(8, 3, D)
    rank = gla_gk1.shape[-1]
    wlr = jnp.zeros((D, LANES), F32).at[:, 0:rank].set(gla_gk1[0, 0]).at[:, rank:2 * rank].set(gla_gk1[0, 1])
    p_o, lr_o = _inproj(x, x, N_CTX_TOK // TM_PROJ, mod3, norm_g[1:2], o_w_in[0].astype(BF16),
                        wlr.astype(BF16), n_tanh=0)
    gk2pad = _pad_dirs(gla_gk2[0])
    gkb = gla_gk_b[0].reshape(2, 1, GLA_H * GLA_DK)
    g_ctx, new_gla = _gla_scan(p_o, lr_o, gk2pad, gkb, None, masks,
                               n_seq=CTX_B, seq_len=CTX_T, n_seg=GLA_CTX_SEG, tok_blk0=0)
    g_smp, _ = _gla_scan(p_o, lr_o, gk2pad, gkb, state_gla[:, 0], masks,
                         n_seq=SMP_B, seq_len=SMP_T, n_seg=1, tok_blk0=N_CTX_TOK // SMP_T)
    ind256, bind256 = _group_indicators(GLA_DV)
    y_ctx, y_smp = _odd_out(x, mod3, p_o, g_ctx, g_smp, jnp.tile(gla_g_norm[0], GLA_H).reshape(1, D),
                            ind256, bind256, o_w_out[0].astype(BF16), final_g.reshape(1, D))
    return (y_ctx.reshape(CTX_B, CTX_T, D), y_smp.reshape(SMP_B, SMP_T, D), new_wkv[:, None], new_gla[:, None])
```

```python
import functools
import math

import jax
import jax.numpy as jnp
from jax import lax
from jax.experimental import pallas as pl
from jax.experimental.pallas import tpu as pltpu

F32 = jnp.float32
BF16 = jnp.bfloat16
HI = lax.Precision.HIGHEST

D = 1024
N_CTX_TOK = 16 * 256
N_TOK = 2 * N_CTX_TOK
CTX_B, CTX_T = 16, 256
SMP_B, SMP_T = 2, 2048
GRID_W = 64
WKV_H, WKV_N = 16, 64
GLA_H, GLA_DK, GLA_DV = 4, 128, 256
GLA_GATE_NORM = 16.0
GN_EPS = 64e-5
NORM_EPS = 1e-6
CHUNK = 64
WKV_GROUP = 4
WKV_HEADS_STEP = 8
WKV_CTX_SEG = 8
GLA_CTX_SEG = 8
GLA_GROUP = 32
LANES = 128
TM_PROJ = 1024
TN_PROJ_MAX = 2048
TM_OUT = 256
VMEM_LIMIT = 56 * 1024 * 1024


def _silu(x):
    return x * jax.nn.sigmoid(x)


def _dot(a, b, precision=None):
    return jnp.dot(a, b, preferred_element_type=F32, precision=precision)


def _dot_nt(a, b, precision=None):
    return lax.dot_general(a, b, (((1,), (1,)), ((), ())), preferred_element_type=F32, precision=precision)


def _dot_tn(a, b, precision=None):
    return lax.dot_general(a, b, (((0,), (0,)), ((), ())), preferred_element_type=F32, precision=precision)


def _bdot(a, b):
    return _dot(a.astype(BF16), b.astype(BF16))


def _bdot_nt(a, b):
    return _dot_nt(a.astype(BF16), b.astype(BF16))


def _bdot_tn(a, b):
    return _dot_tn(a.astype(BF16), b.astype(BF16))


def _split2(x):
    hi = x.astype(BF16)
    lo = (x - hi.astype(F32)).astype(BF16)
    return hi, lo


def _scan_cumsum(x, z):
    n = x.shape[0]
    row = lax.broadcasted_iota(jnp.int32, (n, 1), 0)
    pre = x
    s = 1
    while s < n:
        pre = pre + jnp.where(row >= s, pltpu.roll(pre, s, 0), 0.0)
        s *= 2
    tot = pre[n - 1:n, :]
    return jnp.where(z == 0, pre, tot - pre + x), tot


def _tile_row(i, tile):
    ctx_tiles = N_CTX_TOK // tile
    per_req = SMP_T // tile
    return jnp.where(i < ctx_tiles, 0, 1 + (i - ctx_tiles) // per_req)


def _mod_kernel(c_ref, w_ref, b_ref, o_ref):
    cf = _silu(c_ref[...])
    o_ref[0] = _dot(cf, w_ref[0], HI) + b_ref[0]


def _modulation(cvec, ada_w, ada_b):
    depth = ada_w.shape[0]
    return pl.pallas_call(
        _mod_kernel,
        grid=(depth, 3),
        in_specs=[
            pl.BlockSpec((8, D), lambda l, j: (0, 0)),
            pl.BlockSpec((1, D, D), lambda l, j: (l, 0, j)),
            pl.BlockSpec((1, 1, D), lambda l, j: (l, 0, j)),
        ],
        out_specs=pl.BlockSpec((1, 8, D), lambda l, j: (l, 0, j)),
        out_shape=jax.ShapeDtypeStruct((depth, 8, 3 * D), F32),
        compiler_params=pltpu.CompilerParams(
            dimension_semantics=("arbitrary", "arbitrary"), vmem_limit_bytes=VMEM_LIMIT),
        name="adaln_mod",
    )(cvec, ada_w, ada_b.reshape(depth, 1, 3 * D))


def _inproj_kernel(xa_ref, xb_ref, mod_ref, g_ref, w_ref, wlr_ref, p_ref, lr_ref, h_scr, *, n_tanh):
    i = pl.program_id(0)
    j = pl.program_id(1)

    @pl.when(j == 0)
    def _():
        x = jnp.where(i < N_CTX_TOK // TM_PROJ, xa_ref[...], xb_ref[...])
        y = x * lax.rsqrt(jnp.mean(x * x, axis=-1, keepdims=True) + NORM_EPS) * g_ref[...]
        shift = mod_ref[0, 0:1, :]
        scale = mod_ref[0, 1:2, :]
        h = y * (1.0 + scale) + shift
        hb = h.astype(BF16)
        h_scr[...] = hb
        lr = _dot(hb, wlr_ref[...])
        if n_tanh:
            lane = lax.broadcasted_iota(jnp.int32, lr.shape, 1)
            lr = jnp.where(lane < n_tanh, jnp.tanh(lr), lr)
        lr_ref[...] = lr

    p_ref[...] = _dot(h_scr[...], w_ref[...]).astype(p_ref.dtype)


def _inproj(x_ctx, x_smp, smp_tile0, mod3, g, w_bf, wlr_bf, n_tanh):
    n_out = w_bf.shape[1]
    n_lr = wlr_bf.shape[1]
    n_ctx = N_CTX_TOK // TM_PROJ
    tn = max(t for t in range(256, TN_PROJ_MAX + 1, 256) if n_out % t == 0)
    return pl.pallas_call(
        functools.partial(_inproj_kernel, n_tanh=n_tanh),
        grid=(N_TOK // TM_PROJ, n_out // tn),
        in_specs=[
            pl.BlockSpec((TM_PROJ, D), lambda i, j: (jnp.minimum(i, n_ctx - 1), 0)),
            pl.BlockSpec((TM_PROJ, D), lambda i, j: (smp_tile0 + jnp.maximum(i - n_ctx, 0), 0)),
            pl.BlockSpec((1, 3, D), lambda i, j: (_tile_row(i, TM_PROJ), 0, 0)),
            pl.BlockSpec((1, D), lambda i, j: (0, 0)),
            pl.BlockSpec((D, tn), lambda i, j: (0, j)),
            pl.BlockSpec((D, n_lr), lambda i, j: (0, 0)),
        ],
        out_specs=[
            pl.BlockSpec((TM_PROJ, tn), lambda i, j: (i, j)),
            pl.BlockSpec((TM_PROJ, n_lr), lambda i, j: (i, 0)),
        ],
        out_shape=[
            jax.ShapeDtypeStruct((N_TOK, n_out), BF16),
            jax.ShapeDtypeStruct((N_TOK, n_lr), F32),
        ],
        scratch_shapes=[pltpu.VMEM((TM_PROJ, D), BF16)],
        compiler_params=pltpu.CompilerParams(
            dimension_semantics=("arbitrary", "arbitrary"), vmem_limit_bytes=VMEM_LIMIT),
        name="inproj",
    )(x_ctx, x_smp, mod3, g, w_bf, wlr_bf)


def _chunk_masks():
    idx = jnp.arange(CHUNK)
    lower = (idx[:, None] >= idx[None, :]).astype(F32)
    return jnp.stack([lower, lower.T], axis=0)


def _chunk_rows(c):
    return pl.ds(pl.multiple_of(c * CHUNK, CHUNK), CHUNK)


def _pair_blockdiag(x):
    left = lax.broadcasted_iota(jnp.int32, x.shape, 1) < WKV_N
    xb = x.astype(BF16)
    zero = jnp.zeros_like(xb)
    return jnp.concatenate([jnp.where(left, xb, zero), jnp.where(left, zero, xb)], axis=0)


def _pair_mm(a, b):
    return _dot(a.astype(BF16), _pair_blockdiag(b))


def _inv_unit_tri(lmats, tick=lambda: None):
    row = lax.broadcasted_iota(jnp.int32, (CHUNK, LANES), 0)
    col = lax.broadcasted_iota(jnp.int32, (CHUNK, LANES), 1) & (CHUNK - 1)
    eye = (row == col).astype(F32)

    def same_block(n):
        sh = n.bit_length() - 1
        return lax.shift_right_logical(row, sh) == lax.shift_right_logical(col, sh)

    blk16 = same_block(16)
    l16 = [jnp.where(blk16, m, 0.0) for m in lmats]
    x = [eye - m for m in l16]
    p = [_pair_mm(m, m) for m in l16]
    for step in range(3):
        x = [xi + _pair_mm(xi, pi) for xi, pi in zip(x, p)]
        if step < 2:
            p = [_pair_mm(pi, pi) for pi in p]
        if step == 0:
            tick()
    n = 16
    while n < CHUNK:
        off = same_block(2 * n) & jnp.logical_not(same_block(n))
        xl = [_pair_mm(xi, jnp.where(off, m, 0.0)) for xi, m in zip(x, lmats)]
        x = [xi - _pair_mm(yi, xi) for xi, yi in zip(x, xl)]
        if n == 16:
            tick()
        n *= 2
    return x


_KT, _RT, _KH, _BH, _KP, _BP = range(6)


def _wkv_kernel(r_ref, k_ref, v_ref, lrw_ref, lra_ref, w2_ref, a2_ref, w0_ref, a0_ref,
                kk_ref, ka_ref, *rest, seq_len, n_seg, zero_init):
    s0_ref = None if zero_init else rest[0]
    o_ref, sout_ref, rp_scr, w_scr, h_scr, d_scr, rt_a, ops_a = rest[-8:]
    z = pl.program_id(0)
    n_chunks = seq_len // CHUNK
    seg_chunks = n_chunks // n_seg
    n_groups = n_chunks // WKV_GROUP
    pairs = range(WKV_HEADS_STEP // 2)
    row = lax.broadcasted_iota(jnp.int32, (CHUNK, LANES), 0)
    lane = lax.broadcasted_iota(jnp.int32, (CHUNK, LANES), 1)
    col = lane & (CHUNK - 1)
    left = lane < WKV_N
    incl2 = ((z == 0) & (row >= col)) | ((z == 1) & (row <= col))
    strict2 = incl2 & (row != col)
    row_b = lax.broadcasted_iota(jnp.int32, (LANES, LANES), 0) < WKV_N
    lane_b = lax.broadcasted_iota(jnp.int32, (LANES, LANES), 1) < WKV_N
    diag_blocks = row_b == lane_b
    w2 = w2_ref[0].astype(BF16)
    a2 = a2_ref[0].astype(BF16)
    w0 = w0_ref[0]
    a0 = a0_ref[0]
    k_k = kk_ref[...]
    k_a = ka_ref[...]

    def pair_sl(p):
        return slice(p * LANES, (p + 1) * LANES)

    def stack(a, b):
        return jnp.concatenate([a, b], axis=0)

    def prepare_chunk(gi, j, rt_scr, ops_scr):
        c = gi * WKV_GROUP + j
        rows = _chunk_rows(c)
        w_raw = w0 + _bdot(lrw_ref[rows, :], w2)
        a = jax.nn.sigmoid(a0 + _bdot(lra_ref[rows, :], a2))
        lw = -math.exp(-0.5) * jax.nn.sigmoid(w_raw)
        g, gtot = _scan_cumsum(lw, z)
        r = r_ref[rows, :].astype(F32)
        k = k_ref[rows, :].astype(F32)
        kkr = k * k_k
        sq = kkr * kkr
        nrm = []
        for p in pairs:
            sq_p = sq[:, pair_sl(p)]
            n0 = jnp.sum(jnp.where(left, sq_p, 0.0), axis=-1, keepdims=True)
            n1 = jnp.sum(jnp.where(left, 0.0, sq_p), axis=-1, keepdims=True)
            nrm.append(jnp.where(left, jnp.sqrt(n0), jnp.sqrt(n1)))
        kk = kkr / jnp.maximum(jnp.concatenate(nrm, axis=-1), 1e-12)
        b = kk * a
        kmod = k * (1.0 + (a - 1.0) * k_a)
        e_ng = jnp.exp(-g)
        e_gc = jnp.exp(gtot - g)
        rt = r * jnp.exp(g)
        rt_scr[j] = rt
        ops_scr[j, _KT] = (kk * jnp.exp(g - lw)).astype(BF16)
        ops_scr[j, _RT] = rt.astype(BF16)
        ops_scr[j, _KH] = (kmod * e_ng).astype(BF16)
        ops_scr[j, _BH] = (b * e_ng).astype(BF16)
        ops_scr[j, _KP] = (kmod * e_gc).astype(BF16)
        ops_scr[j, _BP] = (b * e_gc).astype(BF16)
        d_scr[c] = jnp.broadcast_to(jnp.exp(gtot), (8, WKV_HEADS_STEP * WKV_N))

    def chains(gi, rt_scr, ops_scr, fillers):
        pending = iter(fillers)

        def tick():
            next(pending, lambda: None)()

        cs = [gi * WKV_GROUP + j for j in range(WKV_GROUP)]
        rows = [_chunk_rows(c) for c in cs]
        cps = [(j, p) for j in range(WKV_GROUP) for p in pairs]
        n_cp = range(len(cps))

        def opnd(q, i):
            j, p = cps[q]
            return ops_scr[j, i, :, pair_sl(p)]

        def vals(q):
            j, p = cps[q]
            return v_ref[rows[j], pair_sl(p)]

        bk = [stack(_pair_blockdiag(opnd(q, _BH)), _pair_blockdiag(opnd(q, _KH))) for q in n_cp]
        g_k = [_dot_nt(opnd(q, _KT), bk[q]) for q in n_cp]
        g_r = [_dot_nt(opnd(q, _RT), bk[q]) for q in n_cp]
        lmat = [jnp.where(strict2, g[:, :LANES], 0.0) for g in g_k]
        mkk = [jnp.where(strict2, g[:, LANES:], 0.0) for g in g_k]
        mrb = [jnp.where(incl2, g[:, :LANES], 0.0).astype(BF16) for g in g_r]
        mrk = [jnp.where(incl2, g[:, LANES:], 0.0).astype(BF16) for g in g_r]
        tick()
        tinv = _inv_unit_tri(lmat, tick)
        mv = [_pair_mm(mkk[q], vals(q)) for q in n_cp]
        pm = [_pair_mm(tinv[q], opnd(q, _KT)).astype(BF16) for q in n_cp]
        qm = [_pair_mm(tinv[q], mv[q]).astype(BF16) for q in n_cp]
        tick()
        corr_p = [_dot(mrb[q], _pair_blockdiag(pm[q])) for q in n_cp]
        op_m = [_dot(jnp.concatenate([mrk[q], mrb[q]], axis=-1),
                     stack(_pair_blockdiag(vals(q)), -_pair_blockdiag(qm[q]))) for q in n_cp]
        wfull = [_dot_tn(pm[q], opnd(q, _BP)) for q in n_cp]
        hfull = [_dot_tn(stack(vals(q), qm[q]), stack(opnd(q, _KP), -opnd(q, _BP))) for q in n_cp]
        for q in n_cp:
            j, p = cps[q]
            o_ref[0, rows[j], pair_sl(p)] = op_m[q]
            rp_scr[cs[j], p] = (rt_scr[j, :, pair_sl(p)] - corr_p[q]).astype(BF16)
            w_hi, w_lo = _split2(jnp.where(diag_blocks, wfull[q], 0.0))
            w_scr[cs[j], p, 0] = w_hi
            w_scr[cs[j], p, 1] = w_lo
            h_scr[cs[j], p] = jnp.where(left, hfull[q][:WKV_N], hfull[q][WKV_N:])

        for rest_filler in pending:
            rest_filler()

    groups_per_seq = seg_chunks // WKV_GROUP

    def scan_group(i):
        return i + z * (n_groups - 1 - 2 * i)

    def init_state(sq, p):
        if zero_init:
            return jnp.zeros((WKV_N, LANES), F32)
        return jnp.concatenate([s0_ref[sq, 0, 2 * p], s0_ref[sq, 0, 2 * p + 1]], axis=-1)

    def walk_steps(i, state):
        g = scan_group(i)
        sq = g // groups_per_seq
        first = (i % groups_per_seq) == 0

        def step(jj):
            if jj == 0:
                state[:] = [jnp.where(first, init_state(sq, p), state[p]) for p in pairs]
            c = g * WKV_GROUP + jj + z * (WKV_GROUP - 1 - 2 * jj)
            split = [_split2(sp) for sp in state]
            sw_hi = [_dot(stack(split[p][0], split[p][1]), w_scr[c, p, 0]) for p in pairs]
            sw_lo = [_dot(split[p][0], w_scr[c, p, 1]) for p in pairs]
            inter = [_dot_nt(rp_scr[c, p], _pair_blockdiag(split[p][0])) for p in pairs]
            d_tot = d_scr[c][0:1, :]
            state[:] = [state[p] * d_tot[:, pair_sl(p)] - (sw_hi[p][:CHUNK] + sw_hi[p][CHUNK:] + sw_lo[p])
                        + h_scr[c, p] for p in pairs]
            rows = _chunk_rows(c)
            o_ref[0, rows, :] = o_ref[0, rows, :] + jnp.concatenate(inter, axis=-1)
            if jj == WKV_GROUP - 1:
                for p in pairs:
                    sout_ref[sq, 0, 2 * p] = state[p][:, :WKV_N]
                    sout_ref[sq, 0, 2 * p + 1] = state[p][:, WKV_N:]

        return [functools.partial(step, jj) for jj in range(WKV_GROUP)]

    def group_terms(i, fillers):
        g = scan_group(i)
        for j in range(WKV_GROUP):
            prepare_chunk(g, j, rt_a, ops_a)
        chains(g, rt_a, ops_a, fillers)

    group_terms(0, [])

    def body(i, carry):
        state = list(carry)
        group_terms(i, walk_steps(i - 1, state))
        return tuple(state)

    carry = lax.fori_loop(1, n_groups, body, tuple(jnp.zeros((WKV_N, LANES), F32) for p in pairs))
    state = list(carry)
    for last_step in walk_steps(n_groups - 1, state):
        last_step()


def _wkv_scan(p_all, lr, w2pad, a2pad, w0, a0, k_k, k_a, s0, *, n_seq, seq_len, n_seg, tok_blk0):
    def tok(b):
        return tok_blk0 + b

    rows = n_seg * seq_len
    width = WKV_HEADS_STEP * WKV_N
    n_pairs = WKV_HEADS_STEP // 2
    col = lambda base: (lambda z, b, hq: (tok(b), base // width + hq))
    vec = lambda z, b, hq: (z, 0, hq)
    n_chunks = rows // CHUNK
    state_spec = pl.BlockSpec((n_seg, 1, WKV_HEADS_STEP, WKV_N, WKV_N), lambda z, b, hq: (b, z, hq, 0, 0))
    init = () if s0 is None else (s0,)
    return pl.pallas_call(
        functools.partial(_wkv_kernel, seq_len=rows, n_seg=n_seg, zero_init=s0 is None),
        grid=(2, n_seq // n_seg, WKV_H // WKV_HEADS_STEP),
        in_specs=[
            pl.BlockSpec((rows, width), col(4 * D)),
            pl.BlockSpec((rows, width), col(5 * D)),
            pl.BlockSpec((rows, width), col(6 * D)),
            pl.BlockSpec((rows, LANES), lambda z, b, hq: (tok(b), 0)),
            pl.BlockSpec((rows, LANES), lambda z, b, hq: (tok(b), 1)),
            pl.BlockSpec((1, LANES, width), vec),
            pl.BlockSpec((1, LANES, width), vec),
            pl.BlockSpec((1, 1, width), vec),
            pl.BlockSpec((1, 1, width), vec),
            pl.BlockSpec((1, width), lambda z, b, hq: (0, hq)),
            pl.BlockSpec((1, width), lambda z, b, hq: (0, hq)),
        ] + [state_spec] * len(init),
        out_specs=[
            pl.BlockSpec((1, rows, width), lambda z, b, hq: (z, b, hq)),
            state_spec,
        ],
        out_shape=[
            jax.ShapeDtypeStruct((2, n_seq * seq_len, D), F32),
            jax.ShapeDtypeStruct((n_seq, 2, WKV_H, WKV_N, WKV_N), F32),
        ],
        scratch_shapes=[
            pltpu.VMEM((n_chunks, n_pairs, CHUNK, LANES), BF16),
            pltpu.VMEM((n_chunks, n_pairs, 2, LANES, LANES), BF16),
            pltpu.VMEM((n_chunks, n_pairs, CHUNK, LANES), F32),
            pltpu.VMEM((n_chunks, 8, width), F32),
            pltpu.VMEM((WKV_GROUP, CHUNK, width), F32),
            pltpu.VMEM((WKV_GROUP, 6, CHUNK, width), BF16),
        ],
        compiler_params=pltpu.CompilerParams(
            dimension_semantics=("arbitrary", "arbitrary", "arbitrary"), vmem_limit_bytes=VMEM_LIMIT),
        name="wkv_scan_%d" % seq_len,
    )(p_all, p_all, p_all, lr, lr, w2pad, a2pad, w0, a0, k_k, k_a, *init)


def _dot2(x, w_bf):
    hi, lo = _split2(x)
    return _dot(hi, w_bf) + _dot(lo, w_bf)


def _group_stat(x, ind_ref, bind_ref, split_bcast=False):
    per_group = _dot(x.astype(BF16), ind_ref[...])
    if split_bcast:
        return _dot2(per_group, bind_ref[...])
    return _dot(per_group.astype(BF16), bind_ref[...])


N_CTX_TILES = N_CTX_TOK // TM_OUT


def _sum_dirs(i, oc_ref, os_ref):
    is_ctx = i < N_CTX_TILES
    return jnp.where(is_ctx, oc_ref[0], os_ref[0]) + jnp.where(is_ctx, oc_ref[1], os_ref[1])


def _ctx_tile(i):
    return (0, jnp.minimum(i, N_CTX_TILES - 1), 0)


def _smp_tile(i):
    return (0, jnp.maximum(i - N_CTX_TILES, 0), 0)


def _even_out_kernel(xc_ref, xs_ref, mod_ref, pc_ref, pw_ref, oc_ref, os_ref, lra_ref, a2_ref, a0_ref, ka_ref,
                     rk_ref, cw_ref, lnw_ref, lnb_ref, ind_ref, bind_ref, wo_ref, out_ref):
    i = pl.program_id(0)
    u = pc_ref[:, 0:D].astype(F32)
    gb = pc_ref[:, D:2 * D].astype(F32)
    gc = pc_ref[:, 2 * D:3 * D].astype(F32)
    zc = pc_ref[:, 3 * D:4 * D].astype(F32)
    xg = gc * u
    row_len = jnp.where(i < N_CTX_TILES, CTX_T, GRID_W)
    pos = lax.broadcasted_iota(jnp.int32, (TM_OUT, 1), 0) & (row_len - 1)
    prev = jnp.where(pos == 0, 0.0, pltpu.roll(xg, 1, 0))
    nxt = jnp.where(pos == row_len - 1, 0.0, pltpu.roll(xg, TM_OUT - 1, 0))
    conv = cw_ref[0:1, :] * prev + cw_ref[1:2, :] * xg + cw_ref[2:3, :] * nxt
    o_conv = _silu(zc) * gb * conv
    r = pw_ref[:, 0:D].astype(F32)
    k = pw_ref[:, D:2 * D].astype(F32)
    v = pw_ref[:, 2 * D:3 * D].astype(F32)
    zw = pw_ref[:, 3 * D:4 * D].astype(F32)
    o = _sum_dirs(i, oc_ref, os_ref)
    mu = _group_stat(o, ind_ref, bind_ref, split_bcast=True)
    dlt = o - mu
    var = _group_stat(dlt * dlt, ind_ref, bind_ref)
    gn = dlt * lax.rsqrt(var + GN_EPS) * lnw_ref[...] + lnb_ref[...]
    lra = lra_ref[...]
    ic0 = jax.nn.sigmoid(a0_ref[0] + _bdot(lra, a2_ref[0]))
    ic1 = jax.nn.sigmoid(a0_ref[1] + _bdot(lra, a2_ref[1]))
    ksum = k * (2.0 + (ic0 + ic1 - 2.0) * ka_ref[...])
    bonus = _group_stat(r * ksum * rk_ref[...], ind_ref, bind_ref) * float(WKV_N)
    o_wkv = (gn + bonus * v) * _silu(zw)
    y = _dot(o_conv.astype(BF16), wo_ref[0:D, :]) + _dot(o_wkv.astype(BF16), wo_ref[D:2 * D, :])
    gate = mod_ref[0, 2:3, :]
    out_ref[...] = jnp.where(i < N_CTX_TILES, xc_ref[...], xs_ref[...]) + gate * y


def _even_out(x_ctx, x_smp, mod3, p_all, o_ctx, o_smp, lr, a2pad, a0, k_a, r_k, conv_w, ln_w, ln_b, ind, bind,
              wo_bf):
    row = lambda i: (i, 0)
    const2 = lambda i: (0, 0)
    const3 = lambda i: (0, 0, 0)
    return pl.pallas_call(
        _even_out_kernel,
        grid=(N_TOK // TM_OUT,),
        in_specs=[
            pl.BlockSpec((TM_OUT, D), lambda i: _ctx_tile(i)[1:]),
            pl.BlockSpec((TM_OUT, D), lambda i: _smp_tile(i)[1:]),
            pl.BlockSpec((1, 3, D), lambda i: (_tile_row(i, TM_OUT), 0, 0)),
            pl.BlockSpec((TM_OUT, 4 * D), lambda i: (i, 0)),
            pl.BlockSpec((TM_OUT, 4 * D), lambda i: (i, 1)),
            pl.BlockSpec((2, TM_OUT, D), _ctx_tile),
            pl.BlockSpec((2, TM_OUT, D), _smp_tile),
            pl.BlockSpec((TM_OUT, LANES), lambda i: (i, 1)),
            pl.BlockSpec((2, LANES, D), const3),
            pl.BlockSpec((2, 1, D), const3),
            pl.BlockSpec((1, D), const2),
            pl.BlockSpec((1, D), const2),
            pl.BlockSpec((3, D), const2),
            pl.BlockSpec((1, D), const2),
            pl.BlockSpec((1, D), const2),
            pl.BlockSpec((D, LANES), const2),
            pl.BlockSpec((LANES, D), const2),
            pl.BlockSpec((2 * D, D), const2),
        ],
        out_specs=pl.BlockSpec((TM_OUT, D), row),
        out_shape=jax.ShapeDtypeStruct((N_TOK, D), F32),
        compiler_params=pltpu.CompilerParams(
            dimension_semantics=("arbitrary",), vmem_limit_bytes=VMEM_LIMIT),
        name="even_out",
    )(x_ctx, x_smp, mod3, p_all, p_all, o_ctx, o_smp, lr, a2pad, a0, k_a, r_k, conv_w, ln_w, ln_b, ind, bind, wo_bf)


def _gla_kernel(q_ref, k_ref, v_ref, lrg_ref, gk2_ref, gkb_ref, m_ref, *rest, seq_len, n_seg, zero_init):
    s0_ref = None if zero_init else rest[0]
    o_ref, sout_ref, qe_scr, st_scr, e_scr = rest[-5:]
    z = pl.program_id(0)
    n_chunks = seq_len // CHUNK
    seg_chunks = n_chunks // n_seg
    group = min(GLA_GROUP, n_chunks)
    n_groups = n_chunks // group
    incl_b = m_ref[0] > 0.5
    gk2 = gk2_ref[0].astype(BF16)
    gkb = gkb_ref[0]
    scale = GLA_DK ** -0.5

    def phase1(it, carry):
        cs = [it * group + j for j in range(group)]
        rows = [_chunk_rows(c) for c in cs]
        logit = [_bdot(lrg_ref[rw, :], gk2) + gkb for rw in rows]
        g = [jax.nn.log_sigmoid(x) / GLA_GATE_NORM for x in logit]
        bc, b_last = zip(*[_scan_cumsum(x, z) for x in g])
        k = [k_ref[rw, :].astype(F32) for rw in rows]
        qe = [q_ref[rw, :].astype(F32) * scale * jnp.exp(x) for rw, x in zip(rows, bc)]
        ke = [kc * jnp.exp(-x) for kc, x in zip(k, bc)]
        kd = [kc * jnp.exp(bl - x) for kc, x, bl in zip(k, bc, b_last)]
        v = [v_ref[rw, :] for rw in rows]
        att = [jnp.where(incl_b, _bdot_nt(a, b), 0.0) for a, b in zip(qe, ke)]
        o_in = [_bdot(a, b) for a, b in zip(att, v)]
        vtk = [_bdot_tn(a, b) for a, b in zip(v, kd)]
        for j in range(group):
            o_ref[0, rows[j], :] = o_in[j]
            qe_scr[cs[j]] = qe[j]
            st_scr[cs[j]] = vtk[j]
            e_scr[cs[j]] = jnp.broadcast_to(jnp.exp(b_last[j]), (8, GLA_DK))
        return carry

    lax.fori_loop(0, n_groups, phase1, 0)

    def phase2(pos, sts):
        cidx = [sq * seg_chunks + pos + z * (seg_chunks - 1 - 2 * pos) for sq in range(n_seg)]
        new = [st * e_scr[c][0:1, :] + st_scr[c] for st, c in zip(sts, cidx)]
        for st, c in zip(sts, cidx):
            st_scr[c] = st
        return tuple(new)

    if zero_init:
        st_init = tuple(jnp.zeros((GLA_DV, GLA_DK), F32) for sq in range(n_seg))
    else:
        st_init = tuple(s0_ref[sq, 0, 0].T for sq in range(n_seg))
    st_fin = lax.fori_loop(0, seg_chunks, phase2, st_init)
    for sq in range(n_seg):
        sout_ref[sq, 0, 0] = st_fin[sq].T

    def phase3(it, carry):
        cs = [it * group + j for j in range(group)]
        inter = [_bdot_nt(qe_scr[c], st_scr[c]) for c in cs]
        for j in range(group):
            rows = _chunk_rows(cs[j])
            o_ref[0, rows, :] = o_ref[0, rows, :] + inter[j]
        return carry

    lax.fori_loop(0, n_groups, phase3, 0)


def _gla_scan(p_all, lrg, gk2pad, gkb, s0, masks, *, n_seq, seq_len, n_seg, tok_blk0):
    def tok(b):
        return tok_blk0 + b

    rows = n_seg * seq_len
    n_chunks = rows // CHUNK
    state_spec = pl.BlockSpec((n_seg, 1, 1, GLA_DK, GLA_DV), lambda z, b, h: (b, z, h, 0, 0))
    init = () if s0 is None else (s0,)
    return pl.pallas_call(
        functools.partial(_gla_kernel, seq_len=rows, n_seg=n_seg, zero_init=s0 is None),
        grid=(2, n_seq // n_seg, GLA_H),
        in_specs=[
            pl.BlockSpec((rows, GLA_DK), lambda z, b, h: (tok(b), h)),
            pl.BlockSpec((rows, GLA_DK), lambda z, b, h: (tok(b), GLA_H + h)),
            pl.BlockSpec((rows, GLA_DV), lambda z, b, h: (tok(b), GLA_H + h)),
            pl.BlockSpec((rows, LANES), lambda z, b, h: (tok(b), 0)),
            pl.BlockSpec((1, LANES, GLA_DK), lambda z, b, h: (z, 0, h)),
            pl.BlockSpec((1, 1, GLA_DK), lambda z, b, h: (z, 0, h)),
            pl.BlockSpec((1, CHUNK, CHUNK), lambda z, b, h: (z, 0, 0)),
        ] + [state_spec] * len(init),
        out_specs=[
            pl.BlockSpec((1, rows, GLA_DV), lambda z, b, h: (z, b, h)),
            state_spec,
        ],
        out_shape=[
            jax.ShapeDtypeStruct((2, n_seq * seq_len, D), F32),
            jax.ShapeDtypeStruct((n_seq, 2, GLA_H, GLA_DK, GLA_DV), F32),
        ],
        scratch_shapes=[
            pltpu.VMEM((n_chunks, CHUNK, GLA_DK), F32),
            pltpu.VMEM((n_chunks, GLA_DV, GLA_DK), F32),
            pltpu.VMEM((n_chunks, 8, GLA_DK), F32),
        ],
        compiler_params=pltpu.CompilerParams(
            dimension_semantics=("arbitrary", "arbitrary", "arbitrary"), vmem_limit_bytes=VMEM_LIMIT),
        name="gla_scan_%d" % seq_len,
    )(p_all, p_all, p_all, lrg, gk2pad, gkb, masks, *init)


def _odd_out_kernel(x_ref, mod_ref, zg_ref, oc_ref, os_ref, gn_ref, ind_ref, bind_ref, wo_ref, fg_ref,
                    yc_ref, ys_ref):
    i = pl.program_id(0)
    o = _sum_dirs(i, oc_ref, os_ref)
    ms = _group_stat(o * o, ind_ref, bind_ref)
    on = o * lax.rsqrt(ms + NORM_EPS) * gn_ref[...]
    og = on * _silu(zg_ref[...].astype(F32))
    y = _dot(og.astype(BF16), wo_ref[...])
    gate = mod_ref[0, 2:3, :]
    x = x_ref[...] + gate * y
    out = x * lax.rsqrt(jnp.mean(x * x, axis=-1, keepdims=True) + NORM_EPS) * fg_ref[...]

    @pl.when(i < N_CTX_TILES)
    def _():
        yc_ref[...] = out

    @pl.when(i >= N_CTX_TILES)
    def _():
        ys_ref[...] = out


def _odd_out(x, mod3, p_all, o_ctx, o_smp, g_norm, ind, bind, wo_bf, final_g):
    row = lambda i: (i, 0)
    const2 = lambda i: (0, 0)
    return pl.pallas_call(
        _odd_out_kernel,
        grid=(N_TOK // TM_OUT,),
        in_specs=[
            pl.BlockSpec((TM_OUT, D), row),
            pl.BlockSpec((1, 3, D), lambda i: (_tile_row(i, TM_OUT), 0, 0)),
            pl.BlockSpec((TM_OUT, D), lambda i: (i, 2)),
            pl.BlockSpec((2, TM_OUT, D), _ctx_tile),
            pl.BlockSpec((2, TM_OUT, D), _smp_tile),
            pl.BlockSpec((1, D), const2),
            pl.BlockSpec((D, LANES), const2),
            pl.BlockSpec((LANES, D), const2),
            pl.BlockSpec((D, D), const2),
            pl.BlockSpec((1, D), const2),
        ],
        out_specs=[
            pl.BlockSpec((TM_OUT, D), lambda i: _ctx_tile(i)[1:]),
            pl.BlockSpec((TM_OUT, D), lambda i: _smp_tile(i)[1:]),
        ],
        out_shape=[
            jax.ShapeDtypeStruct((N_CTX_TOK, D), F32),
            jax.ShapeDtypeStruct((N_TOK - N_CTX_TOK, D), F32),
        ],
        compiler_params=pltpu.CompilerParams(
            dimension_semantics=("arbitrary",), vmem_limit_bytes=VMEM_LIMIT),
        name="odd_out",
    )(x, mod3, p_all, o_ctx, o_smp, g_norm, ind, bind, wo_bf, final_g)


def _group_indicators(group):
    ch = jnp.arange(D) // group
    lane = jnp.arange(LANES)
    hit = (ch[:, None] == lane[None, :]).astype(F32)
    return (hit / float(group)).astype(BF16), hit.T.astype(BF16)


def _pad_dirs(w):
    r = w.shape[1]
    out = jnp.zeros((2, LANES, w.shape[2]), F32)
    out = out.at[0, 0:r].set(w[0])
    out = out.at[1, r:2 * r].set(w[1])
    return out


def kernel(x_prompt, x_sample, state_wkv, state_gla, c, c_ctx, norm_g, ada_w, ada_b, final_g, e_w_in, e_w_out, conv_w, wkv_w0, wkv_w1, wkv_w2, wkv_a0, wkv_a1, wkv_a2, wkv_k_k, wkv_k_a, wkv_r_k, wkv_ln_w, wkv_ln_b, o_w_in, o_w_out, gla_gk1, gla_gk2, gla_gk_b, gla_g_norm):
    x_ctx = x_prompt.reshape(N_CTX_TOK, D)
    x_smp = x_sample.reshape(N_TOK - N_CTX_TOK, D)
    cvec = jnp.zeros((8, D), F32).at[0].set(c_ctx).at[1:1 + SMP_B].set(c)
    mod = _modulation(cvec, ada_w, ada_b)
    masks = _chunk_masks()

    mod3 = mod[0].reshape(8, 3, D)
    wlr = jnp.concatenate([wkv_w1[0, 0], wkv_w1[0, 1], wkv_a1[0, 0], wkv_a1[0, 1]], axis=1)
    p_e, lr_e = _inproj(x_ctx, x_smp, 0, mod3, norm_g[0:1], e_w_in[0].astype(BF16), wlr.astype(BF16),
                        n_tanh=2 * 64)
    w2pad = _pad_dirs(wkv_w2[0])
    a2pad = _pad_dirs(wkv_a2[0])
    w0 = wkv_w0[0].reshape(2, 1, D)
    a0 = wkv_a0[0].reshape(2, 1, D)
    k_k = wkv_k_k[0].reshape(1, D)
    k_a = wkv_k_a[0].reshape(1, D)
    r_k = wkv_r_k[0].reshape(1, D)
    o_ctx, new_wkv = _wkv_scan(p_e, lr_e, w2pad, a2pad, w0, a0, k_k, k_a, None,
                               n_seq=CTX_B, seq_len=CTX_T, n_seg=WKV_CTX_SEG, tok_blk0=0)
    o_smp, _ = _wkv_scan(p_e, lr_e, w2pad, a2pad, w0, a0, k_k, k_a, state_wkv[:, 0],
                         n_seq=SMP_B, seq_len=SMP_T, n_seg=1, tok_blk0=N_CTX_TOK // SMP_T)
    ind64, bind64 = _group_indicators(WKV_N)
    x = _even_out(x_ctx, x_smp, mod3, p_e, o_ctx, o_smp, lr_e, a2pad, a0, k_a, r_k, conv_w[0],
                  wkv_ln_w[0].reshape(1, D), wkv_ln_b[0].reshape(1, D), ind64, bind64,
                  e_w_out[0].astype(BF16))

    mod3 = mod[1].reshape(8, 3, D)
    rank = gla_gk1.shape[-1]
    wlr = jnp.zeros((D, LANES), F32).at[:, 0:rank].set(gla_gk1[0, 0]).at[:, rank:2 * rank].set(gla_gk1[0, 1])
    p_o, lr_o = _inproj(x, x, N_CTX_TOK // TM_PROJ, mod3, norm_g[1:2], o_w_in[0].astype(BF16),
                        wlr.astype(BF16), n_tanh=0)
    gk2pad = _pad_dirs(gla_gk2[0])
    gkb = gla_gk_b[0].reshape(2, 1, GLA_H * GLA_DK)
    g_ctx, new_gla = _gla_scan(p_o, lr_o, gk2pad, gkb, None, masks,
                               n_seq=CTX_B, seq_len=CTX_T, n_seg=GLA_CTX_SEG, tok_blk0=0)
    g_smp, _ = _gla_scan(p_o, lr_o, gk2pad, gkb, state_gla[:, 0], masks,
                         n_seq=SMP_B, seq_len=SMP_T, n_seg=1, tok_blk0=N_CTX_TOK // SMP_T)
    ind256, bind256 = _group_indicators(GLA_DV)
    y_ctx, y_smp = _odd_out(x, mod3, p_o, g_ctx, g_smp, jnp.tile(gla_g_norm[0], GLA_H).reshape(1, D),
                            ind256, bind256, o_w_out[0].astype(BF16), final_g.reshape(1, D))
    return (y_ctx.reshape(CTX_B, CTX_T, D), y_smp.reshape(SMP_B, SMP_T, D), new_wkv[:, None], new_gla[:, None])
```

```python
import functools
import math

import jax
import jax.numpy as jnp
from jax import lax
from jax.experimental import pallas as pl
from jax.experimental.pallas import tpu as pltpu

F32 = jnp.float32
BF16 = jnp.bfloat16
HI = lax.Precision.HIGHEST

D = 1024
N_CTX_TOK = 16 * 256
N_TOK = 2 * N_CTX_TOK
CTX_B, CTX_T = 16, 256
SMP_B, SMP_T = 2, 2048
GRID_W = 64
WKV_H, WKV_N = 16, 64
GLA_H, GLA_DK, GLA_DV = 4, 128, 256
GLA_GATE_NORM = 16.0
GN_EPS = 64e-5
NORM_EPS = 1e-6
CHUNK = 64
WKV_GROUP = 4
WKV_HEADS_STEP = 8
WKV_CTX_SEG = 8
GLA_CTX_SEG = 8
GLA_GROUP = 32
LANES = 128
TM_PROJ = 1024
TN_PROJ_MAX = 2048
TM_OUT = 256
TM_ODD = 512
VMEM_LIMIT = 56 * 1024 * 1024


def _silu(x):
    return x * jax.nn.sigmoid(x)


def _dot(a, b, precision=None):
    return jnp.dot(a, b, preferred_element_type=F32, precision=precision)


def _dot_nt(a, b, precision=None):
    return lax.dot_general(a, b, (((1,), (1,)), ((), ())), preferred_element_type=F32, precision=precision)


def _dot_tn(a, b, precision=None):
    return lax.dot_general(a, b, (((0,), (0,)), ((), ())), preferred_element_type=F32, precision=precision)


def _bdot(a, b):
    return _dot(a.astype(BF16), b.astype(BF16))


def _bdot_nt(a, b):
    return _dot_nt(a.astype(BF16), b.astype(BF16))


def _bdot_tn(a, b):
    return _dot_tn(a.astype(BF16), b.astype(BF16))


def _split2(x):
    hi = x.astype(BF16)
    lo = (x - hi.astype(F32)).astype(BF16)
    return hi, lo


def _scan_cumsum(x, z):
    n = x.shape[0]
    row = lax.broadcasted_iota(jnp.int32, (n, 1), 0)
    pre = x
    s = 1
    while s < n:
        pre = pre + jnp.where(row >= s, pltpu.roll(pre, s, 0), 0.0)
        s *= 2
    tot = pre[n - 1:n, :]
    return jnp.where(z == 0, pre, tot - pre + x), tot


def _tile_row(i, tile):
    ctx_tiles = N_CTX_TOK // tile
    per_req = SMP_T // tile
    return jnp.where(i < ctx_tiles, 0, 1 + (i - ctx_tiles) // per_req)


def _mod_kernel(c_ref, w_ref, b_ref, o_ref):
    cf = _silu(c_ref[...])
    o_ref[0] = _dot(cf, w_ref[0], HI) + b_ref[0]


def _modulation(cvec, ada_w, ada_b):
    depth = ada_w.shape[0]
    return pl.pallas_call(
        _mod_kernel,
        grid=(depth, 3),
        in_specs=[
            pl.BlockSpec((8, D), lambda l, j: (0, 0)),
            pl.BlockSpec((1, D, D), lambda l, j: (l, 0, j)),
            pl.BlockSpec((1, 1, D), lambda l, j: (l, 0, j)),
        ],
        out_specs=pl.BlockSpec((1, 8, D), lambda l, j: (l, 0, j)),
        out_shape=jax.ShapeDtypeStruct((depth, 8, 3 * D), F32),
        compiler_params=pltpu.CompilerParams(
            dimension_semantics=("arbitrary", "arbitrary"), vmem_limit_bytes=VMEM_LIMIT),
        name="adaln_mod",
    )(cvec, ada_w, ada_b.reshape(depth, 1, 3 * D))


def _inproj_kernel(xa_ref, xb_ref, mod_ref, g_ref, w_ref, wlr_ref, p_ref, lr_ref, h_scr, *, n_tanh):
    i = pl.program_id(0)
    j = pl.program_id(1)

    @pl.when(j == 0)
    def _():
        x = jnp.where(i < N_CTX_TOK // TM_PROJ, xa_ref[...], xb_ref[...])
        y = x * lax.rsqrt(jnp.mean(x * x, axis=-1, keepdims=True) + NORM_EPS) * g_ref[...]
        shift = mod_ref[0, 0:1, :]
        scale = mod_ref[0, 1:2, :]
        h = y * (1.0 + scale) + shift
        hb = h.astype(BF16)
        h_scr[...] = hb
        lr = _dot(hb, wlr_ref[...])
        if n_tanh:
            lane = lax.broadcasted_iota(jnp.int32, lr.shape, 1)
            lr = jnp.where(lane < n_tanh, jnp.tanh(lr), lr)
        lr_ref[...] = lr

    p_ref[...] = _dot(h_scr[...], w_ref[...]).astype(p_ref.dtype)


def _inproj(x_ctx, x_smp, smp_tile0, mod3, g, w_bf, wlr_bf, n_tanh):
    n_out = w_bf.shape[1]
    n_lr = wlr_bf.shape[1]
    n_ctx = N_CTX_TOK // TM_PROJ
    tn = max(t for t in range(256, TN_PROJ_MAX + 1, 256) if n_out % t == 0)
    return pl.pallas_call(
        functools.partial(_inproj_kernel, n_tanh=n_tanh),
        grid=(N_TOK // TM_PROJ, n_out // tn),
        in_specs=[
            pl.BlockSpec((TM_PROJ, D), lambda i, j: (jnp.minimum(i, n_ctx - 1), 0)),
            pl.BlockSpec((TM_PROJ, D), lambda i, j: (smp_tile0 + jnp.maximum(i - n_ctx, 0), 0)),
            pl.BlockSpec((1, 3, D), lambda i, j: (_tile_row(i, TM_PROJ), 0, 0)),
            pl.BlockSpec((1, D), lambda i, j: (0, 0)),
            pl.BlockSpec((D, tn), lambda i, j: (0, j)),
            pl.BlockSpec((D, n_lr), lambda i, j: (0, 0)),
        ],
        out_specs=[
            pl.BlockSpec((TM_PROJ, tn), lambda i, j: (i, j)),
            pl.BlockSpec((TM_PROJ, n_lr), lambda i, j: (i, 0)),
        ],
        out_shape=[
            jax.ShapeDtypeStruct((N_TOK, n_out), BF16),
            jax.ShapeDtypeStruct((N_TOK, n_lr), F32),
        ],
        scratch_shapes=[pltpu.VMEM((TM_PROJ, D), BF16)],
        compiler_params=pltpu.CompilerParams(
            dimension_semantics=("arbitrary", "arbitrary"), vmem_limit_bytes=VMEM_LIMIT),
        name="inproj",
    )(x_ctx, x_smp, mod3, g, w_bf, wlr_bf)


def _chunk_masks():
    idx = jnp.arange(CHUNK)
    lower = (idx[:, None] >= idx[None, :]).astype(F32)
    return jnp.stack([lower, lower.T], axis=0)


def _chunk_rows(c):
    return pl.ds(pl.multiple_of(c * CHUNK, CHUNK), CHUNK)


def _pair_blockdiag(x):
    left = lax.broadcasted_iota(jnp.int32, x.shape, 1) < WKV_N
    xb = x.astype(BF16)
    zero = jnp.zeros_like(xb)
    return jnp.concatenate([jnp.where(left, xb, zero), jnp.where(left, zero, xb)], axis=0)


def _pair_mm(a, b):
    return _dot(a.astype(BF16), _pair_blockdiag(b))


def _inv_unit_tri(lmats, tick=lambda: None):
    row = lax.broadcasted_iota(jnp.int32, (CHUNK, LANES), 0)
    col = lax.broadcasted_iota(jnp.int32, (CHUNK, LANES), 1) & (CHUNK - 1)
    eye = (row == col).astype(F32)

    def same_block(n):
        sh = n.bit_length() - 1
        return lax.shift_right_logical(row, sh) == lax.shift_right_logical(col, sh)

    blk16 = same_block(16)
    l16 = [jnp.where(blk16, m, 0.0) for m in lmats]
    x = [eye - m for m in l16]
    p = [_pair_mm(m, m) for m in l16]
    for step in range(3):
        x = [xi + _pair_mm(xi, pi) for xi, pi in zip(x, p)]
        if step < 2:
            p = [_pair_mm(pi, pi) for pi in p]
        if step == 0:
            tick()
    n = 16
    while n < CHUNK:
        off = same_block(2 * n) & jnp.logical_not(same_block(n))
        xl = [_pair_mm(xi, jnp.where(off, m, 0.0)) for xi, m in zip(x, lmats)]
        x = [xi - _pair_mm(yi, xi) for xi, yi in zip(x, xl)]
        if n == 16:
            tick()
        n *= 2
    return x


_KT, _RT, _KH, _BH, _KP, _BP = range(6)


def _wkv_kernel(r_ref, k_ref, v_ref, lrw_ref, lra_ref, w2_ref, a2_ref, w0_ref, a0_ref,
                kk_ref, ka_ref, *rest, seq_len, n_seg, zero_init):
    s0_ref = None if zero_init else rest[0]
    o_ref, sout_ref, rp_scr, w_scr, h_scr, d_scr, rt_a, ops_a = rest[-8:]
    z = pl.program_id(0)
    n_chunks = seq_len // CHUNK
    seg_chunks = n_chunks // n_seg
    n_groups = n_chunks // WKV_GROUP
    pairs = range(WKV_HEADS_STEP // 2)
    row = lax.broadcasted_iota(jnp.int32, (CHUNK, LANES), 0)
    lane = lax.broadcasted_iota(jnp.int32, (CHUNK, LANES), 1)
    col = lane & (CHUNK - 1)
    left = lane < WKV_N
    incl2 = ((z == 0) & (row >= col)) | ((z == 1) & (row <= col))
    strict2 = incl2 & (row != col)
    row_b = lax.broadcasted_iota(jnp.int32, (LANES, LANES), 0) < WKV_N
    lane_b = lax.broadcasted_iota(jnp.int32, (LANES, LANES), 1) < WKV_N
    diag_blocks = row_b == lane_b
    w2 = w2_ref[0].astype(BF16)
    a2 = a2_ref[0].astype(BF16)
    w0 = w0_ref[0]
    a0 = a0_ref[0]
    k_k = kk_ref[...]
    k_a = ka_ref[...]

    def pair_sl(p):
        return slice(p * LANES, (p + 1) * LANES)

    def stack(a, b):
        return jnp.concatenate([a, b], axis=0)

    def prepare_chunk(gi, j, rt_scr, ops_scr):
        c = gi * WKV_GROUP + j
        rows = _chunk_rows(c)
        w_raw = w0 + _bdot(lrw_ref[rows, :], w2)
        a = jax.nn.sigmoid(a0 + _bdot(lra_ref[rows, :], a2))
        lw = -math.exp(-0.5) * jax.nn.sigmoid(w_raw)
        g, gtot = _scan_cumsum(lw, z)
        r = r_ref[rows, :].astype(F32)
        k = k_ref[rows, :].astype(F32)
        kkr = k * k_k
        sq = kkr * kkr
        nrm = []
        for p in pairs:
            sq_p = sq[:, pair_sl(p)]
            n0 = jnp.sum(jnp.where(left, sq_p, 0.0), axis=-1, keepdims=True)
            n1 = jnp.sum(jnp.where(left, 0.0, sq_p), axis=-1, keepdims=True)
            nrm.append(jnp.where(left, jnp.sqrt(n0), jnp.sqrt(n1)))
        kk = kkr / jnp.maximum(jnp.concatenate(nrm, axis=-1), 1e-12)
        b = kk * a
        kmod = k * (1.0 + (a - 1.0) * k_a)
        e_ng = jnp.exp(-g)
        e_gc = jnp.exp(gtot - g)
        rt = r * jnp.exp(g)
        rt_scr[j] = rt
        ops_scr[j, _KT] = (kk * jnp.exp(g - lw)).astype(BF16)
        ops_scr[j, _RT] = rt.astype(BF16)
        ops_scr[j, _KH] = (kmod * e_ng).astype(BF16)
        ops_scr[j, _BH] = (b * e_ng).astype(BF16)
        ops_scr[j, _KP] = (kmod * e_gc).astype(BF16)
        ops_scr[j, _BP] = (b * e_gc).astype(BF16)
        d_scr[c] = jnp.broadcast_to(jnp.exp(gtot), (8, WKV_HEADS_STEP * WKV_N))

    def chains(gi, rt_scr, ops_scr, fillers):
        pending = iter(fillers)

        def tick():
            next(pending, lambda: None)()

        cs = [gi * WKV_GROUP + j for j in range(WKV_GROUP)]
        rows = [_chunk_rows(c) for c in cs]
        cps = [(j, p) for j in range(WKV_GROUP) for p in pairs]
        n_cp = range(len(cps))

        def opnd(q, i):
            j, p = cps[q]
            return ops_scr[j, i, :, pair_sl(p)]

        def vals(q):
            j, p = cps[q]
            return v_ref[rows[j], pair_sl(p)]

        bk = [stack(_pair_blockdiag(opnd(q, _BH)), _pair_blockdiag(opnd(q, _KH))) for q in n_cp]
        g_k = [_dot_nt(opnd(q, _KT), bk[q]) for q in n_cp]
        g_r = [_dot_nt(opnd(q, _RT), bk[q]) for q in n_cp]
        lmat = [jnp.where(strict2, g[:, :LANES], 0.0) for g in g_k]
        mkk = [jnp.where(strict2, g[:, LANES:], 0.0) for g in g_k]
        mrb = [jnp.where(incl2, g[:, :LANES], 0.0).astype(BF16) for g in g_r]
        mrk = [jnp.where(incl2, g[:, LANES:], 0.0).astype(BF16) for g in g_r]
        tick()
        tinv = _inv_unit_tri(lmat, tick)
        mv = [_pair_mm(mkk[q], vals(q)) for q in n_cp]
        pm = [_pair_mm(tinv[q], opnd(q, _KT)).astype(BF16) for q in n_cp]
        qm = [_pair_mm(tinv[q], mv[q]).astype(BF16) for q in n_cp]
        tick()
        corr_p = [_dot(mrb[q], _pair_blockdiag(pm[q])) for q in n_cp]
        op_m = [_dot(jnp.concatenate([mrk[q], mrb[q]], axis=-1),
                     stack(_pair_blockdiag(vals(q)), -_pair_blockdiag(qm[q]))) for q in n_cp]
        wfull = [_dot_tn(pm[q], opnd(q, _BP)) for q in n_cp]
        hfull = [_dot_tn(stack(vals(q), qm[q]), stack(opnd(q, _KP), -opnd(q, _BP))) for q in n_cp]
        for q in n_cp:
            j, p = cps[q]
            o_ref[0, rows[j], pair_sl(p)] = op_m[q]
            rp_scr[cs[j], p] = (rt_scr[j, :, pair_sl(p)] - corr_p[q]).astype(BF16)
            w_hi, w_lo = _split2(jnp.where(diag_blocks, wfull[q], 0.0))
            w_scr[cs[j], p, 0] = w_hi
            w_scr[cs[j], p, 1] = w_lo
            h_scr[cs[j], p] = jnp.where(left, hfull[q][:WKV_N], hfull[q][WKV_N:])

        for rest_filler in pending:
            rest_filler()

    groups_per_seq = seg_chunks // WKV_GROUP

    def scan_group(i):
        return i + z * (n_groups - 1 - 2 * i)

    def init_state(sq, p):
        if zero_init:
            return jnp.zeros((WKV_N, LANES), F32)
        return jnp.concatenate([s0_ref[sq, 0, 2 * p], s0_ref[sq, 0, 2 * p + 1]], axis=-1)

    def walk_steps(i, state):
        g = scan_group(i)
        sq = g // groups_per_seq
        first = (i % groups_per_seq) == 0

        def step(jj):
            if jj == 0:
                state[:] = [jnp.where(first, init_state(sq, p), state[p]) for p in pairs]
            c = g * WKV_GROUP + jj + z * (WKV_GROUP - 1 - 2 * jj)
            split = [_split2(sp) for sp in state]
            sw_hi = [_dot(stack(split[p][0], split[p][1]), w_scr[c, p, 0]) for p in pairs]
            sw_lo = [_dot(split[p][0], w_scr[c, p, 1]) for p in pairs]
            inter = [_dot_nt(rp_scr[c, p], _pair_blockdiag(split[p][0])) for p in pairs]
            d_tot = d_scr[c][0:1, :]
            state[:] = [state[p] * d_tot[:, pair_sl(p)] - (sw_hi[p][:CHUNK] + sw_hi[p][CHUNK:] + sw_lo[p])
                        + h_scr[c, p] for p in pairs]
            rows = _chunk_rows(c)
            o_ref[0, rows, :] = o_ref[0, rows, :] + jnp.concatenate(inter, axis=-1)
            if jj == WKV_GROUP - 1:
                for p in pairs:
                    sout_ref[sq, 0, 2 * p] = state[p][:, :WKV_N]
                    sout_ref[sq, 0, 2 * p + 1] = state[p][:, WKV_N:]

        return [functools.partial(step, jj) for jj in range(WKV_GROUP)]

    def group_terms(i, fillers):
        g = scan_group(i)
        for j in range(WKV_GROUP):
            prepare_chunk(g, j, rt_a, ops_a)
        chains(g, rt_a, ops_a, fillers)

    group_terms(0, [])

    def body(i, carry):
        state = list(carry)
        group_terms(i, walk_steps(i - 1, state))
        return tuple(state)

    carry = lax.fori_loop(1, n_groups, body, tuple(jnp.zeros((WKV_N, LANES), F32) for p in pairs))
    state = list(carry)
    for last_step in walk_steps(n_groups - 1, state):
        last_step()


def _wkv_scan(p_all, lr, w2pad, a2pad, w0, a0, k_k, k_a, s0, *, n_seq, seq_len, n_seg, tok_blk0):
    def tok(b):
        return tok_blk0 + b

    rows = n_seg * seq_len
    width = WKV_HEADS_STEP * WKV_N
    n_pairs = WKV_HEADS_STEP // 2
    col = lambda base: (lambda z, b, hq: (tok(b), base // width + hq))
    vec = lambda z, b, hq: (z, 0, hq)
    n_chunks = rows // CHUNK
    state_spec = pl.BlockSpec((n_seg, 1, WKV_HEADS_STEP, WKV_N, WKV_N), lambda z, b, hq: (b, z, hq, 0, 0))
    init = () if s0 is None else (s0,)
    return pl.pallas_call(
        functools.partial(_wkv_kernel, seq_len=rows, n_seg=n_seg, zero_init=s0 is None),
        grid=(2, n_seq // n_seg, WKV_H // WKV_HEADS_STEP),
        in_specs=[
            pl.BlockSpec((rows, width), col(4 * D)),
            pl.BlockSpec((rows, width), col(5 * D)),
            pl.BlockSpec((rows, width), col(6 * D)),
            pl.BlockSpec((rows, LANES), lambda z, b, hq: (tok(b), 0)),
            pl.BlockSpec((rows, LANES), lambda z, b, hq: (tok(b), 1)),
            pl.BlockSpec((1, LANES, width), vec),
            pl.BlockSpec((1, LANES, width), vec),
            pl.BlockSpec((1, 1, width), vec),
            pl.BlockSpec((1, 1, width), vec),
            pl.BlockSpec((1, width), lambda z, b, hq: (0, hq)),
            pl.BlockSpec((1, width), lambda z, b, hq: (0, hq)),
        ] + [state_spec] * len(init),
        out_specs=[
            pl.BlockSpec((1, rows, width), lambda z, b, hq: (z, b, hq)),
            state_spec,
        ],
        out_shape=[
            jax.ShapeDtypeStruct((2, n_seq * seq_len, D), F32),
            jax.ShapeDtypeStruct((n_seq, 2, WKV_H, WKV_N, WKV_N), F32),
        ],
        scratch_shapes=[
            pltpu.VMEM((n_chunks, n_pairs, CHUNK, LANES), BF16),
            pltpu.VMEM((n_chunks, n_pairs, 2, LANES, LANES), BF16),
            pltpu.VMEM((n_chunks, n_pairs, CHUNK, LANES), F32),
            pltpu.VMEM((n_chunks, 8, width), F32),
            pltpu.VMEM((WKV_GROUP, CHUNK, width), F32),
            pltpu.VMEM((WKV_GROUP, 6, CHUNK, width), BF16),
        ],
        compiler_params=pltpu.CompilerParams(
            dimension_semantics=("arbitrary", "arbitrary", "arbitrary"), vmem_limit_bytes=VMEM_LIMIT),
        name="wkv_scan_%d" % seq_len,
    )(p_all, p_all, p_all, lr, lr, w2pad, a2pad, w0, a0, k_k, k_a, *init)


def _dot2(x, w_bf):
    hi, lo = _split2(x)
    return _dot(hi, w_bf) + _dot(lo, w_bf)


def _group_stat(x, ind_ref, bind_ref, split_bcast=False):
    per_group = _dot(x.astype(BF16), ind_ref[...])
    if split_bcast:
        return _dot2(per_group, bind_ref[...])
    return _dot(per_group.astype(BF16), bind_ref[...])


N_CTX_TILES = N_CTX_TOK // TM_OUT


def _sum_dirs(i, oc_ref, os_ref):
    is_ctx = i < N_CTX_TILES
    return jnp.where(is_ctx, oc_ref[0], os_ref[0]) + jnp.where(is_ctx, oc_ref[1], os_ref[1])


def _ctx_tile(i):
    return (0, jnp.minimum(i, N_CTX_TILES - 1), 0)


def _smp_tile(i):
    return (0, jnp.maximum(i - N_CTX_TILES, 0), 0)


def _even_out_kernel(xc_ref, xs_ref, mod_ref, pc_ref, pw_ref, oc_ref, os_ref, lra_ref, a2_ref, a0_ref, ka_ref,
                     rk_ref, cw_ref, lnw_ref, lnb_ref, ind_ref, bind_ref, wo_ref, out_ref):
    i = pl.program_id(0)
    u = pc_ref[:, 0:D].astype(F32)
    gb = pc_ref[:, D:2 * D].astype(F32)
    gc = pc_ref[:, 2 * D:3 * D].astype(F32)
    zc = pc_ref[:, 3 * D:4 * D].astype(F32)
    xg = gc * u
    row_len = jnp.where(i < N_CTX_TILES, CTX_T, GRID_W)
    pos = lax.broadcasted_iota(jnp.int32, (TM_OUT, 1), 0) & (row_len - 1)
    prev = jnp.where(pos == 0, 0.0, pltpu.roll(xg, 1, 0))
    nxt = jnp.where(pos == row_len - 1, 0.0, pltpu.roll(xg, TM_OUT - 1, 0))
    conv = cw_ref[0:1, :] * prev + cw_ref[1:2, :] * xg + cw_ref[2:3, :] * nxt
    o_conv = _silu(zc) * gb * conv
    r = pw_ref[:, 0:D].astype(F32)
    k = pw_ref[:, D:2 * D].astype(F32)
    v = pw_ref[:, 2 * D:3 * D].astype(F32)
    zw = pw_ref[:, 3 * D:4 * D].astype(F32)
    o = _sum_dirs(i, oc_ref, os_ref)
    mu = _group_stat(o, ind_ref, bind_ref, split_bcast=True)
    dlt = o - mu
    var = _group_stat(dlt * dlt, ind_ref, bind_ref)
    gn = dlt * lax.rsqrt(var + GN_EPS) * lnw_ref[...] + lnb_ref[...]
    lra = lra_ref[...]
    ic0 = jax.nn.sigmoid(a0_ref[0] + _bdot(lra, a2_ref[0]))
    ic1 = jax.nn.sigmoid(a0_ref[1] + _bdot(lra, a2_ref[1]))
    ksum = k * (2.0 + (ic0 + ic1 - 2.0) * ka_ref[...])
    bonus = _group_stat(r * ksum * rk_ref[...], ind_ref, bind_ref) * float(WKV_N)
    o_wkv = (gn + bonus * v) * _silu(zw)
    y = _dot(o_conv.astype(BF16), wo_ref[0:D, :]) + _dot(o_wkv.astype(BF16), wo_ref[D:2 * D, :])
    gate = mod_ref[0, 2:3, :]
    out_ref[...] = jnp.where(i < N_CTX_TILES, xc_ref[...], xs_ref[...]) + gate * y


def _even_out(x_ctx, x_smp, mod3, p_all, o_ctx, o_smp, lr, a2pad, a0, k_a, r_k, conv_w, ln_w, ln_b, ind, bind,
              wo_bf):
    row = lambda i: (i, 0)
    const2 = lambda i: (0, 0)
    const3 = lambda i: (0, 0, 0)
    return pl.pallas_call(
        _even_out_kernel,
        grid=(N_TOK // TM_OUT,),
        in_specs=[
            pl.BlockSpec((TM_OUT, D), lambda i: _ctx_tile(i)[1:]),
            pl.BlockSpec((TM_OUT, D), lambda i: _smp_tile(i)[1:]),
            pl.BlockSpec((1, 3, D), lambda i: (_tile_row(i, TM_OUT), 0, 0)),
            pl.BlockSpec((TM_OUT, 4 * D), lambda i: (i, 0)),
            pl.BlockSpec((TM_OUT, 4 * D), lambda i: (i, 1)),
            pl.BlockSpec((2, TM_OUT, D), _ctx_tile),
            pl.BlockSpec((2, TM_OUT, D), _smp_tile),
            pl.BlockSpec((TM_OUT, LANES), lambda i: (i, 1)),
            pl.BlockSpec((2, LANES, D), const3),
            pl.BlockSpec((2, 1, D), const3),
            pl.BlockSpec((1, D), const2),
            pl.BlockSpec((1, D), const2),
            pl.BlockSpec((3, D), const2),
            pl.BlockSpec((1, D), const2),
            pl.BlockSpec((1, D), const2),
            pl.BlockSpec((D, LANES), const2),
            pl.BlockSpec((LANES, D), const2),
            pl.BlockSpec((2 * D, D), const2),
        ],
        out_specs=pl.BlockSpec((TM_OUT, D), row),
        out_shape=jax.ShapeDtypeStruct((N_TOK, D), F32),
        compiler_params=pltpu.CompilerParams(
            dimension_semantics=("arbitrary",), vmem_limit_bytes=VMEM_LIMIT),
        name="even_out",
    )(x_ctx, x_smp, mod3, p_all, p_all, o_ctx, o_smp, lr, a2pad, a0, k_a, r_k, conv_w, ln_w, ln_b, ind, bind, wo_bf)


def _gla_kernel(q_ref, k_ref, v_ref, lrg_ref, gk2_ref, gkb_ref, m_ref, *rest, seq_len, n_seg, zero_init):
    s0_ref = None if zero_init else rest[0]
    o_ref, sout_ref, qe_scr, st_scr, e_scr = rest[-5:]
    z = pl.program_id(0)
    n_chunks = seq_len // CHUNK
    seg_chunks = n_chunks // n_seg
    group = min(GLA_GROUP, n_chunks)
    n_groups = n_chunks // group
    incl_b = m_ref[0] > 0.5
    gk2 = gk2_ref[0].astype(BF16)
    gkb = gkb_ref[0]
    scale = GLA_DK ** -0.5

    def phase1(it, carry):
        cs = [it * group + j for j in range(group)]
        rows = [_chunk_rows(c) for c in cs]
        logit = [_bdot(lrg_ref[rw, :], gk2) + gkb for rw in rows]
        g = [jax.nn.log_sigmoid(x) / GLA_GATE_NORM for x in logit]
        bc, b_last = zip(*[_scan_cumsum(x, z) for x in g])
        k = [k_ref[rw, :].astype(F32) for rw in rows]
        qe = [q_ref[rw, :].astype(F32) * scale * jnp.exp(x) for rw, x in zip(rows, bc)]
        ke = [kc * jnp.exp(-x) for kc, x in zip(k, bc)]
        kd = [kc * jnp.exp(bl - x) for kc, x, bl in zip(k, bc, b_last)]
        v = [v_ref[rw, :] for rw in rows]
        att = [jnp.where(incl_b, _bdot_nt(a, b), 0.0) for a, b in zip(qe, ke)]
        o_in = [_bdot(a, b) for a, b in zip(att, v)]
        vtk = [_bdot_tn(a, b) for a, b in zip(v, kd)]
        for j in range(group):
            o_ref[0, rows[j], :] = o_in[j]
            qe_scr[cs[j]] = qe[j]
            st_scr[cs[j]] = vtk[j]
            e_scr[cs[j]] = jnp.broadcast_to(jnp.exp(b_last[j]), (8, GLA_DK))
        return carry

    lax.fori_loop(0, n_groups, phase1, 0)

    def phase2(pos, sts):
        cidx = [sq * seg_chunks + pos + z * (seg_chunks - 1 - 2 * pos) for sq in range(n_seg)]
        new = [st * e_scr[c][0:1, :] + st_scr[c] for st, c in zip(sts, cidx)]
        for st, c in zip(sts, cidx):
            st_scr[c] = st
        return tuple(new)

    if zero_init:
        st_init = tuple(jnp.zeros((GLA_DV, GLA_DK), F32) for sq in range(n_seg))
    else:
        st_init = tuple(s0_ref[sq, 0, 0].T for sq in range(n_seg))
    st_fin = lax.fori_loop(0, seg_chunks, phase2, st_init)
    for sq in range(n_seg):
        sout_ref[sq, 0, 0] = st_fin[sq].T

    def phase3(it, carry):
        cs = [it * group + j for j in range(group)]
        inter = [_bdot_nt(qe_scr[c], st_scr[c]) for c in cs]
        for j in range(group):
            rows = _chunk_rows(cs[j])
            o_ref[0, rows, :] = o_ref[0, rows, :] + inter[j]
        return carry

    lax.fori_loop(0, n_groups, phase3, 0)


def _gla_scan(p_all, lrg, gk2pad, gkb, s0, masks, *, n_seq, seq_len, n_seg, tok_blk0):
    def tok(b):
        return tok_blk0 + b

    rows = n_seg * seq_len
    n_chunks = rows // CHUNK
    state_spec = pl.BlockSpec((n_seg, 1, 1, GLA_DK, GLA_DV), lambda z, b, h: (b, z, h, 0, 0))
    init = () if s0 is None else (s0,)
    return pl.pallas_call(
        functools.partial(_gla_kernel, seq_len=rows, n_seg=n_seg, zero_init=s0 is None),
        grid=(2, n_seq // n_seg, GLA_H),
        in_specs=[
            pl.BlockSpec((rows, GLA_DK), lambda z, b, h: (tok(b), h)),
            pl.BlockSpec((rows, GLA_DK), lambda z, b, h: (tok(b), GLA_H + h)),
            pl.BlockSpec((rows, GLA_DV), lambda z, b, h: (tok(b), GLA_H + h)),
            pl.BlockSpec((rows, LANES), lambda z, b, h: (tok(b), 0)),
            pl.BlockSpec((1, LANES, GLA_DK), lambda z, b, h: (z, 0, h)),
            pl.BlockSpec((1, 1, GLA_DK), lambda z, b, h: (z, 0, h)),
            pl.BlockSpec((1, CHUNK, CHUNK), lambda z, b, h: (z, 0, 0)),
        ] + [state_spec] * len(init),
        out_specs=[
            pl.BlockSpec((1, rows, GLA_DV), lambda z, b, h: (z, b, h)),
            state_spec,
        ],
        out_shape=[
            jax.ShapeDtypeStruct((2, n_seq * seq_len, D), F32),
            jax.ShapeDtypeStruct((n_seq, 2, GLA_H, GLA_DK, GLA_DV), F32),
        ],
        scratch_shapes=[
            pltpu.VMEM((n_chunks, CHUNK, GLA_DK), F32),
            pltpu.VMEM((n_chunks, GLA_DV, GLA_DK), F32),
            pltpu.VMEM((n_chunks, 8, GLA_DK), F32),
        ],
        compiler_params=pltpu.CompilerParams(
            dimension_semantics=("arbitrary", "arbitrary", "arbitrary"), vmem_limit_bytes=VMEM_LIMIT),
        name="gla_scan_%d" % seq_len,
    )(p_all, p_all, p_all, lrg, gk2pad, gkb, masks, *init)


def _odd_out_kernel(x_ref, mod_ref, zg_ref, oc_ref, os_ref, gn_ref, ind_ref, bind_ref, wo_ref, fg_ref,
                    yc_ref, ys_ref):
    i = pl.program_id(0)
    is_ctx = i < N_CTX_TOK // TM_ODD
    o = jnp.where(is_ctx, oc_ref[0], os_ref[0]) + jnp.where(is_ctx, oc_ref[1], os_ref[1])
    ms = _group_stat(o * o, ind_ref, bind_ref)
    on = o * lax.rsqrt(ms + NORM_EPS) * gn_ref[...]
    og = on * _silu(zg_ref[...].astype(F32))
    y = _dot(og.astype(BF16), wo_ref[...])
    gate = mod_ref[0, 2:3, :]
    x = x_ref[...] + gate * y
    out = x * lax.rsqrt(jnp.mean(x * x, axis=-1, keepdims=True) + NORM_EPS) * fg_ref[...]

    @pl.when(is_ctx)
    def _():
        yc_ref[...] = out

    @pl.when(jnp.logical_not(is_ctx))
    def _():
        ys_ref[...] = out


def _odd_out(x, mod3, p_all, o_ctx, o_smp, g_norm, ind, bind, wo_bf, final_g):
    n_ctx = N_CTX_TOK // TM_ODD
    row = lambda i: (i, 0)
    ctx_tile = lambda i: jnp.minimum(i, n_ctx - 1)
    smp_tile = lambda i: jnp.maximum(i - n_ctx, 0)
    const2 = lambda shape: pl.BlockSpec(shape, lambda i: (0, 0), pipeline_mode=pl.Buffered(1))
    return pl.pallas_call(
        _odd_out_kernel,
        grid=(N_TOK // TM_ODD,),
        in_specs=[
            pl.BlockSpec((TM_ODD, D), row),
            pl.BlockSpec((1, 3, D), lambda i: (_tile_row(i, TM_ODD), 0, 0)),
            pl.BlockSpec((TM_ODD, D), lambda i: (i, 2)),
            pl.BlockSpec((2, TM_ODD, D), lambda i: (0, ctx_tile(i), 0)),
            pl.BlockSpec((2, TM_ODD, D), lambda i: (0, smp_tile(i), 0)),
            const2((1, D)),
            const2((D, LANES)),
            const2((LANES, D)),
            const2((D, D)),
            const2((1, D)),
        ],
        out_specs=[
            pl.BlockSpec((TM_ODD, D), lambda i: (ctx_tile(i), 0)),
            pl.BlockSpec((TM_ODD, D), lambda i: (smp_tile(i), 0)),
        ],
        out_shape=[
            jax.ShapeDtypeStruct((N_CTX_TOK, D), F32),
            jax.ShapeDtypeStruct((N_TOK - N_CTX_TOK, D), F32),
        ],
        compiler_params=pltpu.CompilerParams(
            dimension_semantics=("arbitrary",), vmem_limit_bytes=VMEM_LIMIT),
        name="odd_out",
    )(x, mod3, p_all, o_ctx, o_smp, g_norm, ind, bind, wo_bf, final_g)


def _group_indicators(group):
    ch = jnp.arange(D) // group
    lane = jnp.arange(LANES)
    hit = (ch[:, None] == lane[None, :]).astype(F32)
    return (hit / float(group)).astype(BF16), hit.T.astype(BF16)


def _pad_dirs(w):
    r = w.shape[1]
    out = jnp.zeros((2, LANES, w.shape[2]), F32)
    out = out.at[0, 0:r].set(w[0])
    out = out.at[1, r:2 * r].set(w[1])
    return out


def kernel(x_prompt, x_sample, state_wkv, state_gla, c, c_ctx, norm_g, ada_w, ada_b, final_g, e_w_in, e_w_out, conv_w, wkv_w0, wkv_w1, wkv_w2, wkv_a0, wkv_a1, wkv_a2, wkv_k_k, wkv_k_a, wkv_r_k, wkv_ln_w, wkv_ln_b, o_w_in, o_w_out, gla_gk1, gla_gk2, gla_gk_b, gla_g_norm):
    x_ctx = x_prompt.reshape(N_CTX_TOK, D)
    x_smp = x_sample.reshape(N_TOK - N_CTX_TOK, D)
    cvec = jnp.zeros((8, D), F32).at[0].set(c_ctx).at[1:1 + SMP_B].set(c)
    mod = _modulation(cvec, ada_w, ada_b)
    masks = _chunk_masks()

    mod3 = mod[0].reshape(8, 3, D)
    wlr = jnp.concatenate([wkv_w1[0, 0], wkv_w1[0, 1], wkv_a1[0, 0], wkv_a1[0, 1]], axis=1)
    p_e, lr_e = _inproj(x_ctx, x_smp, 0, mod3, norm_g[0:1], e_w_in[0].astype(BF16), wlr.astype(BF16),
                        n_tanh=2 * 64)
    w2pad = _pad_dirs(wkv_w2[0])
    a2pad = _pad_dirs(wkv_a2[0])
    w0 = wkv_w0[0].reshape(2, 1, D)
    a0 = wkv_a0[0].reshape(2, 1, D)
    k_k = wkv_k_k[0].reshape(1, D)
    k_a = wkv_k_a[0].reshape(1, D)
    r_k = wkv_r_k[0].reshape(1, D)
    o_ctx, new_wkv = _wkv_scan(p_e, lr_e, w2pad, a2pad, w0, a0, k_k, k_a, None,
                               n_seq=CTX_B, seq_len=CTX_T, n_seg=WKV_CTX_SEG, tok_blk0=0)
    o_smp, _ = _wkv_scan(p_e, lr_e, w2pad, a2pad, w0, a0, k_k, k_a, state_wkv[:, 0],
                         n_seq=SMP_B, seq_len=SMP_T, n_seg=1, tok_blk0=N_CTX_TOK // SMP_T)
    ind64, bind64 = _group_indicators(WKV_N)
    x = _even_out(x_ctx, x_smp, mod3, p_e, o_ctx, o_smp, lr_e, a2pad, a0, k_a, r_k, conv_w[0],
                  wkv_ln_w[0].reshape(1, D), wkv_ln_b[0].reshape(1, D), ind64, bind64,
                  e_w_out[0].astype(BF16))

    mod3 = mod[1].reshape(8, 3, D)
    rank = gla_gk1.shape[-1]
    wlr = jnp.zeros((D, LANES), F32).at[:, 0:rank].set(gla_gk1[0, 0]).at[:, rank:2 * rank].set(gla_gk1[0, 1])
    p_o, lr_o = _inproj(x, x, N_CTX_TOK // TM_PROJ, mod3, norm_g[1:2], o_w_in[0].astype(BF16),
                        wlr.astype(BF16), n_tanh=0)
    gk2pad = _pad_dirs(gla_gk2[0])
    gkb = gla_gk_b[0].reshape(2, 1, GLA_H * GLA_DK)
    g_ctx, new_gla = _gla_scan(p_o, lr_o, gk2pad, gkb, None, masks,
                               n_seq=CTX_B, seq_len=CTX_T, n_seg=GLA_CTX_SEG, tok_blk0=0)
    g_smp, _ = _gla_scan(p_o, lr_o, gk2pad, gkb, state_gla[:, 0], masks,
                         n_seq=SMP_B, seq_len=SMP_T, n_seg=1, tok_blk0=N_CTX_TOK // SMP_T)
    ind256, bind256 = _group_indicators(GLA_DV)
    y_ctx, y_smp = _odd_out(x, mod3, p_o, g_ctx, g_smp, jnp.tile(gla_g_norm[0], GLA_H).reshape(1, D),
                            ind256, bind256, o_w_out[0].astype(BF16), final_g.reshape(1, D))
    return (y_ctx.reshape(CTX_B, CTX_T, D), y_smp.reshape(SMP_B, SMP_T, D), new_wkv[:, None], new_gla[:, None])
```
